```python
import jax, jax.numpy as jnp
from jax import lax
import numpy as np

D_MODEL = 2048
BATCH = 4
SEQ = 2048
DEPTH = 2
DEC_BATCH = 8
DEC_SEQ = 4096
PAST_LEN = 128

MLA_HEADS = 16
Q_LORA = 512
KV_LORA = 512
NOPE_DIM = 128
ROPE_DIM = 64
V_DIM = 128
ROPE_THETA = 10000.0
Q_BLOCK = 128
RW_HEAD = 64
RW_HEADS = 16
RW_DIM = RW_HEADS * RW_HEAD
DECAY_LORA = 64
AAA_LORA = 64
MV_LORA = 32
GATE_LORA = 160
LNX_EPS = 64e-5
PEER_HEADS = 8
N_KEYS = 128
N_EXPERTS = N_KEYS * N_KEYS
PEER_HALF = 128
PEER_QDIM = 2 * PEER_HALF
PEER_TOPK = 16
PEER_CHUNK = 128
EPS = 1e-6

MLA_COLS = Q_LORA + KV_LORA + ROPE_DIM
GATE_COLS = 2 * D_MODEL
RW_COLS = 3 * RW_DIM + 2 * DECAY_LORA + 2 * AAA_LORA + GATE_LORA
IN_COLS_FIRST = MLA_COLS + GATE_COLS + RW_COLS
IN_COLS_REST = IN_COLS_FIRST + MV_LORA
RW_SPLITS = [RW_DIM, 2 * RW_DIM, 3 * RW_DIM,
             3 * RW_DIM + DECAY_LORA, 3 * RW_DIM + 2 * DECAY_LORA,
             3 * RW_DIM + 2 * DECAY_LORA + AAA_LORA, 3 * RW_DIM + 2 * DECAY_LORA + 2 * AAA_LORA,
             RW_COLS]

kernel_name = "mla_rwkv7_peer_bidir_encoder"


def rmsnorm(x, g):
    xf = x.astype(jnp.float32)
    y = xf * lax.rsqrt(jnp.mean(xf * xf, axis=-1, keepdims=True) + EPS)
    return (y * g.astype(jnp.float32)).astype(x.dtype)


def rope_tables(T):
    inv = 1.0 / (ROPE_THETA ** (jnp.arange(0, ROPE_DIM, 2, dtype=jnp.float32) / ROPE_DIM))
    ang = jnp.arange(T, dtype=jnp.float32)[:, None] * inv[None, :]
    return jnp.cos(ang), jnp.sin(ang)


def apply_rope(x, cos, sin):
    x1, x2 = jnp.split(x, 2, axis=-1)
    c = cos.astype(x.dtype)
    s = sin.astype(x.dtype)
    return jnp.concatenate([x1 * c - x2 * s, x1 * s + x2 * c], axis=-1)


def mla_branch(h_mla, cos, sin, q_norm, w_uq, kv_norm, w_ukv, w_o_attn):
    B, T, _ = h_mla.shape
    c_q, c_kv, k_rope = jnp.split(h_mla, [Q_LORA, Q_LORA + KV_LORA], axis=-1)
    q = (rmsnorm(c_q, q_norm) @ w_uq).reshape(B, T, MLA_HEADS, NOPE_DIM + ROPE_DIM)
    q_nope = q[..., :NOPE_DIM]
    q_rope = apply_rope(q[..., NOPE_DIM:], cos[:, None, :], sin[:, None, :])
    kv = (rmsnorm(c_kv, kv_norm) @ w_ukv).reshape(B, T, MLA_HEADS, NOPE_DIM + V_DIM)
    k_nope = kv[..., :NOPE_DIM]
    v = kv[..., NOPE_DIM:]
    k_rope = apply_rope(k_rope, cos, sin)
    scale = (NOPE_DIM + ROPE_DIM) ** -0.5
    nb = T // Q_BLOCK
    qn_b = jnp.moveaxis(q_nope.reshape(B, nb, Q_BLOCK, MLA_HEADS, NOPE_DIM), 1, 0)
    qr_b = jnp.moveaxis(q_rope.reshape(B, nb, Q_BLOCK, MLA_HEADS, ROPE_DIM), 1, 0)

    def q_block(args):
        qn, qr = args
        s = (jnp.einsum('bqhd,bkhd->bhqk', qn, k_nope)
             + jnp.einsum('bqhr,bkr->bhqk', qr, k_rope))
        p = jax.nn.softmax(s.astype(jnp.float32) * scale, axis=-1).astype(v.dtype)
        return jnp.einsum('bhqk,bkhd->bqhd', p, v)

    o = lax.map(q_block, (qn_b, qr_b))
    o = jnp.moveaxis(o, 0, 1).reshape(B, T, MLA_HEADS * V_DIM)
    return o @ w_o_attn


def token_shift(h, mu):
    zero = jnp.zeros_like(h[:, :1])
    prev = jnp.concatenate([zero, h[:, :-1]], axis=1)
    nxt = jnp.concatenate([h[:, 1:], zero], axis=1)
    return h + (0.5 * (prev + nxt) - h) * mu


def rw_decay(dw, w0, w2):
    w = -jax.nn.softplus(-(w0 + jnp.tanh(dw) @ w2).astype(jnp.float32)) - 0.5
    return jnp.exp(-jnp.exp(w))


def wkv_scan(r, w, k, v, a, b, reverse):
    B, T, H, N = r.shape

    def step(S, inp):
        r_t, w_t, k_t, v_t, a_t, b_t = inp
        sa = jnp.einsum('bhij,bhj->bhi', S, a_t)
        S = (S * w_t[:, :, None, :] + sa[..., None] * b_t[:, :, None, :]
             + v_t[..., None] * k_t[:, :, None, :])
        return S, jnp.einsum('bhij,bhj->bhi', S, r_t)

    xs = tuple(jnp.swapaxes(t, 0, 1) for t in (r, w, k, v, a, b))
    S0 = jnp.zeros((B, H, N, N), jnp.float32)
    _, y = lax.scan(step, S0, xs, reverse=reverse)
    return jnp.swapaxes(y, 0, 1)


def rwkv_branch(h_rw, v_first, mu, w0_f, w2_f, w0_b, w2_b, a0_f, a2_f, a0_b, a2_b,
                g2, k_k, k_a, r_k, lnx_w, lnx_b, v0, v2, w_o_rwkv):
    B, T, _ = h_rw.shape
    f32 = jnp.float32
    h = token_shift(h_rw, mu)
    r, k, v, dwf, dwb, daf, dab, dg, dv = jnp.split(h, RW_SPLITS, axis=-1)
    if v_first is None:
        v_first = v
    else:
        v = v + (v_first - v) * jax.nn.sigmoid(v0 + dv @ v2)
    heads = lambda t: t.astype(f32).reshape(B, T, RW_HEADS, RW_HEAD)
    wf = rw_decay(dwf, w0_f, w2_f)
    wb = rw_decay(dwb, w0_b, w2_b)
    af = jax.nn.sigmoid((a0_f + daf @ a2_f).astype(f32))
    ab = jax.nn.sigmoid((a0_b + dab @ a2_b).astype(f32))
    g = jax.nn.sigmoid(dg) @ g2
    kk = heads(k * k_k)
    kk = kk / jnp.maximum(jnp.sqrt(jnp.sum(kk * kk, axis=-1, keepdims=True)), 1e-12)
    kf = heads(k.astype(f32) * (1.0 + (af - 1.0) * k_a.astype(f32)))
    kb = heads(k.astype(f32) * (1.0 + (ab - 1.0) * k_a.astype(f32)))
    rh, vh = heads(r), heads(v)
    y = (wkv_scan(rh, heads(wf), kf, vh, -kk, kk * heads(af), reverse=False)
         + wkv_scan(rh, heads(wb), kb, vh, -kk, kk * heads(ab), reverse=True))
    mean = jnp.mean(y, axis=-1, keepdims=True)
    var = jnp.mean(jnp.square(y - mean), axis=-1, keepdims=True)
    y = ((y - mean) * lax.rsqrt(var + LNX_EPS)).reshape(B, T, RW_DIM)
    y = y * lnx_w.astype(f32) + lnx_b.astype(f32)
    bonus = jnp.sum(rh * (kf + kb) * r_k.astype(f32), axis=-1, keepdims=True) * vh
    y = (y + bonus.reshape(B, T, RW_DIM)).astype(h_rw.dtype)
    return (y * g) @ w_o_rwkv, v_first


def peer(xn, w_pq, sub_keys, peer_u, peer_v):
    B, T, D = xn.shape
    xt = xn.reshape(-1, PEER_CHUNK, D)

    def chunk(xc):
        C = xc.shape[0]
        q = (xc @ w_pq).reshape(C, PEER_HEADS, 2, PEER_HALF)
        s = jnp.einsum('chpd,pnd->chpn', q, sub_keys).astype(jnp.float32)
        sv, si = lax.top_k(s, PEER_TOPK)
        cand = sv[:, :, 0, :, None] + sv[:, :, 1, None, :]
        cand_id = si[:, :, 0, :, None] * N_KEYS + si[:, :, 1, None, :]
        top_s, top_j = lax.top_k(cand.reshape(C, PEER_HEADS, PEER_TOPK * PEER_TOPK), PEER_TOPK)
        ids = jnp.take_along_axis(cand_id.reshape(C, PEER_HEADS, PEER_TOPK * PEER_TOPK), top_j, axis=-1)
        gate = jax.nn.softmax(top_s, axis=-1)
        act = jax.nn.gelu(jnp.einsum('chkd,cd->chk', peer_u[ids], xc).astype(jnp.float32), approximate=False)
        coef = (gate * act).astype(xc.dtype)
        return jnp.einsum('chk,chkd->cd', coef, peer_v[ids])

    return lax.map(chunk, xt).reshape(B, T, D)


def encoder_layer(x, v_first, cos, sin, ln1, w_in, mu, q_norm, w_uq, kv_norm, w_ukv, w_o_attn,
                  w0_f, w2_f, w0_b, w2_b, a0_f, a2_f, a0_b, a2_b, g2, k_k, k_a, r_k, lnx_w, lnx_b,
                  v0, v2, w_o_rwkv, w_out, ln2, w_pq, sub_keys, peer_u, peer_v):
    h = rmsnorm(x, ln1) @ w_in
    h_mla, h_gate, h_rw = jnp.split(h, [MLA_COLS, MLA_COLS + GATE_COLS], axis=-1)
    attn = mla_branch(h_mla, cos, sin, q_norm, w_uq, kv_norm, w_ukv, w_o_attn)
    rw, v_first = rwkv_branch(h_rw, v_first, mu, w0_f, w2_f, w0_b, w2_b, a0_f, a2_f, a0_b, a2_b,
                              g2, k_k, k_a, r_k, lnx_w, lnx_b, v0, v2, w_o_rwkv)
    gates = jax.nn.sigmoid(h_gate.astype(jnp.float32)).astype(x.dtype)
    g_attn, g_rw = jnp.split(gates, 2, axis=-1)
    x = x + (g_attn * attn + g_rw * rw) @ w_out
    x = x + peer(rmsnorm(x, ln2), w_pq, sub_keys, peer_u, peer_v)
    return x, v_first


def trunk(x, ln1, w_in_first, mu_first, w_in_rest, mu_rest, q_norm, w_uq, kv_norm, w_ukv, w_o_attn,
          w0_f, w2_f, w0_b, w2_b, a0_f, a2_f, a0_b, a2_b, g2, k_k, k_a, r_k, lnx_w, lnx_b,
          v0_rest, v2_rest, w_o_rwkv, w_out, ln2, w_pq, sub_keys, peer_u, peer_v, ln_f):
    cos, sin = rope_tables(x.shape[1])
    v_first = None
    for i in range(DEPTH):
        if i == 0:
            w_in, mu, v0, v2 = w_in_first, mu_first, None, None
        else:
            w_in, mu, v0, v2 = w_in_rest[i - 1], mu_rest[i - 1], v0_rest[i - 1], v2_rest[i - 1]
        x, v_first = encoder_layer(
            x, v_first, cos, sin, ln1[i], w_in, mu, q_norm[i], w_uq[i], kv_norm[i], w_ukv[i], w_o_attn[i],
            w0_f[i], w2_f[i], w0_b[i], w2_b[i], a0_f[i], a2_f[i], a0_b[i], a2_b[i], g2[i], k_k[i], k_a[i],
            r_k[i], lnx_w[i], lnx_b[i], v0, v2, w_o_rwkv[i], w_out[i], ln2[i], w_pq[i], sub_keys[i],
            peer_u[i], peer_v[i])
    return rmsnorm(x, ln_f)


def setup_inputs(seed: int = 0) -> dict:
    key = jax.random.key(seed)
    ks = iter(jax.random.split(key, 64))
    f32 = jnp.float32
    nrm = lambda shape, scale: jax.random.normal(next(ks), shape, f32) * scale
    gain = lambda shape: 1.0 + 0.02 * jax.random.normal(next(ks), shape, f32)
    unif = lambda shape, lo, hi: jax.random.uniform(next(ks), shape, f32, lo, hi)
    L, R, D = DEPTH, DEPTH - 1, D_MODEL
    return {
        "x_prompt": nrm((BATCH, SEQ, D), 1.0),
        "x_sample": nrm((DEC_BATCH, DEC_SEQ, D), 1.0),
        "ln1": gain((L, D)),
        "w_in_first": nrm((D, IN_COLS_FIRST), D ** -0.5),
        "mu_first": unif((RW_COLS,), 0.0, 1.0),
        "w_in_rest": nrm((R, D, IN_COLS_REST), D ** -0.5),
        "mu_rest": unif((R, RW_COLS + MV_LORA), 0.0, 1.0),
        "q_norm": gain((L, Q_LORA)),
        "w_uq": nrm((L, Q_LORA, MLA_HEADS * (NOPE_DIM + ROPE_DIM)), Q_LORA ** -0.5),
        "kv_norm": gain((L, KV_LORA)),
        "w_ukv": nrm((L, KV_LORA, MLA_HEADS * (NOPE_DIM + V_DIM)), KV_LORA ** -0.5),
        "w_o_attn": nrm((L, MLA_HEADS * V_DIM, D), (MLA_HEADS * V_DIM) ** -0.5),
        "w0_f": unif((L, RW_DIM), -5.0, 0.0),
        "w2_f": nrm((L, DECAY_LORA, RW_DIM), DECAY_LORA ** -0.5),
        "w0_b": unif((L, RW_DIM), -5.0, 0.0),
        "w2_b": nrm((L, DECAY_LORA, RW_DIM), DECAY_LORA ** -0.5),
        "a0_f": nrm((L, RW_DIM), 0.1),
        "a2_f": nrm((L, AAA_LORA, RW_DIM), AAA_LORA ** -0.5),
        "a0_b": nrm((L, RW_DIM), 0.1),
        "a2_b": nrm((L, AAA_LORA, RW_DIM), AAA_LORA ** -0.5),
        "g2": nrm((L, GATE_LORA, RW_DIM), GATE_LORA ** -0.5),
        "k_k": 1.0 + nrm((L, RW_DIM), 0.1),
        "k_a": 1.0 + nrm((L, RW_DIM), 0.1),
        "r_k": nrm((L, RW_HEADS, RW_HEAD), 0.1),
        "lnx_w": gain((L, RW_DIM)),
        "lnx_b": nrm((L, RW_DIM), 0.02),
        "v0_rest": 1.0 + nrm((R, RW_DIM), 0.1),
        "v2_rest": nrm((R, MV_LORA, RW_DIM), MV_LORA ** -0.5),
        "w_o_rwkv": nrm((L, RW_DIM, D), RW_DIM ** -0.5),
        "w_out": nrm((L, D, D), D ** -0.5),
        "ln2": gain((L, D)),
        "w_pq": nrm((L, D, PEER_HEADS * PEER_QDIM), D ** -0.5),
        "sub_keys": nrm((L, 2, N_KEYS, PEER_HALF), PEER_HALF ** -0.5),
        "peer_u": nrm((L, N_EXPERTS, D), D ** -0.5),
        "peer_v": nrm((L, N_EXPERTS, D), PEER_HEADS ** -0.5),
        "ln_f": gain((D,)),
    }


def reference(x_prompt, x_sample, ln1, w_in_first, mu_first, w_in_rest, mu_rest, q_norm, w_uq, kv_norm,
              w_ukv, w_o_attn, w0_f, w2_f, w0_b, w2_b, a0_f, a2_f, a0_b, a2_b, g2, k_k, k_a, r_k,
              lnx_w, lnx_b, v0_rest, v2_rest, w_o_rwkv, w_out, ln2, w_pq, sub_keys, peer_u, peer_v, ln_f):
    y_prompt = trunk(x_prompt, ln1, w_in_first, mu_first, w_in_rest, mu_rest, q_norm, w_uq, kv_norm,
                     w_ukv, w_o_attn, w0_f, w2_f, w0_b, w2_b, a0_f, a2_f, a0_b, a2_b, g2, k_k, k_a, r_k,
                     lnx_w, lnx_b, v0_rest, v2_rest, w_o_rwkv, w_out, ln2, w_pq, sub_keys, peer_u, peer_v, ln_f)
    y_sample = trunk(x_sample, ln1, w_in_first, mu_first, w_in_rest, mu_rest, q_norm, w_uq, kv_norm,
                     w_ukv, w_o_attn, w0_f, w2_f, w0_b, w2_b, a0_f, a2_f, a0_b, a2_b, g2, k_k, k_a, r_k,
                     lnx_w, lnx_b, v0_rest, v2_rest, w_o_rwkv, w_out, ln2, w_pq, sub_keys, peer_u, peer_v, ln_f)
    return (y_prompt, y_sample)
```

```python
import functools
import math

import jax
import jax.numpy as jnp
from jax import lax
from jax.experimental import pallas as pl
from jax.experimental.pallas import tpu as pltpu

F32 = jnp.float32
BF16 = jnp.bfloat16
HIGHEST = lax.Precision.HIGHEST

LANES = 128
SUBLANES = 8
VMEM_LIMIT_BYTES = 56 * 1024 * 1024

D_MODEL = 2048
MLA_HEADS = 16
Q_LORA = 512
KV_LORA = 512
NOPE_DIM = 128
ROPE_DIM = 64
V_DIM = 128
ROPE_THETA = 10000.0
RW_HEAD = 64
RW_HEADS = 16
RW_DIM = RW_HEADS * RW_HEAD
DECAY_LORA = 64
AAA_LORA = 64
MV_LORA = 32
GATE_LORA = 160
LNX_EPS = 64e-5
PEER_HEADS = 8
N_KEYS = 128
PEER_HALF = 128
PEER_TOPK = 16
EPS = 1e-6

QK_PAD = 2 * LANES
MLA_W = Q_LORA + KV_LORA + LANES
LORA_W = 2 * DECAY_LORA + 2 * AAA_LORA + 2 * LANES + 2 * LANES
RW_W = 3 * RW_DIM + LORA_W
WKV_CHUNK = 64
G_PITCH = N_KEYS + SUBLANES
NSLOT = PEER_HEADS * PEER_TOPK

_TB = (((1,), (1,)), ((), ()))
_TA = (((0,), (0,)), ((), ()))


def _cparams(*sem):
    return pltpu.CompilerParams(dimension_semantics=sem, vmem_limit_bytes=VMEM_LIMIT_BYTES)


def _rms_rows(x, g, eps):
    ms = jnp.mean(x * x, axis=-1, keepdims=True)
    return x * lax.rsqrt(ms + eps) * g


def _rms_mm_body(x_ref, g_ref, w_ref, o_ref, *rest, emit_xn):
    xn_ref = rest[-1]

    @pl.when(pl.program_id(1) == 0)
    def _():
        xn = _rms_rows(x_ref[...].astype(F32), g_ref[...], EPS).astype(BF16)
        xn_ref[...] = xn
        if emit_xn:
            rest[0][...] = xn

    o_ref[...] = jnp.dot(xn_ref[...], w_ref[...], preferred_element_type=F32).astype(o_ref.dtype)


def rms_matmul(x, gain, w, *, tm, tn, out_dtype=F32, emit_xn=False):
    n, k = x.shape
    nc = w.shape[1]
    tn = min(tn, nc)
    assert n % tm == 0 and nc % tn == 0 and w.shape[0] == k
    out_shape = [jax.ShapeDtypeStruct((n, nc), out_dtype)]
    out_specs = [pl.BlockSpec((tm, tn), lambda i, j: (i, j))]
    if emit_xn:
        out_shape.append(jax.ShapeDtypeStruct((n, k), BF16))
        out_specs.append(pl.BlockSpec((tm, k), lambda i, j: (i, 0)))
    res = pl.pallas_call(
        functools.partial(_rms_mm_body, emit_xn=emit_xn),
        grid=(n // tm, nc // tn),
        in_specs=[pl.BlockSpec((tm, k), lambda i, j: (i, 0)),
                  pl.BlockSpec((1, k), lambda i, j: (0, 0)),
                  pl.BlockSpec((k, tn), lambda i, j: (0, j))],
        out_specs=out_specs,
        out_shape=out_shape,
        scratch_shapes=[pltpu.VMEM((tm, k), BF16)],
        compiler_params=_cparams("parallel", "arbitrary"),
        name="rms_matmul",
    )(x, gain.reshape(1, k), w)
    return res if emit_xn else res[0]


def _rope_lanes(y2, ccss):
    w = y2 * ccss
    return w + pltpu.roll(w, ROPE_DIM, axis=1)


def _qproj_body(c_ref, g_ref, w_ref, t_ref, q_ref, cn_ref, *, scale):
    @pl.when(pl.program_id(1) == 0)
    def _():
        cn_ref[...] = _rms_rows(c_ref[...].astype(F32), g_ref[...], EPS).astype(BF16)

    y = jnp.dot(cn_ref[...], w_ref[...], preferred_element_type=F32)
    q_ref[:, 0:LANES] = (y[:, 0:LANES] * scale).astype(q_ref.dtype)
    rot = _rope_lanes(y[:, LANES:], t_ref[...])
    lane = lax.broadcasted_iota(jnp.int32, rot.shape, 1)
    q_ref[:, LANES:] = jnp.where(lane < ROPE_DIM, rot * scale, 0.0).astype(q_ref.dtype)


def mla_q_proj(h_mla, q_norm, wq, ccss, *, seq, tm):
    n = h_mla.shape[0]
    nt = seq // tm
    scale = (NOPE_DIM + ROPE_DIM) ** -0.5
    return pl.pallas_call(
        functools.partial(_qproj_body, scale=scale),
        grid=(n // tm, MLA_HEADS),
        in_specs=[pl.BlockSpec((tm, Q_LORA), lambda i, h: (i, 0)),
                  pl.BlockSpec((1, Q_LORA), lambda i, h: (0, 0)),
                  pl.BlockSpec((Q_LORA, QK_PAD), lambda i, h: (0, h)),
                  pl.BlockSpec((tm, LANES), lambda i, h: (i % nt, 0))],
        out_specs=pl.BlockSpec((tm, QK_PAD), lambda i, h: (i, h)),
        out_shape=jax.ShapeDtypeStruct((n, MLA_HEADS * QK_PAD), BF16),
        scratch_shapes=[pltpu.VMEM((tm, Q_LORA), BF16)],
        compiler_params=_cparams("parallel", "arbitrary"),
        name="mla_q_proj",
    )(h_mla, q_norm.reshape(1, Q_LORA), wq, ccss)


def _kvproj_body(c_ref, kr_ref, g_ref, w_ref, t_ref, k_ref, v_ref, cn_ref, krr_ref):
    @pl.when(pl.program_id(1) == 0)
    def _():
        cn_ref[...] = _rms_rows(c_ref[...].astype(F32), g_ref[...], EPS).astype(BF16)
        krr_ref[...] = _rope_lanes(kr_ref[...].astype(F32), t_ref[...]).astype(BF16)

    y = jnp.dot(cn_ref[...], w_ref[...], preferred_element_type=F32)
    k_ref[:, 0:LANES] = y[:, 0:LANES].astype(k_ref.dtype)
    k_ref[:, LANES:] = krr_ref[...]
    v_ref[...] = y[:, LANES:].astype(v_ref.dtype)


def mla_kv_proj(h_mla, kv_norm, wkv, ccss, *, seq, tm):
    n = h_mla.shape[0]
    nt = seq // tm
    return pl.pallas_call(
        _kvproj_body,
        grid=(n // tm, MLA_HEADS),
        in_specs=[pl.BlockSpec((tm, KV_LORA), lambda i, h: (i, 1)),
                  pl.BlockSpec((tm, LANES), lambda i, h: (i, (Q_LORA + KV_LORA) // LANES)),
                  pl.BlockSpec((1, KV_LORA), lambda i, h: (0, 0)),
                  pl.BlockSpec((KV_LORA, NOPE_DIM + V_DIM), lambda i, h: (0, h)),
                  pl.BlockSpec((tm, LANES), lambda i, h: (i % nt, 0))],
        out_specs=[pl.BlockSpec((tm, QK_PAD), lambda i, h: (i, h)),
                   pl.BlockSpec((tm, V_DIM), lambda i, h: (i, h))],
        out_shape=[jax.ShapeDtypeStruct((n, MLA_HEADS * QK_PAD), BF16),
                   jax.ShapeDtypeStruct((n, MLA_HEADS * V_DIM), BF16)],
        scratch_shapes=[pltpu.VMEM((tm, KV_LORA), BF16), pltpu.VMEM((tm, LANES), BF16)],
        compiler_params=_cparams("parallel", "arbitrary"),
        name="mla_kv_proj",
    )(h_mla, h_mla, kv_norm.reshape(1, KV_LORA), wkv, ccss)


def _attn_body(q_ref, k_ref, v_ref, o_ref):
    s = lax.dot_general(q_ref[...], k_ref[...], _TB, preferred_element_type=F32)
    m = jnp.max(s, axis=-1, keepdims=True)
    p = jnp.exp(s - m)
    l = jnp.sum(p, axis=-1, keepdims=True)
    o = jnp.dot(p.astype(BF16), v_ref[...], preferred_element_type=F32)
    o_ref[...] = (o / l).astype(o_ref.dtype)


def attention(q, k, v, *, batch, seq, tq):
    n = q.shape[0]
    nq = seq // tq
    return pl.pallas_call(
        _attn_body,
        grid=(batch, MLA_HEADS, nq),
        in_specs=[pl.BlockSpec((tq, QK_PAD), lambda b, h, i: (b * nq + i, h)),
                  pl.BlockSpec((seq, QK_PAD), lambda b, h, i: (b, h)),
                  pl.BlockSpec((seq, V_DIM), lambda b, h, i: (b, h))],
        out_specs=pl.BlockSpec((tq, V_DIM), lambda b, h, i: (b * nq + i, h)),
        out_shape=jax.ShapeDtypeStruct((n, MLA_HEADS * V_DIM), BF16),
        compiler_params=_cparams("parallel", "parallel", "arbitrary"),
        name="mla_attention",
    )(q, k, v)


def _head_sums(x, bd):
    parts = []
    for gi in range(RW_DIM // LANES):
        parts.append(jnp.dot(x[:, gi * LANES:(gi + 1) * LANES], bd, precision=HIGHEST,
                             preferred_element_type=F32))
    return jnp.concatenate(parts, axis=1)


def _sigmoid(x):
    return 1.0 / (1.0 + jnp.exp(-x))


def _rw_prep_body(*refs, tiles_per_seq, tm, has_vfirst):
    (h_ref, hp_ref, hn_ref, mu_ref, vec_ref, w2f_ref, w2b_ref, a2f_ref, a2b_ref, g2_ref, bd_ref) = refs[:11]
    pos = 11
    if has_vfirst:
        v2_ref, vf_ref = refs[pos:pos + 2]
        pos += 2
    r_ref, v_ref, kk_ref, ag_ref, lw_ref, k2_ref, g_ref = refs[pos:pos + 7]

    ti = pl.program_id(0) % tiles_per_seq
    keep_prev = jnp.where(ti == 0, 0.0, 1.0)
    keep_next = jnp.where(ti == tiles_per_seq - 1, 0.0, 1.0)
    row = lax.broadcasted_iota(jnp.int32, (tm, 1), 0)

    def shifted(c0, c1):
        x = h_ref[:, c0:c1]
        prev = jnp.where(row == 0, hp_ref[SUBLANES - 1:SUBLANES, c0:c1] * keep_prev, pltpu.roll(x, 1, axis=0))
        nxt = jnp.where(row == tm - 1, hn_ref[0:1, c0:c1] * keep_next, pltpu.roll(x, tm - 1, axis=0))
        return x + (0.5 * (prev + nxt) - x) * mu_ref[:, c0:c1]

    vec = vec_ref[...]
    w0f, w0b, a0f, a0b, k_k, k_a, v0 = (vec[i:i + 1, :] for i in range(7))

    r = shifted(0, RW_DIM)
    k = shifted(RW_DIM, 2 * RW_DIM)
    v = shifted(2 * RW_DIM, 3 * RW_DIM)
    lo = shifted(3 * RW_DIM, RW_W)
    dw = jnp.tanh(lo[:, 0:LANES]).astype(BF16)
    da = lo[:, LANES:2 * LANES].astype(BF16)
    dg = _sigmoid(lo[:, 2 * LANES:4 * LANES]).astype(BF16)

    def decay(dw_half, w0, w2_ref):
        z = w0 + jnp.dot(dw_half, w2_ref[...], preferred_element_type=F32)
        nz = -z
        softplus = jnp.maximum(nz, 0.0) + jnp.log(1.0 + jnp.exp(-jnp.abs(nz)))
        return -jnp.exp(-softplus - 0.5)

    lw_ref[:, 0:RW_DIM] = decay(dw, w0f, w2f_ref)
    lw_ref[:, RW_DIM:] = decay(dw, w0b, w2b_ref)
    af = _sigmoid(a0f + jnp.dot(da, a2f_ref[...], preferred_element_type=F32))
    ab = _sigmoid(a0b + jnp.dot(da, a2b_ref[...], preferred_element_type=F32))
    ag_ref[:, 0:RW_DIM] = af
    ag_ref[:, RW_DIM:] = ab
    g_ref[...] = jnp.dot(dg, g2_ref[...], preferred_element_type=F32)

    if has_vfirst:
        dv = lo[:, 4 * LANES:5 * LANES].astype(BF16)
        mix = _sigmoid(v0 + jnp.dot(dv, v2_ref[...], preferred_element_type=F32))
        v = v + (vf_ref[...] - v) * mix
    v_ref[...] = v
    r_ref[...] = r

    kk = k * k_k
    ss = _head_sums(kk * kk, bd_ref[...])
    kk_ref[...] = kk * lax.rsqrt(jnp.maximum(ss, 1e-24))
    k2_ref[:, 0:RW_DIM] = k * (1.0 + (af - 1.0) * k_a)
    k2_ref[:, RW_DIM:] = k * (1.0 + (ab - 1.0) * k_a)


def rwkv_prep(h_rw, mu, vec, w2f, w2b, a2f, a2b, g2, bd, v2, v_first, *, seq, tm):
    n = h_rw.shape[0]
    tps = seq // tm
    nb8 = n // SUBLANES
    has_vfirst = v_first is not None
    full = lambda a: pl.BlockSpec(a.shape, lambda i: (0,) * a.ndim)
    in_specs = [pl.BlockSpec((tm, RW_W), lambda i: (i, 0)),
                pl.BlockSpec((SUBLANES, RW_W), lambda i: (jnp.maximum(i * (tm // SUBLANES) - 1, 0), 0)),
                pl.BlockSpec((SUBLANES, RW_W), lambda i: (jnp.minimum((i + 1) * (tm // SUBLANES), nb8 - 1), 0)),
                full(mu), full(vec), full(w2f), full(w2b), full(a2f), full(a2b), full(g2), full(bd)]
    args = [h_rw, h_rw, h_rw, mu, vec, w2f, w2b, a2f, a2b, g2, bd]
    if has_vfirst:
        in_specs += [full(v2), pl.BlockSpec((tm, RW_DIM), lambda i: (i, 0))]
        args += [v2, v_first]
    row = lambda w: pl.BlockSpec((tm, w), lambda i: (i, 0))
    widths = [RW_DIM, RW_DIM, RW_DIM, 2 * RW_DIM, 2 * RW_DIM, 2 * RW_DIM, RW_DIM]
    return pl.pallas_call(
        functools.partial(_rw_prep_body, tiles_per_seq=tps, tm=tm, has_vfirst=has_vfirst),
        grid=(n // tm,),
        in_specs=in_specs,
        out_specs=[row(w) for w in widths],
        out_shape=[jax.ShapeDtypeStruct((n, w), F32) for w in widths],
        compiler_params=_cparams("parallel"),
        name="rwkv_prep",
    )(*args)


def _wkv_body(r_ref, v_ref, kk_ref, lw_ref, k_ref, ag_ref, y_ref, st_ref):
    C = WKV_CHUNK
    rev = pl.program_id(2) == 1

    @pl.when(pl.program_id(3) == 0)
    def _():
        st_ref[...] = jnp.zeros_like(st_ref)

    r, v, kk, lw, k, ag = (x[...] for x in (r_ref, v_ref, kk_ref, lw_ref, k_ref, ag_ref))
    ti = lax.broadcasted_iota(jnp.int32, (C, C), 0)
    si = lax.broadcasted_iota(jnp.int32, (C, C), 1)
    sgn = jnp.where(rev, -1, 1)
    tri = jnp.where((ti - si) * sgn >= 0, 1.0, 0.0)
    cum = jnp.dot(tri, lw, precision=HIGHEST, preferred_element_type=F32)
    tot = jnp.sum(lw, axis=0, keepdims=True)
    at = -kk * jnp.exp(cum - lw)
    rt = r * jnp.exp(cum)
    ei = jnp.exp(-cum)
    b = kk * ag
    bt = b * ei
    kt = k * ei
    ew = jnp.exp(tot - cum)
    btw = b * ew
    ktw = k * ew
    wtot = jnp.exp(tot)

    lane = lax.broadcasted_iota(jnp.int32, (1, LANES), 1)
    hm0 = jnp.where(lane < RW_HEAD, 1.0, 0.0)
    hm1 = 1.0 - hm0
    stack = lambda x: jnp.concatenate([x * hm0, x * hm1], axis=0).astype(BF16)
    dup = lambda x: jnp.concatenate([x, x], axis=0).astype(BF16)
    xa, xr, vst = stack(at), stack(rt), stack(v)
    x2 = jnp.concatenate([xa, xr], axis=0)
    y2 = jnp.concatenate([dup(bt), dup(kt)], axis=0)
    gm = lax.dot_general(x2, y2, _TB, preferred_element_type=F32)

    ri = lax.broadcasted_iota(jnp.int32, (2 * C, 2 * C), 0)
    ci = lax.broadcasted_iota(jnp.int32, (2 * C, 2 * C), 1)
    same = (ri & C) == (ci & C)
    dts = ((ri & (C - 1)) - (ci & (C - 1))) * sgn
    strict = same & (dts > 0)
    incl = same & (dts >= 0)
    l_ab = jnp.where(strict, gm[0:2 * C, 0:2 * C], 0.0)
    l_ak = jnp.where(strict, gm[0:2 * C, 2 * C:], 0.0).astype(BF16)
    m_rb = jnp.where(incl, gm[2 * C:, 0:2 * C], 0.0).astype(BF16)
    m_rk = jnp.where(incl, gm[2 * C:, 2 * C:], 0.0).astype(BF16)

    eye = jnp.where(ri == ci, 1.0, 0.0)
    pinv = eye + l_ab
    lp = l_ab
    for _ in range(int(math.log2(C)) - 1):
        lpb = lp.astype(BF16)
        lp = jnp.dot(lpb, lpb, preferred_element_type=F32)
        pinv = pinv + jnp.dot(pinv.astype(BF16), lp.astype(BF16), preferred_element_type=F32)

    st = st_ref[...]
    stb = st.astype(BF16)
    pre = (lax.dot_general(xa, stb, _TB, preferred_element_type=F32)
           + jnp.dot(l_ak, vst, preferred_element_type=F32))
    valid = (lax.broadcasted_iota(jnp.int32, (2 * C, LANES), 0) & C) == (
        lax.broadcasted_iota(jnp.int32, (2 * C, LANES), 1) & RW_HEAD)
    ust = jnp.where(valid, jnp.dot(pinv.astype(BF16), pre.astype(BF16), preferred_element_type=F32), 0.0)
    ustb = ust.astype(BF16)
    yst = (lax.dot_general(xr, stb, _TB, preferred_element_type=F32)
           + jnp.dot(m_rb, ustb, preferred_element_type=F32)
           + jnp.dot(m_rk, vst, preferred_element_type=F32))
    yst = jnp.where(valid, yst, 0.0)
    y_ref[...] = yst[0:C] + yst[C:]
    st_ref[...] = (st * wtot
                   + lax.dot_general(ustb, stack(btw), _TA, preferred_element_type=F32)
                   + lax.dot_general(vst, stack(ktw), _TA, preferred_element_type=F32))


def wkv(r, v, kk, lw2, k2, ag2, *, batch, seq):
    n = r.shape[0]
    C = WKV_CHUNK
    nc = seq // C
    npair = RW_DIM // LANES

    def rows(b, d, c):
        return b * nc + jnp.where(d == 1, nc - 1 - c, c)

    shared = pl.BlockSpec((C, LANES), lambda b, p, d, c: (rows(b, d, c), p))
    perdir = pl.BlockSpec((C, LANES), lambda b, p, d, c: (rows(b, d, c), d * npair + p))
    return pl.pallas_call(
        _wkv_body,
        grid=(batch, npair, 2, nc),
        in_specs=[shared, shared, shared, perdir, perdir, perdir],
        out_specs=perdir,
        out_shape=jax.ShapeDtypeStruct((n, 2 * RW_DIM), F32),
        scratch_shapes=[pltpu.VMEM((LANES, LANES), F32)],
        compiler_params=_cparams("parallel", "parallel", "parallel", "arbitrary"),
        name="wkv_scan",
    )(r, v, kk, lw2, k2, ag2)


def _rw_post_body(y2_ref, r_ref, k2_ref, v_ref, g_ref, vec_ref, bd_ref, o_ref):
    vec = vec_ref[...]
    lnx_w, lnx_b, r_k = (vec[i:i + 1, :] for i in range(3))
    bd = bd_ref[...]
    y = y2_ref[:, 0:RW_DIM] + y2_ref[:, RW_DIM:]
    inv = 1.0 / RW_HEAD
    mean = _head_sums(y, bd) * inv
    d = y - mean
    var = _head_sums(d * d, bd) * inv
    yn = d * lax.rsqrt(var + LNX_EPS) * lnx_w + lnx_b
    ksum = k2_ref[:, 0:RW_DIM] + k2_ref[:, RW_DIM:]
    bonus = _head_sums(r_ref[...] * ksum * r_k, bd) * v_ref[...]
    o_ref[...] = ((yn + bonus) * g_ref[...]).astype(o_ref.dtype)


def rwkv_post(y2, r, k2, v, g, vec, bd, *, tm):
    n = r.shape[0]
    row = lambda w: pl.BlockSpec((tm, w), lambda i: (i, 0))
    full = lambda a: pl.BlockSpec(a.shape, lambda i: (0,) * a.ndim)
    return pl.pallas_call(
        _rw_post_body,
        grid=(n // tm,),
        in_specs=[row(2 * RW_DIM), row(RW_DIM), row(2 * RW_DIM), row(RW_DIM), row(RW_DIM), full(vec), full(bd)],
        out_specs=row(RW_DIM),
        out_shape=jax.ShapeDtypeStruct((n, RW_DIM), BF16),
        compiler_params=_cparams("parallel"),
        name="rwkv_post",
    )(y2, r, k2, v, g, vec, bd)


def _merge_body(o_ref, yg_ref, ga_ref, gr_ref, wa_ref, wr_ref, m_ref):
    attn = jnp.dot(o_ref[...], wa_ref[...], preferred_element_type=F32)
    rw = jnp.dot(yg_ref[...], wr_ref[...], preferred_element_type=F32)
    m_ref[...] = (_sigmoid(ga_ref[...].astype(F32)) * attn + _sigmoid(gr_ref[...].astype(F32)) * rw).astype(m_ref.dtype)


def merge_branches(o, yg, gates, wa, wr, *, tm, tn):
    n = o.shape[0]
    nj = D_MODEL // tn
    return pl.pallas_call(
        _merge_body,
        grid=(n // tm, nj),
        in_specs=[pl.BlockSpec((tm, MLA_HEADS * V_DIM), lambda i, j: (i, 0)),
                  pl.BlockSpec((tm, RW_DIM), lambda i, j: (i, 0)),
                  pl.BlockSpec((tm, tn), lambda i, j: (i, j)),
                  pl.BlockSpec((tm, tn), lambda i, j: (i, nj + j)),
                  pl.BlockSpec((MLA_HEADS * V_DIM, tn), lambda i, j: (0, j)),
                  pl.BlockSpec((RW_DIM, tn), lambda i, j: (0, j))],
        out_specs=pl.BlockSpec((tm, tn), lambda i, j: (i, j)),
        out_shape=jax.ShapeDtypeStruct((n, D_MODEL), BF16),
        compiler_params=_cparams("parallel", "parallel"),
        name="merge_branches",
    )(o, yg, gates, gates, wa, wr)


def _mm_res_body(a_ref, w_ref, x_ref, o_ref):
    o_ref[...] = x_ref[...] + jnp.dot(a_ref[...], w_ref[...], preferred_element_type=F32)


def matmul_residual(a, w, x, *, tm, tn):
    n, k = a.shape
    nc = w.shape[1]
    return pl.pallas_call(
        _mm_res_body,
        grid=(n // tm, nc // tn),
        in_specs=[pl.BlockSpec((tm, k), lambda i, j: (i, 0)),
                  pl.BlockSpec((k, tn), lambda i, j: (0, j)),
                  pl.BlockSpec((tm, tn), lambda i, j: (i, j))],
        out_specs=pl.BlockSpec((tm, tn), lambda i, j: (i, j)),
        out_shape=jax.ShapeDtypeStruct((n, nc), F32),
        compiler_params=_cparams("parallel", "parallel"),
        name="matmul_residual",
    )(a, w, x)


def _top16_rows(s, n_rows):
    idx = lax.broadcasted_iota(jnp.int32, s.shape, 0).astype(F32)
    vals, poss = [], []
    for _ in range(PEER_TOPK):
        m = jnp.max(s, axis=0, keepdims=True)
        pos = jnp.min(jnp.where(s == m, idx, float(n_rows)), axis=0, keepdims=True)
        vals.append(m)
        poss.append(pos)
        s = jnp.where(idx == pos, -jnp.inf, s)
    return jnp.concatenate(vals, axis=0), jnp.concatenate(poss, axis=0)


def _pick_rows(table, sel):
    out = jnp.zeros(sel.shape, table.dtype)
    for a in range(PEER_TOPK):
        out = jnp.where(sel == a, table[a:a + 1, :], out)
    return out


def _peer_topk_body(q_ref, sk_ref, i1_ref, i2_ref, gt_ref):
    i1s, i2s, gts = [], [], []
    for h in range(PEER_HEADS):
        sv, si = [], []
        for p in range(2):
            c0 = (2 * h + p) * PEER_HALF
            qb = q_ref[:, c0:c0 + PEER_HALF].astype(BF16)
            s = lax.dot_general(sk_ref[p], qb, _TB, preferred_element_type=F32)
            vals, poss = _top16_rows(s, N_KEYS)
            sv.append(vals)
            si.append(poss)
        cand = jnp.concatenate([sv[0][a:a + 1, :] + sv[1] for a in range(PEER_TOPK)], axis=0)
        top_s, top_j = _top16_rows(cand, PEER_TOPK * PEER_TOPK)
        ja = jnp.floor(top_j * (1.0 / PEER_TOPK))
        i1s.append(_pick_rows(si[0], ja))
        i2s.append(_pick_rows(si[1], top_j - ja * PEER_TOPK))
        e = jnp.exp(top_s - top_s[0:1, :])
        gts.append(e / jnp.sum(e, axis=0, keepdims=True))
    i1_ref[...] = jnp.concatenate(i1s, axis=0).T
    i2_ref[...] = jnp.concatenate(i2s, axis=0).T
    gt_ref[...] = jnp.concatenate(gts, axis=0).T


def peer_topk(q, sub_keys):
    n = q.shape[0]
    tmk = LANES
    out = pl.BlockSpec((tmk, NSLOT), lambda i: (i, 0))
    return pl.pallas_call(
        _peer_topk_body,
        grid=(n // tmk,),
        in_specs=[pl.BlockSpec((tmk, 2 * PEER_HALF * PEER_HEADS), lambda i: (i, 0)),
                  pl.BlockSpec(sub_keys.shape, lambda i: (0, 0, 0))],
        out_specs=[out, out, out],
        out_shape=[jax.ShapeDtypeStruct((n, NSLOT), F32)] * 3,
        compiler_params=_cparams("parallel"),
        name="peer_topk",
    )(q, sub_keys)


def _gelu_exact(x):
    return 0.5 * x * (1.0 + lax.erf(x * (1.0 / math.sqrt(2.0))))


def _peer_dense_body(xn_ref, i1_ref, i2_ref, gt_ref, u_ref, v_ref, x_ref, lnf_ref, o_ref, g_ref, acc_ref,
                     *, tm, rows_per_step, final_norm):
    e = pl.program_id(1)

    @pl.when(e == 0)
    def _():
        acc_ref[...] = jnp.zeros_like(acc_ref)
        sub = lax.broadcasted_iota(jnp.int32, (N_KEYS, NSLOT), 0).astype(F32)

        def build(c, carry):
            i1 = i1_ref[pl.ds(c, 1), :]
            i2 = i2_ref[pl.ds(c, 1), :]
            gt = gt_ref[pl.ds(c, 1), :]
            a_t = jnp.where(sub == i1, 1.0, 0.0).astype(BF16)
            b_t = jnp.where(sub == i2, gt, 0.0).astype(BF16)
            g_ref[pl.ds(pl.multiple_of(c * G_PITCH, SUBLANES), N_KEYS), :] = lax.dot_general(
                a_t, b_t, _TB, preferred_element_type=F32)
            return carry

        lax.fori_loop(0, tm, build, 0)

    h = lax.dot_general(xn_ref[...], u_ref[...], _TB, preferred_element_type=F32)
    gate = jnp.concatenate(
        [g_ref[pl.ds(e * rows_per_step + j, tm, stride=G_PITCH), :] for j in range(rows_per_step)], axis=1)
    coef = (gate * _gelu_exact(h)).astype(BF16)
    acc_ref[...] += jnp.dot(coef, v_ref[...], preferred_element_type=F32)

    @pl.when(e == pl.num_programs(1) - 1)
    def _():
        out = x_ref[...] + acc_ref[...]
        if final_norm:
            out = _rms_rows(out, lnf_ref[...], EPS)
        o_ref[...] = out


def peer_dense(xn, i1, i2, gt, u, v, x, ln_f, *, tm, te, final_norm):
    n = xn.shape[0]
    ne = u.shape[0] // te
    rows_per_step = te // N_KEYS
    row = lambda w: pl.BlockSpec((tm, w), lambda i, e: (i, 0))
    return pl.pallas_call(
        functools.partial(_peer_dense_body, tm=tm, rows_per_step=rows_per_step, final_norm=final_norm),
        grid=(n // tm, ne),
        in_specs=[row(D_MODEL), row(NSLOT), row(NSLOT), row(NSLOT),
                  pl.BlockSpec((te, D_MODEL), lambda i, e: (e, 0)),
                  pl.BlockSpec((te, D_MODEL), lambda i, e: (e, 0)),
                  row(D_MODEL),
                  pl.BlockSpec((1, D_MODEL), lambda i, e: (0, 0))],
        out_specs=row(D_MODEL),
        out_shape=jax.ShapeDtypeStruct((n, D_MODEL), F32),
        scratch_shapes=[pltpu.VMEM((tm * G_PITCH, N_KEYS), F32), pltpu.VMEM((tm, D_MODEL), F32)],
        compiler_params=_cparams("parallel", "arbitrary"),
        name="peer_dense",
    )(xn, i1, i2, gt, u, v, x, ln_f.reshape(1, D_MODEL))


def _pad_cols(a, w):
    return jnp.pad(a, ((0, 0), (0, w - a.shape[1])))


def _pad_rows(a, h):
    return jnp.pad(a, ((0, h - a.shape[0]),) + ((0, 0),) * (a.ndim - 1))


def _swap_halves(a):
    half = a.shape[-1] // 2
    return jnp.concatenate([a[..., half:], a[..., :half]], axis=-1)


def _layer_weights(i, p):
    first = i == 0
    w_in = p["w_in_first"] if first else p["w_in_rest"][i - 1]
    mu = p["mu_first"] if first else p["mu_rest"][i - 1]
    mla_cols = Q_LORA + KV_LORA + ROPE_DIM
    gate0 = mla_cols
    rw0 = mla_cols + 2 * D_MODEL
    k_rope_w = w_in[:, Q_LORA + KV_LORA:mla_cols]
    w_mla = jnp.concatenate([w_in[:, :mla_cols], _swap_halves(k_rope_w)], axis=1)
    w_gate = w_in[:, gate0:rw0]

    def rw_layout(a):
        c = 3 * RW_DIM
        dwf, dwb = a[..., c:c + DECAY_LORA], a[..., c + DECAY_LORA:c + 2 * DECAY_LORA]
        c += 2 * DECAY_LORA
        daf, dab = a[..., c:c + AAA_LORA], a[..., c + AAA_LORA:c + 2 * AAA_LORA]
        c += 2 * AAA_LORA
        dg = a[..., c:c + GATE_LORA]
        c += GATE_LORA
        dv = a[..., c:]
        z = lambda w: jnp.zeros(a.shape[:-1] + (w,), a.dtype)
        return jnp.concatenate([a[..., :3 * RW_DIM], dwf, dwb, daf, dab, dg, z(2 * LANES - GATE_LORA),
                                dv, z(2 * LANES - dv.shape[-1])], axis=-1)

    w_rw = rw_layout(w_in[:, rw0:])
    mu_l = rw_layout(mu[None, :])
    zeros64 = jnp.zeros((DECAY_LORA, RW_DIM), F32)
    wq = p["w_uq"][i].reshape(Q_LORA, MLA_HEADS, NOPE_DIM + ROPE_DIM)
    wq = jnp.concatenate([wq, _swap_halves(wq[..., NOPE_DIM:])], axis=-1).reshape(Q_LORA, MLA_HEADS * QK_PAD)
    vec = jnp.stack([p["w0_f"][i], p["w0_b"][i], p["a0_f"][i], p["a0_b"][i], p["k_k"][i], p["k_a"][i],
                     p["v0_rest"][i - 1] if not first else jnp.zeros((RW_DIM,), F32),
                     jnp.zeros((RW_DIM,), F32)])
    lw = {
        "ln1": p["ln1"][i], "w_mla": w_mla.astype(BF16), "w_gate": w_gate.astype(BF16),
        "w_rw": w_rw.astype(BF16),
        "mu": mu_l, "q_norm": p["q_norm"][i], "wq": wq.astype(BF16), "kv_norm": p["kv_norm"][i],
        "wkv": p["w_ukv"][i].astype(BF16), "w_o_attn": p["w_o_attn"][i].astype(BF16),
        "vec": vec,
        "w2f": jnp.concatenate([p["w2_f"][i], zeros64]).astype(BF16),
        "w2b": jnp.concatenate([zeros64, p["w2_b"][i]]).astype(BF16),
        "a2f": jnp.concatenate([p["a2_f"][i], zeros64]).astype(BF16),
        "a2b": jnp.concatenate([zeros64, p["a2_b"][i]]).astype(BF16),
        "g2": _pad_rows(p["g2"][i], 2 * LANES).astype(BF16),
        "v2": None if first else _pad_rows(p["v2_rest"][i - 1], LANES).astype(BF16),
        "post_vec": jnp.stack([p["lnx_w"][i], p["lnx_b"][i], p["r_k"][i].reshape(RW_DIM)]
                              + [jnp.zeros((RW_DIM,), F32)] * 5),
        "w_o_rwkv": p["w_o_rwkv"][i].astype(BF16), "w_out": p["w_out"][i].astype(BF16),
        "ln2": p["ln2"][i], "w_pq": p["w_pq"][i].astype(BF16), "sub_keys": p["sub_keys"][i].astype(BF16),
        "peer_u": p["peer_u"][i].astype(BF16), "peer_v": p["peer_v"][i].astype(BF16),
    }
    return lw


def _rope_table(seq):
    inv = 1.0 / (ROPE_THETA ** (jnp.arange(0, ROPE_DIM, 2, dtype=F32) / ROPE_DIM))
    ang = jnp.arange(seq, dtype=F32)[:, None] * inv[None, :]
    c, s = jnp.cos(ang), jnp.sin(ang)
    return jnp.concatenate([c, c, -s, s], axis=1)


def _tiles(seq):
    return dict(tm=min(512, seq), tq=min(256, seq), tprep=min(256, seq), tpeer=min(256, seq))


def _trunk(x, layers, ln_f, bd):
    batch, seq, _ = x.shape
    n = batch * seq
    t = _tiles(seq)
    tm = t["tm"]
    ccss = _rope_table(seq)
    x = x.reshape(n, D_MODEL)
    v_first = None
    for li, lw in enumerate(layers):
        h_mla = rms_matmul(x, lw["ln1"], lw["w_mla"], tm=tm, tn=MLA_W)
        gates = rms_matmul(x, lw["ln1"], lw["w_gate"], tm=tm, tn=1024)
        h_rw = rms_matmul(x, lw["ln1"], lw["w_rw"], tm=tm, tn=1280)
        q = mla_q_proj(h_mla, lw["q_norm"], lw["wq"], ccss, seq=seq, tm=tm)
        k, v = mla_kv_proj(h_mla, lw["kv_norm"], lw["wkv"], ccss, seq=seq, tm=tm)
        o = attention(q, k, v, batch=batch, seq=seq, tq=t["tq"])
        r, vv, kk, ag2, lw2, k2, g = rwkv_prep(h_rw, lw["mu"], lw["vec"], lw["w2f"], lw["w2b"], lw["a2f"],
                                              lw["a2b"], lw["g2"], bd, lw["v2"], v_first,
                                              seq=seq, tm=t["tprep"])
        if v_first is None:
            v_first = vv
        y2 = wkv(r, vv, kk, lw2, k2, ag2, batch=batch, seq=seq)
        yg = rwkv_post(y2, r, k2, vv, g, lw["post_vec"], bd, tm=t["tprep"])
        m = merge_branches(o, yg, gates, lw["w_o_attn"], lw["w_o_rwkv"], tm=tm, tn=512)
        x = matmul_residual(m, lw["w_out"], x, tm=tm, tn=512)
        qp, xn = rms_matmul(x, lw["ln2"], lw["w_pq"], tm=tm, tn=512, emit_xn=True)
        i1, i2, gt = peer_topk(qp, lw["sub_keys"])
        x = peer_dense(xn, i1, i2, gt, lw["peer_u"], lw["peer_v"], x, ln_f, tm=t["tpeer"], te=2 * N_KEYS,
                       final_norm=li == len(layers) - 1)
    return x.reshape(batch, seq, D_MODEL)


def kernel(x_prompt, x_sample, ln1, w_in_first, mu_first, w_in_rest, mu_rest, q_norm, w_uq, kv_norm, w_ukv, w_o_attn, w0_f, w2_f, w0_b, w2_b, a0_f, a2_f, a0_b, a2_b, g2, k_k, k_a, r_k, lnx_w, lnx_b, v0_rest, v2_rest, w_o_rwkv, w_out, ln2, w_pq, sub_keys, peer_u, peer_v, ln_f):
    p = dict(ln1=ln1, w_in_first=w_in_first, mu_first=mu_first, w_in_rest=w_in_rest, mu_rest=mu_rest,
             q_norm=q_norm, w_uq=w_uq, kv_norm=kv_norm, w_ukv=w_ukv, w_o_attn=w_o_attn, w0_f=w0_f, w2_f=w2_f,
             w0_b=w0_b, w2_b=w2_b, a0_f=a0_f, a2_f=a2_f, a0_b=a0_b, a2_b=a2_b, g2=g2, k_k=k_k, k_a=k_a, r_k=r_k,
             lnx_w=lnx_w, lnx_b=lnx_b, v0_rest=v0_rest, v2_rest=v2_rest, w_o_rwkv=w_o_rwkv, w_out=w_out,
             ln2=ln2, w_pq=w_pq, sub_keys=sub_keys, peer_u=peer_u, peer_v=peer_v)
    depth = ln1.shape[0]
    layers = [_layer_weights(i, p) for i in range(depth)]
    lane = jnp.arange(LANES)
    bd = (lane[:, None] // RW_HEAD == lane[None, :] // RW_HEAD).astype(F32)
    return (_trunk(x_prompt, layers, ln_f, bd), _trunk(x_sample, layers, ln_f, bd))
```

```python
import functools
import math

import jax
import jax.numpy as jnp
from jax import lax
from jax.experimental import pallas as pl
from jax.experimental.pallas import tpu as pltpu

F32 = jnp.float32
BF16 = jnp.bfloat16
HIGHEST = lax.Precision.HIGHEST

LANES = 128
SUBLANES = 8
VMEM_LIMIT_BYTES = 56 * 1024 * 1024

D_MODEL = 2048
MLA_HEADS = 16
Q_LORA = 512
KV_LORA = 512
NOPE_DIM = 128
ROPE_DIM = 64
V_DIM = 128
ROPE_THETA = 10000.0
RW_HEAD = 64
RW_HEADS = 16
RW_DIM = RW_HEADS * RW_HEAD
DECAY_LORA = 64
AAA_LORA = 64
MV_LORA = 32
GATE_LORA = 160
LNX_EPS = 64e-5
PEER_HEADS = 8
N_KEYS = 128
PEER_HALF = 128
PEER_TOPK = 16
EPS = 1e-6

QK_PAD = 2 * LANES
MLA_W = Q_LORA + KV_LORA + LANES
LORA_W = 2 * DECAY_LORA + 2 * AAA_LORA + 2 * LANES + 2 * LANES
RW_W = 3 * RW_DIM + LORA_W
WKV_CHUNK = 64
G_PITCH = N_KEYS + SUBLANES
NSLOT = PEER_HEADS * PEER_TOPK

_TB = (((1,), (1,)), ((), ()))
_TA = (((0,), (0,)), ((), ()))


def _cparams(*sem):
    return pltpu.CompilerParams(dimension_semantics=sem, vmem_limit_bytes=VMEM_LIMIT_BYTES)


def _rms_rows(x, g, eps):
    ms = jnp.mean(x * x, axis=-1, keepdims=True)
    return x * lax.rsqrt(ms + eps) * g


def _rms_mm_body(x_ref, g_ref, w_ref, o_ref, *rest, emit_xn):
    xn_ref = rest[-1]

    @pl.when(pl.program_id(1) == 0)
    def _():
        xn = _rms_rows(x_ref[...].astype(F32), g_ref[...], EPS).astype(BF16)
        xn_ref[...] = xn
        if emit_xn:
            rest[0][...] = xn

    o_ref[...] = jnp.dot(xn_ref[...], w_ref[...], preferred_element_type=F32).astype(o_ref.dtype)


def rms_matmul(x, gain, w, *, tm, tn, out_dtype=F32, emit_xn=False):
    n, k = x.shape
    nc = w.shape[1]
    tn = min(tn, nc)
    assert n % tm == 0 and nc % tn == 0 and w.shape[0] == k
    out_shape = [jax.ShapeDtypeStruct((n, nc), out_dtype)]
    out_specs = [pl.BlockSpec((tm, tn), lambda i, j: (i, j))]
    if emit_xn:
        out_shape.append(jax.ShapeDtypeStruct((n, k), BF16))
        out_specs.append(pl.BlockSpec((tm, k), lambda i, j: (i, 0)))
    res = pl.pallas_call(
        functools.partial(_rms_mm_body, emit_xn=emit_xn),
        grid=(n // tm, nc // tn),
        in_specs=[pl.BlockSpec((tm, k), lambda i, j: (i, 0)),
                  pl.BlockSpec((1, k), lambda i, j: (0, 0)),
                  pl.BlockSpec((k, tn), lambda i, j: (0, j))],
        out_specs=out_specs,
        out_shape=out_shape,
        scratch_shapes=[pltpu.VMEM((tm, k), BF16)],
        compiler_params=_cparams("parallel", "arbitrary"),
        name="rms_matmul",
    )(x, gain.reshape(1, k), w)
    return res if emit_xn else res[0]


def _rope_lanes(y2, ccss):
    w = y2 * ccss
    return w + pltpu.roll(w, ROPE_DIM, axis=1)


def _qproj_body(c_ref, g_ref, w_ref, t_ref, q_ref, cn_ref, *, scale):
    @pl.when(pl.program_id(1) == 0)
    def _():
        cn_ref[...] = _rms_rows(c_ref[...].astype(F32), g_ref[...], EPS).astype(BF16)

    y = jnp.dot(cn_ref[...], w_ref[...], preferred_element_type=F32)
    q_ref[:, 0:LANES] = (y[:, 0:LANES] * scale).astype(q_ref.dtype)
    rot = _rope_lanes(y[:, LANES:], t_ref[...])
    lane = lax.broadcasted_iota(jnp.int32, rot.shape, 1)
    q_ref[:, LANES:] = jnp.where(lane < ROPE_DIM, rot * scale, 0.0).astype(q_ref.dtype)


def mla_q_proj(h_mla, q_norm, wq, ccss, *, seq, tm):
    n = h_mla.shape[0]
    nt = seq // tm
    scale = (NOPE_DIM + ROPE_DIM) ** -0.5
    return pl.pallas_call(
        functools.partial(_qproj_body, scale=scale),
        grid=(n // tm, MLA_HEADS),
        in_specs=[pl.BlockSpec((tm, Q_LORA), lambda i, h: (i, 0)),
                  pl.BlockSpec((1, Q_LORA), lambda i, h: (0, 0)),
                  pl.BlockSpec((Q_LORA, QK_PAD), lambda i, h: (0, h)),
                  pl.BlockSpec((tm, LANES), lambda i, h: (i % nt, 0))],
        out_specs=pl.BlockSpec((tm, QK_PAD), lambda i, h: (i, h)),
        out_shape=jax.ShapeDtypeStruct((n, MLA_HEADS * QK_PAD), BF16),
        scratch_shapes=[pltpu.VMEM((tm, Q_LORA), BF16)],
        compiler_params=_cparams("parallel", "arbitrary"),
        name="mla_q_proj",
    )(h_mla, q_norm.reshape(1, Q_LORA), wq, ccss)


def _kvproj_body(c_ref, kr_ref, g_ref, w_ref, t_ref, k_ref, v_ref, cn_ref, krr_ref):
    @pl.when(pl.program_id(1) == 0)
    def _():
        cn_ref[...] = _rms_rows(c_ref[...].astype(F32), g_ref[...], EPS).astype(BF16)
        krr_ref[...] = _rope_lanes(kr_ref[...].astype(F32), t_ref[...]).astype(BF16)

    y = jnp.dot(cn_ref[...], w_ref[...], preferred_element_type=F32)
    k_ref[:, 0:LANES] = y[:, 0:LANES].astype(k_ref.dtype)
    k_ref[:, LANES:] = krr_ref[...]
    v_ref[...] = y[:, LANES:].astype(v_ref.dtype)


def mla_kv_proj(h_mla, kv_norm, wkv, ccss, *, seq, tm):
    n = h_mla.shape[0]
    nt = seq // tm
    return pl.pallas_call(
        _kvproj_body,
        grid=(n // tm, MLA_HEADS),
        in_specs=[pl.BlockSpec((tm, KV_LORA), lambda i, h: (i, 1)),
                  pl.BlockSpec((tm, LANES), lambda i, h: (i, (Q_LORA + KV_LORA) // LANES)),
                  pl.BlockSpec((1, KV_LORA), lambda i, h: (0, 0)),
                  pl.BlockSpec((KV_LORA, NOPE_DIM + V_DIM), lambda i, h: (0, h)),
                  pl.BlockSpec((tm, LANES), lambda i, h: (i % nt, 0))],
        out_specs=[pl.BlockSpec((tm, QK_PAD), lambda i, h: (i, h)),
                   pl.BlockSpec((tm, V_DIM), lambda i, h: (i, h))],
        out_shape=[jax.ShapeDtypeStruct((n, MLA_HEADS * QK_PAD), BF16),
                   jax.ShapeDtypeStruct((n, MLA_HEADS * V_DIM), BF16)],
        scratch_shapes=[pltpu.VMEM((tm, KV_LORA), BF16), pltpu.VMEM((tm, LANES), BF16)],
        compiler_params=_cparams("parallel", "arbitrary"),
        name="mla_kv_proj",
    )(h_mla, h_mla, kv_norm.reshape(1, KV_LORA), wkv, ccss)


def _attn_body(q_ref, k_ref, v_ref, o_ref):
    s = lax.dot_general(q_ref[...], k_ref[...], _TB, preferred_element_type=F32)
    m = jnp.max(s, axis=-1, keepdims=True)
    p = jnp.exp(s - m)
    l = jnp.sum(p, axis=-1, keepdims=True)
    o = jnp.dot(p.astype(BF16), v_ref[...], preferred_element_type=F32)
    o_ref[...] = (o / l).astype(o_ref.dtype)


def attention(q, k, v, *, batch, seq, tq):
    n = q.shape[0]
    nq = seq // tq
    return pl.pallas_call(
        _attn_body,
        grid=(batch, MLA_HEADS, nq),
        in_specs=[pl.BlockSpec((tq, QK_PAD), lambda b, h, i: (b * nq + i, h)),
                  pl.BlockSpec((seq, QK_PAD), lambda b, h, i: (b, h)),
                  pl.BlockSpec((seq, V_DIM), lambda b, h, i: (b, h))],
        out_specs=pl.BlockSpec((tq, V_DIM), lambda b, h, i: (b * nq + i, h)),
        out_shape=jax.ShapeDtypeStruct((n, MLA_HEADS * V_DIM), BF16),
        compiler_params=_cparams("parallel", "parallel", "arbitrary"),
        name="mla_attention",
    )(q, k, v)


def _head_sums(x, bd):
    parts = []
    for gi in range(RW_DIM // LANES):
        parts.append(jnp.dot(x[:, gi * LANES:(gi + 1) * LANES], bd, precision=HIGHEST,
                             preferred_element_type=F32))
    return jnp.concatenate(parts, axis=1)


def _sigmoid(x):
    return 1.0 / (1.0 + jnp.exp(-x))


def _rw_prep_body(*refs, tiles_per_seq, tm, has_vfirst):
    (h_ref, hp_ref, hn_ref, mu_ref, vec_ref, w2f_ref, w2b_ref, a2f_ref, a2b_ref, g2_ref, bd_ref) = refs[:11]
    pos = 11
    if has_vfirst:
        v2_ref, vf_ref = refs[pos:pos + 2]
        pos += 2
    r_ref, v_ref, kk_ref, ag_ref, lw_ref, k2_ref, g_ref = refs[pos:pos + 7]

    ti = pl.program_id(0) % tiles_per_seq
    keep_prev = jnp.where(ti == 0, 0.0, 1.0)
    keep_next = jnp.where(ti == tiles_per_seq - 1, 0.0, 1.0)
    row = lax.broadcasted_iota(jnp.int32, (tm, 1), 0)

    def shifted(c0, c1):
        x = h_ref[:, c0:c1]
        prev = jnp.where(row == 0, hp_ref[SUBLANES - 1:SUBLANES, c0:c1] * keep_prev, pltpu.roll(x, 1, axis=0))
        nxt = jnp.where(row == tm - 1, hn_ref[0:1, c0:c1] * keep_next, pltpu.roll(x, tm - 1, axis=0))
        return x + (0.5 * (prev + nxt) - x) * mu_ref[:, c0:c1]

    vec = vec_ref[...]
    w0f, w0b, a0f, a0b, k_k, k_a, v0 = (vec[i:i + 1, :] for i in range(7))

    r = shifted(0, RW_DIM)
    k = shifted(RW_DIM, 2 * RW_DIM)
    v = shifted(2 * RW_DIM, 3 * RW_DIM)
    lo = shifted(3 * RW_DIM, RW_W)
    dw = jnp.tanh(lo[:, 0:LANES]).astype(BF16)
    da = lo[:, LANES:2 * LANES].astype(BF16)
    dg = _sigmoid(lo[:, 2 * LANES:4 * LANES]).astype(BF16)

    def decay(dw_half, w0, w2_ref):
        z = w0 + jnp.dot(dw_half, w2_ref[...], preferred_element_type=F32)
        nz = -z
        softplus = jnp.maximum(nz, 0.0) + jnp.log(1.0 + jnp.exp(-jnp.abs(nz)))
        return -jnp.exp(-softplus - 0.5)

    lw_ref[:, 0:RW_DIM] = decay(dw, w0f, w2f_ref)
    lw_ref[:, RW_DIM:] = decay(dw, w0b, w2b_ref)
    af = _sigmoid(a0f + jnp.dot(da, a2f_ref[...], preferred_element_type=F32))
    ab = _sigmoid(a0b + jnp.dot(da, a2b_ref[...], preferred_element_type=F32))
    ag_ref[:, 0:RW_DIM] = af
    ag_ref[:, RW_DIM:] = ab
    g_ref[...] = jnp.dot(dg, g2_ref[...], preferred_element_type=F32)

    if has_vfirst:
        dv = lo[:, 4 * LANES:5 * LANES].astype(BF16)
        mix = _sigmoid(v0 + jnp.dot(dv, v2_ref[...], preferred_element_type=F32))
        v = v + (vf_ref[...] - v) * mix
    v_ref[...] = v
    r_ref[...] = r

    kk = k * k_k
    ss = _head_sums(kk * kk, bd_ref[...])
    kk_ref[...] = kk * lax.rsqrt(jnp.maximum(ss, 1e-24))
    k2_ref[:, 0:RW_DIM] = k * (1.0 + (af - 1.0) * k_a)
    k2_ref[:, RW_DIM:] = k * (1.0 + (ab - 1.0) * k_a)


def rwkv_prep(h_rw, mu, vec, w2f, w2b, a2f, a2b, g2, bd, v2, v_first, *, seq, tm):
    n = h_rw.shape[0]
    tps = seq // tm
    nb8 = n // SUBLANES
    has_vfirst = v_first is not None
    full = lambda a: pl.BlockSpec(a.shape, lambda i: (0,) * a.ndim)
    in_specs = [pl.BlockSpec((tm, RW_W), lambda i: (i, 0)),
                pl.BlockSpec((SUBLANES, RW_W), lambda i: (jnp.maximum(i * (tm // SUBLANES) - 1, 0), 0)),
                pl.BlockSpec((SUBLANES, RW_W), lambda i: (jnp.minimum((i + 1) * (tm // SUBLANES), nb8 - 1), 0)),
                full(mu), full(vec), full(w2f), full(w2b), full(a2f), full(a2b), full(g2), full(bd)]
    args = [h_rw, h_rw, h_rw, mu, vec, w2f, w2b, a2f, a2b, g2, bd]
    if has_vfirst:
        in_specs += [full(v2), pl.BlockSpec((tm, RW_DIM), lambda i: (i, 0))]
        args += [v2, v_first]
    row = lambda w: pl.BlockSpec((tm, w), lambda i: (i, 0))
    widths = [RW_DIM, RW_DIM, RW_DIM, 2 * RW_DIM, 2 * RW_DIM, 2 * RW_DIM, RW_DIM]
    return pl.pallas_call(
        functools.partial(_rw_prep_body, tiles_per_seq=tps, tm=tm, has_vfirst=has_vfirst),
        grid=(n // tm,),
        in_specs=in_specs,
        out_specs=[row(w) for w in widths],
        out_shape=[jax.ShapeDtypeStruct((n, w), F32) for w in widths],
        compiler_params=_cparams("parallel"),
        name="rwkv_prep",
    )(*args)


def _wkv_body(r_ref, v_ref, kk_ref, lw_ref, k_ref, ag_ref, y_ref, st_ref):
    C = WKV_CHUNK
    rev = pl.program_id(1) == 1

    @pl.when(pl.program_id(2) == 0)
    def _():
        st_ref[...] = jnp.zeros_like(st_ref)

    ti = lax.broadcasted_iota(jnp.int32, (C, C), 0)
    si = lax.broadcasted_iota(jnp.int32, (C, C), 1)
    sgn = jnp.where(rev, -1, 1)
    tri = jnp.where((ti - si) * sgn >= 0, 1.0, 0.0)
    lw_all = lw_ref[...]
    cum_all = jnp.dot(tri, lw_all, precision=HIGHEST, preferred_element_type=F32)
    tot_all = jnp.sum(lw_all, axis=0, keepdims=True)

    lane = lax.broadcasted_iota(jnp.int32, (1, LANES), 1)
    hm0 = jnp.where(lane < RW_HEAD, 1.0, 0.0)
    hm1 = 1.0 - hm0
    stack = lambda x: jnp.concatenate([x * hm0, x * hm1], axis=0).astype(BF16)
    dup = lambda x: jnp.concatenate([x, x], axis=0).astype(BF16)
    ri = lax.broadcasted_iota(jnp.int32, (2 * C, 2 * C), 0)
    ci = lax.broadcasted_iota(jnp.int32, (2 * C, 2 * C), 1)
    same = (ri & C) == (ci & C)
    dts = ((ri & (C - 1)) - (ci & (C - 1))) * sgn
    strict = same & (dts > 0)
    incl = same & (dts >= 0)
    eye = jnp.where(ri == ci, 1.0, 0.0)
    valid = (lax.broadcasted_iota(jnp.int32, (2 * C, LANES), 0) & C) == (
        lax.broadcasted_iota(jnp.int32, (2 * C, LANES), 1) & RW_HEAD)

    pairs = range(RW_DIM // LANES)
    mm = functools.partial(jnp.dot, preferred_element_type=F32)
    mm_tb = lambda a, b: lax.dot_general(a, b, _TB, preferred_element_type=F32)
    mm_ta = lambda a, b: lax.dot_general(a, b, _TA, preferred_element_type=F32)
    xa, xr, vst, bws, kws, wtots, gms = [], [], [], [], [], [], []
    for p in pairs:
        ls = slice(p * LANES, (p + 1) * LANES)
        r, v, kk, k, ag = (x[:, ls] for x in (r_ref, v_ref, kk_ref, k_ref, ag_ref))
        lw, cum, tot = lw_all[:, ls], cum_all[:, ls], tot_all[:, ls]
        ei = jnp.exp(-cum)
        ew = jnp.exp(tot - cum)
        b = kk * ag
        xa.append(stack(-kk * jnp.exp(cum - lw)))
        xr.append(stack(r * jnp.exp(cum)))
        vst.append(stack(v))
        bws.append(stack(b * ew))
        kws.append(stack(k * ew))
        wtots.append(jnp.exp(tot))
        x2 = jnp.concatenate([xa[p], xr[p]], axis=0)
        y2 = jnp.concatenate([dup(b * ei), dup(k * ei)], axis=0)
        gms.append(mm_tb(x2, y2))
    l_ab = [jnp.where(strict, g[0:2 * C, 0:2 * C], 0.0) for g in gms]
    l_ak = [jnp.where(strict, g[0:2 * C, 2 * C:], 0.0).astype(BF16) for g in gms]
    m_rb = [jnp.where(incl, g[2 * C:, 0:2 * C], 0.0).astype(BF16) for g in gms]
    m_rk = [jnp.where(incl, g[2 * C:, 2 * C:], 0.0).astype(BF16) for g in gms]

    pinv = [eye + l for l in l_ab]
    lp = l_ab
    for _ in range(int(math.log2(C)) - 1):
        lpb = [l.astype(BF16) for l in lp]
        lp = [mm(l, l) for l in lpb]
        pinv = [pi + mm(pi.astype(BF16), l.astype(BF16)) for pi, l in zip(pinv, lp)]

    st = [st_ref[p] for p in pairs]
    stb = [s.astype(BF16) for s in st]
    pre = [mm_tb(xa[p], stb[p]) + mm(l_ak[p], vst[p]) for p in pairs]
    ustb = [jnp.where(valid, mm(pinv[p].astype(BF16), pre[p].astype(BF16)), 0.0).astype(BF16) for p in pairs]
    for p in pairs:
        yst = jnp.where(valid, mm_tb(xr[p], stb[p]) + mm(m_rb[p], ustb[p]) + mm(m_rk[p], vst[p]), 0.0)
        y_ref[:, p * LANES:(p + 1) * LANES] = yst[0:C] + yst[C:]
    for p in pairs:
        st_ref[p] = st[p] * wtots[p] + mm_ta(ustb[p], bws[p]) + mm_ta(vst[p], kws[p])


def wkv(r, v, kk, lw2, k2, ag2, *, batch, seq):
    n = r.shape[0]
    C = WKV_CHUNK
    nc = seq // C

    def rows(b, d, c):
        return b * nc + jnp.where(d == 1, nc - 1 - c, c)

    shared = pl.BlockSpec((C, RW_DIM), lambda b, d, c: (rows(b, d, c), 0))
    perdir = pl.BlockSpec((C, RW_DIM), lambda b, d, c: (rows(b, d, c), d))
    return pl.pallas_call(
        _wkv_body,
        grid=(batch, 2, nc),
        in_specs=[shared, shared, shared, perdir, perdir, perdir],
        out_specs=perdir,
        out_shape=jax.ShapeDtypeStruct((n, 2 * RW_DIM), F32),
        scratch_shapes=[pltpu.VMEM((RW_DIM // LANES, LANES, LANES), F32)],
        compiler_params=_cparams("parallel", "parallel", "arbitrary"),
        name="wkv_scan",
    )(r, v, kk, lw2, k2, ag2)


def _rw_post_body(y2_ref, r_ref, k2_ref, v_ref, g_ref, vec_ref, bd_ref, o_ref):
    vec = vec_ref[...]
    lnx_w, lnx_b, r_k = (vec[i:i + 1, :] for i in range(3))
    bd = bd_ref[...]
    y = y2_ref[:, 0:RW_DIM] + y2_ref[:, RW_DIM:]
    inv = 1.0 / RW_HEAD
    mean = _head_sums(y, bd) * inv
    d = y - mean
    var = _head_sums(d * d, bd) * inv
    yn = d * lax.rsqrt(var + LNX_EPS) * lnx_w + lnx_b
    ksum = k2_ref[:, 0:RW_DIM] + k2_ref[:, RW_DIM:]
    bonus = _head_sums(r_ref[...] * ksum * r_k, bd) * v_ref[...]
    o_ref[...] = ((yn + bonus) * g_ref[...]).astype(o_ref.dtype)


def rwkv_post(y2, r, k2, v, g, vec, bd, *, tm):
    n = r.shape[0]
    row = lambda w: pl.BlockSpec((tm, w), lambda i: (i, 0))
    full = lambda a: pl.BlockSpec(a.shape, lambda i: (0,) * a.ndim)
    return pl.pallas_call(
        _rw_post_body,
        grid=(n // tm,),
        in_specs=[row(2 * RW_DIM), row(RW_DIM), row(2 * RW_DIM), row(RW_DIM), row(RW_DIM), full(vec), full(bd)],
        out_specs=row(RW_DIM),
        out_shape=jax.ShapeDtypeStruct((n, RW_DIM), BF16),
        compiler_params=_cparams("parallel"),
        name="rwkv_post",
    )(y2, r, k2, v, g, vec, bd)


def _merge_body(o_ref, yg_ref, ga_ref, gr_ref, wa_ref, wr_ref, m_ref):
    attn = jnp.dot(o_ref[...], wa_ref[...], preferred_element_type=F32)
    rw = jnp.dot(yg_ref[...], wr_ref[...], preferred_element_type=F32)
    m_ref[...] = (_sigmoid(ga_ref[...].astype(F32)) * attn + _sigmoid(gr_ref[...].astype(F32)) * rw).astype(m_ref.dtype)


def merge_branches(o, yg, gates, wa, wr, *, tm, tn):
    n = o.shape[0]
    nj = D_MODEL // tn
    return pl.pallas_call(
        _merge_body,
        grid=(n // tm, nj),
        in_specs=[pl.BlockSpec((tm, MLA_HEADS * V_DIM), lambda i, j: (i, 0)),
                  pl.BlockSpec((tm, RW_DIM), lambda i, j: (i, 0)),
                  pl.BlockSpec((tm, tn), lambda i, j: (i, j)),
                  pl.BlockSpec((tm, tn), lambda i, j: (i, nj + j)),
                  pl.BlockSpec((MLA_HEADS * V_DIM, tn), lambda i, j: (0, j)),
                  pl.BlockSpec((RW_DIM, tn), lambda i, j: (0, j))],
        out_specs=pl.BlockSpec((tm, tn), lambda i, j: (i, j)),
        out_shape=jax.ShapeDtypeStruct((n, D_MODEL), BF16),
        compiler_params=_cparams("parallel", "parallel"),
        name="merge_branches",
    )(o, yg, gates, gates, wa, wr)


def _mm_res_body(a_ref, w_ref, x_ref, o_ref):
    o_ref[...] = x_ref[...] + jnp.dot(a_ref[...], w_ref[...], preferred_element_type=F32)


def matmul_residual(a, w, x, *, tm, tn):
    n, k = a.shape
    nc = w.shape[1]
    return pl.pallas_call(
        _mm_res_body,
        grid=(n // tm, nc // tn),
        in_specs=[pl.BlockSpec((tm, k), lambda i, j: (i, 0)),
                  pl.BlockSpec((k, tn), lambda i, j: (0, j)),
                  pl.BlockSpec((tm, tn), lambda i, j: (i, j))],
        out_specs=pl.BlockSpec((tm, tn), lambda i, j: (i, j)),
        out_shape=jax.ShapeDtypeStruct((n, nc), F32),
        compiler_params=_cparams("parallel", "parallel"),
        name="matmul_residual",
    )(a, w, x)


def _top16_rows(s, n_rows):
    idx = lax.broadcasted_iota(jnp.int32, s.shape, 0).astype(F32)
    vals, poss = [], []
    for _ in range(PEER_TOPK):
        m = jnp.max(s, axis=0, keepdims=True)
        pos = jnp.min(jnp.where(s == m, idx, float(n_rows)), axis=0, keepdims=True)
        vals.append(m)
        poss.append(pos)
        s = jnp.where(idx == pos, -jnp.inf, s)
    return jnp.concatenate(vals, axis=0), jnp.concatenate(poss, axis=0)


def _pick_rows(table, sel):
    out = jnp.zeros(sel.shape, table.dtype)
    for a in range(PEER_TOPK):
        out = jnp.where(sel == a, table[a:a + 1, :], out)
    return out


def _peer_topk_body(q_ref, sk_ref, i1_ref, i2_ref, gt_ref):
    i1s, i2s, gts = [], [], []
    for h in range(PEER_HEADS):
        sv, si = [], []
        for p in range(2):
            c0 = (2 * h + p) * PEER_HALF
            qb = q_ref[:, c0:c0 + PEER_HALF].astype(BF16)
            s = lax.dot_general(sk_ref[p], qb, _TB, preferred_element_type=F32)
            vals, poss = _top16_rows(s, N_KEYS)
            sv.append(vals)
            si.append(poss)
        cand = jnp.concatenate([sv[0][a:a + 1, :] + sv[1] for a in range(PEER_TOPK)], axis=0)
        top_s, top_j = _top16_rows(cand, PEER_TOPK * PEER_TOPK)
        ja = jnp.floor(top_j * (1.0 / PEER_TOPK))
        i1s.append(_pick_rows(si[0], ja))
        i2s.append(_pick_rows(si[1], top_j - ja * PEER_TOPK))
        e = jnp.exp(top_s - top_s[0:1, :])
        gts.append(e / jnp.sum(e, axis=0, keepdims=True))
    i1_ref[...] = jnp.concatenate(i1s, axis=0).T
    i2_ref[...] = jnp.concatenate(i2s, axis=0).T
    gt_ref[...] = jnp.concatenate(gts, axis=0).T


def peer_topk(q, sub_keys):
    n = q.shape[0]
    tmk = LANES
    out = pl.BlockSpec((tmk, NSLOT), lambda i: (i, 0))
    return pl.pallas_call(
        _peer_topk_body,
        grid=(n // tmk,),
        in_specs=[pl.BlockSpec((tmk, 2 * PEER_HALF * PEER_HEADS), lambda i: (i, 0)),
                  pl.BlockSpec(sub_keys.shape, lambda i: (0, 0, 0))],
        out_specs=[out, out, out],
        out_shape=[jax.ShapeDtypeStruct((n, NSLOT), F32)] * 3,
        compiler_params=_cparams("parallel"),
        name="peer_topk",
    )(q, sub_keys)


def _gelu_exact(x):
    return 0.5 * x * (1.0 + lax.erf(x * (1.0 / math.sqrt(2.0))))


_HI16 = 0xFFFF0000
G_UNROLL = SUBLANES
E_SUB = 2 * N_KEYS


def _bf16_bits_hi(x):
    u = pltpu.bitcast(x, jnp.uint32)
    return (u + jnp.uint32(0x7FFF) + ((u >> 16) & jnp.uint32(1))) & jnp.uint32(_HI16)


def _peer_dense_body(xn_ref, i1_ref, i2_ref, gt_ref, ut_ref, v_ref, x_ref, lnf_ref, o_ref, g_ref,
                     *, tm, te, final_norm):
    e = pl.program_id(1)
    half = tm // 2
    mm_tb = lambda a, b: lax.dot_general(a, b, _TB, preferred_element_type=F32)

    @pl.when(e == 0)
    def _():
        o_ref[...] = x_ref[...]
        sub = lax.broadcasted_iota(jnp.int32, (N_KEYS, NSLOT), 0).astype(F32)

        def build(it, carry):
            base = pl.multiple_of(it * G_UNROLL, SUBLANES)
            tiles = [(ref[pl.ds(base, G_UNROLL), :], ref[pl.ds(half + base, G_UNROLL), :])
                     for ref in (i1_ref, i2_ref, gt_ref)]
            a_ts, b_ts = [], []
            for j in range(G_UNROLL):
                for t in range(2):
                    i1, i2, gt = (tiles[q][t][j:j + 1, :] for q in range(3))
                    a_ts.append(jnp.where(sub == i1, 1.0, 0.0).astype(BF16))
                    b_ts.append(jnp.where(sub == i2, gt, 0.0).astype(BF16))
            gs = [mm_tb(a, b) for a, b in zip(a_ts, b_ts)]
            for j in range(G_UNROLL):
                word = _bf16_bits_hi(gs[2 * j]) | (_bf16_bits_hi(gs[2 * j + 1]) >> 16)
                g_ref[pl.ds(pl.multiple_of((base + j) * G_PITCH, SUBLANES), N_KEYS), :] = word
            return carry

        lax.fori_loop(0, half // G_UNROLL, build, 0)

    xn = xn_ref[...]
    nsub = te // E_SUB
    hs = [jnp.dot(xn, ut_ref[:, j * E_SUB:(j + 1) * E_SUB], preferred_element_type=F32) for j in range(nsub)]
    coefs = []
    for j in range(nsub):
        row0 = e * (te // N_KEYS) + j * (E_SUB // N_KEYS)
        words = [g_ref[pl.ds(row0 + r, half, stride=G_PITCH), :] for r in range(E_SUB // N_KEYS)]
        top = jnp.concatenate([pltpu.bitcast(w & jnp.uint32(_HI16), F32) for w in words], axis=1)
        bot = jnp.concatenate([pltpu.bitcast(w << 16, F32) for w in words], axis=1)
        gate = jnp.concatenate([top, bot], axis=0)
        coefs.append((gate * _gelu_exact(hs[j])).astype(BF16))
    o_ref[...] += jnp.dot(jnp.concatenate(coefs, axis=1), v_ref[...], preferred_element_type=F32)

    if final_norm:
        @pl.when(e == pl.num_programs(1) - 1)
        def _():
            o_ref[...] = _rms_rows(o_ref[...], lnf_ref[...], EPS)


def peer_dense(xn, i1, i2, gt, ut, v, x, ln_f, *, tm, te, final_norm):
    n = xn.shape[0]
    ne = v.shape[0] // te
    assert tm % (2 * G_UNROLL) == 0 and te % E_SUB == 0
    row = lambda w, **kw: pl.BlockSpec((tm, w), lambda i, e: (i, 0), **kw)
    once = dict(pipeline_mode=pl.Buffered(1))
    return pl.pallas_call(
        functools.partial(_peer_dense_body, tm=tm, te=te, final_norm=final_norm),
        grid=(n // tm, ne),
        in_specs=[row(D_MODEL, **once), row(NSLOT), row(NSLOT), row(NSLOT),
                  pl.BlockSpec((D_MODEL, te), lambda i, e: (0, e)),
                  pl.BlockSpec((te, D_MODEL), lambda i, e: (e, 0)),
                  row(D_MODEL, **once),
                  pl.BlockSpec((1, D_MODEL), lambda i, e: (0, 0))],
        out_specs=row(D_MODEL),
        out_shape=jax.ShapeDtypeStruct((n, D_MODEL), F32),
        scratch_shapes=[pltpu.VMEM((tm // 2 * G_PITCH, N_KEYS), jnp.uint32)],
        compiler_params=_cparams("parallel", "arbitrary"),
        name="peer_dense",
    )(xn, i1, i2, gt, ut, v, x, ln_f.reshape(1, D_MODEL))


def _pad_cols(a, w):
    return jnp.pad(a, ((0, 0), (0, w - a.shape[1])))


def _pad_rows(a, h):
    return jnp.pad(a, ((0, h - a.shape[0]),) + ((0, 0),) * (a.ndim - 1))


def _swap_halves(a):
    half = a.shape[-1] // 2
    return jnp.concatenate([a[..., half:], a[..., :half]], axis=-1)


def _layer_weights(i, p):
    first = i == 0
    w_in = p["w_in_first"] if first else p["w_in_rest"][i - 1]
    mu = p["mu_first"] if first else p["mu_rest"][i - 1]
    mla_cols = Q_LORA + KV_LORA + ROPE_DIM
    gate0 = mla_cols
    rw0 = mla_cols + 2 * D_MODEL
    k_rope_w = w_in[:, Q_LORA + KV_LORA:mla_cols]
    w_mla = jnp.concatenate([w_in[:, :mla_cols], _swap_halves(k_rope_w)], axis=1)
    w_gate = w_in[:, gate0:rw0]

    def rw_layout(a):
        c = 3 * RW_DIM
        dwf, dwb = a[..., c:c + DECAY_LORA], a[..., c + DECAY_LORA:c + 2 * DECAY_LORA]
        c += 2 * DECAY_LORA
        daf, dab = a[..., c:c + AAA_LORA], a[..., c + AAA_LORA:c + 2 * AAA_LORA]
        c += 2 * AAA_LORA
        dg = a[..., c:c + GATE_LORA]
        c += GATE_LORA
        dv = a[..., c:]
        z = lambda w: jnp.zeros(a.shape[:-1] + (w,), a.dtype)
        return jnp.concatenate([a[..., :3 * RW_DIM], dwf, dwb, daf, dab, dg, z(2 * LANES - GATE_LORA),
                                dv, z(2 * LANES - dv.shape[-1])], axis=-1)

    w_rw = rw_layout(w_in[:, rw0:])
    mu_l = rw_layout(mu[None, :])
    zeros64 = jnp.zeros((DECAY_LORA, RW_DIM), F32)
    wq = p["w_uq"][i].reshape(Q_LORA, MLA_HEADS, NOPE_DIM + ROPE_DIM)
    wq = jnp.concatenate([wq, _swap_halves(wq[..., NOPE_DIM:])], axis=-1).reshape(Q_LORA, MLA_HEADS * QK_PAD)
    vec = jnp.stack([p["w0_f"][i], p["w0_b"][i], p["a0_f"][i], p["a0_b"][i], p["k_k"][i], p["k_a"][i],
                     p["v0_rest"][i - 1] if not first else jnp.zeros((RW_DIM,), F32),
                     jnp.zeros((RW_DIM,), F32)])
    lw = {
        "ln1": p["ln1"][i], "w_mla": w_mla.astype(BF16), "w_gate": w_gate.astype(BF16),
        "w_rw": w_rw.astype(BF16),
        "mu": mu_l, "q_norm": p["q_norm"][i], "wq": wq.astype(BF16), "kv_norm": p["kv_norm"][i],
        "wkv": p["w_ukv"][i].astype(BF16), "w_o_attn": p["w_o_attn"][i].astype(BF16),
        "vec": vec,
        "w2f": jnp.concatenate([p["w2_f"][i], zeros64]).astype(BF16),
        "w2b": jnp.concatenate([zeros64, p["w2_b"][i]]).astype(BF16),
        "a2f": jnp.concatenate([p["a2_f"][i], zeros64]).astype(BF16),
        "a2b": jnp.concatenate([zeros64, p["a2_b"][i]]).astype(BF16),
        "g2": _pad_rows(p["g2"][i], 2 * LANES).astype(BF16),
        "v2": None if first else _pad_rows(p["v2_rest"][i - 1], LANES).astype(BF16),
        "post_vec": jnp.stack([p["lnx_w"][i], p["lnx_b"][i], p["r_k"][i].reshape(RW_DIM)]
                              + [jnp.zeros((RW_DIM,), F32)] * 5),
        "w_o_rwkv": p["w_o_rwkv"][i].astype(BF16), "w_out": p["w_out"][i].astype(BF16),
        "ln2": p["ln2"][i], "w_pq": p["w_pq"][i].astype(BF16), "sub_keys": p["sub_keys"][i].astype(BF16),
        "peer_ut": p["peer_u"][i].astype(BF16).T, "peer_v": p["peer_v"][i].astype(BF16),
    }
    return lw


def _rope_table(seq):
    inv = 1.0 / (ROPE_THETA ** (jnp.arange(0, ROPE_DIM, 2, dtype=F32) / ROPE_DIM))
    ang = jnp.arange(seq, dtype=F32)[:, None] * inv[None, :]
    c, s = jnp.cos(ang), jnp.sin(ang)
    return jnp.concatenate([c, c, -s, s], axis=1)


def _tiles(seq):
    return dict(tm=min(512, seq), tq=min(256, seq), tprep=min(256, seq), tpeer=min(512, seq))


def _trunk(x, layers, ln_f, bd):
    batch, seq, _ = x.shape
    n = batch * seq
    t = _tiles(seq)
    tm = t["tm"]
    ccss = _rope_table(seq)
    x = x.reshape(n, D_MODEL)
    v_first = None
    for li, lw in enumerate(layers):
        h_mla = rms_matmul(x, lw["ln1"], lw["w_mla"], tm=tm, tn=MLA_W)
        gates = rms_matmul(x, lw["ln1"], lw["w_gate"], tm=tm, tn=1024)
        h_rw = rms_matmul(x, lw["ln1"], lw["w_rw"], tm=tm, tn=1280)
        q = mla_q_proj(h_mla, lw["q_norm"], lw["wq"], ccss, seq=seq, tm=tm)
        k, v = mla_kv_proj(h_mla, lw["kv_norm"], lw["wkv"], ccss, seq=seq, tm=tm)
        o = attention(q, k, v, batch=batch, seq=seq, tq=t["tq"])
        r, vv, kk, ag2, lw2, k2, g = rwkv_prep(h_rw, lw["mu"], lw["vec"], lw["w2f"], lw["w2b"], lw["a2f"],
                                              lw["a2b"], lw["g2"], bd, lw["v2"], v_first,
                                              seq=seq, tm=t["tprep"])
        if v_first is None:
            v_first = vv
        y2 = wkv(r, vv, kk, lw2, k2, ag2, batch=batch, seq=seq)
        yg = rwkv_post(y2, r, k2, vv, g, lw["post_vec"], bd, tm=t["tprep"])
        m = merge_branches(o, yg, gates, lw["w_o_attn"], lw["w_o_rwkv"], tm=tm, tn=512)
        x = matmul_residual(m, lw["w_out"], x, tm=tm, tn=512)
        qp, xn = rms_matmul(x, lw["ln2"], lw["w_pq"], tm=tm, tn=512, emit_xn=True)
        i1, i2, gt = peer_topk(qp, lw["sub_keys"])
        x = peer_dense(xn, i1, i2, gt, lw["peer_ut"], lw["peer_v"], x, ln_f, tm=t["tpeer"], te=2 * E_SUB,
                       final_norm=li == len(layers) - 1)
    return x.reshape(batch, seq, D_MODEL)


def kernel(x_prompt, x_sample, ln1, w_in_first, mu_first, w_in_rest, mu_rest, q_norm, w_uq, kv_norm, w_ukv, w_o_attn, w0_f, w2_f, w0_b, w2_b, a0_f, a2_f, a0_b, a2_b, g2, k_k, k_a, r_k, lnx_w, lnx_b, v0_rest, v2_rest, w_o_rwkv, w_out, ln2, w_pq, sub_keys, peer_u, peer_v, ln_f):
    p = dict(ln1=ln1, w_in_first=w_in_first, mu_first=mu_first, w_in_rest=w_in_rest, mu_rest=mu_rest,
             q_norm=q_norm, w_uq=w_uq, kv_norm=kv_norm, w_ukv=w_ukv, w_o_attn=w_o_attn, w0_f=w0_f, w2_f=w2_f,
             w0_b=w0_b, w2_b=w2_b, a0_f=a0_f, a2_f=a2_f, a0_b=a0_b, a2_b=a2_b, g2=g2, k_k=k_k, k_a=k_a, r_k=r_k,
             lnx_w=lnx_w, lnx_b=lnx_b, v0_rest=v0_rest, v2_rest=v2_rest, w_o_rwkv=w_o_rwkv, w_out=w_out,
             ln2=ln2, w_pq=w_pq, sub_keys=sub_keys, peer_u=peer_u, peer_v=peer_v)
    depth = ln1.shape[0]
    layers = [_layer_weights(i, p) for i in range(depth)]
    lane = jnp.arange(LANES)
    bd = (lane[:, None] // RW_HEAD == lane[None, :] // RW_HEAD).astype(F32)
    return (_trunk(x_prompt, layers, ln_f, bd), _trunk(x_sample, layers, ln_f, bd))
```

```python
import functools
import math

import jax
import jax.numpy as jnp
from jax import lax
from jax.experimental import pallas as pl
from jax.experimental.pallas import tpu as pltpu

F32 = jnp.float32
BF16 = jnp.bfloat16
HIGHEST = lax.Precision.HIGHEST

LANES = 128
SUBLANES = 8
VMEM_LIMIT_BYTES = 56 * 1024 * 1024

D_MODEL = 2048
MLA_HEADS = 16
Q_LORA = 512
KV_LORA = 512
NOPE_DIM = 128
ROPE_DIM = 64
V_DIM = 128
ROPE_THETA = 10000.0
RW_HEAD = 64
RW_HEADS = 16
RW_DIM = RW_HEADS * RW_HEAD
DECAY_LORA = 64
AAA_LORA = 64
MV_LORA = 32
GATE_LORA = 160
LNX_EPS = 64e-5
PEER_HEADS = 8
N_KEYS = 128
PEER_HALF = 128
PEER_TOPK = 16
EPS = 1e-6

QK_PAD = 2 * LANES
MLA_W = Q_LORA + KV_LORA + LANES
LORA_W = 2 * DECAY_LORA + 2 * AAA_LORA + 2 * LANES + 2 * LANES
RW_W = 3 * RW_DIM + LORA_W
PROJ_HEADS = 4
ATTN_SUB = 256
WKV_CHUNK = 64
G_PITCH = N_KEYS + SUBLANES
NSLOT = PEER_HEADS * PEER_TOPK

_TB = (((1,), (1,)), ((), ()))
_TA = (((0,), (0,)), ((), ()))


def _cparams(*sem):
    return pltpu.CompilerParams(dimension_semantics=sem, vmem_limit_bytes=VMEM_LIMIT_BYTES)


def _rms_rows(x, g, eps):
    ms = jnp.mean(x * x, axis=-1, keepdims=True)
    return x * lax.rsqrt(ms + eps) * g


def _rms_norm_body(x_ref, g_ref, o_ref):
    o_ref[...] = _rms_rows(x_ref[...].astype(F32), g_ref[...], EPS).astype(o_ref.dtype)


def rms_norm(x, gain, *, tm):
    n, k = x.shape
    return pl.pallas_call(
        _rms_norm_body,
        grid=(n // tm,),
        in_specs=[pl.BlockSpec((tm, k), lambda i: (i, 0)), pl.BlockSpec((1, k), lambda i: (0, 0))],
        out_specs=pl.BlockSpec((tm, k), lambda i: (i, 0)),
        out_shape=jax.ShapeDtypeStruct((n, k), BF16),
        compiler_params=_cparams("parallel"),
        name="rms_norm",
    )(x, gain.reshape(1, k))


def _mm_body(a_ref, w_ref, o_ref):
    o_ref[...] = jnp.dot(a_ref[...], w_ref[...], preferred_element_type=F32).astype(o_ref.dtype)


def matmul(a, w, *, tm, tn, out_dtype):
    n, k = a.shape
    nc = w.shape[1]
    tn = min(tn, nc)
    assert n % tm == 0 and nc % tn == 0 and w.shape[0] == k
    return pl.pallas_call(
        _mm_body,
        grid=(n // tm, nc // tn),
        in_specs=[pl.BlockSpec((tm, k), lambda i, j: (i, 0)),
                  pl.BlockSpec((k, tn), lambda i, j: (0, j))],
        out_specs=pl.BlockSpec((tm, tn), lambda i, j: (i, j)),
        out_shape=jax.ShapeDtypeStruct((n, nc), out_dtype),
        compiler_params=_cparams("parallel", "parallel"),
        name="matmul",
    )(a, w)


def _rope_lanes(y2, ccss):
    w = y2 * ccss
    return w + pltpu.roll(w, ROPE_DIM, axis=1)


def _qproj_body(c_ref, g_ref, w_ref, t_ref, q_ref, cn_ref, *, scale):
    @pl.when(pl.program_id(1) == 0)
    def _():
        cn_ref[...] = _rms_rows(c_ref[...].astype(F32), g_ref[...], EPS).astype(BF16)

    y = jnp.dot(cn_ref[...], w_ref[...], preferred_element_type=F32)
    tab = t_ref[...]
    lane = lax.broadcasted_iota(jnp.int32, tab.shape, 1)
    for hh in range(PROJ_HEADS):
        c0 = hh * QK_PAD
        q_ref[:, c0:c0 + LANES] = (y[:, c0:c0 + LANES] * scale).astype(q_ref.dtype)
        rot = _rope_lanes(y[:, c0 + LANES:c0 + QK_PAD], tab)
        q_ref[:, c0 + LANES:c0 + QK_PAD] = jnp.where(lane < ROPE_DIM, rot * scale, 0.0).astype(q_ref.dtype)


def mla_q_proj(h_mla, q_norm, wq, ccss, *, seq, tm):
    n = h_mla.shape[0]
    nt = seq // tm
    scale = (NOPE_DIM + ROPE_DIM) ** -0.5
    return pl.pallas_call(
        functools.partial(_qproj_body, scale=scale),
        grid=(n // tm, MLA_HEADS // PROJ_HEADS),
        in_specs=[pl.BlockSpec((tm, Q_LORA), lambda i, h: (i, 0)),
                  pl.BlockSpec((1, Q_LORA), lambda i, h: (0, 0)),
                  pl.BlockSpec((Q_LORA, PROJ_HEADS * QK_PAD), lambda i, h: (0, h)),
                  pl.BlockSpec((tm, LANES), lambda i, h: (i % nt, 0))],
        out_specs=pl.BlockSpec((tm, PROJ_HEADS * QK_PAD), lambda i, h: (i, h)),
        out_shape=jax.ShapeDtypeStruct((n, MLA_HEADS * QK_PAD), BF16),
        scratch_shapes=[pltpu.VMEM((tm, Q_LORA), BF16)],
        compiler_params=_cparams("parallel", "arbitrary"),
        name="mla_q_proj",
    )(h_mla, q_norm.reshape(1, Q_LORA), wq, ccss)


def _kvproj_body(c_ref, kr_ref, g_ref, w_ref, t_ref, k_ref, v_ref, cn_ref, krr_ref):
    @pl.when(pl.program_id(1) == 0)
    def _():
        cn_ref[...] = _rms_rows(c_ref[...].astype(F32), g_ref[...], EPS).astype(BF16)
        krr_ref[...] = _rope_lanes(kr_ref[...].astype(F32), t_ref[...]).astype(BF16)

    y = jnp.dot(cn_ref[...], w_ref[...], preferred_element_type=F32)
    for hh in range(PROJ_HEADS):
        c0 = hh * (NOPE_DIM + V_DIM)
        k_ref[:, hh * QK_PAD:hh * QK_PAD + LANES] = y[:, c0:c0 + NOPE_DIM].astype(k_ref.dtype)
        k_ref[:, hh * QK_PAD + LANES:(hh + 1) * QK_PAD] = krr_ref[...]
        v_ref[:, hh * V_DIM:(hh + 1) * V_DIM] = y[:, c0 + NOPE_DIM:c0 + NOPE_DIM + V_DIM].astype(v_ref.dtype)


def mla_kv_proj(h_mla, kv_norm, wkv, ccss, *, seq, tm):
    n = h_mla.shape[0]
    nt = seq // tm
    return pl.pallas_call(
        _kvproj_body,
        grid=(n // tm, MLA_HEADS // PROJ_HEADS),
        in_specs=[pl.BlockSpec((tm, KV_LORA), lambda i, h: (i, 1)),
                  pl.BlockSpec((tm, LANES), lambda i, h: (i, (Q_LORA + KV_LORA) // LANES)),
                  pl.BlockSpec((1, KV_LORA), lambda i, h: (0, 0)),
                  pl.BlockSpec((KV_LORA, PROJ_HEADS * (NOPE_DIM + V_DIM)), lambda i, h: (0, h)),
                  pl.BlockSpec((tm, LANES), lambda i, h: (i % nt, 0))],
        out_specs=[pl.BlockSpec((tm, PROJ_HEADS * QK_PAD), lambda i, h: (i, h)),
                   pl.BlockSpec((tm, PROJ_HEADS * V_DIM), lambda i, h: (i, h))],
        out_shape=[jax.ShapeDtypeStruct((n, MLA_HEADS * QK_PAD), BF16),
                   jax.ShapeDtypeStruct((n, MLA_HEADS * V_DIM), BF16)],
        scratch_shapes=[pltpu.VMEM((tm, KV_LORA), BF16), pltpu.VMEM((tm, LANES), BF16)],
        compiler_params=_cparams("parallel", "arbitrary"),
        name="mla_kv_proj",
    )(h_mla, h_mla, kv_norm.reshape(1, KV_LORA), wkv, ccss)


def _attn_body(q_ref, k_ref, v_ref, o_ref, *, nsub):
    ts = q_ref.shape[0] // nsub
    k = k_ref[...]
    v = v_ref[...]

    def scores(j):
        return lax.dot_general(q_ref[j * ts:(j + 1) * ts, :], k, _TB, preferred_element_type=F32)

    def finish(s, j):
        m = jnp.max(s, axis=-1, keepdims=True)
        p = jnp.exp(s - m)
        l = jnp.sum(p, axis=-1, keepdims=True)
        o = jnp.dot(p.astype(BF16), v, preferred_element_type=F32)
        o_ref[j * ts:(j + 1) * ts, :] = (o / l).astype(o_ref.dtype)

    s_prev = scores(0)
    for j in range(1, nsub):
        s_next = scores(j)
        finish(s_prev, j - 1)
        s_prev = s_next
    finish(s_prev, nsub - 1)


def attention(q, k, v, *, batch, seq, tq):
    n = q.shape[0]
    nq = seq // tq
    return pl.pallas_call(
        functools.partial(_attn_body, nsub=tq // ATTN_SUB),
        grid=(batch, MLA_HEADS, nq),
        in_specs=[pl.BlockSpec((tq, QK_PAD), lambda b, h, i: (b * nq + i, h)),
                  pl.BlockSpec((seq, QK_PAD), lambda b, h, i: (b, h)),
                  pl.BlockSpec((seq, V_DIM), lambda b, h, i: (b, h))],
        out_specs=pl.BlockSpec((tq, V_DIM), lambda b, h, i: (b * nq + i, h)),
        out_shape=jax.ShapeDtypeStruct((n, MLA_HEADS * V_DIM), BF16),
        compiler_params=_cparams("parallel", "parallel", "arbitrary"),
        name="mla_attention",
    )(q, k, v)


def _head_sums(x, bd):
    parts = []
    for gi in range(RW_DIM // LANES):
        parts.append(jnp.dot(x[:, gi * LANES:(gi + 1) * LANES], bd, precision=HIGHEST,
                             preferred_element_type=F32))
    return jnp.concatenate(parts, axis=1)


def _sigmoid(x):
    return 1.0 / (1.0 + jnp.exp(-x))


def _rw_prep_body(*refs, tiles_per_seq, tm, has_vfirst):
    (h_ref, hp_ref, hn_ref, mu_ref, vec_ref, w2f_ref, w2b_ref, a2f_ref, a2b_ref, g2_ref, bd_ref) = refs[:11]
    pos = 11
    if has_vfirst:
        v2_ref, vf_ref = refs[pos:pos + 2]
        pos += 2
    r_ref, v_ref, kk_ref, ag_ref, lw_ref, k2_ref, g_ref = refs[pos:pos + 7]

    ti = pl.program_id(0) % tiles_per_seq
    keep_prev = jnp.where(ti == 0, 0.0, 1.0)
    keep_next = jnp.where(ti == tiles_per_seq - 1, 0.0, 1.0)
    row = lax.broadcasted_iota(jnp.int32, (tm, 1), 0)

    def shifted(c0, c1):
        x = h_ref[:, c0:c1]
        prev = jnp.where(row == 0, hp_ref[SUBLANES - 1:SUBLANES, c0:c1] * keep_prev, pltpu.roll(x, 1, axis=0))
        nxt = jnp.where(row == tm - 1, hn_ref[0:1, c0:c1] * keep_next, pltpu.roll(x, tm - 1, axis=0))
        return x + (0.5 * (prev + nxt) - x) * mu_ref[:, c0:c1]

    vec = vec_ref[...]
    w0f, w0b, a0f, a0b, k_k, k_a, v0 = (vec[i:i + 1, :] for i in range(7))

    r = shifted(0, RW_DIM)
    k = shifted(RW_DIM, 2 * RW_DIM)
    v = shifted(2 * RW_DIM, 3 * RW_DIM)
    lo = shifted(3 * RW_DIM, RW_W)
    dw = jnp.tanh(lo[:, 0:LANES]).astype(BF16)
    da = lo[:, LANES:2 * LANES].astype(BF16)
    dg = _sigmoid(lo[:, 2 * LANES:4 * LANES]).astype(BF16)

    def decay(dw_half, w0, w2_ref):
        z = w0 + jnp.dot(dw_half, w2_ref[...], preferred_element_type=F32)
        nz = -z
        softplus = jnp.maximum(nz, 0.0) + jnp.log(1.0 + jnp.exp(-jnp.abs(nz)))
        return -jnp.exp(-softplus - 0.5)

    lw_ref[:, 0:RW_DIM] = decay(dw, w0f, w2f_ref)
    lw_ref[:, RW_DIM:] = decay(dw, w0b, w2b_ref)
    af = _sigmoid(a0f + jnp.dot(da, a2f_ref[...], preferred_element_type=F32))
    ab = _sigmoid(a0b + jnp.dot(da, a2b_ref[...], preferred_element_type=F32))
    ag_ref[:, 0:RW_DIM] = af
    ag_ref[:, RW_DIM:] = ab
    g_ref[...] = jnp.dot(dg, g2_ref[...], preferred_element_type=F32)

    if has_vfirst:
        dv = lo[:, 4 * LANES:5 * LANES].astype(BF16)
        mix = _sigmoid(v0 + jnp.dot(dv, v2_ref[...], preferred_element_type=F32))
        v = v + (vf_ref[...] - v) * mix
    v_ref[...] = v
    r_ref[...] = r

    kk = k * k_k
    ss = _head_sums(kk * kk, bd_ref[...])
    kk_ref[...] = kk * lax.rsqrt(jnp.maximum(ss, 1e-24))
    k2_ref[:, 0:RW_DIM] = k * (1.0 + (af - 1.0) * k_a)
    k2_ref[:, RW_DIM:] = k * (1.0 + (ab - 1.0) * k_a)


def rwkv_prep(h_rw, mu, vec, w2f, w2b, a2f, a2b, g2, bd, v2, v_first, *, seq, tm):
    n = h_rw.shape[0]
    tps = seq // tm
    nb8 = n // SUBLANES
    has_vfirst = v_first is not None
    full = lambda a: pl.BlockSpec(a.shape, lambda i: (0,) * a.ndim)
    in_specs = [pl.BlockSpec((tm, RW_W), lambda i: (i, 0)),
                pl.BlockSpec((SUBLANES, RW_W), lambda i: (jnp.maximum(i * (tm // SUBLANES) - 1, 0), 0)),
                pl.BlockSpec((SUBLANES, RW_W), lambda i: (jnp.minimum((i + 1) * (tm // SUBLANES), nb8 - 1), 0)),
                full(mu), full(vec), full(w2f), full(w2b), full(a2f), full(a2b), full(g2), full(bd)]
    args = [h_rw, h_rw, h_rw, mu, vec, w2f, w2b, a2f, a2b, g2, bd]
    if has_vfirst:
        in_specs += [full(v2), pl.BlockSpec((tm, RW_DIM), lambda i: (i, 0))]
        args += [v2, v_first]
    row = lambda w: pl.BlockSpec((tm, w), lambda i: (i, 0))
    widths = [RW_DIM, RW_DIM, RW_DIM, 2 * RW_DIM, 2 * RW_DIM, 2 * RW_DIM, RW_DIM]
    return pl.pallas_call(
        functools.partial(_rw_prep_body, tiles_per_seq=tps, tm=tm, has_vfirst=has_vfirst),
        grid=(n // tm,),
        in_specs=in_specs,
        out_specs=[row(w) for w in widths],
        out_shape=[jax.ShapeDtypeStruct((n, w), F32) for w in widths],
        compiler_params=_cparams("parallel"),
        name="rwkv_prep",
    )(*args)


def _wkv_body(r_ref, v_ref, kk_ref, lw_ref, k_ref, ag_ref, y_ref, st_ref):
    C = WKV_CHUNK
    rev = pl.program_id(1) == 1

    @pl.when(pl.program_id(2) == 0)
    def _():
        st_ref[...] = jnp.zeros_like(st_ref)

    ti = lax.broadcasted_iota(jnp.int32, (C, C), 0)
    si = lax.broadcasted_iota(jnp.int32, (C, C), 1)
    sgn = jnp.where(rev, -1, 1)
    tri = jnp.where((ti - si) * sgn >= 0, 1.0, 0.0)
    lw_all = lw_ref[...]
    cum_all = jnp.dot(tri, lw_all, precision=HIGHEST, preferred_element_type=F32)
    tot_all = jnp.sum(lw_all, axis=0, keepdims=True)

    lane = lax.broadcasted_iota(jnp.int32, (1, LANES), 1)
    hm0 = jnp.where(lane < RW_HEAD, 1.0, 0.0)
    hm1 = 1.0 - hm0
    stack = lambda x: jnp.concatenate([x * hm0, x * hm1], axis=0).astype(BF16)
    dup = lambda x: jnp.concatenate([x, x], axis=0).astype(BF16)
    ri = lax.broadcasted_iota(jnp.int32, (2 * C, 2 * C), 0)
    ci = lax.broadcasted_iota(jnp.int32, (2 * C, 2 * C), 1)
    same = (ri & C) == (ci & C)
    dts = ((ri & (C - 1)) - (ci & (C - 1))) * sgn
    strict = same & (dts > 0)
    incl = same & (dts >= 0)
    eye = jnp.where(ri == ci, 1.0, 0.0)
    valid = (lax.broadcasted_iota(jnp.int32, (2 * C, LANES), 0) & C) == (
        lax.broadcasted_iota(jnp.int32, (2 * C, LANES), 1) & RW_HEAD)

    pairs = range(RW_DIM // LANES)
    mm = functools.partial(jnp.dot, preferred_element_type=F32)
    mm_tb = lambda a, b: lax.dot_general(a, b, _TB, preferred_element_type=F32)
    mm_ta = lambda a, b: lax.dot_general(a, b, _TA, preferred_element_type=F32)
    xa, xr, vst, bws, kws, wtots, gms = [], [], [], [], [], [], []
    for p in pairs:
        ls = slice(p * LANES, (p + 1) * LANES)
        r, v, kk, k, ag = (x[:, ls] for x in (r_ref, v_ref, kk_ref, k_ref, ag_ref))
        lw, cum, tot = lw_all[:, ls], cum_all[:, ls], tot_all[:, ls]
        ei = jnp.exp(-cum)
        ew = jnp.exp(tot - cum)
        b = kk * ag
        xa.append(stack(-kk * jnp.exp(cum - lw)))
        xr.append(stack(r * jnp.exp(cum)))
        vst.append(stack(v))
        bws.append(stack(b * ew))
        kws.append(stack(k * ew))
        wtots.append(jnp.exp(tot))
        x2 = jnp.concatenate([xa[p], xr[p]], axis=0)
        y2 = jnp.concatenate([dup(b * ei), dup(k * ei)], axis=0)
        gms.append(mm_tb(x2, y2))
    l_ab = [jnp.where(strict, g[0:2 * C, 0:2 * C], 0.0) for g in gms]
    l_ak = [jnp.where(strict, g[0:2 * C, 2 * C:], 0.0).astype(BF16) for g in gms]
    m_rb = [jnp.where(incl, g[2 * C:, 0:2 * C], 0.0).astype(BF16) for g in gms]
    m_rk = [jnp.where(incl, g[2 * C:, 2 * C:], 0.0).astype(BF16) for g in gms]

    pinv = [eye + l for l in l_ab]
    lp = l_ab
    for _ in range(int(math.log2(C)) - 1):
        lpb = [l.astype(BF16) for l in lp]
        lp = [mm(l, l) for l in lpb]
        pinv = [pi + mm(pi.astype(BF16), l.astype(BF16)) for pi, l in zip(pinv, lp)]

    st = [st_ref[p] for p in pairs]
    stb = [s.astype(BF16) for s in st]
    pre = [mm_tb(xa[p], stb[p]) + mm(l_ak[p], vst[p]) for p in pairs]
    ustb = [jnp.where(valid, mm(pinv[p].astype(BF16), pre[p].astype(BF16)), 0.0).astype(BF16) for p in pairs]
    for p in pairs:
        yst = jnp.where(valid, mm_tb(xr[p], stb[p]) + mm(m_rb[p], ustb[p]) + mm(m_rk[p], vst[p]), 0.0)
        y_ref[:, p * LANES:(p + 1) * LANES] = yst[0:C] + yst[C:]
    for p in pairs:
        st_ref[p] = st[p] * wtots[p] + mm_ta(ustb[p], bws[p]) + mm_ta(vst[p], kws[p])


def wkv(r, v, kk, lw2, k2, ag2, *, batch, seq):
    n = r.shape[0]
    C = WKV_CHUNK
    nc = seq // C

    def rows(b, d, c):
        return b * nc + jnp.where(d == 1, nc - 1 - c, c)

    shared = pl.BlockSpec((C, RW_DIM), lambda b, d, c: (rows(b, d, c), 0))
    perdir = pl.BlockSpec((C, RW_DIM), lambda b, d, c: (rows(b, d, c), d))
    return pl.pallas_call(
        _wkv_body,
        grid=(batch, 2, nc),
        in_specs=[shared, shared, shared, perdir, perdir, perdir],
        out_specs=perdir,
        out_shape=jax.ShapeDtypeStruct((n, 2 * RW_DIM), F32),
        scratch_shapes=[pltpu.VMEM((RW_DIM // LANES, LANES, LANES), F32)],
        compiler_params=_cparams("parallel", "parallel", "arbitrary"),
        name="wkv_scan",
    )(r, v, kk, lw2, k2, ag2)


def _rw_post_body(y2_ref, r_ref, k2_ref, v_ref, g_ref, vec_ref, bd_ref, o_ref):
    vec = vec_ref[...]
    lnx_w, lnx_b, r_k = (vec[i:i + 1, :] for i in range(3))
    bd = bd_ref[...]
    y = y2_ref[:, 0:RW_DIM] + y2_ref[:, RW_DIM:]
    inv = 1.0 / RW_HEAD
    mean = _head_sums(y, bd) * inv
    d = y - mean
    var = _head_sums(d * d, bd) * inv
    yn = d * lax.rsqrt(var + LNX_EPS) * lnx_w + lnx_b
    ksum = k2_ref[:, 0:RW_DIM] + k2_ref[:, RW_DIM:]
    bonus = _head_sums(r_ref[...] * ksum * r_k, bd) * v_ref[...]
    o_ref[...] = ((yn + bonus) * g_ref[...]).astype(o_ref.dtype)


def rwkv_post(y2, r, k2, v, g, vec, bd, *, tm):
    n = r.shape[0]
    row = lambda w: pl.BlockSpec((tm, w), lambda i: (i, 0))
    full = lambda a: pl.BlockSpec(a.shape, lambda i: (0,) * a.ndim)
    return pl.pallas_call(
        _rw_post_body,
        grid=(n // tm,),
        in_specs=[row(2 * RW_DIM), row(RW_DIM), row(2 * RW_DIM), row(RW_DIM), row(RW_DIM), full(vec), full(bd)],
        out_specs=row(RW_DIM),
        out_shape=jax.ShapeDtypeStruct((n, RW_DIM), BF16),
        compiler_params=_cparams("parallel"),
        name="rwkv_post",
    )(y2, r, k2, v, g, vec, bd)


def _merge_body(o_ref, yg_ref, ga_ref, gr_ref, wa_ref, wr_ref, m_ref):
    attn = jnp.dot(o_ref[...], wa_ref[...], preferred_element_type=F32)
    rw = jnp.dot(yg_ref[...], wr_ref[...], preferred_element_type=F32)
    m_ref[...] = (_sigmoid(ga_ref[...].astype(F32)) * attn + _sigmoid(gr_ref[...].astype(F32)) * rw).astype(m_ref.dtype)


def merge_branches(o, yg, gates, wa, wr, *, tm, tn):
    n = o.shape[0]
    nj = D_MODEL // tn
    return pl.pallas_call(
        _merge_body,
        grid=(n // tm, nj),
        in_specs=[pl.BlockSpec((tm, MLA_HEADS * V_DIM), lambda i, j: (i, 0)),
                  pl.BlockSpec((tm, RW_DIM), lambda i, j: (i, 0)),
                  pl.BlockSpec((tm, tn), lambda i, j: (i, j)),
                  pl.BlockSpec((tm, tn), lambda i, j: (i, nj + j)),
                  pl.BlockSpec((MLA_HEADS * V_DIM, tn), lambda i, j: (0, j)),
                  pl.BlockSpec((RW_DIM, tn), lambda i, j: (0, j))],
        out_specs=pl.BlockSpec((tm, tn), lambda i, j: (i, j)),
        out_shape=jax.ShapeDtypeStruct((n, D_MODEL), BF16),
        compiler_params=_cparams("parallel", "parallel"),
        name="merge_branches",
    )(o, yg, gates, gates, wa, wr)


def _mm_res_body(a_ref, w_ref, x_ref, o_ref):
    o_ref[...] = x_ref[...] + jnp.dot(a_ref[...], w_ref[...], preferred_element_type=F32)


def matmul_residual(a, w, x, *, tm, tn):
    n, k = a.shape
    nc = w.shape[1]
    return pl.pallas_call(
        _mm_res_body,
        grid=(n // tm, nc // tn),
        in_specs=[pl.BlockSpec((tm, k), lambda i, j: (i, 0)),
                  pl.BlockSpec((k, tn), lambda i, j: (0, j)),
                  pl.BlockSpec((tm, tn), lambda i, j: (i, j))],
        out_specs=pl.BlockSpec((tm, tn), lambda i, j: (i, j)),
        out_shape=jax.ShapeDtypeStruct((n, nc), F32),
        compiler_params=_cparams("parallel", "parallel"),
        name="matmul_residual",
    )(a, w, x)


def _top16_rows(ss, idx):
    big = 3.0e38
    vals = [[] for _ in ss]
    poss = [[] for _ in ss]
    for _ in range(PEER_TOPK):
        ms = [jnp.max(s, axis=0, keepdims=True) for s in ss]
        ps = [jnp.min(jnp.where(s == m, idx, big), axis=0, keepdims=True) for s, m in zip(ss, ms)]
        ss = [jnp.where(idx == p, -jnp.inf, s) for s, p in zip(ss, ps)]
        for i, (m, p) in enumerate(zip(ms, ps)):
            vals[i].append(m)
            poss[i].append(p)
    return [(jnp.concatenate(v, axis=0), jnp.concatenate(p, axis=0)) for v, p in zip(vals, poss)]


def _candidates(sv0, sv1):
    tok = sv0.shape[1]
    io = lambda rows: lax.broadcasted_iota(jnp.int32, (rows, tok), 0).astype(F32)
    half = PEER_TOPK // 2
    parts = [sv0[0:1, :] + sv1]
    idxs = [io(PEER_TOPK)]
    for a in range(1, half):
        parts.append(sv0[a:a + 1, :] + sv1[0:half, :])
        idxs.append(io(half) + float(a * PEER_TOPK))
    parts.append(sv0[half:, :] + sv1[0:1, :])
    idxs.append((io(half) + float(half)) * float(PEER_TOPK))
    return jnp.concatenate(parts, axis=0), jnp.concatenate(idxs, axis=0)


def _pick_rows(table, sel):
    out = jnp.zeros(sel.shape, table.dtype)
    for a in range(PEER_TOPK):
        out = jnp.where(sel == a, table[a:a + 1, :], out)
    return out


def _peer_topk_body(q_ref, sk_ref, i1_ref, i2_ref, gt_ref):
    i1s, i2s, gts = [], [], []
    key_idx = lax.broadcasted_iota(jnp.int32, (N_KEYS, q_ref.shape[0]), 0).astype(F32)
    for h in range(PEER_HEADS):
        ss = []
        for p in range(2):
            c0 = (2 * h + p) * PEER_HALF
            qb = q_ref[:, c0:c0 + PEER_HALF].astype(BF16)
            ss.append(lax.dot_general(sk_ref[p], qb, _TB, preferred_element_type=F32))
        (sv0, si0), (sv1, si1) = _top16_rows(ss, key_idx)
        cand, cand_idx = _candidates(sv0, sv1)
        ((top_s, top_j),) = _top16_rows([cand], cand_idx)
        ja = jnp.floor(top_j * (1.0 / PEER_TOPK))
        i1s.append(_pick_rows(si0, ja))
        i2s.append(_pick_rows(si1, top_j - ja * PEER_TOPK))
        e = jnp.exp(top_s - top_s[0:1, :])
        gts.append(e / jnp.sum(e, axis=0, keepdims=True))
    i1_ref[...] = jnp.concatenate(i1s, axis=0).T
    i2_ref[...] = jnp.concatenate(i2s, axis=0).T
    gt_ref[...] = jnp.concatenate(gts, axis=0).T


def peer_topk(q, sub_keys):
    n = q.shape[0]
    tmk = LANES
    out = pl.BlockSpec((tmk, NSLOT), lambda i: (i, 0))
    return pl.pallas_call(
        _peer_topk_body,
        grid=(n // tmk,),
        in_specs=[pl.BlockSpec((tmk, 2 * PEER_HALF * PEER_HEADS), lambda i: (i, 0)),
                  pl.BlockSpec(sub_keys.shape, lambda i: (0, 0, 0))],
        out_specs=[out, out, out],
        out_shape=[jax.ShapeDtypeStruct((n, NSLOT), F32)] * 3,
        compiler_params=_cparams("parallel"),
        name="peer_topk",
    )(q, sub_keys)


def _gelu_exact(x):
    return 0.5 * x * (1.0 + lax.erf(x * (1.0 / math.sqrt(2.0))))


_HI16 = 0xFFFF0000
G_UNROLL = SUBLANES
E_SUB = 2 * N_KEYS
PEER_TE = 2 * E_SUB


def _bf16_bits_hi(x):
    u = pltpu.bitcast(x, jnp.uint32)
    return (u + jnp.uint32(0x7FFF) + ((u >> 16) & jnp.uint32(1))) & jnp.uint32(_HI16)


def _peer_dense_body(xn_ref, i1_ref, i2_ref, gt_ref, ut_ref, v_ref, x_ref, lnf_ref, o_ref, g_ref,
                     *, tm, te, final_norm):
    e = pl.program_id(1)
    half = tm // 2
    mm_tb = lambda a, b: lax.dot_general(a, b, _TB, preferred_element_type=F32)

    @pl.when(e == 0)
    def _():
        o_ref[...] = x_ref[...]
        sub = lax.broadcasted_iota(jnp.int32, (N_KEYS, NSLOT), 0).astype(F32)

        def build(it, carry):
            base = pl.multiple_of(it * G_UNROLL, SUBLANES)
            tiles = [(ref[pl.ds(base, G_UNROLL), :], ref[pl.ds(half + base, G_UNROLL), :])
                     for ref in (i1_ref, i2_ref, gt_ref)]
            a_ts, b_ts = [], []
            for j in range(G_UNROLL):
                for t in range(2):
                    i1, i2, gt = (tiles[q][t][j:j + 1, :] for q in range(3))
                    a_ts.append(jnp.where(sub == i1, 1.0, 0.0).astype(BF16))
                    b_ts.append(jnp.where(sub == i2, gt, 0.0).astype(BF16))
            gs = [mm_tb(a, b) for a, b in zip(a_ts, b_ts)]
            for j in range(G_UNROLL):
                word = _bf16_bits_hi(gs[2 * j]) | (_bf16_bits_hi(gs[2 * j + 1]) >> 16)
                g_ref[pl.ds(pl.multiple_of((base + j) * G_PITCH, SUBLANES), N_KEYS), :] = word
            return carry

        lax.fori_loop(0, half // G_UNROLL, build, 0)

    xn = xn_ref[...]
    nsub = te // E_SUB
    hs = [jnp.dot(xn, ut_ref[:, j * E_SUB:(j + 1) * E_SUB], preferred_element_type=F32) for j in range(nsub)]
    coefs = []
    for j in range(nsub):
        row0 = e * (te // N_KEYS) + j * (E_SUB // N_KEYS)
        words = [g_ref[pl.ds(row0 + r, half, stride=G_PITCH), :] for r in range(E_SUB // N_KEYS)]
        top = jnp.concatenate([pltpu.bitcast(w & jnp.uint32(_HI16), F32) for w in words], axis=1)
        bot = jnp.concatenate([pltpu.bitcast(w << 16, F32) for w in words], axis=1)
        gate = jnp.concatenate([top, bot], axis=0)
        coefs.append((gate * _gelu_exact(hs[j])).astype(BF16))
    o_ref[...] += jnp.dot(jnp.concatenate(coefs, axis=1), v_ref[...], preferred_element_type=F32)

    if final_norm:
        @pl.when(e == pl.num_programs(1) - 1)
        def _():
            o_ref[...] = _rms_rows(o_ref[...], lnf_ref[...], EPS)


def peer_dense(xn, i1, i2, gt, ut, v, x, ln_f, *, tm, te, final_norm):
    n = xn.shape[0]
    ne = v.shape[0] // te
    assert tm % (2 * G_UNROLL) == 0 and te % E_SUB == 0 and ut.shape == (ne, D_MODEL, te)
    row = lambda w, **kw: pl.BlockSpec((tm, w), lambda i, e: (i, 0), **kw)
    once = dict(pipeline_mode=pl.Buffered(1))
    return pl.pallas_call(
        functools.partial(_peer_dense_body, tm=tm, te=te, final_norm=final_norm),
        grid=(n // tm, ne),
        in_specs=[row(D_MODEL, **once), row(NSLOT), row(NSLOT), row(NSLOT),
                  pl.BlockSpec((None, D_MODEL, te), lambda i, e: (e, 0, 0)),
                  pl.BlockSpec((te, D_MODEL), lambda i, e: (e, 0)),
                  row(D_MODEL, **once),
                  pl.BlockSpec((1, D_MODEL), lambda i, e: (0, 0))],
        out_specs=row(D_MODEL),
        out_shape=jax.ShapeDtypeStruct((n, D_MODEL), F32),
        scratch_shapes=[pltpu.VMEM((tm // 2 * G_PITCH, N_KEYS), jnp.uint32)],
        compiler_params=_cparams("parallel", "arbitrary"),
        name="peer_dense",
    )(xn, i1, i2, gt, ut, v, x, ln_f.reshape(1, D_MODEL))


def _pad_cols(a, w):
    return jnp.pad(a, ((0, 0), (0, w - a.shape[1])))


def _pad_rows(a, h):
    return jnp.pad(a, ((0, h - a.shape[0]),) + ((0, 0),) * (a.ndim - 1))


def _swap_halves(a):
    half = a.shape[-1] // 2
    return jnp.concatenate([a[..., half:], a[..., :half]], axis=-1)


def _layer_weights(i, p):
    first = i == 0
    w_in = p["w_in_first"] if first else p["w_in_rest"][i - 1]
    mu = p["mu_first"] if first else p["mu_rest"][i - 1]
    mla_cols = Q_LORA + KV_LORA + ROPE_DIM
    gate0 = mla_cols
    rw0 = mla_cols + 2 * D_MODEL
    k_rope_w = w_in[:, Q_LORA + KV_LORA:mla_cols]
    w_mla = jnp.concatenate([w_in[:, :mla_cols], _swap_halves(k_rope_w)], axis=1)
    w_gate = w_in[:, gate0:rw0]

    def rw_layout(a):
        c = 3 * RW_DIM
        dwf, dwb = a[..., c:c + DECAY_LORA], a[..., c + DECAY_LORA:c + 2 * DECAY_LORA]
        c += 2 * DECAY_LORA
        daf, dab = a[..., c:c + AAA_LORA], a[..., c + AAA_LORA:c + 2 * AAA_LORA]
        c += 2 * AAA_LORA
        dg = a[..., c:c + GATE_LORA]
        c += GATE_LORA
        dv = a[..., c:]
        z = lambda w: jnp.zeros(a.shape[:-1] + (w,), a.dtype)
        return jnp.concatenate([a[..., :3 * RW_DIM], dwf, dwb, daf, dab, dg, z(2 * LANES - GATE_LORA),
                                dv, z(2 * LANES - dv.shape[-1])], axis=-1)

    w_rw = rw_layout(w_in[:, rw0:])
    mu_l = rw_layout(mu[None, :])
    zeros64 = jnp.zeros((DECAY_LORA, RW_DIM), F32)
    wq = p["w_uq"][i].reshape(Q_LORA, MLA_HEADS, NOPE_DIM + ROPE_DIM)
    wq = jnp.concatenate([wq, _swap_halves(wq[..., NOPE_DIM:])], axis=-1).reshape(Q_LORA, MLA_HEADS * QK_PAD)
    vec = jnp.stack([p["w0_f"][i], p["w0_b"][i], p["a0_f"][i], p["a0_b"][i], p["k_k"][i], p["k_a"][i],
                     p["v0_rest"][i - 1] if not first else jnp.zeros((RW_DIM,), F32),
                     jnp.zeros((RW_DIM,), F32)])
    lw = {
        "ln1": p["ln1"][i], "w_mla": w_mla.astype(BF16), "w_gate": w_gate.astype(BF16),
        "w_rw": w_rw.astype(BF16),
        "mu": mu_l, "q_norm": p["q_norm"][i], "wq": wq.astype(BF16), "kv_norm": p["kv_norm"][i],
        "wkv": p["w_ukv"][i].astype(BF16), "w_o_attn": p["w_o_attn"][i].astype(BF16),
        "vec": vec,
        "w2f": jnp.concatenate([p["w2_f"][i], zeros64]).astype(BF16),
        "w2b": jnp.concatenate([zeros64, p["w2_b"][i]]).astype(BF16),
        "a2f": jnp.concatenate([p["a2_f"][i], zeros64]).astype(BF16),
        "a2b": jnp.concatenate([zeros64, p["a2_b"][i]]).astype(BF16),
        "g2": _pad_rows(p["g2"][i], 2 * LANES).astype(BF16),
        "v2": None if first else _pad_rows(p["v2_rest"][i - 1], LANES).astype(BF16),
        "post_vec": jnp.stack([p["lnx_w"][i], p["lnx_b"][i], p["r_k"][i].reshape(RW_DIM)]
                              + [jnp.zeros((RW_DIM,), F32)] * 5),
        "w_o_rwkv": p["w_o_rwkv"][i].astype(BF16), "w_out": p["w_out"][i].astype(BF16),
        "ln2": p["ln2"][i], "w_pq": p["w_pq"][i].astype(BF16), "sub_keys": p["sub_keys"][i].astype(BF16),
        "peer_ut": p["peer_u"][i].astype(BF16).reshape(-1, PEER_TE, D_MODEL).transpose(0, 2, 1),
        "peer_v": p["peer_v"][i].astype(BF16),
    }
    return lw


def _rope_table(seq):
    inv = 1.0 / (ROPE_THETA ** (jnp.arange(0, ROPE_DIM, 2, dtype=F32) / ROPE_DIM))
    ang = jnp.arange(seq, dtype=F32)[:, None] * inv[None, :]
    c, s = jnp.cos(ang), jnp.sin(ang)
    return jnp.concatenate([c, c, -s, s], axis=1)


def _tiles(seq):
    return dict(tm=min(512, seq), tmm=min(1024, seq), tq=min(4 * ATTN_SUB, seq), tprep=min(256, seq),
                tpeer=min(512, seq))


def _trunk(x, layers, ln_f, bd):
    batch, seq, _ = x.shape
    n = batch * seq
    t = _tiles(seq)
    tm, tmm = t["tm"], t["tmm"]
    ccss = _rope_table(seq)
    x = x.reshape(n, D_MODEL)
    v_first = None
    for li, lw in enumerate(layers):
        xn1 = rms_norm(x, lw["ln1"], tm=tm)
        h_mla = matmul(xn1, lw["w_mla"], tm=tmm, tn=MLA_W, out_dtype=BF16)
        gates = matmul(xn1, lw["w_gate"], tm=tmm, tn=1024, out_dtype=BF16)
        h_rw = matmul(xn1, lw["w_rw"], tm=tmm, tn=1280, out_dtype=F32)
        q = mla_q_proj(h_mla, lw["q_norm"], lw["wq"], ccss, seq=seq, tm=tm)
        k, v = mla_kv_proj(h_mla, lw["kv_norm"], lw["wkv"], ccss, seq=seq, tm=tm)
        o = attention(q, k, v, batch=batch, seq=seq, tq=t["tq"])
        r, vv, kk, ag2, lw2, k2, g = rwkv_prep(h_rw, lw["mu"], lw["vec"], lw["w2f"], lw["w2b"], lw["a2f"],
                                              lw["a2b"], lw["g2"], bd, lw["v2"], v_first,
                                              seq=seq, tm=t["tprep"])
        if v_first is None:
            v_first = vv
        y2 = wkv(r, vv, kk, lw2, k2, ag2, batch=batch, seq=seq)
        yg = rwkv_post(y2, r, k2, vv, g, lw["post_vec"], bd, tm=t["tprep"])
        m = merge_branches(o, yg, gates, lw["w_o_attn"], lw["w_o_rwkv"], tm=tmm, tn=512)
        x = matmul_residual(m, lw["w_out"], x, tm=tmm, tn=1024)
        xn = rms_norm(x, lw["ln2"], tm=tm)
        qp = matmul(xn, lw["w_pq"], tm=tmm, tn=1024, out_dtype=F32)
        i1, i2, gt = peer_topk(qp, lw["sub_keys"])
        x = peer_dense(xn, i1, i2, gt, lw["peer_ut"], lw["peer_v"], x, ln_f, tm=t["tpeer"], te=PEER_TE,
                       final_norm=li == len(layers) - 1)
    return x.reshape(batch, seq, D_MODEL)


def kernel(x_prompt, x_sample, ln1, w_in_first, mu_first, w_in_rest, mu_rest, q_norm, w_uq, kv_norm, w_ukv, w_o_attn, w0_f, w2_f, w0_b, w2_b, a0_f, a2_f, a0_b, a2_b, g2, k_k, k_a, r_k, lnx_w, lnx_b, v0_rest, v2_rest, w_o_rwkv, w_out, ln2, w_pq, sub_keys, peer_u, peer_v, ln_f):
    p = dict(ln1=ln1, w_in_first=w_in_first, mu_first=mu_first, w_in_rest=w_in_rest, mu_rest=mu_rest,
             q_norm=q_norm, w_uq=w_uq, kv_norm=kv_norm, w_ukv=w_ukv, w_o_attn=w_o_attn, w0_f=w0_f, w2_f=w2_f,
             w0_b=w0_b, w2_b=w2_b, a0_f=a0_f, a2_f=a2_f, a0_b=a0_b, a2_b=a2_b, g2=g2, k_k=k_k, k_a=k_a, r_k=r_k,
             lnx_w=lnx_w, lnx_b=lnx_b, v0_rest=v0_rest, v2_rest=v2_rest, w_o_rwkv=w_o_rwkv, w_out=w_out,
             ln2=ln2, w_pq=w_pq, sub_keys=sub_keys, peer_u=peer_u, peer_v=peer_v)
    depth = ln1.shape[0]
    layers = [_layer_weights(i, p) for i in range(depth)]
    lane = jnp.arange(LANES)
    bd = (lane[:, None] // RW_HEAD == lane[None, :] // RW_HEAD).astype(F32)
    return (_trunk(x_prompt, layers, ln_f, bd), _trunk(x_sample, layers, ln_f, bd))
```

```python
import functools
import math

import jax
import jax.numpy as jnp
from jax import lax
from jax.experimental import pallas as pl
from jax.experimental.pallas import tpu as pltpu

F32 = jnp.float32
BF16 = jnp.bfloat16
HIGHEST = lax.Precision.HIGHEST

LANES = 128
SUBLANES = 8
VMEM_LIMIT_BYTES = 56 * 1024 * 1024

D_MODEL = 2048
MLA_HEADS = 16
Q_LORA = 512
KV_LORA = 512
NOPE_DIM = 128
ROPE_DIM = 64
V_DIM = 128
ROPE_THETA = 10000.0
RW_HEAD = 64
RW_HEADS = 16
RW_DIM = RW_HEADS * RW_HEAD
DECAY_LORA = 64
AAA_LORA = 64
MV_LORA = 32
GATE_LORA = 160
LNX_EPS = 64e-5
PEER_HEADS = 8
N_KEYS = 128
PEER_HALF = 128
PEER_TOPK = 16
EPS = 1e-6

QK_PAD = 2 * LANES
MLA_W = Q_LORA + KV_LORA + LANES
LORA_W = 2 * DECAY_LORA + 2 * AAA_LORA + 2 * LANES + 2 * LANES
RW_W = 3 * RW_DIM + LORA_W
PROJ_HEADS = 4
ATTN_SUB = 256
WKV_CHUNK = 64
G_PITCH = N_KEYS + SUBLANES
NSLOT = PEER_HEADS * PEER_TOPK

_TB = (((1,), (1,)), ((), ()))
_TA = (((0,), (0,)), ((), ()))


def _cparams(*sem):
    return pltpu.CompilerParams(dimension_semantics=sem, vmem_limit_bytes=VMEM_LIMIT_BYTES)


def _rms_rows(x, g, eps):
    ms = jnp.mean(x * x, axis=-1, keepdims=True)
    return x * lax.rsqrt(ms + eps) * g


def _rms_norm_body(x_ref, g_ref, o_ref):
    o_ref[...] = _rms_rows(x_ref[...].astype(F32), g_ref[...], EPS).astype(o_ref.dtype)


def rms_norm(x, gain, *, tm):
    n, k = x.shape
    return pl.pallas_call(
        _rms_norm_body,
        grid=(n // tm,),
        in_specs=[pl.BlockSpec((tm, k), lambda i: (i, 0)), pl.BlockSpec((1, k), lambda i: (0, 0))],
        out_specs=pl.BlockSpec((tm, k), lambda i: (i, 0)),
        out_shape=jax.ShapeDtypeStruct((n, k), BF16),
        compiler_params=_cparams("parallel"),
        name="rms_norm",
    )(x, gain.reshape(1, k))


def _mm_body(a_ref, w_ref, o_ref):
    o_ref[...] = jnp.dot(a_ref[...], w_ref[...], preferred_element_type=F32).astype(o_ref.dtype)


def matmul(a, w, *, tm, tn, out_dtype):
    n, k = a.shape
    nc = w.shape[1]
    tn = min(tn, nc)
    assert n % tm == 0 and nc % tn == 0 and w.shape[0] == k
    return pl.pallas_call(
        _mm_body,
        grid=(n // tm, nc // tn),
        in_specs=[pl.BlockSpec((tm, k), lambda i, j: (i, 0)),
                  pl.BlockSpec((k, tn), lambda i, j: (0, j))],
        out_specs=pl.BlockSpec((tm, tn), lambda i, j: (i, j)),
        out_shape=jax.ShapeDtypeStruct((n, nc), out_dtype),
        compiler_params=_cparams("parallel", "parallel"),
        name="matmul",
    )(a, w)


def _rope_lanes(y2, ccss):
    w = y2 * ccss
    return w + pltpu.roll(w, ROPE_DIM, axis=1)


def _qproj_body(c_ref, g_ref, w_ref, t_ref, q_ref, cn_ref, *, scale):
    @pl.when(pl.program_id(1) == 0)
    def _():
        cn_ref[...] = _rms_rows(c_ref[...].astype(F32), g_ref[...], EPS).astype(BF16)

    y = jnp.dot(cn_ref[...], w_ref[...], preferred_element_type=F32)
    tab = t_ref[...]
    lane = lax.broadcasted_iota(jnp.int32, tab.shape, 1)
    for hh in range(PROJ_HEADS):
        c0 = hh * QK_PAD
        q_ref[:, c0:c0 + LANES] = (y[:, c0:c0 + LANES] * scale).astype(q_ref.dtype)
        rot = _rope_lanes(y[:, c0 + LANES:c0 + QK_PAD], tab)
        q_ref[:, c0 + LANES:c0 + QK_PAD] = jnp.where(lane < ROPE_DIM, rot * scale, 0.0).astype(q_ref.dtype)


def mla_q_proj(h_mla, q_norm, wq, ccss, *, seq, tm):
    n = h_mla.shape[0]
    nt = seq // tm
    scale = (NOPE_DIM + ROPE_DIM) ** -0.5
    return pl.pallas_call(
        functools.partial(_qproj_body, scale=scale),
        grid=(n // tm, MLA_HEADS // PROJ_HEADS),
        in_specs=[pl.BlockSpec((tm, Q_LORA), lambda i, h: (i, 0)),
                  pl.BlockSpec((1, Q_LORA), lambda i, h: (0, 0)),
                  pl.BlockSpec((Q_LORA, PROJ_HEADS * QK_PAD), lambda i, h: (0, h)),
                  pl.BlockSpec((tm, LANES), lambda i, h: (i % nt, 0))],
        out_specs=pl.BlockSpec((tm, PROJ_HEADS * QK_PAD), lambda i, h: (i, h)),
        out_shape=jax.ShapeDtypeStruct((n, MLA_HEADS * QK_PAD), BF16),
        scratch_shapes=[pltpu.VMEM((tm, Q_LORA), BF16)],
        compiler_params=_cparams("parallel", "arbitrary"),
        name="mla_q_proj",
    )(h_mla, q_norm.reshape(1, Q_LORA), wq, ccss)


def _kvproj_body(c_ref, kr_ref, g_ref, w_ref, t_ref, k_ref, v_ref, cn_ref, krr_ref):
    @pl.when(pl.program_id(1) == 0)
    def _():
        cn_ref[...] = _rms_rows(c_ref[...].astype(F32), g_ref[...], EPS).astype(BF16)
        krr_ref[...] = _rope_lanes(kr_ref[...].astype(F32), t_ref[...]).astype(BF16)

    y = jnp.dot(cn_ref[...], w_ref[...], preferred_element_type=F32)
    for hh in range(PROJ_HEADS):
        c0 = hh * (NOPE_DIM + V_DIM)
        k_ref[:, hh * QK_PAD:hh * QK_PAD + LANES] = y[:, c0:c0 + NOPE_DIM].astype(k_ref.dtype)
        k_ref[:, hh * QK_PAD + LANES:(hh + 1) * QK_PAD] = krr_ref[...]
        v_ref[:, hh * V_DIM:(hh + 1) * V_DIM] = y[:, c0 + NOPE_DIM:c0 + NOPE_DIM + V_DIM].astype(v_ref.dtype)


def mla_kv_proj(h_mla, kv_norm, wkv, ccss, *, seq, tm):
    n = h_mla.shape[0]
    nt = seq // tm
    return pl.pallas_call(
        _kvproj_body,
        grid=(n // tm, MLA_HEADS // PROJ_HEADS),
        in_specs=[pl.BlockSpec((tm, KV_LORA), lambda i, h: (i, 1)),
                  pl.BlockSpec((tm, LANES), lambda i, h: (i, (Q_LORA + KV_LORA) // LANES)),
                  pl.BlockSpec((1, KV_LORA), lambda i, h: (0, 0)),
                  pl.BlockSpec((KV_LORA, PROJ_HEADS * (NOPE_DIM + V_DIM)), lambda i, h: (0, h)),
                  pl.BlockSpec((tm, LANES), lambda i, h: (i % nt, 0))],
        out_specs=[pl.BlockSpec((tm, PROJ_HEADS * QK_PAD), lambda i, h: (i, h)),
                   pl.BlockSpec((tm, PROJ_HEADS * V_DIM), lambda i, h: (i, h))],
        out_shape=[jax.ShapeDtypeStruct((n, MLA_HEADS * QK_PAD), BF16),
                   jax.ShapeDtypeStruct((n, MLA_HEADS * V_DIM), BF16)],
        scratch_shapes=[pltpu.VMEM((tm, KV_LORA), BF16), pltpu.VMEM((tm, LANES), BF16)],
        compiler_params=_cparams("parallel", "arbitrary"),
        name="mla_kv_proj",
    )(h_mla, h_mla, kv_norm.reshape(1, KV_LORA), wkv, ccss)


def _attn_body(q_ref, k_ref, v_ref, o_ref, *, nsub):
    ts = q_ref.shape[0] // nsub
    k = k_ref[...]
    v = v_ref[...]

    def scores(j):
        return lax.dot_general(q_ref[j * ts:(j + 1) * ts, :], k, _TB, preferred_element_type=F32)

    def finish(s, j):
        m = jnp.max(s, axis=-1, keepdims=True)
        p = jnp.exp(s - m)
        l = jnp.sum(p, axis=-1, keepdims=True)
        o = jnp.dot(p.astype(BF16), v, preferred_element_type=F32)
        o_ref[j * ts:(j + 1) * ts, :] = (o / l).astype(o_ref.dtype)

    s_prev = scores(0)
    for j in range(1, nsub):
        s_next = scores(j)
        finish(s_prev, j - 1)
        s_prev = s_next
    finish(s_prev, nsub - 1)


def attention(q, k, v, *, batch, seq, tq):
    n = q.shape[0]
    nq = seq // tq
    return pl.pallas_call(
        functools.partial(_attn_body, nsub=tq // ATTN_SUB),
        grid=(batch, MLA_HEADS, nq),
        in_specs=[pl.BlockSpec((tq, QK_PAD), lambda b, h, i: (b * nq + i, h)),
                  pl.BlockSpec((seq, QK_PAD), lambda b, h, i: (b, h)),
                  pl.BlockSpec((seq, V_DIM), lambda b, h, i: (b, h))],
        out_specs=pl.BlockSpec((tq, V_DIM), lambda b, h, i: (b * nq + i, h)),
        out_shape=jax.ShapeDtypeStruct((n, MLA_HEADS * V_DIM), BF16),
        compiler_params=_cparams("parallel", "parallel", "arbitrary"),
        name="mla_attention",
    )(q, k, v)


def _head_sums(x, bd):
    parts = []
    for gi in range(RW_DIM // LANES):
        parts.append(jnp.dot(x[:, gi * LANES:(gi + 1) * LANES], bd, precision=HIGHEST,
                             preferred_element_type=F32))
    return jnp.concatenate(parts, axis=1)


def _sigmoid(x):
    return 1.0 / (1.0 + jnp.exp(-x))


def _rw_prep_body(*refs, tiles_per_seq, tm, has_vfirst):
    (h_ref, hp_ref, hn_ref, mu_ref, vec_ref, w2f_ref, w2b_ref, a2f_ref, a2b_ref, g2_ref, bd_ref) = refs[:11]
    pos = 11
    if has_vfirst:
        v2_ref, vf_ref = refs[pos:pos + 2]
        pos += 2
    r_ref, v_ref, kk_ref, ag_ref, lw_ref, k2_ref, g_ref = refs[pos:pos + 7]

    ti = pl.program_id(0) % tiles_per_seq
    keep_prev = jnp.where(ti == 0, 0.0, 1.0)
    keep_next = jnp.where(ti == tiles_per_seq - 1, 0.0, 1.0)
    row = lax.broadcasted_iota(jnp.int32, (tm, 1), 0)

    def shifted(c0, c1):
        x = h_ref[:, c0:c1]
        prev = jnp.where(row == 0, hp_ref[SUBLANES - 1:SUBLANES, c0:c1] * keep_prev, pltpu.roll(x, 1, axis=0))
        nxt = jnp.where(row == tm - 1, hn_ref[0:1, c0:c1] * keep_next, pltpu.roll(x, tm - 1, axis=0))
        return x + (0.5 * (prev + nxt) - x) * mu_ref[:, c0:c1]

    vec = vec_ref[...]
    w0f, w0b, a0f, a0b, k_k, k_a, v0 = (vec[i:i + 1, :] for i in range(7))

    r = shifted(0, RW_DIM)
    k = shifted(RW_DIM, 2 * RW_DIM)
    v = shifted(2 * RW_DIM, 3 * RW_DIM)
    lo = shifted(3 * RW_DIM, RW_W)
    dw = jnp.tanh(lo[:, 0:LANES]).astype(BF16)
    da = lo[:, LANES:2 * LANES].astype(BF16)
    dg = _sigmoid(lo[:, 2 * LANES:4 * LANES]).astype(BF16)

    def decay(dw_half, w0, w2_ref):
        z = w0 + jnp.dot(dw_half, w2_ref[...], preferred_element_type=F32)
        nz = -z
        softplus = jnp.maximum(nz, 0.0) + jnp.log(1.0 + jnp.exp(-jnp.abs(nz)))
        return -jnp.exp(-softplus - 0.5)

    lw_ref[:, 0:RW_DIM] = decay(dw, w0f, w2f_ref)
    lw_ref[:, RW_DIM:] = decay(dw, w0b, w2b_ref)
    af = _sigmoid(a0f + jnp.dot(da, a2f_ref[...], preferred_element_type=F32))
    ab = _sigmoid(a0b + jnp.dot(da, a2b_ref[...], preferred_element_type=F32))
    ag_ref[:, 0:RW_DIM] = af
    ag_ref[:, RW_DIM:] = ab
    g_ref[...] = jnp.dot(dg, g2_ref[...], preferred_element_type=F32)

    if has_vfirst:
        dv = lo[:, 4 * LANES:5 * LANES].astype(BF16)
        mix = _sigmoid(v0 + jnp.dot(dv, v2_ref[...], preferred_element_type=F32))
        v = v + (vf_ref[...] - v) * mix
    v_ref[...] = v
    r_ref[...] = r

    kk = k * k_k
    ss = _head_sums(kk * kk, bd_ref[...])
    kk_ref[...] = kk * lax.rsqrt(jnp.maximum(ss, 1e-24))
    k2_ref[:, 0:RW_DIM] = k * (1.0 + (af - 1.0) * k_a)
    k2_ref[:, RW_DIM:] = k * (1.0 + (ab - 1.0) * k_a)


def rwkv_prep(h_rw, mu, vec, w2f, w2b, a2f, a2b, g2, bd, v2, v_first, *, seq, tm):
    n = h_rw.shape[0]
    tps = seq // tm
    nb8 = n // SUBLANES
    has_vfirst = v_first is not None
    full = lambda a: pl.BlockSpec(a.shape, lambda i: (0,) * a.ndim)
    in_specs = [pl.BlockSpec((tm, RW_W), lambda i: (i, 0)),
                pl.BlockSpec((SUBLANES, RW_W), lambda i: (jnp.maximum(i * (tm // SUBLANES) - 1, 0), 0)),
                pl.BlockSpec((SUBLANES, RW_W), lambda i: (jnp.minimum((i + 1) * (tm // SUBLANES), nb8 - 1), 0)),
                full(mu), full(vec), full(w2f), full(w2b), full(a2f), full(a2b), full(g2), full(bd)]
    args = [h_rw, h_rw, h_rw, mu, vec, w2f, w2b, a2f, a2b, g2, bd]
    if has_vfirst:
        in_specs += [full(v2), pl.BlockSpec((tm, RW_DIM), lambda i: (i, 0))]
        args += [v2, v_first]
    row = lambda w: pl.BlockSpec((tm, w), lambda i: (i, 0))
    widths = [RW_DIM, RW_DIM, RW_DIM, 2 * RW_DIM, 2 * RW_DIM, 2 * RW_DIM, RW_DIM]
    return pl.pallas_call(
        functools.partial(_rw_prep_body, tiles_per_seq=tps, tm=tm, has_vfirst=has_vfirst),
        grid=(n // tm,),
        in_specs=in_specs,
        out_specs=[row(w) for w in widths],
        out_shape=[jax.ShapeDtypeStruct((n, w), F32) for w in widths],
        compiler_params=_cparams("parallel"),
        name="rwkv_prep",
    )(*args)


def _wkv_body(r_ref, v_ref, kk_ref, lw_ref, k_ref, ag_ref, y_ref, st_ref):
    C = WKV_CHUNK
    rev = pl.program_id(1) == 1

    @pl.when(pl.program_id(2) == 0)
    def _():
        st_ref[...] = jnp.zeros_like(st_ref)

    ti = lax.broadcasted_iota(jnp.int32, (C, C), 0)
    si = lax.broadcasted_iota(jnp.int32, (C, C), 1)
    sgn = jnp.where(rev, -1, 1)
    tri = jnp.where((ti - si) * sgn >= 0, 1.0, 0.0)
    lw_all = lw_ref[...]
    cum_all = jnp.dot(tri, lw_all, precision=HIGHEST, preferred_element_type=F32)
    tot_all = jnp.sum(lw_all, axis=0, keepdims=True)

    lane = lax.broadcasted_iota(jnp.int32, (1, LANES), 1)
    hm0 = jnp.where(lane < RW_HEAD, 1.0, 0.0)
    hm1 = 1.0 - hm0
    stack = lambda x: jnp.concatenate([x * hm0, x * hm1], axis=0).astype(BF16)
    dup = lambda x: jnp.concatenate([x, x], axis=0).astype(BF16)
    ri = lax.broadcasted_iota(jnp.int32, (2 * C, 2 * C), 0)
    ci = lax.broadcasted_iota(jnp.int32, (2 * C, 2 * C), 1)
    same = (ri & C) == (ci & C)
    dts = ((ri & (C - 1)) - (ci & (C - 1))) * sgn
    strict = same & (dts > 0)
    incl = same & (dts >= 0)
    eye = jnp.where(ri == ci, 1.0, 0.0)
    valid = (lax.broadcasted_iota(jnp.int32, (2 * C, LANES), 0) & C) == (
        lax.broadcasted_iota(jnp.int32, (2 * C, LANES), 1) & RW_HEAD)

    pairs = range(RW_DIM // LANES)
    mm = functools.partial(jnp.dot, preferred_element_type=F32)
    mm_tb = lambda a, b: lax.dot_general(a, b, _TB, preferred_element_type=F32)
    mm_ta = lambda a, b: lax.dot_general(a, b, _TA, preferred_element_type=F32)
    xa, xr, vst, bws, kws, wtots, gms = [], [], [], [], [], [], []
    for p in pairs:
        ls = slice(p * LANES, (p + 1) * LANES)
        r, v, kk, k, ag = (x[:, ls] for x in (r_ref, v_ref, kk_ref, k_ref, ag_ref))
        lw, cum, tot = lw_all[:, ls], cum_all[:, ls], tot_all[:, ls]
        ei = jnp.exp(-cum)
        ew = jnp.exp(tot - cum)
        b = kk * ag
        xa.append(stack(-kk * jnp.exp(cum - lw)))
        xr.append(stack(r * jnp.exp(cum)))
        vst.append(stack(v))
        bws.append(stack(b * ew))
        kws.append(stack(k * ew))
        wtots.append(jnp.exp(tot))
        x2 = jnp.concatenate([xa[p], xr[p]], axis=0)
        y2 = jnp.concatenate([dup(b * ei), dup(k * ei)], axis=0)
        gms.append(mm_tb(x2, y2))
    l_ab = [jnp.where(strict, g[0:2 * C, 0:2 * C], 0.0) for g in gms]
    l_ak = [jnp.where(strict, g[0:2 * C, 2 * C:], 0.0).astype(BF16) for g in gms]
    m_rb = [jnp.where(incl, g[2 * C:, 0:2 * C], 0.0).astype(BF16) for g in gms]
    m_rk = [jnp.where(incl, g[2 * C:, 2 * C:], 0.0).astype(BF16) for g in gms]

    pinv = [eye + l for l in l_ab]
    lp = l_ab
    for _ in range(int(math.log2(C)) - 1):
        lpb = [l.astype(BF16) for l in lp]
        lp = [mm(l, l) for l in lpb]
        pinv = [pi + mm(pi.astype(BF16), l.astype(BF16)) for pi, l in zip(pinv, lp)]

    st = [st_ref[p] for p in pairs]
    s_kv = [s.T.astype(BF16) for s in st]
    pre = [mm(jnp.concatenate([xa[p], l_ak[p]], axis=1), jnp.concatenate([s_kv[p], vst[p]], axis=0))
           for p in pairs]
    ustb = [jnp.where(valid, mm(pinv[p].astype(BF16), pre[p].astype(BF16)), 0.0).astype(BF16) for p in pairs]
    for p in pairs:
        yst = mm(jnp.concatenate([xr[p], m_rb[p], m_rk[p]], axis=1),
                 jnp.concatenate([s_kv[p], ustb[p], vst[p]], axis=0))
        yst = jnp.where(valid, yst, 0.0)
        y_ref[:, p * LANES:(p + 1) * LANES] = yst[0:C] + yst[C:]
    for p in pairs:
        st_ref[p] = st[p] * wtots[p] + mm_ta(jnp.concatenate([ustb[p], vst[p]], axis=0),
                                             jnp.concatenate([bws[p], kws[p]], axis=0))


def wkv(r, v, kk, lw2, k2, ag2, *, batch, seq):
    n = r.shape[0]
    C = WKV_CHUNK
    nc = seq // C

    def rows(b, d, c):
        return b * nc + jnp.where(d == 1, nc - 1 - c, c)

    shared = pl.BlockSpec((C, RW_DIM), lambda b, d, c: (rows(b, d, c), 0))
    perdir = pl.BlockSpec((C, RW_DIM), lambda b, d, c: (rows(b, d, c), d))
    return pl.pallas_call(
        _wkv_body,
        grid=(batch, 2, nc),
        in_specs=[shared, shared, shared, perdir, perdir, perdir],
        out_specs=perdir,
        out_shape=jax.ShapeDtypeStruct((n, 2 * RW_DIM), F32),
        scratch_shapes=[pltpu.VMEM((RW_DIM // LANES, LANES, LANES), F32)],
        compiler_params=_cparams("parallel", "parallel", "arbitrary"),
        name="wkv_scan",
    )(r, v, kk, lw2, k2, ag2)


def _rw_post_body(y2_ref, r_ref, k2_ref, v_ref, g_ref, vec_ref, bd_ref, o_ref):
    vec = vec_ref[...]
    lnx_w, lnx_b, r_k = (vec[i:i + 1, :] for i in range(3))
    bd = bd_ref[...]
    y = y2_ref[:, 0:RW_DIM] + y2_ref[:, RW_DIM:]
    inv = 1.0 / RW_HEAD
    mean = _head_sums(y, bd) * inv
    d = y - mean
    var = _head_sums(d * d, bd) * inv
    yn = d * lax.rsqrt(var + LNX_EPS) * lnx_w + lnx_b
    ksum = k2_ref[:, 0:RW_DIM] + k2_ref[:, RW_DIM:]
    bonus = _head_sums(r_ref[...] * ksum * r_k, bd) * v_ref[...]
    o_ref[...] = ((yn + bonus) * g_ref[...]).astype(o_ref.dtype)


def rwkv_post(y2, r, k2, v, g, vec, bd, *, tm):
    n = r.shape[0]
    row = lambda w: pl.BlockSpec((tm, w), lambda i: (i, 0))
    full = lambda a: pl.BlockSpec(a.shape, lambda i: (0,) * a.ndim)
    return pl.pallas_call(
        _rw_post_body,
        grid=(n // tm,),
        in_specs=[row(2 * RW_DIM), row(RW_DIM), row(2 * RW_DIM), row(RW_DIM), row(RW_DIM), full(vec), full(bd)],
        out_specs=row(RW_DIM),
        out_shape=jax.ShapeDtypeStruct((n, RW_DIM), BF16),
        compiler_params=_cparams("parallel"),
        name="rwkv_post",
    )(y2, r, k2, v, g, vec, bd)


def _merge_body(o_ref, yg_ref, ga_ref, gr_ref, wa_ref, wr_ref, m_ref):
    attn = jnp.dot(o_ref[...], wa_ref[...], preferred_element_type=F32)
    rw = jnp.dot(yg_ref[...], wr_ref[...], preferred_element_type=F32)
    m_ref[...] = (_sigmoid(ga_ref[...].astype(F32)) * attn + _sigmoid(gr_ref[...].astype(F32)) * rw).astype(m_ref.dtype)


def merge_branches(o, yg, gates, wa, wr, *, tm, tn):
    n = o.shape[0]
    nj = D_MODEL // tn
    return pl.pallas_call(
        _merge_body,
        grid=(n // tm, nj),
        in_specs=[pl.BlockSpec((tm, MLA_HEADS * V_DIM), lambda i, j: (i, 0)),
                  pl.BlockSpec((tm, RW_DIM), lambda i, j: (i, 0)),
                  pl.BlockSpec((tm, tn), lambda i, j: (i, j)),
                  pl.BlockSpec((tm, tn), lambda i, j: (i, nj + j)),
                  pl.BlockSpec((MLA_HEADS * V_DIM, tn), lambda i, j: (0, j)),
                  pl.BlockSpec((RW_DIM, tn), lambda i, j: (0, j))],
        out_specs=pl.BlockSpec((tm, tn), lambda i, j: (i, j)),
        out_shape=jax.ShapeDtypeStruct((n, D_MODEL), BF16),
        compiler_params=_cparams("parallel", "parallel"),
        name="merge_branches",
    )(o, yg, gates, gates, wa, wr)


def _mm_res_body(a_ref, w_ref, x_ref, o_ref):
    o_ref[...] = x_ref[...] + jnp.dot(a_ref[...], w_ref[...], preferred_element_type=F32)


def matmul_residual(a, w, x, *, tm, tn):
    n, k = a.shape
    nc = w.shape[1]
    return pl.pallas_call(
        _mm_res_body,
        grid=(n // tm, nc // tn),
        in_specs=[pl.BlockSpec((tm, k), lambda i, j: (i, 0)),
                  pl.BlockSpec((k, tn), lambda i, j: (0, j)),
                  pl.BlockSpec((tm, tn), lambda i, j: (i, j))],
        out_specs=pl.BlockSpec((tm, tn), lambda i, j: (i, j)),
        out_shape=jax.ShapeDtypeStruct((n, nc), F32),
        compiler_params=_cparams("parallel", "parallel"),
        name="matmul_residual",
    )(a, w, x)


def _top16_rows(ss, idx):
    big = 3.0e38
    vals = [[] for _ in ss]
    poss = [[] for _ in ss]
    for _ in range(PEER_TOPK):
        ms = [jnp.max(s, axis=0, keepdims=True) for s in ss]
        ps = [jnp.min(jnp.where(s == m, idx, big), axis=0, keepdims=True) for s, m in zip(ss, ms)]
        ss = [jnp.where(idx == p, -jnp.inf, s) for s, p in zip(ss, ps)]
        for i, (m, p) in enumerate(zip(ms, ps)):
            vals[i].append(m)
            poss[i].append(p)
    return [(jnp.concatenate(v, axis=0), jnp.concatenate(p, axis=0)) for v, p in zip(vals, poss)]


def _candidates(sv0, sv1):
    tok = sv0.shape[1]
    io = lambda rows: lax.broadcasted_iota(jnp.int32, (rows, tok), 0).astype(F32)
    half = PEER_TOPK // 2
    parts = [sv0[0:1, :] + sv1]
    idxs = [io(PEER_TOPK)]
    for a in range(1, half):
        parts.append(sv0[a:a + 1, :] + sv1[0:half, :])
        idxs.append(io(half) + float(a * PEER_TOPK))
    parts.append(sv0[half:, :] + sv1[0:1, :])
    idxs.append((io(half) + float(half)) * float(PEER_TOPK))
    return jnp.concatenate(parts, axis=0), jnp.concatenate(idxs, axis=0)


def _pick_rows(table, sel):
    out = jnp.zeros(sel.shape, table.dtype)
    for a in range(PEER_TOPK):
        out = jnp.where(sel == a, table[a:a + 1, :], out)
    return out


def _peer_topk_body(q_ref, sk_ref, i1_ref, i2_ref, gt_ref):
    i1s, i2s, gts = [], [], []
    key_idx = lax.broadcasted_iota(jnp.int32, (N_KEYS, q_ref.shape[0]), 0).astype(F32)
    for h in range(PEER_HEADS):
        ss = []
        for p in range(2):
            c0 = (2 * h + p) * PEER_HALF
            qb = q_ref[:, c0:c0 + PEER_HALF].astype(BF16)
            ss.append(lax.dot_general(sk_ref[p], qb, _TB, preferred_element_type=F32))
        (sv0, si0), (sv1, si1) = _top16_rows(ss, key_idx)
        cand, cand_idx = _candidates(sv0, sv1)
        ((top_s, top_j),) = _top16_rows([cand], cand_idx)
        ja = jnp.floor(top_j * (1.0 / PEER_TOPK))
        i1s.append(_pick_rows(si0, ja))
        i2s.append(_pick_rows(si1, top_j - ja * PEER_TOPK))
        e = jnp.exp(top_s - top_s[0:1, :])
        gts.append(e / jnp.sum(e, axis=0, keepdims=True))
    i1_ref[...] = jnp.concatenate(i1s, axis=0).T
    i2_ref[...] = jnp.concatenate(i2s, axis=0).T
    gt_ref[...] = jnp.concatenate(gts, axis=0).T


def peer_topk(q, sub_keys):
    n = q.shape[0]
    tmk = LANES
    out = pl.BlockSpec((tmk, NSLOT), lambda i: (i, 0))
    return pl.pallas_call(
        _peer_topk_body,
        grid=(n // tmk,),
        in_specs=[pl.BlockSpec((tmk, 2 * PEER_HALF * PEER_HEADS), lambda i: (i, 0)),
                  pl.BlockSpec(sub_keys.shape, lambda i: (0, 0, 0))],
        out_specs=[out, out, out],
        out_shape=[jax.ShapeDtypeStruct((n, NSLOT), F32)] * 3,
        compiler_params=_cparams("parallel"),
        name="peer_topk",
    )(q, sub_keys)


def _gelu_exact(x):
    return 0.5 * x * (1.0 + lax.erf(x * (1.0 / math.sqrt(2.0))))


_HI16 = 0xFFFF0000
G_UNROLL = SUBLANES
E_SUB = 2 * N_KEYS
PEER_TE = 2 * E_SUB


def _bf16_bits(x):
    return pltpu.bitcast(x, jnp.uint32) + jnp.uint32(0x8000)


def _peer_dense_body(xn_ref, i1_ref, i2_ref, gt_ref, ut_ref, v_ref, x_ref, lnf_ref, o_ref, g_ref,
                     *, tm, te, final_norm):
    e = pl.program_id(1)
    half = tm // 2
    mm_tb = lambda a, b: lax.dot_general(a, b, _TB, preferred_element_type=F32)

    @pl.when(e == 0)
    def _():
        o_ref[...] = x_ref[...]
        sub = lax.broadcasted_iota(jnp.int32, (N_KEYS, NSLOT), 0).astype(F32).astype(BF16)
        one = jnp.ones((), BF16)
        zero = jnp.zeros((), BF16)

        def build(it, carry):
            base = pl.multiple_of(it * G_UNROLL, SUBLANES)
            tiles = [(ref[pl.ds(base, G_UNROLL), :], ref[pl.ds(half + base, G_UNROLL), :])
                     for ref in (i1_ref, i2_ref, gt_ref)]
            a_ts, b_ts = [], []
            for j in range(G_UNROLL):
                for t in range(2):
                    i1, i2, gt = (tiles[q][t][j:j + 1, :].astype(BF16) for q in range(3))
                    a_ts.append(jnp.where(sub == i1, one, zero))
                    b_ts.append(jnp.where(sub == i2, gt, zero))
            gs = [mm_tb(a, b) for a, b in zip(a_ts, b_ts)]
            for j in range(G_UNROLL):
                word = (_bf16_bits(gs[2 * j]) & jnp.uint32(_HI16)) | (_bf16_bits(gs[2 * j + 1]) >> 16)
                g_ref[pl.ds(pl.multiple_of((base + j) * G_PITCH, SUBLANES), N_KEYS), :] = word
            return carry

        lax.fori_loop(0, half // G_UNROLL, build, 0)

    xn = xn_ref[...]
    nsub = te // E_SUB
    hs = [jnp.dot(xn, ut_ref[:, j * E_SUB:(j + 1) * E_SUB], preferred_element_type=F32) for j in range(nsub)]
    coefs = []
    for j in range(nsub):
        row0 = e * (te // N_KEYS) + j * (E_SUB // N_KEYS)
        words = [g_ref[pl.ds(row0 + r, half, stride=G_PITCH), :] for r in range(E_SUB // N_KEYS)]
        top = jnp.concatenate([pltpu.bitcast(w & jnp.uint32(_HI16), F32) for w in words], axis=1)
        bot = jnp.concatenate([pltpu.bitcast(w << 16, F32) for w in words], axis=1)
        gate = jnp.concatenate([top, bot], axis=0)
        coefs.append((gate * _gelu_exact(hs[j])).astype(BF16))
    o_ref[...] += jnp.dot(jnp.concatenate(coefs, axis=1), v_ref[...], preferred_element_type=F32)

    if final_norm:
        @pl.when(e == pl.num_programs(1) - 1)
        def _():
            o_ref[...] = _rms_rows(o_ref[...], lnf_ref[...], EPS)


def peer_dense(xn, i1, i2, gt, ut, v, x, ln_f, *, tm, te, final_norm):
    n = xn.shape[0]
    ne = v.shape[0] // te
    assert tm % (2 * G_UNROLL) == 0 and te % E_SUB == 0 and ut.shape == (ne, D_MODEL, te)
    row = lambda w, **kw: pl.BlockSpec((tm, w), lambda i, e: (i, 0), **kw)
    once = dict(pipeline_mode=pl.Buffered(1))
    return pl.pallas_call(
        functools.partial(_peer_dense_body, tm=tm, te=te, final_norm=final_norm),
        grid=(n // tm, ne),
        in_specs=[row(D_MODEL, **once), row(NSLOT), row(NSLOT), row(NSLOT),
                  pl.BlockSpec((None, D_MODEL, te), lambda i, e: (e, 0, 0)),
                  pl.BlockSpec((te, D_MODEL), lambda i, e: (e, 0)),
                  row(D_MODEL, **once),
                  pl.BlockSpec((1, D_MODEL), lambda i, e: (0, 0))],
        out_specs=row(D_MODEL),
        out_shape=jax.ShapeDtypeStruct((n, D_MODEL), F32),
        scratch_shapes=[pltpu.VMEM((tm // 2 * G_PITCH, N_KEYS), jnp.uint32)],
        compiler_params=_cparams("parallel", "arbitrary"),
        name="peer_dense",
    )(xn, i1, i2, gt, ut, v, x, ln_f.reshape(1, D_MODEL))


def _pad_cols(a, w):
    return jnp.pad(a, ((0, 0), (0, w - a.shape[1])))


def _pad_rows(a, h):
    return jnp.pad(a, ((0, h - a.shape[0]),) + ((0, 0),) * (a.ndim - 1))


def _swap_halves(a):
    half = a.shape[-1] // 2
    return jnp.concatenate([a[..., half:], a[..., :half]], axis=-1)


def _layer_weights(i, p):
    first = i == 0
    w_in = p["w_in_first"] if first else p["w_in_rest"][i - 1]
    mu = p["mu_first"] if first else p["mu_rest"][i - 1]
    mla_cols = Q_LORA + KV_LORA + ROPE_DIM
    gate0 = mla_cols
    rw0 = mla_cols + 2 * D_MODEL
    k_rope_w = w_in[:, Q_LORA + KV_LORA:mla_cols]
    w_mla = jnp.concatenate([w_in[:, :mla_cols], _swap_halves(k_rope_w)], axis=1)
    w_gate = w_in[:, gate0:rw0]

    def rw_layout(a):
        c = 3 * RW_DIM
        dwf, dwb = a[..., c:c + DECAY_LORA], a[..., c + DECAY_LORA:c + 2 * DECAY_LORA]
        c += 2 * DECAY_LORA
        daf, dab = a[..., c:c + AAA_LORA], a[..., c + AAA_LORA:c + 2 * AAA_LORA]
        c += 2 * AAA_LORA
        dg = a[..., c:c + GATE_LORA]
        c += GATE_LORA
        dv = a[..., c:]
        z = lambda w: jnp.zeros(a.shape[:-1] + (w,), a.dtype)
        return jnp.concatenate([a[..., :3 * RW_DIM], dwf, dwb, daf, dab, dg, z(2 * LANES - GATE_LORA),
                                dv, z(2 * LANES - dv.shape[-1])], axis=-1)

    w_rw = rw_layout(w_in[:, rw0:])
    mu_l = rw_layout(mu[None, :])
    zeros64 = jnp.zeros((DECAY_LORA, RW_DIM), F32)
    wq = p["w_uq"][i].reshape(Q_LORA, MLA_HEADS, NOPE_DIM + ROPE_DIM)
    wq = jnp.concatenate([wq, _swap_halves(wq[..., NOPE_DIM:])], axis=-1).reshape(Q_LORA, MLA_HEADS * QK_PAD)
    vec = jnp.stack([p["w0_f"][i], p["w0_b"][i], p["a0_f"][i], p["a0_b"][i], p["k_k"][i], p["k_a"][i],
                     p["v0_rest"][i - 1] if not first else jnp.zeros((RW_DIM,), F32),
                     jnp.zeros((RW_DIM,), F32)])
    lw = {
        "ln1": p["ln1"][i], "w_mla": w_mla.astype(BF16), "w_gate": w_gate.astype(BF16),
        "w_rw": w_rw.astype(BF16),
        "mu": mu_l, "q_norm": p["q_norm"][i], "wq": wq.astype(BF16), "kv_norm": p["kv_norm"][i],
        "wkv": p["w_ukv"][i].astype(BF16), "w_o_attn": p["w_o_attn"][i].astype(BF16),
        "vec": vec,
        "w2f": jnp.concatenate([p["w2_f"][i], zeros64]).astype(BF16),
        "w2b": jnp.concatenate([zeros64, p["w2_b"][i]]).astype(BF16),
        "a2f": jnp.concatenate([p["a2_f"][i], zeros64]).astype(BF16),
        "a2b": jnp.concatenate([zeros64, p["a2_b"][i]]).astype(BF16),
        "g2": _pad_rows(p["g2"][i], 2 * LANES).astype(BF16),
        "v2": None if first else _pad_rows(p["v2_rest"][i - 1], LANES).astype(BF16),
        "post_vec": jnp.stack([p["lnx_w"][i], p["lnx_b"][i], p["r_k"][i].reshape(RW_DIM)]
                              + [jnp.zeros((RW_DIM,), F32)] * 5),
        "w_o_rwkv": p["w_o_rwkv"][i].astype(BF16), "w_out": p["w_out"][i].astype(BF16),
        "ln2": p["ln2"][i], "w_pq": p["w_pq"][i].astype(BF16), "sub_keys": p["sub_keys"][i].astype(BF16),
        "peer_ut": p["peer_u"][i].astype(BF16).reshape(-1, PEER_TE, D_MODEL).transpose(0, 2, 1),
        "peer_v": p["peer_v"][i].astype(BF16),
    }
    return lw


def _rope_table(seq):
    inv = 1.0 / (ROPE_THETA ** (jnp.arange(0, ROPE_DIM, 2, dtype=F32) / ROPE_DIM))
    ang = jnp.arange(seq, dtype=F32)[:, None] * inv[None, :]
    c, s = jnp.cos(ang), jnp.sin(ang)
    return jnp.concatenate([c, c, -s, s], axis=1)


def _tiles(seq):
    return dict(tm=min(512, seq), tmm=min(1024, seq), tq=min(8 * ATTN_SUB, seq), tprep=min(256, seq),
                tpeer=min(512, seq))


def _trunk(x, layers, ln_f, bd):
    batch, seq, _ = x.shape
    n = batch * seq
    t = _tiles(seq)
    tm, tmm = t["tm"], t["tmm"]
    ccss = _rope_table(seq)
    x = x.reshape(n, D_MODEL)
    v_first = None
    for li, lw in enumerate(layers):
        xn1 = rms_norm(x, lw["ln1"], tm=tm)
        h_mla = matmul(xn1, lw["w_mla"], tm=tmm, tn=MLA_W, out_dtype=BF16)
        gates = matmul(xn1, lw["w_gate"], tm=tmm, tn=1024, out_dtype=BF16)
        h_rw = matmul(xn1, lw["w_rw"], tm=tmm, tn=1280, out_dtype=F32)
        q = mla_q_proj(h_mla, lw["q_norm"], lw["wq"], ccss, seq=seq, tm=tm)
        k, v = mla_kv_proj(h_mla, lw["kv_norm"], lw["wkv"], ccss, seq=seq, tm=tm)
        o = attention(q, k, v, batch=batch, seq=seq, tq=t["tq"])
        r, vv, kk, ag2, lw2, k2, g = rwkv_prep(h_rw, lw["mu"], lw["vec"], lw["w2f"], lw["w2b"], lw["a2f"],
                                              lw["a2b"], lw["g2"], bd, lw["v2"], v_first,
                                              seq=seq, tm=t["tprep"])
        if v_first is None:
            v_first = vv
        y2 = wkv(r, vv, kk, lw2, k2, ag2, batch=batch, seq=seq)
        yg = rwkv_post(y2, r, k2, vv, g, lw["post_vec"], bd, tm=t["tprep"])
        m = merge_branches(o, yg, gates, lw["w_o_attn"], lw["w_o_rwkv"], tm=tmm, tn=512)
        x = matmul_residual(m, lw["w_out"], x, tm=tmm, tn=1024)
        xn = rms_norm(x, lw["ln2"], tm=tm)
        qp = matmul(xn, lw["w_pq"], tm=tmm, tn=1024, out_dtype=F32)
        i1, i2, gt = peer_topk(qp, lw["sub_keys"])
        x = peer_dense(xn, i1, i2, gt, lw["peer_ut"], lw["peer_v"], x, ln_f, tm=t["tpeer"], te=PEER_TE,
                       final_norm=li == len(layers) - 1)
    return x.reshape(batch, seq, D_MODEL)


def kernel(x_prompt, x_sample, ln1, w_in_first, mu_first, w_in_rest, mu_rest, q_norm, w_uq, kv_norm, w_ukv, w_o_attn, w0_f, w2_f, w0_b, w2_b, a0_f, a2_f, a0_b, a2_b, g2, k_k, k_a, r_k, lnx_w, lnx_b, v0_rest, v2_rest, w_o_rwkv, w_out, ln2, w_pq, sub_keys, peer_u, peer_v, ln_f):
    p = dict(ln1=ln1, w_in_first=w_in_first, mu_first=mu_first, w_in_rest=w_in_rest, mu_rest=mu_rest,
             q_norm=q_norm, w_uq=w_uq, kv_norm=kv_norm, w_ukv=w_ukv, w_o_attn=w_o_attn, w0_f=w0_f, w2_f=w2_f,
             w0_b=w0_b, w2_b=w2_b, a0_f=a0_f, a2_f=a2_f, a0_b=a0_b, a2_b=a2_b, g2=g2, k_k=k_k, k_a=k_a, r_k=r_k,
             lnx_w=lnx_w, lnx_b=lnx_b, v0_rest=v0_rest, v2_rest=v2_rest, w_o_rwkv=w_o_rwkv, w_out=w_out,
             ln2=ln2, w_pq=w_pq, sub_keys=sub_keys, peer_u=peer_u, peer_v=peer_v)
    depth = ln1.shape[0]
    layers = [_layer_weights(i, p) for i in range(depth)]
    lane = jnp.arange(LANES)
    bd = (lane[:, None] // RW_HEAD == lane[None, :] // RW_HEAD).astype(F32)
    return (_trunk(x_prompt, layers, ln_f, bd), _trunk(x_sample, layers, ln_f, bd))
```

```python
import functools
import math

import jax
import jax.numpy as jnp
from jax import lax
from jax.experimental import pallas as pl
from jax.experimental.pallas import tpu as pltpu

F32 = jnp.float32
BF16 = jnp.bfloat16
FP8 = jnp.float8_e4m3fn
FP8_MAX = 448.0
HIGHEST = lax.Precision.HIGHEST

LANES = 128
SUBLANES = 8
VMEM_LIMIT_BYTES = 56 * 1024 * 1024

D_MODEL = 2048
MLA_HEADS = 16
Q_LORA = 512
KV_LORA = 512
NOPE_DIM = 128
ROPE_DIM = 64
V_DIM = 128
ROPE_THETA = 10000.0
RW_HEAD = 64
RW_HEADS = 16
RW_DIM = RW_HEADS * RW_HEAD
DECAY_LORA = 64
AAA_LORA = 64
MV_LORA = 32
GATE_LORA = 160
LNX_EPS = 64e-5
PEER_HEADS = 8
N_KEYS = 128
PEER_HALF = 128
PEER_TOPK = 16
EPS = 1e-6

QK_PAD = 2 * LANES
MLA_W = Q_LORA + KV_LORA + LANES
LORA_W = 2 * DECAY_LORA + 2 * AAA_LORA + 2 * LANES + 2 * LANES
RW_W = 3 * RW_DIM + LORA_W
PROJ_HEADS = 4
ATTN_SUB = 256
WKV_CHUNK = 64
G_PITCH = N_KEYS + SUBLANES
NSLOT = PEER_HEADS * PEER_TOPK

_TB = (((1,), (1,)), ((), ()))
_TA = (((0,), (0,)), ((), ()))


def _cparams(*sem):
    return pltpu.CompilerParams(dimension_semantics=sem, vmem_limit_bytes=VMEM_LIMIT_BYTES)


def _rms_rows(x, g, eps):
    ms = jnp.mean(x * x, axis=-1, keepdims=True)
    return x * lax.rsqrt(ms + eps) * g


def _rms_norm_body(x_ref, g_ref, *o_refs):
    x = x_ref[...].astype(F32)
    xhat = x * lax.rsqrt(jnp.mean(x * x, axis=-1, keepdims=True) + EPS)
    for row, o_ref in enumerate(o_refs):
        o_ref[...] = (xhat * g_ref[row:row + 1, :]).astype(o_ref.dtype)


def rms_norm(x, gains, out_dtypes, *, tm):
    n, k = x.shape
    spec = pl.BlockSpec((tm, k), lambda i: (i, 0))
    return pl.pallas_call(
        _rms_norm_body,
        grid=(n // tm,),
        in_specs=[spec, pl.BlockSpec(gains.shape, lambda i: (0, 0))],
        out_specs=[spec] * len(out_dtypes),
        out_shape=[jax.ShapeDtypeStruct((n, k), dt) for dt in out_dtypes],
        compiler_params=_cparams("parallel"),
        name="rms_norm",
    )(x, gains)


def _mm_body(a_ref, w_ref, o_ref):
    o_ref[...] = jnp.dot(a_ref[...], w_ref[...], preferred_element_type=F32).astype(o_ref.dtype)


def matmul(a, w, *, tm, tn, out_dtype):
    n, k = a.shape
    nc = w.shape[1]
    tn = min(tn, nc)
    assert n % tm == 0 and nc % tn == 0 and w.shape[0] == k
    return pl.pallas_call(
        _mm_body,
        grid=(n // tm, nc // tn),
        in_specs=[pl.BlockSpec((tm, k), lambda i, j: (i, 0)),
                  pl.BlockSpec((k, tn), lambda i, j: (0, j))],
        out_specs=pl.BlockSpec((tm, tn), lambda i, j: (i, j)),
        out_shape=jax.ShapeDtypeStruct((n, nc), out_dtype),
        compiler_params=_cparams("parallel", "parallel"),
        name="matmul",
    )(a, w)


def _rope_lanes(y2, ccss):
    w = y2 * ccss
    return w + pltpu.roll(w, ROPE_DIM, axis=1)


def _qproj_body(c_ref, g_ref, w_ref, t_ref, q_ref, cn_ref, *, scale):
    @pl.when(pl.program_id(1) == 0)
    def _():
        cn_ref[...] = _rms_rows(c_ref[...].astype(F32), g_ref[...], EPS).astype(BF16)

    y = jnp.dot(cn_ref[...], w_ref[...], preferred_element_type=F32)
    tab = t_ref[...]
    lane = lax.broadcasted_iota(jnp.int32, tab.shape, 1)
    for hh in range(PROJ_HEADS):
        c0 = hh * QK_PAD
        q_ref[:, c0:c0 + LANES] = (y[:, c0:c0 + LANES] * scale).astype(q_ref.dtype)
        rot = _rope_lanes(y[:, c0 + LANES:c0 + QK_PAD], tab)
        q_ref[:, c0 + LANES:c0 + QK_PAD] = jnp.where(lane < ROPE_DIM, rot * scale, 0.0).astype(q_ref.dtype)


def mla_q_proj(h_mla, q_norm, wq, ccss, *, seq, tm):
    n = h_mla.shape[0]
    nt = seq // tm
    scale = (NOPE_DIM + ROPE_DIM) ** -0.5
    return pl.pallas_call(
        functools.partial(_qproj_body, scale=scale),
        grid=(n // tm, MLA_HEADS // PROJ_HEADS),
        in_specs=[pl.BlockSpec((tm, Q_LORA), lambda i, h: (i, 0)),
                  pl.BlockSpec((1, Q_LORA), lambda i, h: (0, 0)),
                  pl.BlockSpec((Q_LORA, PROJ_HEADS * QK_PAD), lambda i, h: (0, h)),
                  pl.BlockSpec((tm, LANES), lambda i, h: (i % nt, 0))],
        out_specs=pl.BlockSpec((tm, PROJ_HEADS * QK_PAD), lambda i, h: (i, h)),
        out_shape=jax.ShapeDtypeStruct((n, MLA_HEADS * QK_PAD), BF16),
        scratch_shapes=[pltpu.VMEM((tm, Q_LORA), BF16)],
        compiler_params=_cparams("parallel", "arbitrary"),
        name="mla_q_proj",
    )(h_mla, q_norm.reshape(1, Q_LORA), wq, ccss)


def _kvproj_body(c_ref, kr_ref, g_ref, w_ref, t_ref, k_ref, v_ref, cn_ref, krr_ref):
    @pl.when(pl.program_id(1) == 0)
    def _():
        cn_ref[...] = _rms_rows(c_ref[...].astype(F32), g_ref[...], EPS).astype(BF16)
        krr_ref[...] = _rope_lanes(kr_ref[...].astype(F32), t_ref[...]).astype(BF16)

    y = jnp.dot(cn_ref[...], w_ref[...], preferred_element_type=F32)
    for hh in range(PROJ_HEADS):
        c0 = hh * (NOPE_DIM + V_DIM)
        k_ref[:, hh * QK_PAD:hh * QK_PAD + LANES] = y[:, c0:c0 + NOPE_DIM].astype(k_ref.dtype)
        k_ref[:, hh * QK_PAD + LANES:(hh + 1) * QK_PAD] = krr_ref[...]
        v_ref[:, hh * V_DIM:(hh + 1) * V_DIM] = y[:, c0 + NOPE_DIM:c0 + NOPE_DIM + V_DIM].astype(v_ref.dtype)


def mla_kv_proj(h_mla, kv_norm, wkv, ccss, *, seq, tm):
    n = h_mla.shape[0]
    nt = seq // tm
    return pl.pallas_call(
        _kvproj_body,
        grid=(n // tm, MLA_HEADS // PROJ_HEADS),
        in_specs=[pl.BlockSpec((tm, KV_LORA), lambda i, h: (i, 1)),
                  pl.BlockSpec((tm, LANES), lambda i, h: (i, (Q_LORA + KV_LORA) // LANES)),
                  pl.BlockSpec((1, KV_LORA), lambda i, h: (0, 0)),
                  pl.BlockSpec((KV_LORA, PROJ_HEADS * (NOPE_DIM + V_DIM)), lambda i, h: (0, h)),
                  pl.BlockSpec((tm, LANES), lambda i, h: (i % nt, 0))],
        out_specs=[pl.BlockSpec((tm, PROJ_HEADS * QK_PAD), lambda i, h: (i, h)),
                   pl.BlockSpec((tm, PROJ_HEADS * V_DIM), lambda i, h: (i, h))],
        out_shape=[jax.ShapeDtypeStruct((n, MLA_HEADS * QK_PAD), BF16),
                   jax.ShapeDtypeStruct((n, MLA_HEADS * V_DIM), BF16)],
        scratch_shapes=[pltpu.VMEM((tm, KV_LORA), BF16), pltpu.VMEM((tm, LANES), BF16)],
        compiler_params=_cparams("parallel", "arbitrary"),
        name="mla_kv_proj",
    )(h_mla, h_mla, kv_norm.reshape(1, KV_LORA), wkv, ccss)


def _attn_body(q_ref, k_ref, v_ref, o_ref, *, nsub):
    ts = q_ref.shape[0] // nsub
    k = k_ref[...]
    v = v_ref[...]

    def scores(j):
        return lax.dot_general(q_ref[j * ts:(j + 1) * ts, :], k, _TB, preferred_element_type=F32)

    def finish(s, j):
        m = jnp.max(s, axis=-1, keepdims=True)
        p = jnp.exp(s - m)
        l = jnp.sum(p, axis=-1, keepdims=True)
        o = jnp.dot(p.astype(BF16), v, preferred_element_type=F32)
        o_ref[j * ts:(j + 1) * ts, :] = (o / l).astype(o_ref.dtype)

    s_prev = scores(0)
    for j in range(1, nsub):
        s_next = scores(j)
        finish(s_prev, j - 1)
        s_prev = s_next
    finish(s_prev, nsub - 1)


def attention(q, k, v, *, batch, seq, tq):
    n = q.shape[0]
    nq = seq // tq
    return pl.pallas_call(
        functools.partial(_attn_body, nsub=tq // ATTN_SUB),
        grid=(batch, MLA_HEADS, nq),
        in_specs=[pl.BlockSpec((tq, QK_PAD), lambda b, h, i: (b * nq + i, h)),
                  pl.BlockSpec((seq, QK_PAD), lambda b, h, i: (b, h)),
                  pl.BlockSpec((seq, V_DIM), lambda b, h, i: (b, h))],
        out_specs=pl.BlockSpec((tq, V_DIM), lambda b, h, i: (b * nq + i, h)),
        out_shape=jax.ShapeDtypeStruct((n, MLA_HEADS * V_DIM), BF16),
        compiler_params=_cparams("parallel", "parallel", "arbitrary"),
        name="mla_attention",
    )(q, k, v)


def _head_sums(x, bd):
    parts = []
    for gi in range(RW_DIM // LANES):
        parts.append(jnp.dot(x[:, gi * LANES:(gi + 1) * LANES], bd, precision=HIGHEST,
                             preferred_element_type=F32))
    return jnp.concatenate(parts, axis=1)


def _sigmoid(x):
    return 1.0 / (1.0 + jnp.exp(-x))


def _rw_prep_body(*refs, tiles_per_seq, tm, has_vfirst):
    (h_ref, hp_ref, hn_ref, mu_ref, vec_ref, w2f_ref, w2b_ref, a2f_ref, a2b_ref, g2_ref, bd_ref) = refs[:11]
    pos = 11
    if has_vfirst:
        v2_ref, vf_ref = refs[pos:pos + 2]
        pos += 2
    r_ref, v_ref, kk_ref, ag_ref, lw_ref, k2_ref, g_ref = refs[pos:pos + 7]

    ti = pl.program_id(0) % tiles_per_seq
    keep_prev = jnp.where(ti == 0, 0.0, 1.0)
    keep_next = jnp.where(ti == tiles_per_seq - 1, 0.0, 1.0)
    row = lax.broadcasted_iota(jnp.int32, (tm, 1), 0)

    def shifted(c0, c1):
        x = h_ref[:, c0:c1]
        prev = jnp.where(row == 0, hp_ref[SUBLANES - 1:SUBLANES, c0:c1] * keep_prev, pltpu.roll(x, 1, axis=0))
        nxt = jnp.where(row == tm - 1, hn_ref[0:1, c0:c1] * keep_next, pltpu.roll(x, tm - 1, axis=0))
        return x + (0.5 * (prev + nxt) - x) * mu_ref[:, c0:c1]

    vec = vec_ref[...]
    w0f, w0b, a0f, a0b, k_k, k_a, v0 = (vec[i:i + 1, :] for i in range(7))

    r = shifted(0, RW_DIM)
    k = shifted(RW_DIM, 2 * RW_DIM)
    v = shifted(2 * RW_DIM, 3 * RW_DIM)
    lo = shifted(3 * RW_DIM, RW_W)
    dw = jnp.tanh(lo[:, 0:LANES]).astype(BF16)
    da = lo[:, LANES:2 * LANES].astype(BF16)
    dg = _sigmoid(lo[:, 2 * LANES:4 * LANES]).astype(BF16)

    def decay(dw_half, w0, w2_ref):
        z = w0 + jnp.dot(dw_half, w2_ref[...], preferred_element_type=F32)
        nz = -z
        softplus = jnp.maximum(nz, 0.0) + jnp.log(1.0 + jnp.exp(-jnp.abs(nz)))
        return -jnp.exp(-softplus - 0.5)

    lw_ref[:, 0:RW_DIM] = decay(dw, w0f, w2f_ref)
    lw_ref[:, RW_DIM:] = decay(dw, w0b, w2b_ref)
    af = _sigmoid(a0f + jnp.dot(da, a2f_ref[...], preferred_element_type=F32))
    ab = _sigmoid(a0b + jnp.dot(da, a2b_ref[...], preferred_element_type=F32))
    ag_ref[:, 0:RW_DIM] = af
    ag_ref[:, RW_DIM:] = ab
    g_ref[...] = jnp.dot(dg, g2_ref[...], preferred_element_type=F32)

    if has_vfirst:
        dv = lo[:, 4 * LANES:5 * LANES].astype(BF16)
        mix = _sigmoid(v0 + jnp.dot(dv, v2_ref[...], preferred_element_type=F32))
        v = v + (vf_ref[...] - v) * mix
    v_ref[...] = v
    r_ref[...] = r

    kk = k * k_k
    ss = _head_sums(kk * kk, bd_ref[...])
    kk_ref[...] = kk * lax.rsqrt(jnp.maximum(ss, 1e-24))
    k2_ref[:, 0:RW_DIM] = k * (1.0 + (af - 1.0) * k_a)
    k2_ref[:, RW_DIM:] = k * (1.0 + (ab - 1.0) * k_a)


def rwkv_prep(h_rw, mu, vec, w2f, w2b, a2f, a2b, g2, bd, v2, v_first, *, seq, tm):
    n = h_rw.shape[0]
    tps = seq // tm
    nb8 = n // SUBLANES
    has_vfirst = v_first is not None
    full = lambda a: pl.BlockSpec(a.shape, lambda i: (0,) * a.ndim)
    in_specs = [pl.BlockSpec((tm, RW_W), lambda i: (i, 0)),
                pl.BlockSpec((SUBLANES, RW_W), lambda i: (jnp.maximum(i * (tm // SUBLANES) - 1, 0), 0)),
                pl.BlockSpec((SUBLANES, RW_W), lambda i: (jnp.minimum((i + 1) * (tm // SUBLANES), nb8 - 1), 0)),
                full(mu), full(vec), full(w2f), full(w2b), full(a2f), full(a2b), full(g2), full(bd)]
    args = [h_rw, h_rw, h_rw, mu, vec, w2f, w2b, a2f, a2b, g2, bd]
    if has_vfirst:
        in_specs += [full(v2), pl.BlockSpec((tm, RW_DIM), lambda i: (i, 0))]
        args += [v2, v_first]
    row = lambda w: pl.BlockSpec((tm, w), lambda i: (i, 0))
    widths = [RW_DIM, RW_DIM, RW_DIM, 2 * RW_DIM, 2 * RW_DIM, 2 * RW_DIM, RW_DIM]
    return pl.pallas_call(
        functools.partial(_rw_prep_body, tiles_per_seq=tps, tm=tm, has_vfirst=has_vfirst),
        grid=(n // tm,),
        in_specs=in_specs,
        out_specs=[row(w) for w in widths],
        out_shape=[jax.ShapeDtypeStruct((n, w), F32) for w in widths],
        compiler_params=_cparams("parallel"),
        name="rwkv_prep",
    )(*args)


def _wkv_body(r_ref, v_ref, kk_ref, lw_ref, k_ref, ag_ref, y_ref, st_ref):
    C = WKV_CHUNK
    rev = pl.program_id(1) == 1

    @pl.when(pl.program_id(2) == 0)
    def _():
        st_ref[...] = jnp.zeros_like(st_ref)

    ti = lax.broadcasted_iota(jnp.int32, (C, C), 0)
    si = lax.broadcasted_iota(jnp.int32, (C, C), 1)
    sgn = jnp.where(rev, -1, 1)
    tri = jnp.where((ti - si) * sgn >= 0, 1.0, 0.0)
    lw_all = lw_ref[...]
    cum_all = jnp.dot(tri, lw_all, precision=HIGHEST, preferred_element_type=F32)
    tot_all = jnp.sum(lw_all, axis=0, keepdims=True)

    lane = lax.broadcasted_iota(jnp.int32, (1, LANES), 1)
    hm0 = jnp.where(lane < RW_HEAD, 1.0, 0.0)
    hm1 = 1.0 - hm0
    stack = lambda x: jnp.concatenate([x * hm0, x * hm1], axis=0).astype(BF16)
    dup = lambda x: jnp.concatenate([x, x], axis=0).astype(BF16)
    ri = lax.broadcasted_iota(jnp.int32, (2 * C, 2 * C), 0)
    ci = lax.broadcasted_iota(jnp.int32, (2 * C, 2 * C), 1)
    same = (ri & C) == (ci & C)
    dts = ((ri & (C - 1)) - (ci & (C - 1))) * sgn
    strict = same & (dts > 0)
    incl = same & (dts >= 0)
    eye = jnp.where(ri == ci, 1.0, 0.0)
    valid = (lax.broadcasted_iota(jnp.int32, (2 * C, LANES), 0) & C) == (
        lax.broadcasted_iota(jnp.int32, (2 * C, LANES), 1) & RW_HEAD)

    pairs = range(RW_DIM // LANES)
    mm = functools.partial(jnp.dot, preferred_element_type=F32)
    mm_tb = lambda a, b: lax.dot_general(a, b, _TB, preferred_element_type=F32)
    mm_ta = lambda a, b: lax.dot_general(a, b, _TA, preferred_element_type=F32)
    xa, xr, vst, bws, kws, wtots, gms = [], [], [], [], [], [], []
    for p in pairs:
        ls = slice(p * LANES, (p + 1) * LANES)
        r, v, kk, k, ag = (x[:, ls] for x in (r_ref, v_ref, kk_ref, k_ref, ag_ref))
        lw, cum, tot = lw_all[:, ls], cum_all[:, ls], tot_all[:, ls]
        ei = jnp.exp(-cum)
        ew = jnp.exp(tot - cum)
        b = kk * ag
        xa.append(stack(-kk * jnp.exp(cum - lw)))
        xr.append(stack(r * jnp.exp(cum)))
        vst.append(stack(v))
        bws.append(stack(b * ew))
        kws.append(stack(k * ew))
        wtots.append(jnp.exp(tot))
        x2 = jnp.concatenate([xa[p], xr[p]], axis=0)
        y2 = jnp.concatenate([dup(b * ei), dup(k * ei)], axis=0)
        gms.append(mm_tb(x2, y2))
    l_ab = [jnp.where(strict, g[0:2 * C, 0:2 * C], 0.0) for g in gms]
    l_ak = [jnp.where(strict, g[0:2 * C, 2 * C:], 0.0).astype(BF16) for g in gms]
    m_rb = [jnp.where(incl, g[2 * C:, 0:2 * C], 0.0).astype(BF16) for g in gms]
    m_rk = [jnp.where(incl, g[2 * C:, 2 * C:], 0.0).astype(BF16) for g in gms]

    pinv = [eye + l for l in l_ab]
    lp = l_ab
    for _ in range(int(math.log2(C)) - 1):
        lpb = [l.astype(BF16) for l in lp]
        lp = [mm(l, l) for l in lpb]
        pinv = [pi + mm(pi.astype(BF16), l.astype(BF16)) for pi, l in zip(pinv, lp)]

    st = [st_ref[p] for p in pairs]
    s_kv = [s.T.astype(BF16) for s in st]
    pre = [mm(jnp.concatenate([xa[p], l_ak[p]], axis=1), jnp.concatenate([s_kv[p], vst[p]], axis=0))
           for p in pairs]
    ustb = [jnp.where(valid, mm(pinv[p].astype(BF16), pre[p].astype(BF16)), 0.0).astype(BF16) for p in pairs]
    for p in pairs:
        yst = mm(jnp.concatenate([xr[p], m_rb[p], m_rk[p]], axis=1),
                 jnp.concatenate([s_kv[p], ustb[p], vst[p]], axis=0))
        yst = jnp.where(valid, yst, 0.0)
        y_ref[:, p * LANES:(p + 1) * LANES] = yst[0:C] + yst[C:]
    for p in pairs:
        st_ref[p] = st[p] * wtots[p] + mm_ta(jnp.concatenate([ustb[p], vst[p]], axis=0),
                                             jnp.concatenate([bws[p], kws[p]], axis=0))


def wkv(r, v, kk, lw2, k2, ag2, *, batch, seq):
    n = r.shape[0]
    C = WKV_CHUNK
    nc = seq // C

    def rows(b, d, c):
        return b * nc + jnp.where(d == 1, nc - 1 - c, c)

    shared = pl.BlockSpec((C, RW_DIM), lambda b, d, c: (rows(b, d, c), 0))
    perdir = pl.BlockSpec((C, RW_DIM), lambda b, d, c: (rows(b, d, c), d))
    return pl.pallas_call(
        _wkv_body,
        grid=(batch, 2, nc),
        in_specs=[shared, shared, shared, perdir, perdir, perdir],
        out_specs=perdir,
        out_shape=jax.ShapeDtypeStruct((n, 2 * RW_DIM), F32),
        scratch_shapes=[pltpu.VMEM((RW_DIM // LANES, LANES, LANES), F32)],
        compiler_params=_cparams("parallel", "parallel", "arbitrary"),
        name="wkv_scan",
    )(r, v, kk, lw2, k2, ag2)


def _rw_post_body(y2_ref, r_ref, k2_ref, v_ref, g_ref, vec_ref, bd_ref, o_ref):
    vec = vec_ref[...]
    lnx_w, lnx_b, r_k = (vec[i:i + 1, :] for i in range(3))
    bd = bd_ref[...]
    y = y2_ref[:, 0:RW_DIM] + y2_ref[:, RW_DIM:]
    inv = 1.0 / RW_HEAD
    mean = _head_sums(y, bd) * inv
    d = y - mean
    var = _head_sums(d * d, bd) * inv
    yn = d * lax.rsqrt(var + LNX_EPS) * lnx_w + lnx_b
    ksum = k2_ref[:, 0:RW_DIM] + k2_ref[:, RW_DIM:]
    bonus = _head_sums(r_ref[...] * ksum * r_k, bd) * v_ref[...]
    o_ref[...] = ((yn + bonus) * g_ref[...]).astype(o_ref.dtype)


def rwkv_post(y2, r, k2, v, g, vec, bd, *, tm):
    n = r.shape[0]
    row = lambda w: pl.BlockSpec((tm, w), lambda i: (i, 0))
    full = lambda a: pl.BlockSpec(a.shape, lambda i: (0,) * a.ndim)
    return pl.pallas_call(
        _rw_post_body,
        grid=(n // tm,),
        in_specs=[row(2 * RW_DIM), row(RW_DIM), row(2 * RW_DIM), row(RW_DIM), row(RW_DIM), full(vec), full(bd)],
        out_specs=row(RW_DIM),
        out_shape=jax.ShapeDtypeStruct((n, RW_DIM), BF16),
        compiler_params=_cparams("parallel"),
        name="rwkv_post",
    )(y2, r, k2, v, g, vec, bd)


def _merge_body(o_ref, yg_ref, ga_ref, gr_ref, wa_ref, wr_ref, m_ref):
    attn = jnp.dot(o_ref[...], wa_ref[...], preferred_element_type=F32)
    rw = jnp.dot(yg_ref[...], wr_ref[...], preferred_element_type=F32)
    m_ref[...] = (_sigmoid(ga_ref[...].astype(F32)) * attn + _sigmoid(gr_ref[...].astype(F32)) * rw).astype(m_ref.dtype)


def merge_branches(o, yg, gates, wa, wr, *, tm, tn):
    n = o.shape[0]
    nj = D_MODEL // tn
    return pl.pallas_call(
        _merge_body,
        grid=(n // tm, nj),
        in_specs=[pl.BlockSpec((tm, MLA_HEADS * V_DIM), lambda i, j: (i, 0)),
                  pl.BlockSpec((tm, RW_DIM), lambda i, j: (i, 0)),
                  pl.BlockSpec((tm, tn), lambda i, j: (i, j)),
                  pl.BlockSpec((tm, tn), lambda i, j: (i, nj + j)),
                  pl.BlockSpec((MLA_HEADS * V_DIM, tn), lambda i, j: (0, j)),
                  pl.BlockSpec((RW_DIM, tn), lambda i, j: (0, j))],
        out_specs=pl.BlockSpec((tm, tn), lambda i, j: (i, j)),
        out_shape=jax.ShapeDtypeStruct((n, D_MODEL), BF16),
        compiler_params=_cparams("parallel", "parallel"),
        name="merge_branches",
    )(o, yg, gates, gates, wa, wr)


def _mm_res_body(a_ref, w_ref, x_ref, o_ref):
    o_ref[...] = x_ref[...] + jnp.dot(a_ref[...], w_ref[...], preferred_element_type=F32)


def matmul_residual(a, w, x, *, tm, tn):
    n, k = a.shape
    nc = w.shape[1]
    return pl.pallas_call(
        _mm_res_body,
        grid=(n // tm, nc // tn),
        in_specs=[pl.BlockSpec((tm, k), lambda i, j: (i, 0)),
                  pl.BlockSpec((k, tn), lambda i, j: (0, j)),
                  pl.BlockSpec((tm, tn), lambda i, j: (i, j))],
        out_specs=pl.BlockSpec((tm, tn), lambda i, j: (i, j)),
        out_shape=jax.ShapeDtypeStruct((n, nc), F32),
        compiler_params=_cparams("parallel", "parallel"),
        name="matmul_residual",
    )(a, w, x)


def _top16_rows(ss, idx):
    big = 3.0e38
    vals = [[] for _ in ss]
    poss = [[] for _ in ss]
    for _ in range(PEER_TOPK):
        ms = [jnp.max(s, axis=0, keepdims=True) for s in ss]
        ps = [jnp.min(jnp.where(s == m, idx, big), axis=0, keepdims=True) for s, m in zip(ss, ms)]
        ss = [jnp.where(idx == p, -jnp.inf, s) for s, p in zip(ss, ps)]
        for i, (m, p) in enumerate(zip(ms, ps)):
            vals[i].append(m)
            poss[i].append(p)
    return [(jnp.concatenate(v, axis=0), jnp.concatenate(p, axis=0)) for v, p in zip(vals, poss)]


def _candidates(sv0, sv1):
    tok = sv0.shape[1]
    io = lambda rows: lax.broadcasted_iota(jnp.int32, (rows, tok), 0).astype(F32)
    half = PEER_TOPK // 2
    parts = [sv0[0:1, :] + sv1]
    idxs = [io(PEER_TOPK)]
    for a in range(1, half):
        parts.append(sv0[a:a + 1, :] + sv1[0:half, :])
        idxs.append(io(half) + float(a * PEER_TOPK))
    parts.append(sv0[half:, :] + sv1[0:1, :])
    idxs.append((io(half) + float(half)) * float(PEER_TOPK))
    return jnp.concatenate(parts, axis=0), jnp.concatenate(idxs, axis=0)


def _pick_rows(table, sel):
    out = jnp.zeros(sel.shape, table.dtype)
    for a in range(PEER_TOPK):
        out = jnp.where(sel == a, table[a:a + 1, :], out)
    return out


def _peer_topk_body(q_ref, sk_ref, i1_ref, i2_ref, gt_ref):
    i1s, i2s, gts = [], [], []
    key_idx = lax.broadcasted_iota(jnp.int32, (N_KEYS, q_ref.shape[0]), 0).astype(F32)
    for h in range(PEER_HEADS):
        ss = []
        for p in range(2):
            c0 = (2 * h + p) * PEER_HALF
            qb = q_ref[:, c0:c0 + PEER_HALF].astype(BF16)
            ss.append(lax.dot_general(sk_ref[p], qb, _TB, preferred_element_type=F32))
        (sv0, si0), (sv1, si1) = _top16_rows(ss, key_idx)
        cand, cand_idx = _candidates(sv0, sv1)
        ((top_s, top_j),) = _top16_rows([cand], cand_idx)
        ja = jnp.floor(top_j * (1.0 / PEER_TOPK))
        i1s.append(_pick_rows(si0, ja))
        i2s.append(_pick_rows(si1, top_j - ja * PEER_TOPK))
        e = jnp.exp(top_s - top_s[0:1, :])
        gts.append(e / jnp.sum(e, axis=0, keepdims=True))
    i1_ref[...] = jnp.concatenate(i1s, axis=0).T
    i2_ref[...] = jnp.concatenate(i2s, axis=0).T
    gt_ref[...] = jnp.concatenate(gts, axis=0).T


def peer_topk(q, sub_keys):
    n = q.shape[0]
    tmk = LANES
    out = pl.BlockSpec((tmk, NSLOT), lambda i: (i, 0))
    return pl.pallas_call(
        _peer_topk_body,
        grid=(n // tmk,),
        in_specs=[pl.BlockSpec((tmk, 2 * PEER_HALF * PEER_HEADS), lambda i: (i, 0)),
                  pl.BlockSpec(sub_keys.shape, lambda i: (0, 0, 0))],
        out_specs=[out, out, out],
        out_shape=[jax.ShapeDtypeStruct((n, NSLOT), F32)] * 3,
        compiler_params=_cparams("parallel"),
        name="peer_topk",
    )(q, sub_keys)


def _gelu_exact(x):
    return 0.5 * x * (1.0 + lax.erf(x * (1.0 / math.sqrt(2.0))))


_HI16 = 0xFFFF0000
G_UNROLL = SUBLANES
E_SUB = 2 * N_KEYS
PEER_TE = 2 * E_SUB


def _bf16_bits(x):
    return pltpu.bitcast(x, jnp.uint32) + jnp.uint32(0x8000)


def _peer_dense_body(xn_ref, i1_ref, i2_ref, gt_ref, ut_ref, v_ref, x_ref, lnf_ref, hsc_ref, o_ref, g_ref,
                     *, tm, te, final_norm):
    e = pl.program_id(1)
    half = tm // 2
    mm_tb = lambda a, b: lax.dot_general(a, b, _TB, preferred_element_type=F32)

    @pl.when(e == 0)
    def _():
        o_ref[...] = x_ref[...]
        sub = lax.broadcasted_iota(jnp.int32, (N_KEYS, NSLOT), 0).astype(F32).astype(BF16)
        one = jnp.ones((), BF16)
        zero = jnp.zeros((), BF16)

        def build(it, carry):
            base = pl.multiple_of(it * G_UNROLL, SUBLANES)
            tiles = [(ref[pl.ds(base, G_UNROLL), :], ref[pl.ds(half + base, G_UNROLL), :])
                     for ref in (i1_ref, i2_ref, gt_ref)]
            a_ts, b_ts = [], []
            for j in range(G_UNROLL):
                for t in range(2):
                    i1, i2, gt = (tiles[q][t][j:j + 1, :].astype(BF16) for q in range(3))
                    a_ts.append(jnp.where(sub == i1, one, zero))
                    b_ts.append(jnp.where(sub == i2, gt, zero))
            gs = [mm_tb(a, b) for a, b in zip(a_ts, b_ts)]
            for j in range(G_UNROLL):
                word = (_bf16_bits(gs[2 * j]) & jnp.uint32(_HI16)) | (_bf16_bits(gs[2 * j + 1]) >> 16)
                g_ref[pl.ds(pl.multiple_of((base + j) * G_PITCH, SUBLANES), N_KEYS), :] = word
            return carry

        lax.fori_loop(0, half // G_UNROLL, build, 0)

    xn = xn_ref[...]
    nsub = te // E_SUB
    hs = [jnp.dot(xn, ut_ref[:, j * E_SUB:(j + 1) * E_SUB], preferred_element_type=F32) * hsc_ref[...]
          for j in range(nsub)]
    coefs = []
    for j in range(nsub):
        row0 = e * (te // N_KEYS) + j * (E_SUB // N_KEYS)
        words = [g_ref[pl.ds(row0 + r, half, stride=G_PITCH), :] for r in range(E_SUB // N_KEYS)]
        top = jnp.concatenate([pltpu.bitcast(w & jnp.uint32(_HI16), F32) for w in words], axis=1)
        bot = jnp.concatenate([pltpu.bitcast(w << 16, F32) for w in words], axis=1)
        gate = jnp.concatenate([top, bot], axis=0)
        coefs.append((gate * _gelu_exact(hs[j])).astype(BF16))
    o_ref[...] += jnp.dot(jnp.concatenate(coefs, axis=1), v_ref[...], preferred_element_type=F32)

    if final_norm:
        @pl.when(e == pl.num_programs(1) - 1)
        def _():
            o_ref[...] = _rms_rows(o_ref[...], lnf_ref[...], EPS)


def peer_dense(xn, i1, i2, gt, ut, v, x, ln_f, hscale, *, tm, te, final_norm):
    n = xn.shape[0]
    ne = v.shape[0] // te
    assert tm % (2 * G_UNROLL) == 0 and te % E_SUB == 0 and ut.shape == (ne, D_MODEL, te)
    row = lambda w, **kw: pl.BlockSpec((tm, w), lambda i, e: (i, 0), **kw)
    once = dict(pipeline_mode=pl.Buffered(1))
    return pl.pallas_call(
        functools.partial(_peer_dense_body, tm=tm, te=te, final_norm=final_norm),
        grid=(n // tm, ne),
        in_specs=[row(D_MODEL, **once), row(NSLOT), row(NSLOT), row(NSLOT),
                  pl.BlockSpec((None, D_MODEL, te), lambda i, e: (e, 0, 0)),
                  pl.BlockSpec((te, D_MODEL), lambda i, e: (e, 0)),
                  row(D_MODEL, **once),
                  pl.BlockSpec((1, D_MODEL), lambda i, e: (0, 0)),
                  pl.BlockSpec((1, E_SUB), lambda i, e: (0, 0))],
        out_specs=row(D_MODEL),
        out_shape=jax.ShapeDtypeStruct((n, D_MODEL), F32),
        scratch_shapes=[pltpu.VMEM((tm // 2 * G_PITCH, N_KEYS), jnp.uint32)],
        compiler_params=_cparams("parallel", "arbitrary"),
        name="peer_dense",
    )(xn, i1, i2, gt, ut, v, x, ln_f.reshape(1, D_MODEL), jnp.broadcast_to(hscale, (1, E_SUB)).astype(F32))


def _pad_cols(a, w):
    return jnp.pad(a, ((0, 0), (0, w - a.shape[1])))


def _pad_rows(a, h):
    return jnp.pad(a, ((0, h - a.shape[0]),) + ((0, 0),) * (a.ndim - 1))


def _swap_halves(a):
    half = a.shape[-1] // 2
    return jnp.concatenate([a[..., half:], a[..., :half]], axis=-1)


def _layer_weights(i, p):
    first = i == 0
    w_in = p["w_in_first"] if first else p["w_in_rest"][i - 1]
    mu = p["mu_first"] if first else p["mu_rest"][i - 1]
    mla_cols = Q_LORA + KV_LORA + ROPE_DIM
    gate0 = mla_cols
    rw0 = mla_cols + 2 * D_MODEL
    k_rope_w = w_in[:, Q_LORA + KV_LORA:mla_cols]
    w_mla = jnp.concatenate([w_in[:, :mla_cols], _swap_halves(k_rope_w)], axis=1)
    w_gate = w_in[:, gate0:rw0]

    def rw_layout(a):
        c = 3 * RW_DIM
        dwf, dwb = a[..., c:c + DECAY_LORA], a[..., c + DECAY_LORA:c + 2 * DECAY_LORA]
        c += 2 * DECAY_LORA
        daf, dab = a[..., c:c + AAA_LORA], a[..., c + AAA_LORA:c + 2 * AAA_LORA]
        c += 2 * AAA_LORA
        dg = a[..., c:c + GATE_LORA]
        c += GATE_LORA
        dv = a[..., c:]
        z = lambda w: jnp.zeros(a.shape[:-1] + (w,), a.dtype)
        return jnp.concatenate([a[..., :3 * RW_DIM], dwf, dwb, daf, dab, dg, z(2 * LANES - GATE_LORA),
                                dv, z(2 * LANES - dv.shape[-1])], axis=-1)

    w_rw = rw_layout(w_in[:, rw0:])
    mu_l = rw_layout(mu[None, :])
    zeros64 = jnp.zeros((DECAY_LORA, RW_DIM), F32)
    wq = p["w_uq"][i].reshape(Q_LORA, MLA_HEADS, NOPE_DIM + ROPE_DIM)
    wq = jnp.concatenate([wq, _swap_halves(wq[..., NOPE_DIM:])], axis=-1).reshape(Q_LORA, MLA_HEADS * QK_PAD)
    vec = jnp.stack([p["w0_f"][i], p["w0_b"][i], p["a0_f"][i], p["a0_b"][i], p["k_k"][i], p["k_a"][i],
                     p["v0_rest"][i - 1] if not first else jnp.zeros((RW_DIM,), F32),
                     jnp.zeros((RW_DIM,), F32)])
    lw = {
        "ln1": p["ln1"][i], "w_mla": w_mla.astype(BF16), "w_gate": w_gate.astype(BF16),
        "w_rw": w_rw.astype(BF16),
        "mu": mu_l, "q_norm": p["q_norm"][i], "wq": wq.astype(BF16), "kv_norm": p["kv_norm"][i],
        "wkv": p["w_ukv"][i].astype(BF16), "w_o_attn": p["w_o_attn"][i].astype(BF16),
        "vec": vec,
        "w2f": jnp.concatenate([p["w2_f"][i], zeros64]).astype(BF16),
        "w2b": jnp.concatenate([zeros64, p["w2_b"][i]]).astype(BF16),
        "a2f": jnp.concatenate([p["a2_f"][i], zeros64]).astype(BF16),
        "a2b": jnp.concatenate([zeros64, p["a2_b"][i]]).astype(BF16),
        "g2": _pad_rows(p["g2"][i], 2 * LANES).astype(BF16),
        "v2": None if first else _pad_rows(p["v2_rest"][i - 1], LANES).astype(BF16),
        "post_vec": jnp.stack([p["lnx_w"][i], p["lnx_b"][i], p["r_k"][i].reshape(RW_DIM)]
                              + [jnp.zeros((RW_DIM,), F32)] * 5),
        "w_o_rwkv": p["w_o_rwkv"][i].astype(BF16), "w_out": p["w_out"][i].astype(BF16),
        "ln2": p["ln2"][i], "w_pq": p["w_pq"][i].astype(BF16), "sub_keys": p["sub_keys"][i].astype(BF16),
        "peer_v": p["peer_v"][i].astype(BF16),
    }
    pow2_below = lambda v: jnp.exp2(jnp.floor(jnp.log2(v)))
    sx = pow2_below(FP8_MAX / (math.sqrt(D_MODEL) * jnp.maximum(jnp.max(jnp.abs(p["ln2"][i])), 1e-30)))
    su = pow2_below(FP8_MAX / jnp.maximum(jnp.max(jnp.abs(p["peer_u"][i])), 1e-30))
    lw["ln2_rows"] = jnp.stack([p["ln2"][i], p["ln2"][i] * sx])
    lw["peer_ut"] = (p["peer_u"][i] * su).astype(FP8).reshape(-1, PEER_TE, D_MODEL).transpose(0, 2, 1)
    lw["peer_hscale"] = 1.0 / (sx * su)
    return lw


def _rope_table(seq):
    inv = 1.0 / (ROPE_THETA ** (jnp.arange(0, ROPE_DIM, 2, dtype=F32) / ROPE_DIM))
    ang = jnp.arange(seq, dtype=F32)[:, None] * inv[None, :]
    c, s = jnp.cos(ang), jnp.sin(ang)
    return jnp.concatenate([c, c, -s, s], axis=1)


def _tiles(seq):
    return dict(tm=min(512, seq), tmm=min(1024, seq), tq=min(8 * ATTN_SUB, seq), tprep=min(256, seq),
                tpeer=min(512, seq))


def _trunk(x, layers, ln_f, bd):
    batch, seq, _ = x.shape
    n = batch * seq
    t = _tiles(seq)
    tm, tmm = t["tm"], t["tmm"]
    ccss = _rope_table(seq)
    x = x.reshape(n, D_MODEL)
    v_first = None
    for li, lw in enumerate(layers):
        (xn1,) = rms_norm(x, lw["ln1"].reshape(1, D_MODEL), (BF16,), tm=tm)
        h_mla = matmul(xn1, lw["w_mla"], tm=tmm, tn=MLA_W, out_dtype=BF16)
        gates = matmul(xn1, lw["w_gate"], tm=tmm, tn=1024, out_dtype=BF16)
        h_rw = matmul(xn1, lw["w_rw"], tm=tmm, tn=1280, out_dtype=F32)
        q = mla_q_proj(h_mla, lw["q_norm"], lw["wq"], ccss, seq=seq, tm=tm)
        k, v = mla_kv_proj(h_mla, lw["kv_norm"], lw["wkv"], ccss, seq=seq, tm=tm)
        o = attention(q, k, v, batch=batch, seq=seq, tq=t["tq"])
        r, vv, kk, ag2, lw2, k2, g = rwkv_prep(h_rw, lw["mu"], lw["vec"], lw["w2f"], lw["w2b"], lw["a2f"],
                                              lw["a2b"], lw["g2"], bd, lw["v2"], v_first,
                                              seq=seq, tm=t["tprep"])
        if v_first is None:
            v_first = vv
        y2 = wkv(r, vv, kk, lw2, k2, ag2, batch=batch, seq=seq)
        yg = rwkv_post(y2, r, k2, vv, g, lw["post_vec"], bd, tm=t["tprep"])
        m = merge_branches(o, yg, gates, lw["w_o_attn"], lw["w_o_rwkv"], tm=tmm, tn=512)
        x = matmul_residual(m, lw["w_out"], x, tm=tmm, tn=1024)
        xn, xn8 = rms_norm(x, lw["ln2_rows"], (BF16, FP8), tm=tm)
        qp = matmul(xn, lw["w_pq"], tm=tmm, tn=1024, out_dtype=F32)
        i1, i2, gt = peer_topk(qp, lw["sub_keys"])
        x = peer_dense(xn8, i1, i2, gt, lw["peer_ut"], lw["peer_v"], x, ln_f, lw["peer_hscale"],
                       tm=t["tpeer"], te=PEER_TE,
                       final_norm=li == len(layers) - 1)
    return x.reshape(batch, seq, D_MODEL)


def kernel(x_prompt, x_sample, ln1, w_in_first, mu_first, w_in_rest, mu_rest, q_norm, w_uq, kv_norm, w_ukv, w_o_attn, w0_f, w2_f, w0_b, w2_b, a0_f, a2_f, a0_b, a2_b, g2, k_k, k_a, r_k, lnx_w, lnx_b, v0_rest, v2_rest, w_o_rwkv, w_out, ln2, w_pq, sub_keys, peer_u, peer_v, ln_f):
    p = dict(ln1=ln1, w_in_first=w_in_first, mu_first=mu_first, w_in_rest=w_in_rest, mu_rest=mu_rest,
             q_norm=q_norm, w_uq=w_uq, kv_norm=kv_norm, w_ukv=w_ukv, w_o_attn=w_o_attn, w0_f=w0_f, w2_f=w2_f,
             w0_b=w0_b, w2_b=w2_b, a0_f=a0_f, a2_f=a2_f, a0_b=a0_b, a2_b=a2_b, g2=g2, k_k=k_k, k_a=k_a, r_k=r_k,
             lnx_w=lnx_w, lnx_b=lnx_b, v0_rest=v0_rest, v2_rest=v2_rest, w_o_rwkv=w_o_rwkv, w_out=w_out,
             ln2=ln2, w_pq=w_pq, sub_keys=sub_keys, peer_u=peer_u, peer_v=peer_v)
    depth = ln1.shape[0]
    layers = [_layer_weights(i, p) for i in range(depth)]
    lane = jnp.arange(LANES)
    bd = (lane[:, None] // RW_HEAD == lane[None, :] // RW_HEAD).astype(F32)
    return (_trunk(x_prompt, layers, ln_f, bd), _trunk(x_sample, layers, ln_f, bd))
```

```python
import functools
import math

import jax
import jax.numpy as jnp
from jax import lax
from jax.experimental import pallas as pl
from jax.experimental.pallas import tpu as pltpu

F32 = jnp.float32
BF16 = jnp.bfloat16
FP8 = jnp.float8_e4m3fn
FP8_MAX = 448.0
HIGHEST = lax.Precision.HIGHEST

LANES = 128
SUBLANES = 8
VMEM_LIMIT_BYTES = 56 * 1024 * 1024

D_MODEL = 2048
MLA_HEADS = 16
Q_LORA = 512
KV_LORA = 512
NOPE_DIM = 128
ROPE_DIM = 64
V_DIM = 128
ROPE_THETA = 10000.0
RW_HEAD = 64
RW_HEADS = 16
RW_DIM = RW_HEADS * RW_HEAD
DECAY_LORA = 64
AAA_LORA = 64
MV_LORA = 32
GATE_LORA = 160
LNX_EPS = 64e-5
PEER_HEADS = 8
N_KEYS = 128
PEER_HALF = 128
PEER_TOPK = 16
EPS = 1e-6

QK_PAD = 2 * LANES
MLA_W = Q_LORA + KV_LORA + LANES
LORA_W = 2 * DECAY_LORA + 2 * AAA_LORA + 2 * LANES + 2 * LANES
RW_W = 3 * RW_DIM + LORA_W
PROJ_HEADS = 4
ATTN_SUB = 256
WKV_CHUNK = 64
G_PITCH = N_KEYS + SUBLANES
NSLOT = PEER_HEADS * PEER_TOPK

_TB = (((1,), (1,)), ((), ()))
_TA = (((0,), (0,)), ((), ()))


def _cparams(*sem):
    return pltpu.CompilerParams(dimension_semantics=sem, vmem_limit_bytes=VMEM_LIMIT_BYTES)


def _rms_rows(x, g, eps):
    ms = jnp.mean(x * x, axis=-1, keepdims=True)
    return x * lax.rsqrt(ms + eps) * g


def _rms_norm_body(x_ref, g_ref, *o_refs):
    x = x_ref[...].astype(F32)
    xhat = x * lax.rsqrt(jnp.mean(x * x, axis=-1, keepdims=True) + EPS)
    for row, o_ref in enumerate(o_refs):
        o_ref[...] = (xhat * g_ref[row:row + 1, :]).astype(o_ref.dtype)


def rms_norm(x, gains, out_dtypes, *, tm):
    n, k = x.shape
    spec = pl.BlockSpec((tm, k), lambda i: (i, 0))
    return pl.pallas_call(
        _rms_norm_body,
        grid=(n // tm,),
        in_specs=[spec, pl.BlockSpec(gains.shape, lambda i: (0, 0))],
        out_specs=[spec] * len(out_dtypes),
        out_shape=[jax.ShapeDtypeStruct((n, k), dt) for dt in out_dtypes],
        compiler_params=_cparams("parallel"),
        name="rms_norm",
    )(x, gains)


def _mm_body(a_ref, w_ref, o_ref):
    o_ref[...] = jnp.dot(a_ref[...], w_ref[...], preferred_element_type=F32).astype(o_ref.dtype)


def matmul(a, w, *, tm, tn, out_dtype):
    n, k = a.shape
    nc = w.shape[1]
    tn = min(tn, nc)
    assert n % tm == 0 and nc % tn == 0 and w.shape[0] == k
    return pl.pallas_call(
        _mm_body,
        grid=(n // tm, nc // tn),
        in_specs=[pl.BlockSpec((tm, k), lambda i, j: (i, 0)),
                  pl.BlockSpec((k, tn), lambda i, j: (0, j))],
        out_specs=pl.BlockSpec((tm, tn), lambda i, j: (i, j)),
        out_shape=jax.ShapeDtypeStruct((n, nc), out_dtype),
        compiler_params=_cparams("parallel", "parallel"),
        name="matmul",
    )(a, w)


def _rope_lanes(y2, ccss):
    w = y2 * ccss
    return w + pltpu.roll(w, ROPE_DIM, axis=1)


def _qproj_body(c_ref, g_ref, w_ref, t_ref, q_ref, cn_ref, *, scale):
    @pl.when(pl.program_id(1) == 0)
    def _():
        cn_ref[...] = _rms_rows(c_ref[...].astype(F32), g_ref[...], EPS).astype(BF16)

    y = jnp.dot(cn_ref[...], w_ref[...], preferred_element_type=F32)
    tab = t_ref[...]
    lane = lax.broadcasted_iota(jnp.int32, tab.shape, 1)
    for hh in range(PROJ_HEADS):
        c0 = hh * QK_PAD
        q_ref[:, c0:c0 + LANES] = (y[:, c0:c0 + LANES] * scale).astype(q_ref.dtype)
        rot = _rope_lanes(y[:, c0 + LANES:c0 + QK_PAD], tab)
        q_ref[:, c0 + LANES:c0 + QK_PAD] = jnp.where(lane < ROPE_DIM, rot * scale, 0.0).astype(q_ref.dtype)


def mla_q_proj(h_mla, q_norm, wq, ccss, *, seq, tm):
    n = h_mla.shape[0]
    nt = seq // tm
    scale = (NOPE_DIM + ROPE_DIM) ** -0.5
    return pl.pallas_call(
        functools.partial(_qproj_body, scale=scale),
        grid=(n // tm, MLA_HEADS // PROJ_HEADS),
        in_specs=[pl.BlockSpec((tm, Q_LORA), lambda i, h: (i, 0)),
                  pl.BlockSpec((1, Q_LORA), lambda i, h: (0, 0)),
                  pl.BlockSpec((Q_LORA, PROJ_HEADS * QK_PAD), lambda i, h: (0, h)),
                  pl.BlockSpec((tm, LANES), lambda i, h: (i % nt, 0))],
        out_specs=pl.BlockSpec((tm, PROJ_HEADS * QK_PAD), lambda i, h: (i, h)),
        out_shape=jax.ShapeDtypeStruct((n, MLA_HEADS * QK_PAD), BF16),
        scratch_shapes=[pltpu.VMEM((tm, Q_LORA), BF16)],
        compiler_params=_cparams("parallel", "arbitrary"),
        name="mla_q_proj",
    )(h_mla, q_norm.reshape(1, Q_LORA), wq, ccss)


def _kvproj_body(c_ref, kr_ref, g_ref, w_ref, t_ref, k_ref, v_ref, cn_ref, krr_ref):
    @pl.when(pl.program_id(1) == 0)
    def _():
        cn_ref[...] = _rms_rows(c_ref[...].astype(F32), g_ref[...], EPS).astype(BF16)
        krr_ref[...] = _rope_lanes(kr_ref[...].astype(F32), t_ref[...]).astype(BF16)

    y = jnp.dot(cn_ref[...], w_ref[...], preferred_element_type=F32)
    for hh in range(PROJ_HEADS):
        c0 = hh * (NOPE_DIM + V_DIM)
        k_ref[:, hh * QK_PAD:hh * QK_PAD + LANES] = y[:, c0:c0 + NOPE_DIM].astype(k_ref.dtype)
        k_ref[:, hh * QK_PAD + LANES:(hh + 1) * QK_PAD] = krr_ref[...]
        v_ref[:, hh * V_DIM:(hh + 1) * V_DIM] = y[:, c0 + NOPE_DIM:c0 + NOPE_DIM + V_DIM].astype(v_ref.dtype)


def mla_kv_proj(h_mla, kv_norm, wkv, ccss, *, seq, tm):
    n = h_mla.shape[0]
    nt = seq // tm
    return pl.pallas_call(
        _kvproj_body,
        grid=(n // tm, MLA_HEADS // PROJ_HEADS),
        in_specs=[pl.BlockSpec((tm, KV_LORA), lambda i, h: (i, 1)),
                  pl.BlockSpec((tm, LANES), lambda i, h: (i, (Q_LORA + KV_LORA) // LANES)),
                  pl.BlockSpec((1, KV_LORA), lambda i, h: (0, 0)),
                  pl.BlockSpec((KV_LORA, PROJ_HEADS * (NOPE_DIM + V_DIM)), lambda i, h: (0, h)),
                  pl.BlockSpec((tm, LANES), lambda i, h: (i % nt, 0))],
        out_specs=[pl.BlockSpec((tm, PROJ_HEADS * QK_PAD), lambda i, h: (i, h)),
                   pl.BlockSpec((tm, PROJ_HEADS * V_DIM), lambda i, h: (i, h))],
        out_shape=[jax.ShapeDtypeStruct((n, MLA_HEADS * QK_PAD), BF16),
                   jax.ShapeDtypeStruct((n, MLA_HEADS * V_DIM), BF16)],
        scratch_shapes=[pltpu.VMEM((tm, KV_LORA), BF16), pltpu.VMEM((tm, LANES), BF16)],
        compiler_params=_cparams("parallel", "arbitrary"),
        name="mla_kv_proj",
    )(h_mla, h_mla, kv_norm.reshape(1, KV_LORA), wkv, ccss)


def _attn_body(q_ref, k_ref, v_ref, o_ref, *, nsub):
    ts = q_ref.shape[0] // nsub
    k = k_ref[...]
    v = v_ref[...]

    def scores(j):
        return lax.dot_general(q_ref[j * ts:(j + 1) * ts, :], k, _TB, preferred_element_type=F32)

    def finish(s, j):
        m = jnp.max(s, axis=-1, keepdims=True)
        p = jnp.exp(s - m)
        l = jnp.sum(p, axis=-1, keepdims=True)
        o = jnp.dot(p.astype(BF16), v, preferred_element_type=F32)
        o_ref[j * ts:(j + 1) * ts, :] = (o / l).astype(o_ref.dtype)

    s_prev = scores(0)
    for j in range(1, nsub):
        s_next = scores(j)
        finish(s_prev, j - 1)
        s_prev = s_next
    finish(s_prev, nsub - 1)


def attention(q, k, v, *, batch, seq, tq):
    n = q.shape[0]
    nq = seq // tq
    return pl.pallas_call(
        functools.partial(_attn_body, nsub=tq // ATTN_SUB),
        grid=(batch, MLA_HEADS, nq),
        in_specs=[pl.BlockSpec((tq, QK_PAD), lambda b, h, i: (b * nq + i, h)),
                  pl.BlockSpec((seq, QK_PAD), lambda b, h, i: (b, h)),
                  pl.BlockSpec((seq, V_DIM), lambda b, h, i: (b, h))],
        out_specs=pl.BlockSpec((tq, V_DIM), lambda b, h, i: (b * nq + i, h)),
        out_shape=jax.ShapeDtypeStruct((n, MLA_HEADS * V_DIM), BF16),
        compiler_params=_cparams("parallel", "parallel", "arbitrary"),
        name="mla_attention",
    )(q, k, v)


def _head_sums(x, bd):
    parts = []
    for gi in range(RW_DIM // LANES):
        parts.append(jnp.dot(x[:, gi * LANES:(gi + 1) * LANES], bd, precision=HIGHEST,
                             preferred_element_type=F32))
    return jnp.concatenate(parts, axis=1)


def _sigmoid(x):
    return 1.0 / (1.0 + jnp.exp(-x))


def _rw_prep_body(*refs, tiles_per_seq, tm, has_vfirst):
    (h_ref, hp_ref, hn_ref, mu_ref, vec_ref, w2f_ref, w2b_ref, a2f_ref, a2b_ref, g2_ref, bd_ref) = refs[:11]
    pos = 11
    if has_vfirst:
        v2_ref, vf_ref = refs[pos:pos + 2]
        pos += 2
    r_ref, v_ref, kk_ref, ag_ref, lw_ref, k2_ref, g_ref = refs[pos:pos + 7]

    ti = pl.program_id(0) % tiles_per_seq
    keep_prev = jnp.where(ti == 0, 0.0, 1.0)
    keep_next = jnp.where(ti == tiles_per_seq - 1, 0.0, 1.0)
    row = lax.broadcasted_iota(jnp.int32, (tm, 1), 0)

    def shifted(c0, c1):
        x = h_ref[:, c0:c1]
        prev = jnp.where(row == 0, hp_ref[SUBLANES - 1:SUBLANES, c0:c1] * keep_prev, pltpu.roll(x, 1, axis=0))
        nxt = jnp.where(row == tm - 1, hn_ref[0:1, c0:c1] * keep_next, pltpu.roll(x, tm - 1, axis=0))
        return x + (0.5 * (prev + nxt) - x) * mu_ref[:, c0:c1]

    vec = vec_ref[...]
    w0f, w0b, a0f, a0b, k_k, k_a, v0 = (vec[i:i + 1, :] for i in range(7))

    r = shifted(0, RW_DIM)
    k = shifted(RW_DIM, 2 * RW_DIM)
    v = shifted(2 * RW_DIM, 3 * RW_DIM)
    lo = shifted(3 * RW_DIM, RW_W)
    dw = jnp.tanh(lo[:, 0:LANES]).astype(BF16)
    da = lo[:, LANES:2 * LANES].astype(BF16)
    dg = _sigmoid(lo[:, 2 * LANES:4 * LANES]).astype(BF16)

    def decay(dw_half, w0, w2_ref):
        z = w0 + jnp.dot(dw_half, w2_ref[...], preferred_element_type=F32)
        nz = -z
        softplus = jnp.maximum(nz, 0.0) + jnp.log(1.0 + jnp.exp(-jnp.abs(nz)))
        return -jnp.exp(-softplus - 0.5)

    lw_ref[:, 0:RW_DIM] = decay(dw, w0f, w2f_ref)
    lw_ref[:, RW_DIM:] = decay(dw, w0b, w2b_ref)
    af = _sigmoid(a0f + jnp.dot(da, a2f_ref[...], preferred_element_type=F32))
    ab = _sigmoid(a0b + jnp.dot(da, a2b_ref[...], preferred_element_type=F32))
    ag_ref[:, 0:RW_DIM] = af
    ag_ref[:, RW_DIM:] = ab
    g_ref[...] = jnp.dot(dg, g2_ref[...], preferred_element_type=F32)

    if has_vfirst:
        dv = lo[:, 4 * LANES:5 * LANES].astype(BF16)
        mix = _sigmoid(v0 + jnp.dot(dv, v2_ref[...], preferred_element_type=F32))
        v = v + (vf_ref[...] - v) * mix
    v_ref[...] = v
    r_ref[...] = r

    kk = k * k_k
    ss = _head_sums(kk * kk, bd_ref[...])
    kk_ref[...] = kk * lax.rsqrt(jnp.maximum(ss, 1e-24))
    k2_ref[:, 0:RW_DIM] = k * (1.0 + (af - 1.0) * k_a)
    k2_ref[:, RW_DIM:] = k * (1.0 + (ab - 1.0) * k_a)


def rwkv_prep(h_rw, mu, vec, w2f, w2b, a2f, a2b, g2, bd, v2, v_first, *, seq, tm):
    n = h_rw.shape[0]
    tps = seq // tm
    nb8 = n // SUBLANES
    has_vfirst = v_first is not None
    full = lambda a: pl.BlockSpec(a.shape, lambda i: (0,) * a.ndim)
    in_specs = [pl.BlockSpec((tm, RW_W), lambda i: (i, 0)),
                pl.BlockSpec((SUBLANES, RW_W), lambda i: (jnp.maximum(i * (tm // SUBLANES) - 1, 0), 0)),
                pl.BlockSpec((SUBLANES, RW_W), lambda i: (jnp.minimum((i + 1) * (tm // SUBLANES), nb8 - 1), 0)),
                full(mu), full(vec), full(w2f), full(w2b), full(a2f), full(a2b), full(g2), full(bd)]
    args = [h_rw, h_rw, h_rw, mu, vec, w2f, w2b, a2f, a2b, g2, bd]
    if has_vfirst:
        in_specs += [full(v2), pl.BlockSpec((tm, RW_DIM), lambda i: (i, 0))]
        args += [v2, v_first]
    row = lambda w: pl.BlockSpec((tm, w), lambda i: (i, 0))
    widths = [RW_DIM, RW_DIM, RW_DIM, 2 * RW_DIM, 2 * RW_DIM, 2 * RW_DIM, RW_DIM]
    return pl.pallas_call(
        functools.partial(_rw_prep_body, tiles_per_seq=tps, tm=tm, has_vfirst=has_vfirst),
        grid=(n // tm,),
        in_specs=in_specs,
        out_specs=[row(w) for w in widths],
        out_shape=[jax.ShapeDtypeStruct((n, w), F32) for w in widths],
        compiler_params=_cparams("parallel"),
        name="rwkv_prep",
    )(*args)


def _wkv_body(r_ref, v_ref, kk_ref, lw_ref, k_ref, ag_ref, y_ref, st_ref):
    C = WKV_CHUNK
    rev = pl.program_id(1) == 1

    @pl.when(pl.program_id(2) == 0)
    def _():
        st_ref[...] = jnp.zeros_like(st_ref)

    ti = lax.broadcasted_iota(jnp.int32, (C, C), 0)
    si = lax.broadcasted_iota(jnp.int32, (C, C), 1)
    sgn = jnp.where(rev, -1, 1)
    tri = jnp.where((ti - si) * sgn >= 0, 1.0, 0.0)
    lw_all = lw_ref[...]
    cum_all = jnp.dot(tri, lw_all, precision=HIGHEST, preferred_element_type=F32)
    tot_all = jnp.sum(lw_all, axis=0, keepdims=True)

    lane = lax.broadcasted_iota(jnp.int32, (1, LANES), 1)
    hm0 = jnp.where(lane < RW_HEAD, 1.0, 0.0)
    hm1 = 1.0 - hm0
    stack = lambda x: jnp.concatenate([x * hm0, x * hm1], axis=0).astype(BF16)
    dup = lambda x: jnp.concatenate([x, x], axis=0).astype(BF16)
    ri = lax.broadcasted_iota(jnp.int32, (2 * C, 2 * C), 0)
    ci = lax.broadcasted_iota(jnp.int32, (2 * C, 2 * C), 1)
    same = (ri & C) == (ci & C)
    dts = ((ri & (C - 1)) - (ci & (C - 1))) * sgn
    strict = same & (dts > 0)
    incl = same & (dts >= 0)
    eye = jnp.where(ri == ci, 1.0, 0.0)
    valid = (lax.broadcasted_iota(jnp.int32, (2 * C, LANES), 0) & C) == (
        lax.broadcasted_iota(jnp.int32, (2 * C, LANES), 1) & RW_HEAD)

    pairs = range(RW_DIM // LANES)
    mm = functools.partial(jnp.dot, preferred_element_type=F32)
    mm_tb = lambda a, b: lax.dot_general(a, b, _TB, preferred_element_type=F32)
    mm_ta = lambda a, b: lax.dot_general(a, b, _TA, preferred_element_type=F32)
    xa, xr, vst, bws, kws, wtots, gms = [], [], [], [], [], [], []
    for p in pairs:
        ls = slice(p * LANES, (p + 1) * LANES)
        r, v, kk, k, ag = (x[:, ls] for x in (r_ref, v_ref, kk_ref, k_ref, ag_ref))
        lw, cum, tot = lw_all[:, ls], cum_all[:, ls], tot_all[:, ls]
        ei = jnp.exp(-cum)
        ew = jnp.exp(tot - cum)
        b = kk * ag
        xa.append(stack(-kk * jnp.exp(cum - lw)))
        xr.append(stack(r * jnp.exp(cum)))
        vst.append(stack(v))
        bws.append(stack(b * ew))
        kws.append(stack(k * ew))
        wtots.append(jnp.exp(tot))
        x2 = jnp.concatenate([xa[p], xr[p]], axis=0)
        y2 = jnp.concatenate([dup(b * ei), dup(k * ei)], axis=0)
        gms.append(mm_tb(x2, y2))
    l_ab = [jnp.where(strict, g[0:2 * C, 0:2 * C], 0.0) for g in gms]
    l_ak = [jnp.where(strict, g[0:2 * C, 2 * C:], 0.0).astype(BF16) for g in gms]
    m_rb = [jnp.where(incl, g[2 * C:, 0:2 * C], 0.0).astype(BF16) for g in gms]
    m_rk = [jnp.where(incl, g[2 * C:, 2 * C:], 0.0).astype(BF16) for g in gms]

    pinv = [eye + l for l in l_ab]
    lp = l_ab
    for _ in range(int(math.log2(C)) - 1):
        lpb = [l.astype(BF16) for l in lp]
        lp = [mm(l, l) for l in lpb]
        pinv = [pi + mm(pi.astype(BF16), l.astype(BF16)) for pi, l in zip(pinv, lp)]

    st = [st_ref[p] for p in pairs]
    s_kv = [s.T.astype(BF16) for s in st]
    pre = [mm(jnp.concatenate([xa[p], l_ak[p]], axis=1), jnp.concatenate([s_kv[p], vst[p]], axis=0))
           for p in pairs]
    ustb = [jnp.where(valid, mm(pinv[p].astype(BF16), pre[p].astype(BF16)), 0.0).astype(BF16) for p in pairs]
    for p in pairs:
        yst = mm(jnp.concatenate([xr[p], m_rb[p], m_rk[p]], axis=1),
                 jnp.concatenate([s_kv[p], ustb[p], vst[p]], axis=0))
        yst = jnp.where(valid, yst, 0.0)
        y_ref[:, p * LANES:(p + 1) * LANES] = yst[0:C] + yst[C:]
    for p in pairs:
        st_ref[p] = st[p] * wtots[p] + mm_ta(jnp.concatenate([ustb[p], vst[p]], axis=0),
                                             jnp.concatenate([bws[p], kws[p]], axis=0))


def wkv(r, v, kk, lw2, k2, ag2, *, batch, seq):
    n = r.shape[0]
    C = WKV_CHUNK
    nc = seq // C

    def rows(b, d, c):
        return b * nc + jnp.where(d == 1, nc - 1 - c, c)

    shared = pl.BlockSpec((C, RW_DIM), lambda b, d, c: (rows(b, d, c), 0))
    perdir = pl.BlockSpec((C, RW_DIM), lambda b, d, c: (rows(b, d, c), d))
    return pl.pallas_call(
        _wkv_body,
        grid=(batch, 2, nc),
        in_specs=[shared, shared, shared, perdir, perdir, perdir],
        out_specs=perdir,
        out_shape=jax.ShapeDtypeStruct((n, 2 * RW_DIM), F32),
        scratch_shapes=[pltpu.VMEM((RW_DIM // LANES, LANES, LANES), F32)],
        compiler_params=_cparams("parallel", "parallel", "arbitrary"),
        name="wkv_scan",
    )(r, v, kk, lw2, k2, ag2)


def _rw_post_body(y2_ref, r_ref, k2_ref, v_ref, g_ref, vec_ref, bd_ref, o_ref):
    vec = vec_ref[...]
    lnx_w, lnx_b, r_k = (vec[i:i + 1, :] for i in range(3))
    bd = bd_ref[...]
    y = y2_ref[:, 0:RW_DIM] + y2_ref[:, RW_DIM:]
    inv = 1.0 / RW_HEAD
    mean = _head_sums(y, bd) * inv
    d = y - mean
    var = _head_sums(d * d, bd) * inv
    yn = d * lax.rsqrt(var + LNX_EPS) * lnx_w + lnx_b
    ksum = k2_ref[:, 0:RW_DIM] + k2_ref[:, RW_DIM:]
    bonus = _head_sums(r_ref[...] * ksum * r_k, bd) * v_ref[...]
    o_ref[...] = ((yn + bonus) * g_ref[...]).astype(o_ref.dtype)


def rwkv_post(y2, r, k2, v, g, vec, bd, *, tm):
    n = r.shape[0]
    row = lambda w: pl.BlockSpec((tm, w), lambda i: (i, 0))
    full = lambda a: pl.BlockSpec(a.shape, lambda i: (0,) * a.ndim)
    return pl.pallas_call(
        _rw_post_body,
        grid=(n // tm,),
        in_specs=[row(2 * RW_DIM), row(RW_DIM), row(2 * RW_DIM), row(RW_DIM), row(RW_DIM), full(vec), full(bd)],
        out_specs=row(RW_DIM),
        out_shape=jax.ShapeDtypeStruct((n, RW_DIM), BF16),
        compiler_params=_cparams("parallel"),
        name="rwkv_post",
    )(y2, r, k2, v, g, vec, bd)


def _merge_body(o_ref, yg_ref, ga_ref, gr_ref, wa_ref, wr_ref, m_ref):
    attn = jnp.dot(o_ref[...], wa_ref[...], preferred_element_type=F32)
    rw = jnp.dot(yg_ref[...], wr_ref[...], preferred_element_type=F32)
    m_ref[...] = (_sigmoid(ga_ref[...].astype(F32)) * attn + _sigmoid(gr_ref[...].astype(F32)) * rw).astype(m_ref.dtype)


def merge_branches(o, yg, gates, wa, wr, *, tm, tn):
    n = o.shape[0]
    nj = D_MODEL // tn
    return pl.pallas_call(
        _merge_body,
        grid=(n // tm, nj),
        in_specs=[pl.BlockSpec((tm, MLA_HEADS * V_DIM), lambda i, j: (i, 0)),
                  pl.BlockSpec((tm, RW_DIM), lambda i, j: (i, 0)),
                  pl.BlockSpec((tm, tn), lambda i, j: (i, j)),
                  pl.BlockSpec((tm, tn), lambda i, j: (i, nj + j)),
                  pl.BlockSpec((MLA_HEADS * V_DIM, tn), lambda i, j: (0, j)),
                  pl.BlockSpec((RW_DIM, tn), lambda i, j: (0, j))],
        out_specs=pl.BlockSpec((tm, tn), lambda i, j: (i, j)),
        out_shape=jax.ShapeDtypeStruct((n, D_MODEL), BF16),
        compiler_params=_cparams("parallel", "parallel"),
        name="merge_branches",
    )(o, yg, gates, gates, wa, wr)


def _mm_res_body(a_ref, w_ref, x_ref, o_ref):
    o_ref[...] = x_ref[...] + jnp.dot(a_ref[...], w_ref[...], preferred_element_type=F32)


def matmul_residual(a, w, x, *, tm, tn):
    n, k = a.shape
    nc = w.shape[1]
    return pl.pallas_call(
        _mm_res_body,
        grid=(n // tm, nc // tn),
        in_specs=[pl.BlockSpec((tm, k), lambda i, j: (i, 0)),
                  pl.BlockSpec((k, tn), lambda i, j: (0, j)),
                  pl.BlockSpec((tm, tn), lambda i, j: (i, j))],
        out_specs=pl.BlockSpec((tm, tn), lambda i, j: (i, j)),
        out_shape=jax.ShapeDtypeStruct((n, nc), F32),
        compiler_params=_cparams("parallel", "parallel"),
        name="matmul_residual",
    )(a, w, x)


def _top16_rows(ss, idx):
    big = 3.0e38
    vals = [[] for _ in ss]
    poss = [[] for _ in ss]
    for _ in range(PEER_TOPK):
        ms = [jnp.max(s, axis=0, keepdims=True) for s in ss]
        ps = [jnp.min(jnp.where(s == m, idx, big), axis=0, keepdims=True) for s, m in zip(ss, ms)]
        ss = [jnp.where(idx == p, -jnp.inf, s) for s, p in zip(ss, ps)]
        for i, (m, p) in enumerate(zip(ms, ps)):
            vals[i].append(m)
            poss[i].append(p)
    return [(jnp.concatenate(v, axis=0), jnp.concatenate(p, axis=0)) for v, p in zip(vals, poss)]


def _candidates(sv0, sv1):
    tok = sv0.shape[1]
    io = lambda rows: lax.broadcasted_iota(jnp.int32, (rows, tok), 0).astype(F32)
    half = PEER_TOPK // 2
    parts = [sv0[0:1, :] + sv1]
    idxs = [io(PEER_TOPK)]
    for a in range(1, half):
        parts.append(sv0[a:a + 1, :] + sv1[0:half, :])
        idxs.append(io(half) + float(a * PEER_TOPK))
    parts.append(sv0[half:, :] + sv1[0:1, :])
    idxs.append((io(half) + float(half)) * float(PEER_TOPK))
    return jnp.concatenate(parts, axis=0), jnp.concatenate(idxs, axis=0)


def _pick_rows(table, sel):
    out = jnp.zeros(sel.shape, table.dtype)
    for a in range(PEER_TOPK):
        out = jnp.where(sel == a, table[a:a + 1, :], out)
    return out


def _peer_topk_body(q_ref, sk_ref, i1_ref, i2_ref, gt_ref):
    i1s, i2s, gts = [], [], []
    key_idx = lax.broadcasted_iota(jnp.int32, (N_KEYS, q_ref.shape[0]), 0).astype(F32)
    for h in range(PEER_HEADS):
        ss = []
        for p in range(2):
            c0 = (2 * h + p) * PEER_HALF
            qb = q_ref[:, c0:c0 + PEER_HALF].astype(BF16)
            ss.append(lax.dot_general(sk_ref[p], qb, _TB, preferred_element_type=F32))
        (sv0, si0), (sv1, si1) = _top16_rows(ss, key_idx)
        cand, cand_idx = _candidates(sv0, sv1)
        ((top_s, top_j),) = _top16_rows([cand], cand_idx)
        ja = jnp.floor(top_j * (1.0 / PEER_TOPK))
        i1s.append(_pick_rows(si0, ja))
        i2s.append(_pick_rows(si1, top_j - ja * PEER_TOPK))
        e = jnp.exp(top_s - top_s[0:1, :])
        gts.append(e / jnp.sum(e, axis=0, keepdims=True))
    i1_ref[...] = jnp.concatenate(i1s, axis=0).T
    i2_ref[...] = jnp.concatenate(i2s, axis=0).T
    gt_ref[...] = jnp.concatenate(gts, axis=0).T


def peer_topk(q, sub_keys):
    n = q.shape[0]
    tmk = LANES
    out = pl.BlockSpec((tmk, NSLOT), lambda i: (i, 0))
    return pl.pallas_call(
        _peer_topk_body,
        grid=(n // tmk,),
        in_specs=[pl.BlockSpec((tmk, 2 * PEER_HALF * PEER_HEADS), lambda i: (i, 0)),
                  pl.BlockSpec(sub_keys.shape, lambda i: (0, 0, 0))],
        out_specs=[out, out, out],
        out_shape=[jax.ShapeDtypeStruct((n, NSLOT), F32)] * 3,
        compiler_params=_cparams("parallel"),
        name="peer_topk",
    )(q, sub_keys)


def _gelu_exact(x):
    return 0.5 * x * (1.0 + lax.erf(x * (1.0 / math.sqrt(2.0))))


_HI16 = 0xFFFF0000
G_UNROLL = SUBLANES
E_SUB = 2 * N_KEYS
PEER_TE = 2 * E_SUB


def _bf16_bits(x):
    return pltpu.bitcast(x, jnp.uint32) + jnp.uint32(0x8000)


def _peer_dense_body(xn_ref, i1_ref, i2_ref, gt_ref, ut0_ref, utb_ref, utn_ref, v_ref, x_ref, lnf_ref, scal_ref,
                     o_ref, g_ref, h_ref, *, tm, te, final_norm):
    e = pl.program_id(1)
    half = tm // 2
    mm_tb = lambda a, b: lax.dot_general(a, b, _TB, preferred_element_type=F32)
    hsc, csc, osc = (scal_ref[r:r + 1, 0:1] for r in range(3))
    nsub = te // E_SUB

    def activations(ut_ref):
        xn = xn_ref[...]
        return [jnp.dot(xn, ut_ref[:, j * E_SUB:(j + 1) * E_SUB], preferred_element_type=F32) * hsc
                for j in range(nsub)]

    def accumulate(block, hs):
        coefs = []
        for j in range(nsub):
            row0 = block * (te // N_KEYS) + j * (E_SUB // N_KEYS)
            words = [g_ref[pl.ds(row0 + r, half, stride=G_PITCH), :] for r in range(E_SUB // N_KEYS)]
            top = jnp.concatenate([pltpu.bitcast(w & jnp.uint32(_HI16), F32) for w in words], axis=1)
            bot = jnp.concatenate([pltpu.bitcast(w << 16, F32) for w in words], axis=1)
            gate = jnp.concatenate([top, bot], axis=0)
            coef = gate * _gelu_exact(hs[j]) * csc
            coefs.append(jnp.clip(coef, -FP8_MAX, FP8_MAX).astype(FP8))
        return coefs

    @pl.when(e == 0)
    def _():
        for j, h in enumerate(activations(ut0_ref)):
            h_ref[:, j * E_SUB:(j + 1) * E_SUB] = h
        o_ref[...] = x_ref[...] * osc
        sub = lax.broadcasted_iota(jnp.int32, (N_KEYS, NSLOT), 0).astype(F32).astype(BF16)
        one = jnp.ones((), BF16)
        zero = jnp.zeros((), BF16)

        def build(it, carry):
            base = pl.multiple_of(it * G_UNROLL, SUBLANES)
            tiles = [(ref[pl.ds(base, G_UNROLL), :], ref[pl.ds(half + base, G_UNROLL), :])
                     for ref in (i1_ref, i2_ref, gt_ref)]
            a_ts, b_ts = [], []
            for j in range(G_UNROLL):
                for t in range(2):
                    i1, i2, gt = (tiles[q][t][j:j + 1, :].astype(BF16) for q in range(3))
                    a_ts.append(jnp.where(sub == i1, one, zero))
                    b_ts.append(jnp.where(sub == i2, gt, zero))
            gs = [mm_tb(a, b) for a, b in zip(a_ts, b_ts)]
            for j in range(G_UNROLL):
                word = (_bf16_bits(gs[2 * j]) & jnp.uint32(_HI16)) | (_bf16_bits(gs[2 * j + 1]) >> 16)
                g_ref[pl.ds(pl.multiple_of((base + j) * G_PITCH, SUBLANES), N_KEYS), :] = word
            return carry

        lax.fori_loop(0, half // G_UNROLL, build, 0)

    hs_b = activations(utb_ref)
    coef_a = accumulate(2 * e, [h_ref[:, j * E_SUB:(j + 1) * E_SUB] for j in range(nsub)])
    for j, h in enumerate(activations(utn_ref)):
        h_ref[:, j * E_SUB:(j + 1) * E_SUB] = h
    coef_b = accumulate(2 * e + 1, hs_b)
    o_ref[...] += jnp.dot(jnp.concatenate(coef_a + coef_b, axis=1), v_ref[...], preferred_element_type=F32)

    @pl.when(e == pl.num_programs(1) - 1)
    def _():
        out = o_ref[...] * scal_ref[3:4, 0:1]
        if final_norm:
            out = _rms_rows(out, lnf_ref[...], EPS)
        o_ref[...] = out


def peer_dense(xn, i1, i2, gt, ut, v, x, ln_f, scal, *, tm, te, final_norm):
    n = xn.shape[0]
    ne = v.shape[0] // te
    assert tm % (2 * G_UNROLL) == 0 and te % E_SUB == 0 and ne % 2 == 0 and ut.shape == (ne, D_MODEL, te)
    row = lambda w, **kw: pl.BlockSpec((tm, w), lambda i, e: (i, 0), **kw)
    once = dict(pipeline_mode=pl.Buffered(1))
    return pl.pallas_call(
        functools.partial(_peer_dense_body, tm=tm, te=te, final_norm=final_norm),
        grid=(n // tm, ne // 2),
        in_specs=[row(D_MODEL, **once), row(NSLOT), row(NSLOT), row(NSLOT),
                  pl.BlockSpec((None, D_MODEL, te), lambda i, e: (0, 0, 0)),
                  pl.BlockSpec((None, D_MODEL, te), lambda i, e: (2 * e + 1, 0, 0)),
                  pl.BlockSpec((None, D_MODEL, te), lambda i, e: (jnp.minimum(2 * e + 2, ne - 1), 0, 0)),
                  pl.BlockSpec((2 * te, D_MODEL), lambda i, e: (e, 0)),
                  row(D_MODEL, **once),
                  pl.BlockSpec((1, D_MODEL), lambda i, e: (0, 0)),
                  pl.BlockSpec(scal.shape, lambda i, e: (0, 0))],
        out_specs=row(D_MODEL),
        out_shape=jax.ShapeDtypeStruct((n, D_MODEL), F32),
        scratch_shapes=[pltpu.VMEM((tm // 2 * G_PITCH, N_KEYS), jnp.uint32), pltpu.VMEM((tm, te), F32)],
        compiler_params=_cparams("parallel", "arbitrary"),
        name="peer_dense",
    )(xn, i1, i2, gt, ut, ut, ut, v, x, ln_f.reshape(1, D_MODEL), scal)


def _pad_cols(a, w):
    return jnp.pad(a, ((0, 0), (0, w - a.shape[1])))


def _pad_rows(a, h):
    return jnp.pad(a, ((0, h - a.shape[0]),) + ((0, 0),) * (a.ndim - 1))


def _swap_halves(a):
    half = a.shape[-1] // 2
    return jnp.concatenate([a[..., half:], a[..., :half]], axis=-1)


def _layer_weights(i, p):
    first = i == 0
    w_in = p["w_in_first"] if first else p["w_in_rest"][i - 1]
    mu = p["mu_first"] if first else p["mu_rest"][i - 1]
    mla_cols = Q_LORA + KV_LORA + ROPE_DIM
    gate0 = mla_cols
    rw0 = mla_cols + 2 * D_MODEL
    k_rope_w = w_in[:, Q_LORA + KV_LORA:mla_cols]
    w_mla = jnp.concatenate([w_in[:, :mla_cols], _swap_halves(k_rope_w)], axis=1)
    w_gate = w_in[:, gate0:rw0]

    def rw_layout(a):
        c = 3 * RW_DIM
        dwf, dwb = a[..., c:c + DECAY_LORA], a[..., c + DECAY_LORA:c + 2 * DECAY_LORA]
        c += 2 * DECAY_LORA
        daf, dab = a[..., c:c + AAA_LORA], a[..., c + AAA_LORA:c + 2 * AAA_LORA]
        c += 2 * AAA_LORA
        dg = a[..., c:c + GATE_LORA]
        c += GATE_LORA
        dv = a[..., c:]
        z = lambda w: jnp.zeros(a.shape[:-1] + (w,), a.dtype)
        return jnp.concatenate([a[..., :3 * RW_DIM], dwf, dwb, daf, dab, dg, z(2 * LANES - GATE_LORA),
                                dv, z(2 * LANES - dv.shape[-1])], axis=-1)

    w_rw = rw_layout(w_in[:, rw0:])
    mu_l = rw_layout(mu[None, :])
    zeros64 = jnp.zeros((DECAY_LORA, RW_DIM), F32)
    wq = p["w_uq"][i].reshape(Q_LORA, MLA_HEADS, NOPE_DIM + ROPE_DIM)
    wq = jnp.concatenate([wq, _swap_halves(wq[..., NOPE_DIM:])], axis=-1).reshape(Q_LORA, MLA_HEADS * QK_PAD)
    vec = jnp.stack([p["w0_f"][i], p["w0_b"][i], p["a0_f"][i], p["a0_b"][i], p["k_k"][i], p["k_a"][i],
                     p["v0_rest"][i - 1] if not first else jnp.zeros((RW_DIM,), F32),
                     jnp.zeros((RW_DIM,), F32)])
    lw = {
        "ln1": p["ln1"][i], "w_mla": w_mla.astype(BF16), "w_gate": w_gate.astype(BF16),
        "w_rw": w_rw.astype(BF16),
        "mu": mu_l, "q_norm": p["q_norm"][i], "wq": wq.astype(BF16), "kv_norm": p["kv_norm"][i],
        "wkv": p["w_ukv"][i].astype(BF16), "w_o_attn": p["w_o_attn"][i].astype(BF16),
        "vec": vec,
        "w2f": jnp.concatenate([p["w2_f"][i], zeros64]).astype(BF16),
        "w2b": jnp.concatenate([zeros64, p["w2_b"][i]]).astype(BF16),
        "a2f": jnp.concatenate([p["a2_f"][i], zeros64]).astype(BF16),
        "a2b": jnp.concatenate([zeros64, p["a2_b"][i]]).astype(BF16),
        "g2": _pad_rows(p["g2"][i], 2 * LANES).astype(BF16),
        "v2": None if first else _pad_rows(p["v2_rest"][i - 1], LANES).astype(BF16),
        "post_vec": jnp.stack([p["lnx_w"][i], p["lnx_b"][i], p["r_k"][i].reshape(RW_DIM)]
                              + [jnp.zeros((RW_DIM,), F32)] * 5),
        "w_o_rwkv": p["w_o_rwkv"][i].astype(BF16), "w_out": p["w_out"][i].astype(BF16),
        "ln2": p["ln2"][i], "w_pq": p["w_pq"][i].astype(BF16), "sub_keys": p["sub_keys"][i].astype(BF16),
    }
    pow2_below = lambda v: jnp.exp2(jnp.floor(jnp.log2(v)))
    tiny = 1e-30
    u, v, g2n = p["peer_u"][i], p["peer_v"][i], p["ln2"][i]
    xmax = math.sqrt(D_MODEL) * jnp.maximum(jnp.max(jnp.abs(g2n)), tiny)
    sx = pow2_below(FP8_MAX / xmax)
    su = pow2_below(FP8_MAX / jnp.maximum(jnp.max(jnp.abs(u)), tiny))
    sv = pow2_below(FP8_MAX / jnp.maximum(jnp.max(jnp.abs(v)), tiny))
    hmax = xmax * jnp.sqrt(jnp.max(jnp.sum(u * u, axis=1))) * (1.0 + 2.0 ** -4) ** 2
    sc = pow2_below(FP8_MAX / (PEER_HEADS * jnp.maximum(hmax, tiny)))
    lw["ln2_rows"] = jnp.stack([g2n, g2n * sx])
    lw["peer_ut"] = (u * su).astype(FP8).reshape(-1, PEER_TE, D_MODEL).transpose(0, 2, 1)
    lw["peer_v"] = (v * sv).astype(FP8)
    scal = jnp.stack([1.0 / (sx * su), sc, sc * sv, 1.0 / (sc * sv)] + [jnp.zeros((), F32)] * 4)
    lw["peer_scal"] = jnp.broadcast_to(scal[:, None], (SUBLANES, LANES)).astype(F32)
    return lw


def _rope_table(seq):
    inv = 1.0 / (ROPE_THETA ** (jnp.arange(0, ROPE_DIM, 2, dtype=F32) / ROPE_DIM))
    ang = jnp.arange(seq, dtype=F32)[:, None] * inv[None, :]
    c, s = jnp.cos(ang), jnp.sin(ang)
    return jnp.concatenate([c, c, -s, s], axis=1)


def _tiles(seq):
    return dict(tm=min(512, seq), tmm=min(1024, seq), tq=min(8 * ATTN_SUB, seq), tprep=min(256, seq),
                tpeer=min(512, seq))


def _trunk(x, layers, ln_f, bd):
    batch, seq, _ = x.shape
    n = batch * seq
    t = _tiles(seq)
    tm, tmm = t["tm"], t["tmm"]
    ccss = _rope_table(seq)
    x = x.reshape(n, D_MODEL)
    v_first = None
    for li, lw in enumerate(layers):
        (xn1,) = rms_norm(x, lw["ln1"].reshape(1, D_MODEL), (BF16,), tm=tm)
        h_mla = matmul(xn1, lw["w_mla"], tm=tmm, tn=MLA_W, out_dtype=BF16)
        gates = matmul(xn1, lw["w_gate"], tm=tmm, tn=1024, out_dtype=BF16)
        h_rw = matmul(xn1, lw["w_rw"], tm=tmm, tn=1280, out_dtype=F32)
        q = mla_q_proj(h_mla, lw["q_norm"], lw["wq"], ccss, seq=seq, tm=tm)
        k, v = mla_kv_proj(h_mla, lw["kv_norm"], lw["wkv"], ccss, seq=seq, tm=tm)
        o = attention(q, k, v, batch=batch, seq=seq, tq=t["tq"])
        r, vv, kk, ag2, lw2, k2, g = rwkv_prep(h_rw, lw["mu"], lw["vec"], lw["w2f"], lw["w2b"], lw["a2f"],
                                              lw["a2b"], lw["g2"], bd, lw["v2"], v_first,
                                              seq=seq, tm=t["tprep"])
        if v_first is None:
            v_first = vv
        y2 = wkv(r, vv, kk, lw2, k2, ag2, batch=batch, seq=seq)
        yg = rwkv_post(y2, r, k2, vv, g, lw["post_vec"], bd, tm=t["tprep"])
        m = merge_branches(o, yg, gates, lw["w_o_attn"], lw["w_o_rwkv"], tm=tmm, tn=512)
        x = matmul_residual(m, lw["w_out"], x, tm=tmm, tn=1024)
        xn, xn8 = rms_norm(x, lw["ln2_rows"], (BF16, FP8), tm=tm)
        qp = matmul(xn, lw["w_pq"], tm=tmm, tn=1024, out_dtype=F32)
        i1, i2, gt = peer_topk(qp, lw["sub_keys"])
        x = peer_dense(xn8, i1, i2, gt, lw["peer_ut"], lw["peer_v"], x, ln_f, lw["peer_scal"],
                       tm=t["tpeer"], te=PEER_TE,
                       final_norm=li == len(layers) - 1)
    return x.reshape(batch, seq, D_MODEL)


def kernel(x_prompt, x_sample, ln1, w_in_first, mu_first, w_in_rest, mu_rest, q_norm, w_uq, kv_norm, w_ukv, w_o_attn, w0_f, w2_f, w0_b, w2_b, a0_f, a2_f, a0_b, a2_b, g2, k_k, k_a, r_k, lnx_w, lnx_b, v0_rest, v2_rest, w_o_rwkv, w_out, ln2, w_pq, sub_keys, peer_u, peer_v, ln_f):
    p = dict(ln1=ln1, w_in_first=w_in_first, mu_first=mu_first, w_in_rest=w_in_rest, mu_rest=mu_rest,
             q_norm=q_norm, w_uq=w_uq, kv_norm=kv_norm, w_ukv=w_ukv, w_o_attn=w_o_attn, w0_f=w0_f, w2_f=w2_f,
             w0_b=w0_b, w2_b=w2_b, a0_f=a0_f, a2_f=a2_f, a0_b=a0_b, a2_b=a2_b, g2=g2, k_k=k_k, k_a=k_a, r_k=r_k,
             lnx_w=lnx_w, lnx_b=lnx_b, v0_rest=v0_rest, v2_rest=v2_rest, w_o_rwkv=w_o_rwkv, w_out=w_out,
             ln2=ln2, w_pq=w_pq, sub_keys=sub_keys, peer_u=peer_u, peer_v=peer_v)
    depth = ln1.shape[0]
    layers = [_layer_weights(i, p) for i in range(depth)]
    lane = jnp.arange(LANES)
    bd = (lane[:, None] // RW_HEAD == lane[None, :] // RW_HEAD).astype(F32)
    return (_trunk(x_prompt, layers, ln_f, bd), _trunk(x_sample, layers, ln_f, bd))
```

```python
import functools
import math

import jax
import jax.numpy as jnp
from jax import lax
from jax.experimental import pallas as pl
from jax.experimental.pallas import tpu as pltpu

F32 = jnp.float32
BF16 = jnp.bfloat16
FP8 = jnp.float8_e4m3fn
FP8_MAX = 448.0
HIGHEST = lax.Precision.HIGHEST

LANES = 128
SUBLANES = 8
VMEM_LIMIT_BYTES = 56 * 1024 * 1024

D_MODEL = 2048
MLA_HEADS = 16
Q_LORA = 512
KV_LORA = 512
NOPE_DIM = 128
ROPE_DIM = 64
V_DIM = 128
ROPE_THETA = 10000.0
RW_HEAD = 64
RW_HEADS = 16
RW_DIM = RW_HEADS * RW_HEAD
DECAY_LORA = 64
AAA_LORA = 64
MV_LORA = 32
GATE_LORA = 160
LNX_EPS = 64e-5
PEER_HEADS = 8
N_KEYS = 128
PEER_HALF = 128
PEER_TOPK = 16
EPS = 1e-6

QK_PAD = 2 * LANES
MLA_W = Q_LORA + KV_LORA + LANES
LORA_W = 2 * DECAY_LORA + 2 * AAA_LORA + 2 * LANES + 2 * LANES
RW_W = 3 * RW_DIM + LORA_W
PROJ_HEADS = 4
ATTN_SUB = 256
WKV_CHUNK = 64
G_PITCH = N_KEYS + SUBLANES
NSLOT = PEER_HEADS * PEER_TOPK

_TB = (((1,), (1,)), ((), ()))
_TA = (((0,), (0,)), ((), ()))


def _cparams(*sem):
    return pltpu.CompilerParams(dimension_semantics=sem, vmem_limit_bytes=VMEM_LIMIT_BYTES)


def _rms_rows(x, g, eps):
    ms = jnp.mean(x * x, axis=-1, keepdims=True)
    return x * lax.rsqrt(ms + eps) * g


def _rms_norm_body(x_ref, g_ref, *o_refs):
    x = x_ref[...].astype(F32)
    xhat = x * lax.rsqrt(jnp.mean(x * x, axis=-1, keepdims=True) + EPS)
    for row, o_ref in enumerate(o_refs):
        o_ref[...] = (xhat * g_ref[row:row + 1, :]).astype(o_ref.dtype)


def rms_norm(x, gains, out_dtypes, *, tm):
    n, k = x.shape
    spec = pl.BlockSpec((tm, k), lambda i: (i, 0))
    return pl.pallas_call(
        _rms_norm_body,
        grid=(n // tm,),
        in_specs=[spec, pl.BlockSpec(gains.shape, lambda i: (0, 0))],
        out_specs=[spec] * len(out_dtypes),
        out_shape=[jax.ShapeDtypeStruct((n, k), dt) for dt in out_dtypes],
        compiler_params=_cparams("parallel"),
        name="rms_norm",
    )(x, gains)


def _mm_body(a_ref, w_ref, o_ref):
    o_ref[...] = jnp.dot(a_ref[...], w_ref[...], preferred_element_type=F32).astype(o_ref.dtype)


def matmul(a, w, *, tm, tn, out_dtype):
    n, k = a.shape
    nc = w.shape[1]
    tn = min(tn, nc)
    assert n % tm == 0 and nc % tn == 0 and w.shape[0] == k
    return pl.pallas_call(
        _mm_body,
        grid=(n // tm, nc // tn),
        in_specs=[pl.BlockSpec((tm, k), lambda i, j: (i, 0)),
                  pl.BlockSpec((k, tn), lambda i, j: (0, j))],
        out_specs=pl.BlockSpec((tm, tn), lambda i, j: (i, j)),
        out_shape=jax.ShapeDtypeStruct((n, nc), out_dtype),
        compiler_params=_cparams("parallel", "parallel"),
        name="matmul",
    )(a, w)


def _rope_lanes(y2, ccss):
    w = y2 * ccss
    return w + pltpu.roll(w, ROPE_DIM, axis=1)


def _qproj_body(c_ref, g_ref, w_ref, t_ref, q_ref, cn_ref, *, scale):
    @pl.when(pl.program_id(1) == 0)
    def _():
        cn_ref[...] = _rms_rows(c_ref[...].astype(F32), g_ref[...], EPS).astype(BF16)

    y = jnp.dot(cn_ref[...], w_ref[...], preferred_element_type=F32)
    tab = t_ref[...]
    lane = lax.broadcasted_iota(jnp.int32, tab.shape, 1)
    for hh in range(PROJ_HEADS):
        c0 = hh * QK_PAD
        q_ref[:, c0:c0 + LANES] = (y[:, c0:c0 + LANES] * scale).astype(q_ref.dtype)
        rot = _rope_lanes(y[:, c0 + LANES:c0 + QK_PAD], tab)
        q_ref[:, c0 + LANES:c0 + QK_PAD] = jnp.where(lane < ROPE_DIM, rot * scale, 0.0).astype(q_ref.dtype)


def mla_q_proj(h_mla, q_norm, wq, ccss, *, seq, tm):
    n = h_mla.shape[0]
    nt = seq // tm
    scale = (NOPE_DIM + ROPE_DIM) ** -0.5
    return pl.pallas_call(
        functools.partial(_qproj_body, scale=scale),
        grid=(n // tm, MLA_HEADS // PROJ_HEADS),
        in_specs=[pl.BlockSpec((tm, Q_LORA), lambda i, h: (i, 0)),
                  pl.BlockSpec((1, Q_LORA), lambda i, h: (0, 0)),
                  pl.BlockSpec((Q_LORA, PROJ_HEADS * QK_PAD), lambda i, h: (0, h)),
                  pl.BlockSpec((tm, LANES), lambda i, h: (i % nt, 0))],
        out_specs=pl.BlockSpec((tm, PROJ_HEADS * QK_PAD), lambda i, h: (i, h)),
        out_shape=jax.ShapeDtypeStruct((n, MLA_HEADS * QK_PAD), BF16),
        scratch_shapes=[pltpu.VMEM((tm, Q_LORA), BF16)],
        compiler_params=_cparams("parallel", "arbitrary"),
        name="mla_q_proj",
    )(h_mla, q_norm.reshape(1, Q_LORA), wq, ccss)


def _kvproj_body(c_ref, kr_ref, g_ref, w_ref, t_ref, k_ref, v_ref, cn_ref, krr_ref):
    @pl.when(pl.program_id(1) == 0)
    def _():
        cn_ref[...] = _rms_rows(c_ref[...].astype(F32), g_ref[...], EPS).astype(BF16)
        krr_ref[...] = _rope_lanes(kr_ref[...].astype(F32), t_ref[...]).astype(BF16)

    y = jnp.dot(cn_ref[...], w_ref[...], preferred_element_type=F32)
    for hh in range(PROJ_HEADS):
        c0 = hh * (NOPE_DIM + V_DIM)
        k_ref[:, hh * QK_PAD:hh * QK_PAD + LANES] = y[:, c0:c0 + NOPE_DIM].astype(k_ref.dtype)
        k_ref[:, hh * QK_PAD + LANES:(hh + 1) * QK_PAD] = krr_ref[...]
        v_ref[:, hh * V_DIM:(hh + 1) * V_DIM] = y[:, c0 + NOPE_DIM:c0 + NOPE_DIM + V_DIM].astype(v_ref.dtype)


def mla_kv_proj(h_mla, kv_norm, wkv, ccss, *, seq, tm):
    n = h_mla.shape[0]
    nt = seq // tm
    return pl.pallas_call(
        _kvproj_body,
        grid=(n // tm, MLA_HEADS // PROJ_HEADS),
        in_specs=[pl.BlockSpec((tm, KV_LORA), lambda i, h: (i, 1)),
                  pl.BlockSpec((tm, LANES), lambda i, h: (i, (Q_LORA + KV_LORA) // LANES)),
                  pl.BlockSpec((1, KV_LORA), lambda i, h: (0, 0)),
                  pl.BlockSpec((KV_LORA, PROJ_HEADS * (NOPE_DIM + V_DIM)), lambda i, h: (0, h)),
                  pl.BlockSpec((tm, LANES), lambda i, h: (i % nt, 0))],
        out_specs=[pl.BlockSpec((tm, PROJ_HEADS * QK_PAD), lambda i, h: (i, h)),
                   pl.BlockSpec((tm, PROJ_HEADS * V_DIM), lambda i, h: (i, h))],
        out_shape=[jax.ShapeDtypeStruct((n, MLA_HEADS * QK_PAD), BF16),
                   jax.ShapeDtypeStruct((n, MLA_HEADS * V_DIM), BF16)],
        scratch_shapes=[pltpu.VMEM((tm, KV_LORA), BF16), pltpu.VMEM((tm, LANES), BF16)],
        compiler_params=_cparams("parallel", "arbitrary"),
        name="mla_kv_proj",
    )(h_mla, h_mla, kv_norm.reshape(1, KV_LORA), wkv, ccss)


def _attn_body(q_ref, k_ref, v_ref, o_ref, *, nsub):
    ts = q_ref.shape[0] // nsub
    k = k_ref[...]
    v = v_ref[...]

    def scores(j):
        return lax.dot_general(q_ref[j * ts:(j + 1) * ts, :], k, _TB, preferred_element_type=F32)

    def finish(s, j):
        m = jnp.max(s, axis=-1, keepdims=True)
        p = jnp.exp(s - m)
        l = jnp.sum(p, axis=-1, keepdims=True)
        o = jnp.dot(p.astype(BF16), v, preferred_element_type=F32)
        o_ref[j * ts:(j + 1) * ts, :] = (o / l).astype(o_ref.dtype)

    s_prev = scores(0)
    for j in range(1, nsub):
        s_next = scores(j)
        finish(s_prev, j - 1)
        s_prev = s_next
    finish(s_prev, nsub - 1)


def attention(q, k, v, *, batch, seq, tq):
    n = q.shape[0]
    nq = seq // tq
    return pl.pallas_call(
        functools.partial(_attn_body, nsub=tq // ATTN_SUB),
        grid=(batch, MLA_HEADS, nq),
        in_specs=[pl.BlockSpec((tq, QK_PAD), lambda b, h, i: (b * nq + i, h)),
                  pl.BlockSpec((seq, QK_PAD), lambda b, h, i: (b, h)),
                  pl.BlockSpec((seq, V_DIM), lambda b, h, i: (b, h))],
        out_specs=pl.BlockSpec((tq, V_DIM), lambda b, h, i: (b * nq + i, h)),
        out_shape=jax.ShapeDtypeStruct((n, MLA_HEADS * V_DIM), BF16),
        compiler_params=_cparams("parallel", "parallel", "arbitrary"),
        name="mla_attention",
    )(q, k, v)


def _head_sums(x, bd):
    parts = []
    for gi in range(RW_DIM // LANES):
        parts.append(jnp.dot(x[:, gi * LANES:(gi + 1) * LANES], bd, precision=HIGHEST,
                             preferred_element_type=F32))
    return jnp.concatenate(parts, axis=1)


def _sigmoid(x):
    return 1.0 / (1.0 + jnp.exp(-x))


def _rw_prep_body(*refs, tiles_per_seq, tm, has_vfirst):
    (h_ref, hp_ref, hn_ref, mu_ref, vec_ref, w2f_ref, w2b_ref, a2f_ref, a2b_ref, g2_ref, bd_ref) = refs[:11]
    pos = 11
    if has_vfirst:
        v2_ref, vf_ref = refs[pos:pos + 2]
        pos += 2
    r_ref, v_ref, kk_ref, ag_ref, lw_ref, k2_ref, g_ref = refs[pos:pos + 7]

    ti = pl.program_id(0) % tiles_per_seq
    keep_prev = jnp.where(ti == 0, 0.0, 1.0)
    keep_next = jnp.where(ti == tiles_per_seq - 1, 0.0, 1.0)
    row = lax.broadcasted_iota(jnp.int32, (tm, 1), 0)

    def shifted(c0, c1):
        x = h_ref[:, c0:c1]
        prev = jnp.where(row == 0, hp_ref[SUBLANES - 1:SUBLANES, c0:c1] * keep_prev, pltpu.roll(x, 1, axis=0))
        nxt = jnp.where(row == tm - 1, hn_ref[0:1, c0:c1] * keep_next, pltpu.roll(x, tm - 1, axis=0))
        return x + (0.5 * (prev + nxt) - x) * mu_ref[:, c0:c1]

    vec = vec_ref[...]
    w0f, w0b, a0f, a0b, k_k, k_a, v0 = (vec[i:i + 1, :] for i in range(7))

    r = shifted(0, RW_DIM)
    k = shifted(RW_DIM, 2 * RW_DIM)
    v = shifted(2 * RW_DIM, 3 * RW_DIM)
    lo = shifted(3 * RW_DIM, RW_W)
    dw = jnp.tanh(lo[:, 0:LANES]).astype(BF16)
    da = lo[:, LANES:2 * LANES].astype(BF16)
    dg = _sigmoid(lo[:, 2 * LANES:4 * LANES]).astype(BF16)

    def decay(dw_half, w0, w2_ref):
        z = w0 + jnp.dot(dw_half, w2_ref[...], preferred_element_type=F32)
        nz = -z
        softplus = jnp.maximum(nz, 0.0) + jnp.log(1.0 + jnp.exp(-jnp.abs(nz)))
        return -jnp.exp(-softplus - 0.5)

    lw_ref[:, 0:RW_DIM] = decay(dw, w0f, w2f_ref)
    lw_ref[:, RW_DIM:] = decay(dw, w0b, w2b_ref)
    af = _sigmoid(a0f + jnp.dot(da, a2f_ref[...], preferred_element_type=F32))
    ab = _sigmoid(a0b + jnp.dot(da, a2b_ref[...], preferred_element_type=F32))
    ag_ref[:, 0:RW_DIM] = af
    ag_ref[:, RW_DIM:] = ab
    g_ref[...] = jnp.dot(dg, g2_ref[...], preferred_element_type=F32)

    if has_vfirst:
        dv = lo[:, 4 * LANES:5 * LANES].astype(BF16)
        mix = _sigmoid(v0 + jnp.dot(dv, v2_ref[...], preferred_element_type=F32))
        v = v + (vf_ref[...] - v) * mix
    v_ref[...] = v
    r_ref[...] = r

    kk = k * k_k
    ss = _head_sums(kk * kk, bd_ref[...])
    kk_ref[...] = kk * lax.rsqrt(jnp.maximum(ss, 1e-24))
    k2_ref[:, 0:RW_DIM] = k * (1.0 + (af - 1.0) * k_a)
    k2_ref[:, RW_DIM:] = k * (1.0 + (ab - 1.0) * k_a)


def rwkv_prep(h_rw, mu, vec, w2f, w2b, a2f, a2b, g2, bd, v2, v_first, *, seq, tm):
    n = h_rw.shape[0]
    tps = seq // tm
    nb8 = n // SUBLANES
    has_vfirst = v_first is not None
    full = lambda a: pl.BlockSpec(a.shape, lambda i: (0,) * a.ndim)
    in_specs = [pl.BlockSpec((tm, RW_W), lambda i: (i, 0)),
                pl.BlockSpec((SUBLANES, RW_W), lambda i: (jnp.maximum(i * (tm // SUBLANES) - 1, 0), 0)),
                pl.BlockSpec((SUBLANES, RW_W), lambda i: (jnp.minimum((i + 1) * (tm // SUBLANES), nb8 - 1), 0)),
                full(mu), full(vec), full(w2f), full(w2b), full(a2f), full(a2b), full(g2), full(bd)]
    args = [h_rw, h_rw, h_rw, mu, vec, w2f, w2b, a2f, a2b, g2, bd]
    if has_vfirst:
        in_specs += [full(v2), pl.BlockSpec((tm, RW_DIM), lambda i: (i, 0))]
        args += [v2, v_first]
    row = lambda w: pl.BlockSpec((tm, w), lambda i: (i, 0))
    widths = [RW_DIM, RW_DIM, RW_DIM, 2 * RW_DIM, 2 * RW_DIM, 2 * RW_DIM, RW_DIM]
    return pl.pallas_call(
        functools.partial(_rw_prep_body, tiles_per_seq=tps, tm=tm, has_vfirst=has_vfirst),
        grid=(n // tm,),
        in_specs=in_specs,
        out_specs=[row(w) for w in widths],
        out_shape=[jax.ShapeDtypeStruct((n, w), F32) for w in widths],
        compiler_params=_cparams("parallel"),
        name="rwkv_prep",
    )(*args)


def _wkv_body(*refs):
    C = WKV_CHUNK
    in_refs, (yf_ref, yb_ref, st_ref) = refs[:12], refs[12:]
    npair = RW_DIM // LANES

    @pl.when(pl.program_id(1) == 0)
    def _():
        st_ref[...] = jnp.zeros_like(st_ref)

    ti = lax.broadcasted_iota(jnp.int32, (C, C), 0)
    si = lax.broadcasted_iota(jnp.int32, (C, C), 1)
    lane = lax.broadcasted_iota(jnp.int32, (1, LANES), 1)
    hm0 = jnp.where(lane < RW_HEAD, 1.0, 0.0)
    hm1 = 1.0 - hm0
    stack = lambda x: jnp.concatenate([x * hm0, x * hm1], axis=0).astype(BF16)
    dup = lambda x: jnp.concatenate([x, x], axis=0).astype(BF16)
    ri = lax.broadcasted_iota(jnp.int32, (2 * C, 2 * C), 0)
    ci = lax.broadcasted_iota(jnp.int32, (2 * C, 2 * C), 1)
    same = (ri & C) == (ci & C)
    dt = (ri & (C - 1)) - (ci & (C - 1))
    eye = jnp.where(ri == ci, 1.0, 0.0)
    valid = (lax.broadcasted_iota(jnp.int32, (2 * C, LANES), 0) & C) == (
        lax.broadcasted_iota(jnp.int32, (2 * C, LANES), 1) & RW_HEAD)
    mm = functools.partial(jnp.dot, preferred_element_type=F32)
    mm_tb = lambda a, b: lax.dot_general(a, b, _TB, preferred_element_type=F32)
    mm_ta = lambda a, b: lax.dot_general(a, b, _TA, preferred_element_type=F32)

    chains = [(d, p) for d in range(2) for p in range(npair)]
    xa, xr, vst, bws, kws, wtots, gms, strict, incl = [], [], [], [], [], [], [], [], []
    for d in range(2):
        r_ref, v_ref, kk_ref, lw_ref, k_ref, ag_ref = in_refs[6 * d:6 * d + 6]
        sgn = 1 - 2 * d
        tri = jnp.where((ti - si) * sgn >= 0, 1.0, 0.0)
        lw_all = lw_ref[...]
        cum_all = jnp.dot(tri, lw_all, precision=HIGHEST, preferred_element_type=F32)
        tot_all = jnp.sum(lw_all, axis=0, keepdims=True)
        strict_d = same & (dt * sgn > 0)
        incl_d = same & (dt * sgn >= 0)
        for p in range(npair):
            ls = slice(p * LANES, (p + 1) * LANES)
            r, v, kk, k, ag = (x[:, ls] for x in (r_ref, v_ref, kk_ref, k_ref, ag_ref))
            lw, cum, tot = lw_all[:, ls], cum_all[:, ls], tot_all[:, ls]
            ei = jnp.exp(-cum)
            ew = jnp.exp(tot - cum)
            b = kk * ag
            xa.append(stack(-kk * jnp.exp(cum - lw)))
            xr.append(stack(r * jnp.exp(cum)))
            vst.append(stack(v))
            bws.append(stack(b * ew))
            kws.append(stack(k * ew))
            wtots.append(jnp.exp(tot))
            strict.append(strict_d)
            incl.append(incl_d)
            x2 = jnp.concatenate([xa[-1], xr[-1]], axis=0)
            y2 = jnp.concatenate([dup(b * ei), dup(k * ei)], axis=0)
            gms.append(mm_tb(x2, y2))
    nch = range(len(chains))
    l_ab = [jnp.where(strict[i], gms[i][0:2 * C, 0:2 * C], 0.0) for i in nch]
    l_ak = [jnp.where(strict[i], gms[i][0:2 * C, 2 * C:], 0.0).astype(BF16) for i in nch]
    m_rb = [jnp.where(incl[i], gms[i][2 * C:, 0:2 * C], 0.0).astype(BF16) for i in nch]
    m_rk = [jnp.where(incl[i], gms[i][2 * C:, 2 * C:], 0.0).astype(BF16) for i in nch]

    pinv = [eye + l for l in l_ab]
    lp = l_ab
    for _ in range(int(math.log2(C)) - 1):
        lpb = [l.astype(BF16) for l in lp]
        lp = [mm(l, l) for l in lpb]
        pinv = [pi + mm(pi.astype(BF16), l.astype(BF16)) for pi, l in zip(pinv, lp)]

    st = [st_ref[d, p] for d, p in chains]
    s_kv = [s.T.astype(BF16) for s in st]
    pre = [mm(jnp.concatenate([xa[i], l_ak[i]], axis=1), jnp.concatenate([s_kv[i], vst[i]], axis=0))
           for i in nch]
    ustb = [jnp.where(valid, mm(pinv[i].astype(BF16), pre[i].astype(BF16)), 0.0).astype(BF16) for i in nch]
    for i, (d, p) in enumerate(chains):
        yst = mm(jnp.concatenate([xr[i], m_rb[i], m_rk[i]], axis=1),
                 jnp.concatenate([s_kv[i], ustb[i], vst[i]], axis=0))
        yst = jnp.where(valid, yst, 0.0)
        (yf_ref, yb_ref)[d][:, p * LANES:(p + 1) * LANES] = yst[0:C] + yst[C:]
    for i, (d, p) in enumerate(chains):
        st_ref[d, p] = st[i] * wtots[i] + mm_ta(jnp.concatenate([ustb[i], vst[i]], axis=0),
                                                jnp.concatenate([bws[i], kws[i]], axis=0))


def wkv(r, v, kk, lw2, k2, ag2, *, batch, seq):
    n = r.shape[0]
    C = WKV_CHUNK
    nc = seq // C
    fwd = lambda col: pl.BlockSpec((C, RW_DIM), lambda b, c: (b * nc + c, col))
    bwd = lambda col: pl.BlockSpec((C, RW_DIM), lambda b, c: (b * nc + nc - 1 - c, col))
    return pl.pallas_call(
        _wkv_body,
        grid=(batch, nc),
        in_specs=[fwd(0), fwd(0), fwd(0), fwd(0), fwd(0), fwd(0),
                  bwd(0), bwd(0), bwd(0), bwd(1), bwd(1), bwd(1)],
        out_specs=[fwd(0), bwd(0)],
        out_shape=[jax.ShapeDtypeStruct((n, RW_DIM), F32)] * 2,
        scratch_shapes=[pltpu.VMEM((2, RW_DIM // LANES, LANES, LANES), F32)],
        compiler_params=_cparams("parallel", "arbitrary"),
        name="wkv_scan",
    )(r, v, kk, lw2, k2, ag2, r, v, kk, lw2, k2, ag2)


def _rw_post_body(yf_ref, yb_ref, r_ref, k2_ref, v_ref, g_ref, vec_ref, bd_ref, o_ref):
    vec = vec_ref[...]
    lnx_w, lnx_b, r_k = (vec[i:i + 1, :] for i in range(3))
    bd = bd_ref[...]
    y = yf_ref[...] + yb_ref[...]
    inv = 1.0 / RW_HEAD
    mean = _head_sums(y, bd) * inv
    d = y - mean
    var = _head_sums(d * d, bd) * inv
    yn = d * lax.rsqrt(var + LNX_EPS) * lnx_w + lnx_b
    ksum = k2_ref[:, 0:RW_DIM] + k2_ref[:, RW_DIM:]
    bonus = _head_sums(r_ref[...] * ksum * r_k, bd) * v_ref[...]
    o_ref[...] = ((yn + bonus) * g_ref[...]).astype(o_ref.dtype)


def rwkv_post(yf, yb, r, k2, v, g, vec, bd, *, tm):
    n = r.shape[0]
    row = lambda w: pl.BlockSpec((tm, w), lambda i: (i, 0))
    full = lambda a: pl.BlockSpec(a.shape, lambda i: (0,) * a.ndim)
    return pl.pallas_call(
        _rw_post_body,
        grid=(n // tm,),
        in_specs=[row(RW_DIM), row(RW_DIM), row(RW_DIM), row(2 * RW_DIM), row(RW_DIM), row(RW_DIM),
                  full(vec), full(bd)],
        out_specs=row(RW_DIM),
        out_shape=jax.ShapeDtypeStruct((n, RW_DIM), BF16),
        compiler_params=_cparams("parallel"),
        name="rwkv_post",
    )(yf, yb, r, k2, v, g, vec, bd)


def _merge_body(o_ref, yg_ref, ga_ref, gr_ref, wa_ref, wr_ref, m_ref):
    attn = jnp.dot(o_ref[...], wa_ref[...], preferred_element_type=F32)
    rw = jnp.dot(yg_ref[...], wr_ref[...], preferred_element_type=F32)
    m_ref[...] = (_sigmoid(ga_ref[...].astype(F32)) * attn + _sigmoid(gr_ref[...].astype(F32)) * rw).astype(m_ref.dtype)


def merge_branches(o, yg, gates, wa, wr, *, tm, tn):
    n = o.shape[0]
    nj = D_MODEL // tn
    return pl.pallas_call(
        _merge_body,
        grid=(n // tm, nj),
        in_specs=[pl.BlockSpec((tm, MLA_HEADS * V_DIM), lambda i, j: (i, 0)),
                  pl.BlockSpec((tm, RW_DIM), lambda i, j: (i, 0)),
                  pl.BlockSpec((tm, tn), lambda i, j: (i, j)),
                  pl.BlockSpec((tm, tn), lambda i, j: (i, nj + j)),
                  pl.BlockSpec((MLA_HEADS * V_DIM, tn), lambda i, j: (0, j)),
                  pl.BlockSpec((RW_DIM, tn), lambda i, j: (0, j))],
        out_specs=pl.BlockSpec((tm, tn), lambda i, j: (i, j)),
        out_shape=jax.ShapeDtypeStruct((n, D_MODEL), BF16),
        compiler_params=_cparams("parallel", "parallel"),
        name="merge_branches",
    )(o, yg, gates, gates, wa, wr)


def _mm_res_body(a_ref, w_ref, x_ref, o_ref):
    o_ref[...] = x_ref[...] + jnp.dot(a_ref[...], w_ref[...], preferred_element_type=F32)


def matmul_residual(a, w, x, *, tm, tn):
    n, k = a.shape
    nc = w.shape[1]
    return pl.pallas_call(
        _mm_res_body,
        grid=(n // tm, nc // tn),
        in_specs=[pl.BlockSpec((tm, k), lambda i, j: (i, 0)),
                  pl.BlockSpec((k, tn), lambda i, j: (0, j)),
                  pl.BlockSpec((tm, tn), lambda i, j: (i, j))],
        out_specs=pl.BlockSpec((tm, tn), lambda i, j: (i, j)),
        out_shape=jax.ShapeDtypeStruct((n, nc), F32),
        compiler_params=_cparams("parallel", "parallel"),
        name="matmul_residual",
    )(a, w, x)


def _top16_rows(ss, idx):
    big = 3.0e38
    vals = [[] for _ in ss]
    poss = [[] for _ in ss]
    for _ in range(PEER_TOPK):
        ms = [jnp.max(s, axis=0, keepdims=True) for s in ss]
        ps = [jnp.min(jnp.where(s == m, idx, big), axis=0, keepdims=True) for s, m in zip(ss, ms)]
        ss = [jnp.where(idx == p, -jnp.inf, s) for s, p in zip(ss, ps)]
        for i, (m, p) in enumerate(zip(ms, ps)):
            vals[i].append(m)
            poss[i].append(p)
    return [(jnp.concatenate(v, axis=0), jnp.concatenate(p, axis=0)) for v, p in zip(vals, poss)]


def _candidates(sv0, sv1):
    tok = sv0.shape[1]
    io = lambda rows: lax.broadcasted_iota(jnp.int32, (rows, tok), 0).astype(F32)
    half = PEER_TOPK // 2
    parts = [sv0[0:1, :] + sv1]
    idxs = [io(PEER_TOPK)]
    for a in range(1, half):
        parts.append(sv0[a:a + 1, :] + sv1[0:half, :])
        idxs.append(io(half) + float(a * PEER_TOPK))
    parts.append(sv0[half:, :] + sv1[0:1, :])
    idxs.append((io(half) + float(half)) * float(PEER_TOPK))
    return jnp.concatenate(parts, axis=0), jnp.concatenate(idxs, axis=0)


def _pick_rows(table, sel):
    out = jnp.zeros(sel.shape, table.dtype)
    for a in range(PEER_TOPK):
        out = jnp.where(sel == a, table[a:a + 1, :], out)
    return out


def _peer_topk_body(q_ref, sk_ref, i1_ref, i2_ref, gt_ref):
    i1s, i2s, gts = [], [], []
    key_idx = lax.broadcasted_iota(jnp.int32, (N_KEYS, q_ref.shape[0]), 0).astype(F32)
    for h in range(PEER_HEADS):
        ss = []
        for p in range(2):
            c0 = (2 * h + p) * PEER_HALF
            qb = q_ref[:, c0:c0 + PEER_HALF].astype(BF16)
            ss.append(lax.dot_general(sk_ref[p], qb, _TB, preferred_element_type=F32))
        (sv0, si0), (sv1, si1) = _top16_rows(ss, key_idx)
        cand, cand_idx = _candidates(sv0, sv1)
        ((top_s, top_j),) = _top16_rows([cand], cand_idx)
        ja = jnp.floor(top_j * (1.0 / PEER_TOPK))
        i1s.append(_pick_rows(si0, ja))
        i2s.append(_pick_rows(si1, top_j - ja * PEER_TOPK))
        e = jnp.exp(top_s - top_s[0:1, :])
        gts.append(e / jnp.sum(e, axis=0, keepdims=True))
    i1_ref[...] = jnp.concatenate(i1s, axis=0).T
    i2_ref[...] = jnp.concatenate(i2s, axis=0).T
    gt_ref[...] = jnp.concatenate(gts, axis=0).T


def peer_topk(q, sub_keys):
    n = q.shape[0]
    tmk = LANES
    out = pl.BlockSpec((tmk, NSLOT), lambda i: (i, 0))
    return pl.pallas_call(
        _peer_topk_body,
        grid=(n // tmk,),
        in_specs=[pl.BlockSpec((tmk, 2 * PEER_HALF * PEER_HEADS), lambda i: (i, 0)),
                  pl.BlockSpec(sub_keys.shape, lambda i: (0, 0, 0))],
        out_specs=[out, out, out],
        out_shape=[jax.ShapeDtypeStruct((n, NSLOT), F32)] * 3,
        compiler_params=_cparams("parallel"),
        name="peer_topk",
    )(q, sub_keys)


def _gelu_exact(x):
    return 0.5 * x * (1.0 + lax.erf(x * (1.0 / math.sqrt(2.0))))


_HI16 = 0xFFFF0000
G_UNROLL = SUBLANES
E_SUB = 2 * N_KEYS
PEER_TE = 2 * E_SUB


def _bf16_bits(x):
    return pltpu.bitcast(x, jnp.uint32) + jnp.uint32(0x8000)


def _peer_dense_body(xn_ref, i1_ref, i2_ref, gt_ref, ut0_ref, utb_ref, utn_ref, v_ref, x_ref, lnf_ref, scal_ref,
                     o_ref, g_ref, h_ref, *, tm, te, final_norm):
    e = pl.program_id(1)
    half = tm // 2
    mm_tb = lambda a, b: lax.dot_general(a, b, _TB, preferred_element_type=F32)
    hsc, csc, osc = (scal_ref[r:r + 1, 0:1] for r in range(3))
    nsub = te // E_SUB

    def activations(ut_ref):
        xn = xn_ref[...]
        return [jnp.dot(xn, ut_ref[:, j * E_SUB:(j + 1) * E_SUB], preferred_element_type=F32) * hsc
                for j in range(nsub)]

    def accumulate(block, hs):
        coefs = []
        for j in range(nsub):
            row0 = block * (te // N_KEYS) + j * (E_SUB // N_KEYS)
            words = [g_ref[pl.ds(row0 + r, half, stride=G_PITCH), :] for r in range(E_SUB // N_KEYS)]
            top = jnp.concatenate([pltpu.bitcast(w & jnp.uint32(_HI16), F32) for w in words], axis=1)
            bot = jnp.concatenate([pltpu.bitcast(w << 16, F32) for w in words], axis=1)
            gate = jnp.concatenate([top, bot], axis=0)
            coef = gate * _gelu_exact(hs[j]) * csc
            coefs.append(jnp.clip(coef, -FP8_MAX, FP8_MAX).astype(FP8))
        return coefs

    @pl.when(e == 0)
    def _():
        for j, h in enumerate(activations(ut0_ref)):
            h_ref[:, j * E_SUB:(j + 1) * E_SUB] = h
        o_ref[...] = x_ref[...] * osc
        sub = lax.broadcasted_iota(jnp.int32, (N_KEYS, NSLOT), 0).astype(F32).astype(BF16)
        one = jnp.ones((), BF16)
        zero = jnp.zeros((), BF16)

        def build(it, carry):
            base = pl.multiple_of(it * G_UNROLL, SUBLANES)
            tiles = [(ref[pl.ds(base, G_UNROLL), :], ref[pl.ds(half + base, G_UNROLL), :])
                     for ref in (i1_ref, i2_ref, gt_ref)]
            a_ts, b_ts = [], []
            for j in range(G_UNROLL):
                for t in range(2):
                    i1, i2, gt = (tiles[q][t][j:j + 1, :].astype(BF16) for q in range(3))
                    a_ts.append(jnp.where(sub == i1, one, zero))
                    b_ts.append(jnp.where(sub == i2, gt, zero))
            gs = [mm_tb(a, b) for a, b in zip(a_ts, b_ts)]
            for j in range(G_UNROLL):
                word = (_bf16_bits(gs[2 * j]) & jnp.uint32(_HI16)) | (_bf16_bits(gs[2 * j + 1]) >> 16)
                g_ref[pl.ds(pl.multiple_of((base + j) * G_PITCH, SUBLANES), N_KEYS), :] = word
            return carry

        lax.fori_loop(0, half // G_UNROLL, build, 0)

    hs_b = activations(utb_ref)
    coef_a = accumulate(2 * e, [h_ref[:, j * E_SUB:(j + 1) * E_SUB] for j in range(nsub)])
    for j, h in enumerate(activations(utn_ref)):
        h_ref[:, j * E_SUB:(j + 1) * E_SUB] = h
    coef_b = accumulate(2 * e + 1, hs_b)
    o_ref[...] += jnp.dot(jnp.concatenate(coef_a + coef_b, axis=1), v_ref[...], preferred_element_type=F32)

    @pl.when(e == pl.num_programs(1) - 1)
    def _():
        out = o_ref[...] * scal_ref[3:4, 0:1]
        if final_norm:
            out = _rms_rows(out, lnf_ref[...], EPS)
        o_ref[...] = out


def peer_dense(xn, i1, i2, gt, ut, v, x, ln_f, scal, *, tm, te, final_norm):
    n = xn.shape[0]
    ne = v.shape[0] // te
    assert tm % (2 * G_UNROLL) == 0 and te % E_SUB == 0 and ne % 2 == 0 and ut.shape == (ne, D_MODEL, te)
    row = lambda w, **kw: pl.BlockSpec((tm, w), lambda i, e: (i, 0), **kw)
    once = dict(pipeline_mode=pl.Buffered(1))
    return pl.pallas_call(
        functools.partial(_peer_dense_body, tm=tm, te=te, final_norm=final_norm),
        grid=(n // tm, ne // 2),
        in_specs=[row(D_MODEL, **once), row(NSLOT), row(NSLOT), row(NSLOT),
                  pl.BlockSpec((None, D_MODEL, te), lambda i, e: (0, 0, 0)),
                  pl.BlockSpec((None, D_MODEL, te), lambda i, e: (2 * e + 1, 0, 0)),
                  pl.BlockSpec((None, D_MODEL, te), lambda i, e: (jnp.minimum(2 * e + 2, ne - 1), 0, 0)),
                  pl.BlockSpec((2 * te, D_MODEL), lambda i, e: (e, 0)),
                  row(D_MODEL, **once),
                  pl.BlockSpec((1, D_MODEL), lambda i, e: (0, 0)),
                  pl.BlockSpec(scal.shape, lambda i, e: (0, 0))],
        out_specs=row(D_MODEL),
        out_shape=jax.ShapeDtypeStruct((n, D_MODEL), F32),
        scratch_shapes=[pltpu.VMEM((tm // 2 * G_PITCH, N_KEYS), jnp.uint32), pltpu.VMEM((tm, te), F32)],
        compiler_params=_cparams("parallel", "arbitrary"),
        name="peer_dense",
    )(xn, i1, i2, gt, ut, ut, ut, v, x, ln_f.reshape(1, D_MODEL), scal)


def _pad_cols(a, w):
    return jnp.pad(a, ((0, 0), (0, w - a.shape[1])))


def _pad_rows(a, h):
    return jnp.pad(a, ((0, h - a.shape[0]),) + ((0, 0),) * (a.ndim - 1))


def _swap_halves(a):
    half = a.shape[-1] // 2
    return jnp.concatenate([a[..., half:], a[..., :half]], axis=-1)


def _layer_weights(i, p):
    first = i == 0
    w_in = p["w_in_first"] if first else p["w_in_rest"][i - 1]
    mu = p["mu_first"] if first else p["mu_rest"][i - 1]
    mla_cols = Q_LORA + KV_LORA + ROPE_DIM
    gate0 = mla_cols
    rw0 = mla_cols + 2 * D_MODEL
    k_rope_w = w_in[:, Q_LORA + KV_LORA:mla_cols]
    w_mla = jnp.concatenate([w_in[:, :mla_cols], _swap_halves(k_rope_w)], axis=1)
    w_gate = w_in[:, gate0:rw0]

    def rw_layout(a):
        c = 3 * RW_DIM
        dwf, dwb = a[..., c:c + DECAY_LORA], a[..., c + DECAY_LORA:c + 2 * DECAY_LORA]
        c += 2 * DECAY_LORA
        daf, dab = a[..., c:c + AAA_LORA], a[..., c + AAA_LORA:c + 2 * AAA_LORA]
        c += 2 * AAA_LORA
        dg = a[..., c:c + GATE_LORA]
        c += GATE_LORA
        dv = a[..., c:]
        z = lambda w: jnp.zeros(a.shape[:-1] + (w,), a.dtype)
        return jnp.concatenate([a[..., :3 * RW_DIM], dwf, dwb, daf, dab, dg, z(2 * LANES - GATE_LORA),
                                dv, z(2 * LANES - dv.shape[-1])], axis=-1)

    w_rw = rw_layout(w_in[:, rw0:])
    mu_l = rw_layout(mu[None, :])
    zeros64 = jnp.zeros((DECAY_LORA, RW_DIM), F32)
    wq = p["w_uq"][i].reshape(Q_LORA, MLA_HEADS, NOPE_DIM + ROPE_DIM)
    wq = jnp.concatenate([wq, _swap_halves(wq[..., NOPE_DIM:])], axis=-1).reshape(Q_LORA, MLA_HEADS * QK_PAD)
    vec = jnp.stack([p["w0_f"][i], p["w0_b"][i], p["a0_f"][i], p["a0_b"][i], p["k_k"][i], p["k_a"][i],
                     p["v0_rest"][i - 1] if not first else jnp.zeros((RW_DIM,), F32),
                     jnp.zeros((RW_DIM,), F32)])
    lw = {
        "ln1": p["ln1"][i], "w_mla": w_mla.astype(BF16), "w_gate": w_gate.astype(BF16),
        "w_rw": w_rw.astype(BF16),
        "mu": mu_l, "q_norm": p["q_norm"][i], "wq": wq.astype(BF16), "kv_norm": p["kv_norm"][i],
        "wkv": p["w_ukv"][i].astype(BF16), "w_o_attn": p["w_o_attn"][i].astype(BF16),
        "vec": vec,
        "w2f": jnp.concatenate([p["w2_f"][i], zeros64]).astype(BF16),
        "w2b": jnp.concatenate([zeros64, p["w2_b"][i]]).astype(BF16),
        "a2f": jnp.concatenate([p["a2_f"][i], zeros64]).astype(BF16),
        "a2b": jnp.concatenate([zeros64, p["a2_b"][i]]).astype(BF16),
        "g2": _pad_rows(p["g2"][i], 2 * LANES).astype(BF16),
        "v2": None if first else _pad_rows(p["v2_rest"][i - 1], LANES).astype(BF16),
        "post_vec": jnp.stack([p["lnx_w"][i], p["lnx_b"][i], p["r_k"][i].reshape(RW_DIM)]
                              + [jnp.zeros((RW_DIM,), F32)] * 5),
        "w_o_rwkv": p["w_o_rwkv"][i].astype(BF16), "w_out": p["w_out"][i].astype(BF16),
        "ln2": p["ln2"][i], "w_pq": p["w_pq"][i].astype(BF16), "sub_keys": p["sub_keys"][i].astype(BF16),
    }
    pow2_below = lambda v: jnp.exp2(jnp.floor(jnp.log2(v)))
    tiny = 1e-30
    u, v, g2n = p["peer_u"][i], p["peer_v"][i], p["ln2"][i]
    xmax = math.sqrt(D_MODEL) * jnp.maximum(jnp.max(jnp.abs(g2n)), tiny)
    sx = pow2_below(FP8_MAX / xmax)
    su = pow2_below(FP8_MAX / jnp.maximum(jnp.max(jnp.abs(u)), tiny))
    sv = pow2_below(FP8_MAX / jnp.maximum(jnp.max(jnp.abs(v)), tiny))
    hmax = xmax * jnp.sqrt(jnp.max(jnp.sum(u * u, axis=1))) * (1.0 + 2.0 ** -4) ** 2
    sc = pow2_below(FP8_MAX / (PEER_HEADS * jnp.maximum(hmax, tiny)))
    lw["ln2_rows"] = jnp.stack([g2n, g2n * sx])
    lw["peer_ut"] = (u * su).astype(FP8).reshape(-1, PEER_TE, D_MODEL).transpose(0, 2, 1)
    lw["peer_v"] = (v * sv).astype(FP8)
    scal = jnp.stack([1.0 / (sx * su), sc, sc * sv, 1.0 / (sc * sv)] + [jnp.zeros((), F32)] * 4)
    lw["peer_scal"] = jnp.broadcast_to(scal[:, None], (SUBLANES, LANES)).astype(F32)
    return lw


def _rope_table(seq):
    inv = 1.0 / (ROPE_THETA ** (jnp.arange(0, ROPE_DIM, 2, dtype=F32) / ROPE_DIM))
    ang = jnp.arange(seq, dtype=F32)[:, None] * inv[None, :]
    c, s = jnp.cos(ang), jnp.sin(ang)
    return jnp.concatenate([c, c, -s, s], axis=1)


def _tiles(seq):
    return dict(tm=min(512, seq), tmm=min(1024, seq), tq=min(16 * ATTN_SUB, seq), tprep=min(256, seq),
                tpeer=min(512, seq))


def _trunk(x, layers, ln_f, bd):
    batch, seq, _ = x.shape
    n = batch * seq
    t = _tiles(seq)
    tm, tmm = t["tm"], t["tmm"]
    ccss = _rope_table(seq)
    x = x.reshape(n, D_MODEL)
    v_first = None
    for li, lw in enumerate(layers):
        (xn1,) = rms_norm(x, lw["ln1"].reshape(1, D_MODEL), (BF16,), tm=tm)
        h_mla = matmul(xn1, lw["w_mla"], tm=tmm, tn=MLA_W, out_dtype=BF16)
        gates = matmul(xn1, lw["w_gate"], tm=tmm, tn=1024, out_dtype=BF16)
        h_rw = matmul(xn1, lw["w_rw"], tm=tmm, tn=1280, out_dtype=F32)
        q = mla_q_proj(h_mla, lw["q_norm"], lw["wq"], ccss, seq=seq, tm=tm)
        k, v = mla_kv_proj(h_mla, lw["kv_norm"], lw["wkv"], ccss, seq=seq, tm=tm)
        o = attention(q, k, v, batch=batch, seq=seq, tq=t["tq"])
        r, vv, kk, ag2, lw2, k2, g = rwkv_prep(h_rw, lw["mu"], lw["vec"], lw["w2f"], lw["w2b"], lw["a2f"],
                                              lw["a2b"], lw["g2"], bd, lw["v2"], v_first,
                                              seq=seq, tm=t["tprep"])
        if v_first is None:
            v_first = vv
        yf, yb = wkv(r, vv, kk, lw2, k2, ag2, batch=batch, seq=seq)
        yg = rwkv_post(yf, yb, r, k2, vv, g, lw["post_vec"], bd, tm=t["tprep"])
        m = merge_branches(o, yg, gates, lw["w_o_attn"], lw["w_o_rwkv"], tm=tmm, tn=512)
        x = matmul_residual(m, lw["w_out"], x, tm=tmm, tn=1024)
        xn, xn8 = rms_norm(x, lw["ln2_rows"], (BF16, FP8), tm=tm)
        qp = matmul(xn, lw["w_pq"], tm=tmm, tn=1024, out_dtype=F32)
        i1, i2, gt = peer_topk(qp, lw["sub_keys"])
        x = peer_dense(xn8, i1, i2, gt, lw["peer_ut"], lw["peer_v"], x, ln_f, lw["peer_scal"],
                       tm=t["tpeer"], te=PEER_TE,
                       final_norm=li == len(layers) - 1)
    return x.reshape(batch, seq, D_MODEL)


def kernel(x_prompt, x_sample, ln1, w_in_first, mu_first, w_in_rest, mu_rest, q_norm, w_uq, kv_norm, w_ukv, w_o_attn, w0_f, w2_f, w0_b, w2_b, a0_f, a2_f, a0_b, a2_b, g2, k_k, k_a, r_k, lnx_w, lnx_b, v0_rest, v2_rest, w_o_rwkv, w_out, ln2, w_pq, sub_keys, peer_u, peer_v, ln_f):
    p = dict(ln1=ln1, w_in_first=w_in_first, mu_first=mu_first, w_in_rest=w_in_rest, mu_rest=mu_rest,
             q_norm=q_norm, w_uq=w_uq, kv_norm=kv_norm, w_ukv=w_ukv, w_o_attn=w_o_attn, w0_f=w0_f, w2_f=w2_f,
             w0_b=w0_b, w2_b=w2_b, a0_f=a0_f, a2_f=a2_f, a0_b=a0_b, a2_b=a2_b, g2=g2, k_k=k_k, k_a=k_a, r_k=r_k,
             lnx_w=lnx_w, lnx_b=lnx_b, v0_rest=v0_rest, v2_rest=v2_rest, w_o_rwkv=w_o_rwkv, w_out=w_out,
             ln2=ln2, w_pq=w_pq, sub_keys=sub_keys, peer_u=peer_u, peer_v=peer_v)
    depth = ln1.shape[0]
    layers = [_layer_weights(i, p) for i in range(depth)]
    lane = jnp.arange(LANES)
    bd = (lane[:, None] // RW_HEAD == lane[None, :] // RW_HEAD).astype(F32)
    return (_trunk(x_prompt, layers, ln_f, bd), _trunk(x_sample, layers, ln_f, bd))
```

```python
import functools
import math

import jax
import jax.numpy as jnp
from jax import lax
from jax.experimental import pallas as pl
from jax.experimental.pallas import tpu as pltpu

F32 = jnp.float32
BF16 = jnp.bfloat16
FP8 = jnp.float8_e4m3fn
FP8_MAX = 448.0
HIGHEST = lax.Precision.HIGHEST

LANES = 128
SUBLANES = 8
VMEM_LIMIT_BYTES = 56 * 1024 * 1024

D_MODEL = 2048
MLA_HEADS = 16
Q_LORA = 512
KV_LORA = 512
NOPE_DIM = 128
ROPE_DIM = 64
V_DIM = 128
ROPE_THETA = 10000.0
RW_HEAD = 64
RW_HEADS = 16
RW_DIM = RW_HEADS * RW_HEAD
DECAY_LORA = 64
AAA_LORA = 64
MV_LORA = 32
GATE_LORA = 160
LNX_EPS = 64e-5
PEER_HEADS = 8
N_KEYS = 128
PEER_HALF = 128
PEER_TOPK = 16
EPS = 1e-6

QK_PAD = 2 * LANES
MLA_W = Q_LORA + KV_LORA + LANES
LORA_W = 2 * DECAY_LORA + 2 * AAA_LORA + 2 * LANES + 2 * LANES
RW_W = 3 * RW_DIM + LORA_W
PROJ_HEADS = 4
ATTN_SUB = 256
WKV_CHUNK = 64
G_PITCH = N_KEYS + SUBLANES
NSLOT = PEER_HEADS * PEER_TOPK

_TB = (((1,), (1,)), ((), ()))
_TA = (((0,), (0,)), ((), ()))


def _cparams(*sem):
    return pltpu.CompilerParams(dimension_semantics=sem, vmem_limit_bytes=VMEM_LIMIT_BYTES)


def _rms_rows(x, g, eps):
    ms = jnp.mean(x * x, axis=-1, keepdims=True)
    return x * lax.rsqrt(ms + eps) * g


def _rms_norm_body(x_ref, g_ref, *o_refs):
    x = x_ref[...].astype(F32)
    xhat = x * lax.rsqrt(jnp.mean(x * x, axis=-1, keepdims=True) + EPS)
    for row, o_ref in enumerate(o_refs):
        o_ref[...] = (xhat * g_ref[row:row + 1, :]).astype(o_ref.dtype)


def rms_norm(x, gains, out_dtypes, *, tm):
    n, k = x.shape
    spec = pl.BlockSpec((tm, k), lambda i: (i, 0))
    return pl.pallas_call(
        _rms_norm_body,
        grid=(n // tm,),
        in_specs=[spec, pl.BlockSpec(gains.shape, lambda i: (0, 0))],
        out_specs=[spec] * len(out_dtypes),
        out_shape=[jax.ShapeDtypeStruct((n, k), dt) for dt in out_dtypes],
        compiler_params=_cparams("parallel"),
        name="rms_norm",
    )(x, gains)


def _mm_body(a_ref, w_ref, o_ref):
    o_ref[...] = jnp.dot(a_ref[...], w_ref[...], preferred_element_type=F32).astype(o_ref.dtype)


def matmul(a, w, *, tm, tn, out_dtype):
    n, k = a.shape
    nc = w.shape[1]
    tn = min(tn, nc)
    assert n % tm == 0 and nc % tn == 0 and w.shape[0] == k
    return pl.pallas_call(
        _mm_body,
        grid=(n // tm, nc // tn),
        in_specs=[pl.BlockSpec((tm, k), lambda i, j: (i, 0)),
                  pl.BlockSpec((k, tn), lambda i, j: (0, j))],
        out_specs=pl.BlockSpec((tm, tn), lambda i, j: (i, j)),
        out_shape=jax.ShapeDtypeStruct((n, nc), out_dtype),
        compiler_params=_cparams("parallel", "parallel"),
        name="matmul",
    )(a, w)


def _rope_lanes(y2, ccss):
    w = y2 * ccss
    return w + pltpu.roll(w, ROPE_DIM, axis=1)


def _qproj_body(c_ref, g_ref, w_ref, t_ref, q_ref, cn_ref, *, scale):
    @pl.when(pl.program_id(1) == 0)
    def _():
        cn_ref[...] = _rms_rows(c_ref[...].astype(F32), g_ref[...], EPS).astype(BF16)

    y = jnp.dot(cn_ref[...], w_ref[...], preferred_element_type=F32)
    tab = t_ref[...]
    lane = lax.broadcasted_iota(jnp.int32, tab.shape, 1)
    for hh in range(PROJ_HEADS):
        c0 = hh * QK_PAD
        q_ref[:, c0:c0 + LANES] = (y[:, c0:c0 + LANES] * scale).astype(q_ref.dtype)
        rot = _rope_lanes(y[:, c0 + LANES:c0 + QK_PAD], tab)
        q_ref[:, c0 + LANES:c0 + QK_PAD] = jnp.where(lane < ROPE_DIM, rot * scale, 0.0).astype(q_ref.dtype)


def mla_q_proj(h_mla, q_norm, wq, ccss, *, seq, tm):
    n = h_mla.shape[0]
    nt = seq // tm
    scale = (NOPE_DIM + ROPE_DIM) ** -0.5
    return pl.pallas_call(
        functools.partial(_qproj_body, scale=scale),
        grid=(n // tm, MLA_HEADS // PROJ_HEADS),
        in_specs=[pl.BlockSpec((tm, Q_LORA), lambda i, h: (i, 0)),
                  pl.BlockSpec((1, Q_LORA), lambda i, h: (0, 0)),
                  pl.BlockSpec((Q_LORA, PROJ_HEADS * QK_PAD), lambda i, h: (0, h)),
                  pl.BlockSpec((tm, LANES), lambda i, h: (i % nt, 0))],
        out_specs=pl.BlockSpec((tm, PROJ_HEADS * QK_PAD), lambda i, h: (i, h)),
        out_shape=jax.ShapeDtypeStruct((n, MLA_HEADS * QK_PAD), BF16),
        scratch_shapes=[pltpu.VMEM((tm, Q_LORA), BF16)],
        compiler_params=_cparams("parallel", "arbitrary"),
        name="mla_q_proj",
    )(h_mla, q_norm.reshape(1, Q_LORA), wq, ccss)


def _kvproj_body(c_ref, kr_ref, g_ref, w_ref, t_ref, k_ref, vt_ref, cn_ref, krr_ref):
    @pl.when(pl.program_id(1) == 0)
    def _():
        cn_ref[...] = _rms_rows(c_ref[...].astype(F32), g_ref[...], EPS).astype(BF16)
        krr_ref[...] = _rope_lanes(kr_ref[...].astype(F32), t_ref[...]).astype(BF16)

    y = jnp.dot(cn_ref[...], w_ref[...], preferred_element_type=F32)
    for hh in range(PROJ_HEADS):
        c0 = hh * (NOPE_DIM + V_DIM)
        k_ref[:, hh * QK_PAD:hh * QK_PAD + LANES] = y[:, c0:c0 + NOPE_DIM].astype(k_ref.dtype)
        k_ref[:, hh * QK_PAD + LANES:(hh + 1) * QK_PAD] = krr_ref[...]
        vt_ref[hh * V_DIM:(hh + 1) * V_DIM, :] = y[:, c0 + NOPE_DIM:c0 + NOPE_DIM + V_DIM].T.astype(vt_ref.dtype)


def mla_kv_proj(h_mla, kv_norm, wkv, ccss, *, seq, tm):
    n = h_mla.shape[0]
    nt = seq // tm
    hsteps = MLA_HEADS // PROJ_HEADS
    return pl.pallas_call(
        _kvproj_body,
        grid=(n // tm, MLA_HEADS // PROJ_HEADS),
        in_specs=[pl.BlockSpec((tm, KV_LORA), lambda i, h: (i, 1)),
                  pl.BlockSpec((tm, LANES), lambda i, h: (i, (Q_LORA + KV_LORA) // LANES)),
                  pl.BlockSpec((1, KV_LORA), lambda i, h: (0, 0)),
                  pl.BlockSpec((KV_LORA, PROJ_HEADS * (NOPE_DIM + V_DIM)), lambda i, h: (0, h)),
                  pl.BlockSpec((tm, LANES), lambda i, h: (i % nt, 0))],
        out_specs=[pl.BlockSpec((tm, PROJ_HEADS * QK_PAD), lambda i, h: (i, h)),
                   pl.BlockSpec((PROJ_HEADS * V_DIM, tm), lambda i, h: ((i // nt) * hsteps + h, i % nt))],
        out_shape=[jax.ShapeDtypeStruct((n, MLA_HEADS * QK_PAD), BF16),
                   jax.ShapeDtypeStruct((n // seq * MLA_HEADS * V_DIM, seq), BF16)],
        scratch_shapes=[pltpu.VMEM((tm, KV_LORA), BF16), pltpu.VMEM((tm, LANES), BF16)],
        compiler_params=_cparams("parallel", "arbitrary"),
        name="mla_kv_proj",
    )(h_mla, h_mla, kv_norm.reshape(1, KV_LORA), wkv, ccss)


def _attn_body(q_ref, k_ref, vt_ref, o_ref, *, nsub):
    ts = q_ref.shape[0] // nsub
    k = k_ref[...]
    vt = vt_ref[...]

    def scores(j):
        return lax.dot_general(k, q_ref[j * ts:(j + 1) * ts, :], _TB, preferred_element_type=F32)

    def finish(s, j):
        m = jnp.max(s, axis=0, keepdims=True)
        p = jnp.exp(s - m)
        l = jnp.sum(p, axis=0, keepdims=True)
        ot = jnp.dot(vt, p.astype(BF16), preferred_element_type=F32)
        o_ref[j * ts:(j + 1) * ts, :] = (ot / l).T.astype(o_ref.dtype)

    s_prev = scores(0)
    for j in range(1, nsub):
        s_next = scores(j)
        finish(s_prev, j - 1)
        s_prev = s_next
    finish(s_prev, nsub - 1)


def attention(q, k, vt, *, batch, seq, tq):
    n = q.shape[0]
    nq = seq // tq
    return pl.pallas_call(
        functools.partial(_attn_body, nsub=tq // ATTN_SUB),
        grid=(batch, MLA_HEADS, nq),
        in_specs=[pl.BlockSpec((tq, QK_PAD), lambda b, h, i: (b * nq + i, h)),
                  pl.BlockSpec((seq, QK_PAD), lambda b, h, i: (b, h)),
                  pl.BlockSpec((V_DIM, seq), lambda b, h, i: (b * MLA_HEADS + h, 0))],
        out_specs=pl.BlockSpec((tq, V_DIM), lambda b, h, i: (b * nq + i, h)),
        out_shape=jax.ShapeDtypeStruct((n, MLA_HEADS * V_DIM), BF16),
        compiler_params=_cparams("parallel", "parallel", "arbitrary"),
        name="mla_attention",
    )(q, k, vt)


def _head_sums(x, bd):
    parts = []
    for gi in range(RW_DIM // LANES):
        parts.append(jnp.dot(x[:, gi * LANES:(gi + 1) * LANES], bd, precision=HIGHEST,
                             preferred_element_type=F32))
    return jnp.concatenate(parts, axis=1)


def _sigmoid(x):
    return 1.0 / (1.0 + jnp.exp(-x))


def _rw_prep_body(*refs, tiles_per_seq, tm, has_vfirst):
    (h_ref, hp_ref, hn_ref, mu_ref, vec_ref, w2f_ref, w2b_ref, a2f_ref, a2b_ref, g2_ref, bd_ref) = refs[:11]
    pos = 11
    if has_vfirst:
        v2_ref, vf_ref = refs[pos:pos + 2]
        pos += 2
    r_ref, v_ref, kk_ref, ag_ref, lw_ref, k2_ref, g_ref = refs[pos:pos + 7]

    ti = pl.program_id(0) % tiles_per_seq
    keep_prev = jnp.where(ti == 0, 0.0, 1.0)
    keep_next = jnp.where(ti == tiles_per_seq - 1, 0.0, 1.0)
    row = lax.broadcasted_iota(jnp.int32, (tm, 1), 0)

    def shifted(c0, c1):
        x = h_ref[:, c0:c1]
        prev = jnp.where(row == 0, hp_ref[SUBLANES - 1:SUBLANES, c0:c1] * keep_prev, pltpu.roll(x, 1, axis=0))
        nxt = jnp.where(row == tm - 1, hn_ref[0:1, c0:c1] * keep_next, pltpu.roll(x, tm - 1, axis=0))
        return x + (0.5 * (prev + nxt) - x) * mu_ref[:, c0:c1]

    vec = vec_ref[...]
    w0f, w0b, a0f, a0b, k_k, k_a, v0 = (vec[i:i + 1, :] for i in range(7))

    r = shifted(0, RW_DIM)
    k = shifted(RW_DIM, 2 * RW_DIM)
    v = shifted(2 * RW_DIM, 3 * RW_DIM)
    lo = shifted(3 * RW_DIM, RW_W)
    dw = jnp.tanh(lo[:, 0:LANES]).astype(BF16)
    da = lo[:, LANES:2 * LANES].astype(BF16)
    dg = _sigmoid(lo[:, 2 * LANES:4 * LANES]).astype(BF16)

    def decay(dw_half, w0, w2_ref):
        z = w0 + jnp.dot(dw_half, w2_ref[...], preferred_element_type=F32)
        nz = -z
        softplus = jnp.maximum(nz, 0.0) + jnp.log(1.0 + jnp.exp(-jnp.abs(nz)))
        return -jnp.exp(-softplus - 0.5)

    lw_ref[:, 0:RW_DIM] = decay(dw, w0f, w2f_ref)
    lw_ref[:, RW_DIM:] = decay(dw, w0b, w2b_ref)
    af = _sigmoid(a0f + jnp.dot(da, a2f_ref[...], preferred_element_type=F32))
    ab = _sigmoid(a0b + jnp.dot(da, a2b_ref[...], preferred_element_type=F32))
    ag_ref[:, 0:RW_DIM] = af
    ag_ref[:, RW_DIM:] = ab
    g_ref[...] = jnp.dot(dg, g2_ref[...], preferred_element_type=F32)

    if has_vfirst:
        dv = lo[:, 4 * LANES:5 * LANES].astype(BF16)
        mix = _sigmoid(v0 + jnp.dot(dv, v2_ref[...], preferred_element_type=F32))
        v = v + (vf_ref[...] - v) * mix
    v_ref[...] = v
    r_ref[...] = r

    kk = k * k_k
    ss = _head_sums(kk * kk, bd_ref[...])
    kk_ref[...] = kk * lax.rsqrt(jnp.maximum(ss, 1e-24))
    k2_ref[:, 0:RW_DIM] = k * (1.0 + (af - 1.0) * k_a)
    k2_ref[:, RW_DIM:] = k * (1.0 + (ab - 1.0) * k_a)


def rwkv_prep(h_rw, mu, vec, w2f, w2b, a2f, a2b, g2, bd, v2, v_first, *, seq, tm):
    n = h_rw.shape[0]
    tps = seq // tm
    nb8 = n // SUBLANES
    has_vfirst = v_first is not None
    full = lambda a: pl.BlockSpec(a.shape, lambda i: (0,) * a.ndim)
    in_specs = [pl.BlockSpec((tm, RW_W), lambda i: (i, 0)),
                pl.BlockSpec((SUBLANES, RW_W), lambda i: (jnp.maximum(i * (tm // SUBLANES) - 1, 0), 0)),
                pl.BlockSpec((SUBLANES, RW_W), lambda i: (jnp.minimum((i + 1) * (tm // SUBLANES), nb8 - 1), 0)),
                full(mu), full(vec), full(w2f), full(w2b), full(a2f), full(a2b), full(g2), full(bd)]
    args = [h_rw, h_rw, h_rw, mu, vec, w2f, w2b, a2f, a2b, g2, bd]
    if has_vfirst:
        in_specs += [full(v2), pl.BlockSpec((tm, RW_DIM), lambda i: (i, 0))]
        args += [v2, v_first]
    row = lambda w: pl.BlockSpec((tm, w), lambda i: (i, 0))
    widths = [RW_DIM, RW_DIM, RW_DIM, 2 * RW_DIM, 2 * RW_DIM, 2 * RW_DIM, RW_DIM]
    return pl.pallas_call(
        functools.partial(_rw_prep_body, tiles_per_seq=tps, tm=tm, has_vfirst=has_vfirst),
        grid=(n // tm,),
        in_specs=in_specs,
        out_specs=[row(w) for w in widths],
        out_shape=[jax.ShapeDtypeStruct((n, w), F32) for w in widths],
        compiler_params=_cparams("parallel"),
        name="rwkv_prep",
    )(*args)


def _wkv_body(*refs):
    C = WKV_CHUNK
    in_refs, (yf_ref, yb_ref, st_ref) = refs[:12], refs[12:]
    npair = RW_DIM // LANES

    @pl.when(pl.program_id(1) == 0)
    def _():
        st_ref[...] = jnp.zeros_like(st_ref)

    ti = lax.broadcasted_iota(jnp.int32, (C, C), 0)
    si = lax.broadcasted_iota(jnp.int32, (C, C), 1)
    lane = lax.broadcasted_iota(jnp.int32, (1, LANES), 1)
    hm0 = jnp.where(lane < RW_HEAD, 1.0, 0.0)
    hm1 = 1.0 - hm0
    stack = lambda x: jnp.concatenate([x * hm0, x * hm1], axis=0).astype(BF16)
    dup = lambda x: jnp.concatenate([x, x], axis=0).astype(BF16)
    ri = lax.broadcasted_iota(jnp.int32, (2 * C, 2 * C), 0)
    ci = lax.broadcasted_iota(jnp.int32, (2 * C, 2 * C), 1)
    same = (ri & C) == (ci & C)
    dt = (ri & (C - 1)) - (ci & (C - 1))
    eye = jnp.where(ri == ci, 1.0, 0.0)
    valid = (lax.broadcasted_iota(jnp.int32, (2 * C, LANES), 0) & C) == (
        lax.broadcasted_iota(jnp.int32, (2 * C, LANES), 1) & RW_HEAD)
    mm = functools.partial(jnp.dot, preferred_element_type=F32)
    mm_tb = lambda a, b: lax.dot_general(a, b, _TB, preferred_element_type=F32)
    mm_ta = lambda a, b: lax.dot_general(a, b, _TA, preferred_element_type=F32)

    chains = [(d, p) for d in range(2) for p in range(npair)]
    xa, xr, vst, bws, kws, wtots, gms, strict, incl = [], [], [], [], [], [], [], [], []
    for d in range(2):
        r_ref, v_ref, kk_ref, lw_ref, k_ref, ag_ref = in_refs[6 * d:6 * d + 6]
        sgn = 1 - 2 * d
        tri = jnp.where((ti - si) * sgn >= 0, 1.0, 0.0)
        lw_all = lw_ref[...]
        cum_all = jnp.dot(tri, lw_all, precision=HIGHEST, preferred_element_type=F32)
        tot_all = jnp.sum(lw_all, axis=0, keepdims=True)
        strict_d = same & (dt * sgn > 0)
        incl_d = same & (dt * sgn >= 0)
        for p in range(npair):
            ls = slice(p * LANES, (p + 1) * LANES)
            r, v, kk, k, ag = (x[:, ls] for x in (r_ref, v_ref, kk_ref, k_ref, ag_ref))
            lw, cum, tot = lw_all[:, ls], cum_all[:, ls], tot_all[:, ls]
            ei = jnp.exp(-cum)
            ew = jnp.exp(tot - cum)
            b = kk * ag
            xa.append(stack(-kk * jnp.exp(cum - lw)))
            xr.append(stack(r * jnp.exp(cum)))
            vst.append(stack(v))
            bws.append(stack(b * ew))
            kws.append(stack(k * ew))
            wtots.append(jnp.exp(tot))
            strict.append(strict_d)
            incl.append(incl_d)
            x2 = jnp.concatenate([xa[-1], xr[-1]], axis=0)
            y2 = jnp.concatenate([dup(b * ei), dup(k * ei)], axis=0)
            gms.append(mm_tb(x2, y2))
    nch = range(len(chains))
    l_ab = [jnp.where(strict[i], gms[i][0:2 * C, 0:2 * C], 0.0) for i in nch]
    l_ak = [jnp.where(strict[i], gms[i][0:2 * C, 2 * C:], 0.0).astype(BF16) for i in nch]
    m_rb = [jnp.where(incl[i], gms[i][2 * C:, 0:2 * C], 0.0).astype(BF16) for i in nch]
    m_rk = [jnp.where(incl[i], gms[i][2 * C:, 2 * C:], 0.0).astype(BF16) for i in nch]

    pinv = [eye + l for l in l_ab]
    lp = l_ab
    for _ in range(int(math.log2(C)) - 1):
        lpb = [l.astype(BF16) for l in lp]
        lp = [mm(l, l) for l in lpb]
        pinv = [pi + mm(pi.astype(BF16), l.astype(BF16)) for pi, l in zip(pinv, lp)]

    st = [st_ref[d, p] for d, p in chains]
    s_kv = [s.T.astype(BF16) for s in st]
    pre = [mm(jnp.concatenate([xa[i], l_ak[i]], axis=1), jnp.concatenate([s_kv[i], vst[i]], axis=0))
           for i in nch]
    ustb = [jnp.where(valid, mm(pinv[i].astype(BF16), pre[i].astype(BF16)), 0.0).astype(BF16) for i in nch]
    for i, (d, p) in enumerate(chains):
        yst = mm(jnp.concatenate([xr[i], m_rb[i], m_rk[i]], axis=1),
                 jnp.concatenate([s_kv[i], ustb[i], vst[i]], axis=0))
        yst = jnp.where(valid, yst, 0.0)
        (yf_ref, yb_ref)[d][:, p * LANES:(p + 1) * LANES] = yst[0:C] + yst[C:]
    for i, (d, p) in enumerate(chains):
        st_ref[d, p] = st[i] * wtots[i] + mm_ta(jnp.concatenate([ustb[i], vst[i]], axis=0),
                                                jnp.concatenate([bws[i], kws[i]], axis=0))


def wkv(r, v, kk, lw2, k2, ag2, *, batch, seq):
    n = r.shape[0]
    C = WKV_CHUNK
    nc = seq // C
    fwd = lambda col: pl.BlockSpec((C, RW_DIM), lambda b, c: (b * nc + c, col))
    bwd = lambda col: pl.BlockSpec((C, RW_DIM), lambda b, c: (b * nc + nc - 1 - c, col))
    return pl.pallas_call(
        _wkv_body,
        grid=(batch, nc),
        in_specs=[fwd(0), fwd(0), fwd(0), fwd(0), fwd(0), fwd(0),
                  bwd(0), bwd(0), bwd(0), bwd(1), bwd(1), bwd(1)],
        out_specs=[fwd(0), bwd(0)],
        out_shape=[jax.ShapeDtypeStruct((n, RW_DIM), F32)] * 2,
        scratch_shapes=[pltpu.VMEM((2, RW_DIM // LANES, LANES, LANES), F32)],
        compiler_params=_cparams("parallel", "arbitrary"),
        name="wkv_scan",
    )(r, v, kk, lw2, k2, ag2, r, v, kk, lw2, k2, ag2)


def _rw_post_body(yf_ref, yb_ref, r_ref, k2_ref, v_ref, g_ref, vec_ref, bd_ref, o_ref):
    vec = vec_ref[...]
    lnx_w, lnx_b, r_k = (vec[i:i + 1, :] for i in range(3))
    bd = bd_ref[...]
    y = yf_ref[...] + yb_ref[...]
    inv = 1.0 / RW_HEAD
    mean = _head_sums(y, bd) * inv
    d = y - mean
    var = _head_sums(d * d, bd) * inv
    yn = d * lax.rsqrt(var + LNX_EPS) * lnx_w + lnx_b
    ksum = k2_ref[:, 0:RW_DIM] + k2_ref[:, RW_DIM:]
    bonus = _head_sums(r_ref[...] * ksum * r_k, bd) * v_ref[...]
    o_ref[...] = ((yn + bonus) * g_ref[...]).astype(o_ref.dtype)


def rwkv_post(yf, yb, r, k2, v, g, vec, bd, *, tm):
    n = r.shape[0]
    row = lambda w: pl.BlockSpec((tm, w), lambda i: (i, 0))
    full = lambda a: pl.BlockSpec(a.shape, lambda i: (0,) * a.ndim)
    return pl.pallas_call(
        _rw_post_body,
        grid=(n // tm,),
        in_specs=[row(RW_DIM), row(RW_DIM), row(RW_DIM), row(2 * RW_DIM), row(RW_DIM), row(RW_DIM),
                  full(vec), full(bd)],
        out_specs=row(RW_DIM),
        out_shape=jax.ShapeDtypeStruct((n, RW_DIM), BF16),
        compiler_params=_cparams("parallel"),
        name="rwkv_post",
    )(yf, yb, r, k2, v, g, vec, bd)


def _merge_body(o_ref, yg_ref, ga_ref, gr_ref, wa_ref, wr_ref, m_ref):
    attn = jnp.dot(o_ref[...], wa_ref[...], preferred_element_type=F32)
    rw = jnp.dot(yg_ref[...], wr_ref[...], preferred_element_type=F32)
    m_ref[...] = (_sigmoid(ga_ref[...].astype(F32)) * attn + _sigmoid(gr_ref[...].astype(F32)) * rw).astype(m_ref.dtype)


def merge_branches(o, yg, gates, wa, wr, *, tm, tn):
    n = o.shape[0]
    nj = D_MODEL // tn
    return pl.pallas_call(
        _merge_body,
        grid=(n // tm, nj),
        in_specs=[pl.BlockSpec((tm, MLA_HEADS * V_DIM), lambda i, j: (i, 0)),
                  pl.BlockSpec((tm, RW_DIM), lambda i, j: (i, 0)),
                  pl.BlockSpec((tm, tn), lambda i, j: (i, j)),
                  pl.BlockSpec((tm, tn), lambda i, j: (i, nj + j)),
                  pl.BlockSpec((MLA_HEADS * V_DIM, tn), lambda i, j: (0, j)),
                  pl.BlockSpec((RW_DIM, tn), lambda i, j: (0, j))],
        out_specs=pl.BlockSpec((tm, tn), lambda i, j: (i, j)),
        out_shape=jax.ShapeDtypeStruct((n, D_MODEL), BF16),
        compiler_params=_cparams("parallel", "parallel"),
        name="merge_branches",
    )(o, yg, gates, gates, wa, wr)


def _mm_res_body(a_ref, w_ref, x_ref, o_ref):
    o_ref[...] = x_ref[...] + jnp.dot(a_ref[...], w_ref[...], preferred_element_type=F32)


def matmul_residual(a, w, x, *, tm, tn):
    n, k = a.shape
    nc = w.shape[1]
    return pl.pallas_call(
        _mm_res_body,
        grid=(n // tm, nc // tn),
        in_specs=[pl.BlockSpec((tm, k), lambda i, j: (i, 0)),
                  pl.BlockSpec((k, tn), lambda i, j: (0, j)),
                  pl.BlockSpec((tm, tn), lambda i, j: (i, j))],
        out_specs=pl.BlockSpec((tm, tn), lambda i, j: (i, j)),
        out_shape=jax.ShapeDtypeStruct((n, nc), F32),
        compiler_params=_cparams("parallel", "parallel"),
        name="matmul_residual",
    )(a, w, x)


def _top16_rows(ss, idx):
    big = 3.0e38
    vals = [[] for _ in ss]
    poss = [[] for _ in ss]
    for _ in range(PEER_TOPK):
        ms = [jnp.max(s, axis=0, keepdims=True) for s in ss]
        ps = [jnp.min(jnp.where(s == m, idx, big), axis=0, keepdims=True) for s, m in zip(ss, ms)]
        ss = [jnp.where(idx == p, -jnp.inf, s) for s, p in zip(ss, ps)]
        for i, (m, p) in enumerate(zip(ms, ps)):
            vals[i].append(m)
            poss[i].append(p)
    return [(jnp.concatenate(v, axis=0), jnp.concatenate(p, axis=0)) for v, p in zip(vals, poss)]


def _candidates(sv0, sv1):
    tok = sv0.shape[1]
    io = lambda rows: lax.broadcasted_iota(jnp.int32, (rows, tok), 0).astype(F32)
    half = PEER_TOPK // 2
    parts = [sv0[0:1, :] + sv1]
    idxs = [io(PEER_TOPK)]
    for a in range(1, half):
        parts.append(sv0[a:a + 1, :] + sv1[0:half, :])
        idxs.append(io(half) + float(a * PEER_TOPK))
    parts.append(sv0[half:, :] + sv1[0:1, :])
    idxs.append((io(half) + float(half)) * float(PEER_TOPK))
    return jnp.concatenate(parts, axis=0), jnp.concatenate(idxs, axis=0)


def _pick_rows(table, sel):
    out = jnp.zeros(sel.shape, table.dtype)
    for a in range(PEER_TOPK):
        out = jnp.where(sel == a, table[a:a + 1, :], out)
    return out


def _peer_topk_body(q_ref, sk_ref, i1_ref, i2_ref, gt_ref):
    i1s, i2s, gts = [], [], []
    key_idx = lax.broadcasted_iota(jnp.int32, (N_KEYS, q_ref.shape[0]), 0).astype(F32)
    for h in range(PEER_HEADS):
        ss = []
        for p in range(2):
            c0 = (2 * h + p) * PEER_HALF
            qb = q_ref[:, c0:c0 + PEER_HALF].astype(BF16)
            ss.append(lax.dot_general(sk_ref[p], qb, _TB, preferred_element_type=F32))
        (sv0, si0), (sv1, si1) = _top16_rows(ss, key_idx)
        cand, cand_idx = _candidates(sv0, sv1)
        ((top_s, top_j),) = _top16_rows([cand], cand_idx)
        ja = jnp.floor(top_j * (1.0 / PEER_TOPK))
        i1s.append(_pick_rows(si0, ja))
        i2s.append(_pick_rows(si1, top_j - ja * PEER_TOPK))
        e = jnp.exp(top_s - top_s[0:1, :])
        gts.append(e / jnp.sum(e, axis=0, keepdims=True))
    i1_ref[...] = jnp.concatenate(i1s, axis=0).T
    i2_ref[...] = jnp.concatenate(i2s, axis=0).T
    gt_ref[...] = jnp.concatenate(gts, axis=0).T


def peer_topk(q, sub_keys):
    n = q.shape[0]
    tmk = LANES
    out = pl.BlockSpec((tmk, NSLOT), lambda i: (i, 0))
    return pl.pallas_call(
        _peer_topk_body,
        grid=(n // tmk,),
        in_specs=[pl.BlockSpec((tmk, 2 * PEER_HALF * PEER_HEADS), lambda i: (i, 0)),
                  pl.BlockSpec(sub_keys.shape, lambda i: (0, 0, 0))],
        out_specs=[out, out, out],
        out_shape=[jax.ShapeDtypeStruct((n, NSLOT), F32)] * 3,
        compiler_params=_cparams("parallel"),
        name="peer_topk",
    )(q, sub_keys)


def _gelu_exact(x):
    return 0.5 * x * (1.0 + lax.erf(x * (1.0 / math.sqrt(2.0))))


_HI16 = 0xFFFF0000
G_UNROLL = SUBLANES
E_SUB = 2 * N_KEYS
PEER_TE = 2 * E_SUB


def _bf16_bits(x):
    return pltpu.bitcast(x, jnp.uint32) + jnp.uint32(0x8000)


def _peer_dense_body(xn_ref, i1_ref, i2_ref, gt_ref, ut0_ref, utb_ref, utn_ref, v_ref, x_ref, lnf_ref, scal_ref,
                     o_ref, g_ref, h_ref, *, tm, te, final_norm):
    e = pl.program_id(1)
    half = tm // 2
    mm_tb = lambda a, b: lax.dot_general(a, b, _TB, preferred_element_type=F32)
    hsc, csc, osc = (scal_ref[r:r + 1, 0:1] for r in range(3))
    nsub = te // E_SUB

    def activations(ut_ref):
        xn = xn_ref[...]
        return [jnp.dot(xn, ut_ref[:, j * E_SUB:(j + 1) * E_SUB], preferred_element_type=F32) * hsc
                for j in range(nsub)]

    def accumulate(block, hs):
        coefs = []
        for j in range(nsub):
            row0 = block * (te // N_KEYS) + j * (E_SUB // N_KEYS)
            words = [g_ref[pl.ds(row0 + r, half, stride=G_PITCH), :] for r in range(E_SUB // N_KEYS)]
            top = jnp.concatenate([pltpu.bitcast(w & jnp.uint32(_HI16), F32) for w in words], axis=1)
            bot = jnp.concatenate([pltpu.bitcast(w << 16, F32) for w in words], axis=1)
            gate = jnp.concatenate([top, bot], axis=0)
            coef = gate * _gelu_exact(hs[j]) * csc
            coefs.append(jnp.clip(coef, -FP8_MAX, FP8_MAX).astype(FP8))
        return coefs

    @pl.when(e == 0)
    def _():
        for j, h in enumerate(activations(ut0_ref)):
            h_ref[:, j * E_SUB:(j + 1) * E_SUB] = h
        o_ref[...] = x_ref[...] * osc
        sub = lax.broadcasted_iota(jnp.int32, (N_KEYS, NSLOT), 0).astype(F32).astype(BF16)
        one = jnp.ones((), BF16)
        zero = jnp.zeros((), BF16)

        def build(it, carry):
            base = pl.multiple_of(it * G_UNROLL, SUBLANES)
            tiles = [(ref[pl.ds(base, G_UNROLL), :], ref[pl.ds(half + base, G_UNROLL), :])
                     for ref in (i1_ref, i2_ref, gt_ref)]
            a_ts, b_ts = [], []
            for j in range(G_UNROLL):
                for t in range(2):
                    i1, i2, gt = (tiles[q][t][j:j + 1, :].astype(BF16) for q in range(3))
                    a_ts.append(jnp.where(sub == i1, one, zero))
                    b_ts.append(jnp.where(sub == i2, gt, zero))
            gs = [mm_tb(a, b) for a, b in zip(a_ts, b_ts)]
            for j in range(G_UNROLL):
                word = (_bf16_bits(gs[2 * j]) & jnp.uint32(_HI16)) | (_bf16_bits(gs[2 * j + 1]) >> 16)
                g_ref[pl.ds(pl.multiple_of((base + j) * G_PITCH, SUBLANES), N_KEYS), :] = word
            return carry

        lax.fori_loop(0, half // G_UNROLL, build, 0)

    hs_b = activations(utb_ref)
    coef_a = accumulate(2 * e, [h_ref[:, j * E_SUB:(j + 1) * E_SUB] for j in range(nsub)])
    for j, h in enumerate(activations(utn_ref)):
        h_ref[:, j * E_SUB:(j + 1) * E_SUB] = h
    coef_b = accumulate(2 * e + 1, hs_b)
    o_ref[...] += jnp.dot(jnp.concatenate(coef_a + coef_b, axis=1), v_ref[...], preferred_element_type=F32)

    @pl.when(e == pl.num_programs(1) - 1)
    def _():
        out = o_ref[...] * scal_ref[3:4, 0:1]
        if final_norm:
            out = _rms_rows(out, lnf_ref[...], EPS)
        o_ref[...] = out


def peer_dense(xn, i1, i2, gt, ut, v, x, ln_f, scal, *, tm, te, final_norm):
    n = xn.shape[0]
    ne = v.shape[0] // te
    assert tm % (2 * G_UNROLL) == 0 and te % E_SUB == 0 and ne % 2 == 0 and ut.shape == (ne, D_MODEL, te)
    row = lambda w, **kw: pl.BlockSpec((tm, w), lambda i, e: (i, 0), **kw)
    once = dict(pipeline_mode=pl.Buffered(1))
    return pl.pallas_call(
        functools.partial(_peer_dense_body, tm=tm, te=te, final_norm=final_norm),
        grid=(n // tm, ne // 2),
        in_specs=[row(D_MODEL, **once), row(NSLOT), row(NSLOT), row(NSLOT),
                  pl.BlockSpec((None, D_MODEL, te), lambda i, e: (0, 0, 0)),
                  pl.BlockSpec((None, D_MODEL, te), lambda i, e: (2 * e + 1, 0, 0)),
                  pl.BlockSpec((None, D_MODEL, te), lambda i, e: (jnp.minimum(2 * e + 2, ne - 1), 0, 0)),
                  pl.BlockSpec((2 * te, D_MODEL), lambda i, e: (e, 0)),
                  row(D_MODEL, **once),
                  pl.BlockSpec((1, D_MODEL), lambda i, e: (0, 0)),
                  pl.BlockSpec(scal.shape, lambda i, e: (0, 0))],
        out_specs=row(D_MODEL),
        out_shape=jax.ShapeDtypeStruct((n, D_MODEL), F32),
        scratch_shapes=[pltpu.VMEM((tm // 2 * G_PITCH, N_KEYS), jnp.uint32), pltpu.VMEM((tm, te), F32)],
        compiler_params=_cparams("parallel", "arbitrary"),
        name="peer_dense",
    )(xn, i1, i2, gt, ut, ut, ut, v, x, ln_f.reshape(1, D_MODEL), scal)


def _pad_rows(a, h):
    return jnp.pad(a, ((0, h - a.shape[0]),) + ((0, 0),) * (a.ndim - 1))


def _swap_halves(a):
    half = a.shape[-1] // 2
    return jnp.concatenate([a[..., half:], a[..., :half]], axis=-1)


def _layer_weights(i, p):
    first = i == 0
    w_in = p["w_in_first"] if first else p["w_in_rest"][i - 1]
    mu = p["mu_first"] if first else p["mu_rest"][i - 1]
    mla_cols = Q_LORA + KV_LORA + ROPE_DIM
    gate0 = mla_cols
    rw0 = mla_cols + 2 * D_MODEL
    k_rope_w = w_in[:, Q_LORA + KV_LORA:mla_cols]
    w_mla = jnp.concatenate([w_in[:, :mla_cols], _swap_halves(k_rope_w)], axis=1)
    w_gate = w_in[:, gate0:rw0]

    def rw_layout(a):
        c = 3 * RW_DIM
        dwf, dwb = a[..., c:c + DECAY_LORA], a[..., c + DECAY_LORA:c + 2 * DECAY_LORA]
        c += 2 * DECAY_LORA
        daf, dab = a[..., c:c + AAA_LORA], a[..., c + AAA_LORA:c + 2 * AAA_LORA]
        c += 2 * AAA_LORA
        dg = a[..., c:c + GATE_LORA]
        c += GATE_LORA
        dv = a[..., c:]
        z = lambda w: jnp.zeros(a.shape[:-1] + (w,), a.dtype)
        return jnp.concatenate([a[..., :3 * RW_DIM], dwf, dwb, daf, dab, dg, z(2 * LANES - GATE_LORA),
                                dv, z(2 * LANES - dv.shape[-1])], axis=-1)

    w_rw = rw_layout(w_in[:, rw0:])
    mu_l = rw_layout(mu[None, :])
    zeros64 = jnp.zeros((DECAY_LORA, RW_DIM), F32)
    wq = p["w_uq"][i].reshape(Q_LORA, MLA_HEADS, NOPE_DIM + ROPE_DIM)
    wq = jnp.concatenate([wq, _swap_halves(wq[..., NOPE_DIM:])], axis=-1).reshape(Q_LORA, MLA_HEADS * QK_PAD)
    vec = jnp.stack([p["w0_f"][i], p["w0_b"][i], p["a0_f"][i], p["a0_b"][i], p["k_k"][i], p["k_a"][i],
                     p["v0_rest"][i - 1] if not first else jnp.zeros((RW_DIM,), F32),
                     jnp.zeros((RW_DIM,), F32)])
    lw = {
        "ln1": p["ln1"][i], "w_mla": w_mla.astype(BF16), "w_gate": w_gate.astype(BF16),
        "w_rw": w_rw.astype(BF16),
        "mu": mu_l, "q_norm": p["q_norm"][i], "wq": wq.astype(BF16), "kv_norm": p["kv_norm"][i],
        "wkv": p["w_ukv"][i].astype(BF16), "w_o_attn": p["w_o_attn"][i].astype(BF16),
        "vec": vec,
        "w2f": jnp.concatenate([p["w2_f"][i], zeros64]).astype(BF16),
        "w2b": jnp.concatenate([zeros64, p["w2_b"][i]]).astype(BF16),
        "a2f": jnp.concatenate([p["a2_f"][i], zeros64]).astype(BF16),
        "a2b": jnp.concatenate([zeros64, p["a2_b"][i]]).astype(BF16),
        "g2": _pad_rows(p["g2"][i], 2 * LANES).astype(BF16),
        "v2": None if first else _pad_rows(p["v2_rest"][i - 1], LANES).astype(BF16),
        "post_vec": jnp.stack([p["lnx_w"][i], p["lnx_b"][i], p["r_k"][i].reshape(RW_DIM)]
                              + [jnp.zeros((RW_DIM,), F32)] * 5),
        "w_o_rwkv": p["w_o_rwkv"][i].astype(BF16), "w_out": p["w_out"][i].astype(BF16),
        "ln2": p["ln2"][i], "w_pq": p["w_pq"][i].astype(BF16), "sub_keys": p["sub_keys"][i].astype(BF16),
    }
    pow2_below = lambda v: jnp.exp2(jnp.floor(jnp.log2(v)))
    tiny = 1e-30
    u, v, g2n = p["peer_u"][i], p["peer_v"][i], p["ln2"][i]
    xmax = math.sqrt(D_MODEL) * jnp.maximum(jnp.max(jnp.abs(g2n)), tiny)
    sx = pow2_below(FP8_MAX / xmax)
    su = pow2_below(FP8_MAX / jnp.maximum(jnp.max(jnp.abs(u)), tiny))
    sv = pow2_below(FP8_MAX / jnp.maximum(jnp.max(jnp.abs(v)), tiny))
    hmax = xmax * jnp.sqrt(jnp.max(jnp.sum(u * u, axis=1))) * (1.0 + 2.0 ** -4) ** 2
    sc = pow2_below(FP8_MAX / (PEER_HEADS * jnp.maximum(hmax, tiny)))
    lw["ln2_rows"] = jnp.stack([g2n, g2n * sx])
    lw["peer_ut"] = (u * su).astype(FP8).reshape(-1, PEER_TE, D_MODEL).transpose(0, 2, 1)
    lw["peer_v"] = (v * sv).astype(FP8)
    scal = jnp.stack([1.0 / (sx * su), sc, sc * sv, 1.0 / (sc * sv)] + [jnp.zeros((), F32)] * 4)
    lw["peer_scal"] = jnp.broadcast_to(scal[:, None], (SUBLANES, LANES)).astype(F32)
    return lw


def _rope_table(seq):
    inv = 1.0 / (ROPE_THETA ** (jnp.arange(0, ROPE_DIM, 2, dtype=F32) / ROPE_DIM))
    ang = jnp.arange(seq, dtype=F32)[:, None] * inv[None, :]
    c, s = jnp.cos(ang), jnp.sin(ang)
    return jnp.concatenate([c, c, -s, s], axis=1)


def _tiles(seq):
    return dict(tm=min(512, seq), tmm=min(1024, seq), tq=min(16 * ATTN_SUB, seq), tprep=min(256, seq),
                tpeer=min(512, seq))


def _trunk(x, layers, ln_f, bd):
    batch, seq, _ = x.shape
    n = batch * seq
    t = _tiles(seq)
    tm, tmm = t["tm"], t["tmm"]
    ccss = _rope_table(seq)
    x = x.reshape(n, D_MODEL)
    v_first = None
    for li, lw in enumerate(layers):
        (xn1,) = rms_norm(x, lw["ln1"].reshape(1, D_MODEL), (BF16,), tm=tm)
        h_mla = matmul(xn1, lw["w_mla"], tm=tmm, tn=MLA_W, out_dtype=BF16)
        gates = matmul(xn1, lw["w_gate"], tm=tmm, tn=1024, out_dtype=BF16)
        h_rw = matmul(xn1, lw["w_rw"], tm=tmm, tn=1280, out_dtype=F32)
        q = mla_q_proj(h_mla, lw["q_norm"], lw["wq"], ccss, seq=seq, tm=tm)
        k, vt = mla_kv_proj(h_mla, lw["kv_norm"], lw["wkv"], ccss, seq=seq, tm=tm)
        o = attention(q, k, vt, batch=batch, seq=seq, tq=t["tq"])
        r, vv, kk, ag2, lw2, k2, g = rwkv_prep(h_rw, lw["mu"], lw["vec"], lw["w2f"], lw["w2b"], lw["a2f"],
                                              lw["a2b"], lw["g2"], bd, lw["v2"], v_first,
                                              seq=seq, tm=t["tprep"])
        if v_first is None:
            v_first = vv
        yf, yb = wkv(r, vv, kk, lw2, k2, ag2, batch=batch, seq=seq)
        yg = rwkv_post(yf, yb, r, k2, vv, g, lw["post_vec"], bd, tm=t["tprep"])
        m = merge_branches(o, yg, gates, lw["w_o_attn"], lw["w_o_rwkv"], tm=tmm, tn=512)
        x = matmul_residual(m, lw["w_out"], x, tm=tmm, tn=1024)
        xn, xn8 = rms_norm(x, lw["ln2_rows"], (BF16, FP8), tm=tm)
        qp = matmul(xn, lw["w_pq"], tm=tmm, tn=1024, out_dtype=F32)
        i1, i2, gt = peer_topk(qp, lw["sub_keys"])
        x = peer_dense(xn8, i1, i2, gt, lw["peer_ut"], lw["peer_v"], x, ln_f, lw["peer_scal"],
                       tm=t["tpeer"], te=PEER_TE,
                       final_norm=li == len(layers) - 1)
    return x.reshape(batch, seq, D_MODEL)


def kernel(x_prompt, x_sample, ln1, w_in_first, mu_first, w_in_rest, mu_rest, q_norm, w_uq, kv_norm, w_ukv, w_o_attn, w0_f, w2_f, w0_b, w2_b, a0_f, a2_f, a0_b, a2_b, g2, k_k, k_a, r_k, lnx_w, lnx_b, v0_rest, v2_rest, w_o_rwkv, w_out, ln2, w_pq, sub_keys, peer_u, peer_v, ln_f):
    p = dict(ln1=ln1, w_in_first=w_in_first, mu_first=mu_first, w_in_rest=w_in_rest, mu_rest=mu_rest,
             q_norm=q_norm, w_uq=w_uq, kv_norm=kv_norm, w_ukv=w_ukv, w_o_attn=w_o_attn, w0_f=w0_f, w2_f=w2_f,
             w0_b=w0_b, w2_b=w2_b, a0_f=a0_f, a2_f=a2_f, a0_b=a0_b, a2_b=a2_b, g2=g2, k_k=k_k, k_a=k_a, r_k=r_k,
             lnx_w=lnx_w, lnx_b=lnx_b, v0_rest=v0_rest, v2_rest=v2_rest, w_o_rwkv=w_o_rwkv, w_out=w_out,
             ln2=ln2, w_pq=w_pq, sub_keys=sub_keys, peer_u=peer_u, peer_v=peer_v)
    depth = ln1.shape[0]
    layers = [_layer_weights(i, p) for i in range(depth)]
    lane = jnp.arange(LANES)
    bd = (lane[:, None] // RW_HEAD == lane[None, :] // RW_HEAD).astype(F32)
    return (_trunk(x_prompt, layers, ln_f, bd), _trunk(x_sample, layers, ln_f, bd))
```

```python
import functools
import math

import jax
import jax.numpy as jnp
from jax import lax
from jax.experimental import pallas as pl
from jax.experimental.pallas import tpu as pltpu

F32 = jnp.float32
BF16 = jnp.bfloat16
FP8 = jnp.float8_e4m3fn
FP8_MAX = 448.0
HIGHEST = lax.Precision.HIGHEST

LANES = 128
SUBLANES = 8
VMEM_LIMIT_BYTES = 56 * 1024 * 1024

D_MODEL = 2048
MLA_HEADS = 16
Q_LORA = 512
KV_LORA = 512
NOPE_DIM = 128
ROPE_DIM = 64
V_DIM = 128
ROPE_THETA = 10000.0
RW_HEAD = 64
RW_HEADS = 16
RW_DIM = RW_HEADS * RW_HEAD
DECAY_LORA = 64
AAA_LORA = 64
MV_LORA = 32
GATE_LORA = 160
LNX_EPS = 64e-5
PEER_HEADS = 8
N_KEYS = 128
PEER_HALF = 128
PEER_TOPK = 16
EPS = 1e-6

QK_PAD = 2 * LANES
MLA_W = Q_LORA + KV_LORA + LANES
LORA_W = 2 * DECAY_LORA + 2 * AAA_LORA + 2 * LANES + 2 * LANES
RW_W = 3 * RW_DIM + LORA_W
PROJ_HEADS = 4
ATTN_SUB = 256
WKV_CHUNK = 64
G_PITCH = N_KEYS + SUBLANES
NSLOT = PEER_HEADS * PEER_TOPK

_TB = (((1,), (1,)), ((), ()))
_TA = (((0,), (0,)), ((), ()))


def _cparams(*sem):
    return pltpu.CompilerParams(dimension_semantics=sem, vmem_limit_bytes=VMEM_LIMIT_BYTES)


def _rms_rows(x, g, eps):
    ms = jnp.mean(x * x, axis=-1, keepdims=True)
    return x * lax.rsqrt(ms + eps) * g


def _rms_norm_body(x_ref, g_ref, *o_refs):
    x = x_ref[...].astype(F32)
    xhat = x * lax.rsqrt(jnp.mean(x * x, axis=-1, keepdims=True) + EPS)
    for row, o_ref in enumerate(o_refs):
        o_ref[...] = (xhat * g_ref[row:row + 1, :]).astype(o_ref.dtype)


def rms_norm(x, gains, out_dtypes, *, tm):
    n, k = x.shape
    spec = pl.BlockSpec((tm, k), lambda i: (i, 0))
    return pl.pallas_call(
        _rms_norm_body,
        grid=(n // tm,),
        in_specs=[spec, pl.BlockSpec(gains.shape, lambda i: (0, 0))],
        out_specs=[spec] * len(out_dtypes),
        out_shape=[jax.ShapeDtypeStruct((n, k), dt) for dt in out_dtypes],
        compiler_params=_cparams("parallel"),
        name="rms_norm",
    )(x, gains)


def _mm_body(a_ref, w_ref, o_ref):
    o_ref[...] = jnp.dot(a_ref[...], w_ref[...], preferred_element_type=F32).astype(o_ref.dtype)


def matmul(a, w, *, tm, tn, out_dtype):
    n, k = a.shape
    nc = w.shape[1]
    tn = min(tn, nc)
    assert n % tm == 0 and nc % tn == 0 and w.shape[0] == k
    return pl.pallas_call(
        _mm_body,
        grid=(n // tm, nc // tn),
        in_specs=[pl.BlockSpec((tm, k), lambda i, j: (i, 0)),
                  pl.BlockSpec((k, tn), lambda i, j: (0, j))],
        out_specs=pl.BlockSpec((tm, tn), lambda i, j: (i, j)),
        out_shape=jax.ShapeDtypeStruct((n, nc), out_dtype),
        compiler_params=_cparams("parallel", "parallel"),
        name="matmul",
    )(a, w)


def _rope_lanes(y2, ccss):
    w = y2 * ccss
    return w + pltpu.roll(w, ROPE_DIM, axis=1)


def _qproj_body(c_ref, g_ref, w_ref, t_ref, q_ref, cn_ref):
    @pl.when(pl.program_id(1) == 0)
    def _():
        cn_ref[...] = _rms_rows(c_ref[...].astype(F32), g_ref[...], EPS).astype(BF16)

    y = jnp.dot(cn_ref[...], w_ref[...], preferred_element_type=F32)
    tab = t_ref[...]
    lane = lax.broadcasted_iota(jnp.int32, tab.shape, 1)
    for hh in range(PROJ_HEADS):
        c0 = hh * QK_PAD
        q_ref[:, c0:c0 + LANES] = y[:, c0:c0 + LANES].astype(q_ref.dtype)
        rot = _rope_lanes(y[:, c0 + LANES:c0 + QK_PAD], tab)
        q_ref[:, c0 + LANES:c0 + QK_PAD] = jnp.where(lane < ROPE_DIM, rot, 0.0).astype(q_ref.dtype)


def mla_q_proj(h_mla, q_norm, wq, ccss, *, seq, tm):
    n = h_mla.shape[0]
    nt = seq // tm
    return pl.pallas_call(
        _qproj_body,
        grid=(n // tm, MLA_HEADS // PROJ_HEADS),
        in_specs=[pl.BlockSpec((tm, Q_LORA), lambda i, h: (i, 0)),
                  pl.BlockSpec((1, Q_LORA), lambda i, h: (0, 0)),
                  pl.BlockSpec((Q_LORA, PROJ_HEADS * QK_PAD), lambda i, h: (0, h)),
                  pl.BlockSpec((tm, LANES), lambda i, h: (i % nt, 0))],
        out_specs=pl.BlockSpec((tm, PROJ_HEADS * QK_PAD), lambda i, h: (i, h)),
        out_shape=jax.ShapeDtypeStruct((n, MLA_HEADS * QK_PAD), BF16),
        scratch_shapes=[pltpu.VMEM((tm, Q_LORA), BF16)],
        compiler_params=_cparams("parallel", "arbitrary"),
        name="mla_q_proj",
    )(h_mla, q_norm.reshape(1, Q_LORA), wq, ccss)


def _kvproj_body(c_ref, kr_ref, g_ref, w_ref, t_ref, k_ref, v_ref, cn_ref, krr_ref):
    @pl.when(pl.program_id(1) == 0)
    def _():
        cn_ref[...] = _rms_rows(c_ref[...].astype(F32), g_ref[...], EPS).astype(BF16)
        krr_ref[...] = _rope_lanes(kr_ref[...].astype(F32), t_ref[...]).astype(BF16)

    y = jnp.dot(cn_ref[...], w_ref[...], preferred_element_type=F32)
    for hh in range(PROJ_HEADS):
        c0 = hh * (NOPE_DIM + V_DIM)
        k_ref[:, hh * QK_PAD:hh * QK_PAD + LANES] = y[:, c0:c0 + NOPE_DIM].astype(k_ref.dtype)
        k_ref[:, hh * QK_PAD + LANES:(hh + 1) * QK_PAD] = krr_ref[...]
        v_ref[:, hh * V_DIM:(hh + 1) * V_DIM] = y[:, c0 + NOPE_DIM:c0 + NOPE_DIM + V_DIM].astype(v_ref.dtype)


def mla_kv_proj(h_mla, kv_norm, wkv, ccss, *, seq, tm):
    n = h_mla.shape[0]
    nt = seq // tm
    return pl.pallas_call(
        _kvproj_body,
        grid=(n // tm, MLA_HEADS // PROJ_HEADS),
        in_specs=[pl.BlockSpec((tm, KV_LORA), lambda i, h: (i, 1)),
                  pl.BlockSpec((tm, LANES), lambda i, h: (i, (Q_LORA + KV_LORA) // LANES)),
                  pl.BlockSpec((1, KV_LORA), lambda i, h: (0, 0)),
                  pl.BlockSpec((KV_LORA, PROJ_HEADS * (NOPE_DIM + V_DIM)), lambda i, h: (0, h)),
                  pl.BlockSpec((tm, LANES), lambda i, h: (i % nt, 0))],
        out_specs=[pl.BlockSpec((tm, PROJ_HEADS * QK_PAD), lambda i, h: (i, h)),
                   pl.BlockSpec((tm, PROJ_HEADS * V_DIM), lambda i, h: (i, h))],
        out_shape=[jax.ShapeDtypeStruct((n, MLA_HEADS * QK_PAD), BF16),
                   jax.ShapeDtypeStruct((n, MLA_HEADS * V_DIM), BF16)],
        scratch_shapes=[pltpu.VMEM((tm, KV_LORA), BF16), pltpu.VMEM((tm, LANES), BF16)],
        compiler_params=_cparams("parallel", "arbitrary"),
        name="mla_kv_proj",
    )(h_mla, h_mla, kv_norm.reshape(1, KV_LORA), wkv, ccss)


def _attn_body(q_ref, k_ref, v_ref, o_ref, *, nsub):
    ts = q_ref.shape[0] // nsub
    k = k_ref[...]
    v = v_ref[...]

    def scores(j):
        return lax.dot_general(q_ref[j * ts:(j + 1) * ts, :], k, _TB, preferred_element_type=F32)

    def finish(s, j):
        m = jnp.max(s, axis=-1, keepdims=True)
        p = jnp.exp(s - m)
        l = jnp.sum(p, axis=-1, keepdims=True)
        o = jnp.dot(p.astype(BF16), v, preferred_element_type=F32)
        o_ref[j * ts:(j + 1) * ts, :] = (o / l).astype(o_ref.dtype)

    s_prev = scores(0)
    for j in range(1, nsub):
        s_next = scores(j)
        finish(s_prev, j - 1)
        s_prev = s_next
    finish(s_prev, nsub - 1)


def attention(q, k, v, *, batch, seq, tq):
    n = q.shape[0]
    nq = seq // tq
    return pl.pallas_call(
        functools.partial(_attn_body, nsub=tq // ATTN_SUB),
        grid=(batch, MLA_HEADS, nq),
        in_specs=[pl.BlockSpec((tq, QK_PAD), lambda b, h, i: (b * nq + i, h)),
                  pl.BlockSpec((seq, QK_PAD), lambda b, h, i: (b, h)),
                  pl.BlockSpec((seq, V_DIM), lambda b, h, i: (b, h))],
        out_specs=pl.BlockSpec((tq, V_DIM), lambda b, h, i: (b * nq + i, h)),
        out_shape=jax.ShapeDtypeStruct((n, MLA_HEADS * V_DIM), BF16),
        compiler_params=_cparams("parallel", "parallel", "arbitrary"),
        name="mla_attention",
    )(q, k, v)


def _head_sums(x, bd):
    hi = x.astype(BF16)
    lo = (x - hi.astype(F32)).astype(BF16)
    bd2 = jnp.concatenate([bd, bd], axis=0).astype(BF16)
    parts = []
    for gi in range(RW_DIM // LANES):
        ls = slice(gi * LANES, (gi + 1) * LANES)
        parts.append(jnp.dot(jnp.concatenate([hi[:, ls], lo[:, ls]], axis=1), bd2, preferred_element_type=F32))
    return jnp.concatenate(parts, axis=1)


def _sigmoid(x):
    return 1.0 / (1.0 + jnp.exp(-x))


def _rw_prep_body(*refs, tiles_per_seq, tm, has_vfirst):
    (h_ref, hp_ref, hn_ref, mu_ref, vec_ref, w2f_ref, w2b_ref, a2f_ref, a2b_ref, g2_ref, bd_ref) = refs[:11]
    pos = 11
    if has_vfirst:
        v2_ref, vf_ref = refs[pos:pos + 2]
        pos += 2
    r_ref, v_ref, kk_ref, ag_ref, lw_ref, k2_ref, g_ref = refs[pos:pos + 7]

    ti = pl.program_id(0) % tiles_per_seq
    keep_prev = jnp.where(ti == 0, 0.0, 1.0)
    keep_next = jnp.where(ti == tiles_per_seq - 1, 0.0, 1.0)
    row = lax.broadcasted_iota(jnp.int32, (tm, 1), 0)

    def shifted(c0, c1):
        x = h_ref[:, c0:c1]
        prev = jnp.where(row == 0, hp_ref[SUBLANES - 1:SUBLANES, c0:c1] * keep_prev, pltpu.roll(x, 1, axis=0))
        nxt = jnp.where(row == tm - 1, hn_ref[0:1, c0:c1] * keep_next, pltpu.roll(x, tm - 1, axis=0))
        return x * mu_ref[0:1, c0:c1] + (prev + nxt) * mu_ref[1:2, c0:c1]

    vec = vec_ref[...]
    w0f, w0b, a0f, a0b, k_k, k_a, v0 = (vec[i:i + 1, :] for i in range(7))

    r = shifted(0, RW_DIM)
    k = shifted(RW_DIM, 2 * RW_DIM)
    v = shifted(2 * RW_DIM, 3 * RW_DIM)
    lo = shifted(3 * RW_DIM, RW_W)
    dw = jnp.tanh(lo[:, 0:LANES]).astype(BF16)
    da = lo[:, LANES:2 * LANES].astype(BF16)
    dg = _sigmoid(lo[:, 2 * LANES:4 * LANES]).astype(BF16)

    def decay(dw_half, w0, w2_ref):
        z = w0 + jnp.dot(dw_half, w2_ref[...], preferred_element_type=F32)
        return _sigmoid(z) * (-math.exp(-0.5))

    lw_ref[:, 0:RW_DIM] = decay(dw, w0f, w2f_ref)
    lw_ref[:, RW_DIM:] = decay(dw, w0b, w2b_ref)
    af = _sigmoid(a0f + jnp.dot(da, a2f_ref[...], preferred_element_type=F32))
    ab = _sigmoid(a0b + jnp.dot(da, a2b_ref[...], preferred_element_type=F32))
    ag_ref[:, 0:RW_DIM] = af
    ag_ref[:, RW_DIM:] = ab
    g_ref[...] = jnp.dot(dg, g2_ref[...], preferred_element_type=F32)

    if has_vfirst:
        dv = lo[:, 4 * LANES:5 * LANES].astype(BF16)
        mix = _sigmoid(v0 + jnp.dot(dv, v2_ref[...], preferred_element_type=F32))
        v = v + (vf_ref[...] - v) * mix
    v_ref[...] = v
    r_ref[...] = r

    kk = k * k_k
    ss = _head_sums(kk * kk, bd_ref[...])
    kk_ref[...] = kk * lax.rsqrt(jnp.maximum(ss, 1e-24))
    k2_ref[:, 0:RW_DIM] = k * (1.0 + (af - 1.0) * k_a)
    k2_ref[:, RW_DIM:] = k * (1.0 + (ab - 1.0) * k_a)


def rwkv_prep(h_rw, mu, vec, w2f, w2b, a2f, a2b, g2, bd, v2, v_first, *, seq, tm):
    n = h_rw.shape[0]
    tps = seq // tm
    nb8 = n // SUBLANES
    has_vfirst = v_first is not None
    full = lambda a: pl.BlockSpec(a.shape, lambda i: (0,) * a.ndim)
    in_specs = [pl.BlockSpec((tm, RW_W), lambda i: (i, 0)),
                pl.BlockSpec((SUBLANES, RW_W), lambda i: (jnp.maximum(i * (tm // SUBLANES) - 1, 0), 0)),
                pl.BlockSpec((SUBLANES, RW_W), lambda i: (jnp.minimum((i + 1) * (tm // SUBLANES), nb8 - 1), 0)),
                full(mu), full(vec), full(w2f), full(w2b), full(a2f), full(a2b), full(g2), full(bd)]
    args = [h_rw, h_rw, h_rw, mu, vec, w2f, w2b, a2f, a2b, g2, bd]
    if has_vfirst:
        in_specs += [full(v2), pl.BlockSpec((tm, RW_DIM), lambda i: (i, 0))]
        args += [v2, v_first]
    row = lambda w: pl.BlockSpec((tm, w), lambda i: (i, 0))
    widths = [RW_DIM, RW_DIM, RW_DIM, 2 * RW_DIM, 2 * RW_DIM, 2 * RW_DIM, RW_DIM]
    return pl.pallas_call(
        functools.partial(_rw_prep_body, tiles_per_seq=tps, tm=tm, has_vfirst=has_vfirst),
        grid=(n // tm,),
        in_specs=in_specs,
        out_specs=[row(w) for w in widths],
        out_shape=[jax.ShapeDtypeStruct((n, w), F32) for w in widths],
        compiler_params=_cparams("parallel"),
        name="rwkv_prep",
    )(*args)


def _wkv_body(*refs):
    C = WKV_CHUNK
    in_refs, (yf_ref, yb_ref, st_ref) = refs[:12], refs[12:]
    npair = RW_DIM // LANES

    @pl.when(pl.program_id(1) == 0)
    def _():
        st_ref[...] = jnp.zeros_like(st_ref)

    ti = lax.broadcasted_iota(jnp.int32, (C, C), 0)
    si = lax.broadcasted_iota(jnp.int32, (C, C), 1)
    lane = lax.broadcasted_iota(jnp.int32, (1, LANES), 1)
    hm0 = jnp.where(lane < RW_HEAD, 1.0, 0.0)
    hm1 = 1.0 - hm0
    stack = lambda x: jnp.concatenate([x * hm0, x * hm1], axis=0).astype(BF16)
    dup = lambda x: jnp.concatenate([x, x], axis=0).astype(BF16)
    ri = lax.broadcasted_iota(jnp.int32, (2 * C, 2 * C), 0)
    ci = lax.broadcasted_iota(jnp.int32, (2 * C, 2 * C), 1)
    same = (ri & C) == (ci & C)
    dt = (ri & (C - 1)) - (ci & (C - 1))
    eye = jnp.where(ri == ci, 1.0, 0.0)
    valid = (lax.broadcasted_iota(jnp.int32, (2 * C, LANES), 0) & C) == (
        lax.broadcasted_iota(jnp.int32, (2 * C, LANES), 1) & RW_HEAD)
    mm = functools.partial(jnp.dot, preferred_element_type=F32)
    mm_tb = lambda a, b: lax.dot_general(a, b, _TB, preferred_element_type=F32)
    mm_ta = lambda a, b: lax.dot_general(a, b, _TA, preferred_element_type=F32)

    chains = [(d, p) for d in range(2) for p in range(npair)]
    xa, xr, vst, bws, kws, wtots, gms, strict, incl = [], [], [], [], [], [], [], [], []
    for d in range(2):
        r_ref, v_ref, kk_ref, lw_ref, k_ref, ag_ref = in_refs[6 * d:6 * d + 6]
        sgn = 1 - 2 * d
        tri = jnp.where((ti - si) * sgn >= 0, 1.0, 0.0)
        lw_all = lw_ref[...]
        cum_all = jnp.dot(tri, lw_all, precision=HIGHEST, preferred_element_type=F32)
        tot_all = jnp.sum(lw_all, axis=0, keepdims=True)
        strict_d = same & (dt * sgn > 0)
        incl_d = same & (dt * sgn >= 0)
        for p in range(npair):
            ls = slice(p * LANES, (p + 1) * LANES)
            r, v, kk, k, ag = (x[:, ls] for x in (r_ref, v_ref, kk_ref, k_ref, ag_ref))
            lw, cum, tot = lw_all[:, ls], cum_all[:, ls], tot_all[:, ls]
            ei = jnp.exp(-cum)
            ew = jnp.exp(tot - cum)
            b = kk * ag
            xa.append(stack(-kk * jnp.exp(cum - lw)))
            xr.append(stack(r * jnp.exp(cum)))
            vst.append(stack(v))
            bws.append(stack(b * ew))
            kws.append(stack(k * ew))
            wtots.append(jnp.exp(tot))
            strict.append(strict_d)
            incl.append(incl_d)
            x2 = jnp.concatenate([xa[-1], xr[-1]], axis=0)
            y2 = jnp.concatenate([dup(b * ei), dup(k * ei)], axis=0)
            gms.append(mm_tb(x2, y2))
    nch = range(len(chains))
    l_ab = [jnp.where(strict[i], gms[i][0:2 * C, 0:2 * C], 0.0) for i in nch]
    l_ak = [jnp.where(strict[i], gms[i][0:2 * C, 2 * C:], 0.0).astype(BF16) for i in nch]
    m_rb = [jnp.where(incl[i], gms[i][2 * C:, 0:2 * C], 0.0).astype(BF16) for i in nch]
    m_rk = [jnp.where(incl[i], gms[i][2 * C:, 2 * C:], 0.0).astype(BF16) for i in nch]

    pinv = [eye + l for l in l_ab]
    lp = l_ab
    for _ in range(int(math.log2(C)) - 1):
        lpb = [l.astype(BF16) for l in lp]
        lp = [mm(l, l) for l in lpb]
        pinv = [pi + mm(pi.astype(BF16), l.astype(BF16)) for pi, l in zip(pinv, lp)]

    st = [st_ref[d, p] for d, p in chains]
    s_kv = [s.T.astype(BF16) for s in st]
    pre = [mm(jnp.concatenate([xa[i], l_ak[i]], axis=1), jnp.concatenate([s_kv[i], vst[i]], axis=0))
           for i in nch]
    ustb = [jnp.where(valid, mm(pinv[i].astype(BF16), pre[i].astype(BF16)), 0.0).astype(BF16) for i in nch]
    for i, (d, p) in enumerate(chains):
        yst = mm(jnp.concatenate([xr[i], m_rb[i], m_rk[i]], axis=1),
                 jnp.concatenate([s_kv[i], ustb[i], vst[i]], axis=0))
        yst = jnp.where(valid, yst, 0.0)
        (yf_ref, yb_ref)[d][:, p * LANES:(p + 1) * LANES] = yst[0:C] + yst[C:]
    for i, (d, p) in enumerate(chains):
        st_ref[d, p] = st[i] * wtots[i] + mm_ta(jnp.concatenate([ustb[i], vst[i]], axis=0),
                                                jnp.concatenate([bws[i], kws[i]], axis=0))


def wkv(r, v, kk, lw2, k2, ag2, *, batch, seq):
    n = r.shape[0]
    C = WKV_CHUNK
    nc = seq // C
    fwd = lambda col: pl.BlockSpec((C, RW_DIM), lambda b, c: (b * nc + c, col))
    bwd = lambda col: pl.BlockSpec((C, RW_DIM), lambda b, c: (b * nc + nc - 1 - c, col))
    return pl.pallas_call(
        _wkv_body,
        grid=(batch, nc),
        in_specs=[fwd(0), fwd(0), fwd(0), fwd(0), fwd(0), fwd(0),
                  bwd(0), bwd(0), bwd(0), bwd(1), bwd(1), bwd(1)],
        out_specs=[fwd(0), bwd(0)],
        out_shape=[jax.ShapeDtypeStruct((n, RW_DIM), F32)] * 2,
        scratch_shapes=[pltpu.VMEM((2, RW_DIM // LANES, LANES, LANES), F32)],
        compiler_params=_cparams("parallel", "arbitrary"),
        name="wkv_scan",
    )(r, v, kk, lw2, k2, ag2, r, v, kk, lw2, k2, ag2)


def _rw_post_body(yf_ref, yb_ref, r_ref, k2_ref, v_ref, g_ref, vec_ref, bd_ref, o_ref):
    vec = vec_ref[...]
    lnx_w, lnx_b, r_k = (vec[i:i + 1, :] for i in range(3))
    bd = bd_ref[...]
    y = yf_ref[...] + yb_ref[...]
    inv = 1.0 / RW_HEAD
    mean = _head_sums(y, bd) * inv
    d = y - mean
    var = _head_sums(d * d, bd) * inv
    yn = d * lax.rsqrt(var + LNX_EPS) * lnx_w + lnx_b
    ksum = k2_ref[:, 0:RW_DIM] + k2_ref[:, RW_DIM:]
    bonus = _head_sums(r_ref[...] * ksum * r_k, bd) * v_ref[...]
    o_ref[...] = ((yn + bonus) * g_ref[...]).astype(o_ref.dtype)


def rwkv_post(yf, yb, r, k2, v, g, vec, bd, *, tm):
    n = r.shape[0]
    row = lambda w: pl.BlockSpec((tm, w), lambda i: (i, 0))
    full = lambda a: pl.BlockSpec(a.shape, lambda i: (0,) * a.ndim)
    return pl.pallas_call(
        _rw_post_body,
        grid=(n // tm,),
        in_specs=[row(RW_DIM), row(RW_DIM), row(RW_DIM), row(2 * RW_DIM), row(RW_DIM), row(RW_DIM),
                  full(vec), full(bd)],
        out_specs=row(RW_DIM),
        out_shape=jax.ShapeDtypeStruct((n, RW_DIM), BF16),
        compiler_params=_cparams("parallel"),
        name="rwkv_post",
    )(yf, yb, r, k2, v, g, vec, bd)


def _merge_body(o_ref, yg_ref, ga_ref, gr_ref, wa_ref, wr_ref, m_ref):
    attn = jnp.dot(o_ref[...], wa_ref[...], preferred_element_type=F32)
    rw = jnp.dot(yg_ref[...], wr_ref[...], preferred_element_type=F32)
    m_ref[...] = (_sigmoid(ga_ref[...].astype(F32)) * attn + _sigmoid(gr_ref[...].astype(F32)) * rw).astype(m_ref.dtype)


def merge_branches(o, yg, gates, wa, wr, *, tm, tn):
    n = o.shape[0]
    nj = D_MODEL // tn
    return pl.pallas_call(
        _merge_body,
        grid=(n // tm, nj),
        in_specs=[pl.BlockSpec((tm, MLA_HEADS * V_DIM), lambda i, j: (i, 0)),
                  pl.BlockSpec((tm, RW_DIM), lambda i, j: (i, 0)),
                  pl.BlockSpec((tm, tn), lambda i, j: (i, j)),
                  pl.BlockSpec((tm, tn), lambda i, j: (i, nj + j)),
                  pl.BlockSpec((MLA_HEADS * V_DIM, tn), lambda i, j: (0, j)),
                  pl.BlockSpec((RW_DIM, tn), lambda i, j: (0, j))],
        out_specs=pl.BlockSpec((tm, tn), lambda i, j: (i, j)),
        out_shape=jax.ShapeDtypeStruct((n, D_MODEL), BF16),
        compiler_params=_cparams("parallel", "parallel"),
        name="merge_branches",
    )(o, yg, gates, gates, wa, wr)


def _mm_res_body(a_ref, w_ref, x_ref, o_ref):
    o_ref[...] = x_ref[...] + jnp.dot(a_ref[...], w_ref[...], preferred_element_type=F32)


def matmul_residual(a, w, x, *, tm, tn):
    n, k = a.shape
    nc = w.shape[1]
    return pl.pallas_call(
        _mm_res_body,
        grid=(n // tm, nc // tn),
        in_specs=[pl.BlockSpec((tm, k), lambda i, j: (i, 0)),
                  pl.BlockSpec((k, tn), lambda i, j: (0, j)),
                  pl.BlockSpec((tm, tn), lambda i, j: (i, j))],
        out_specs=pl.BlockSpec((tm, tn), lambda i, j: (i, j)),
        out_shape=jax.ShapeDtypeStruct((n, nc), F32),
        compiler_params=_cparams("parallel", "parallel"),
        name="matmul_residual",
    )(a, w, x)


def _top16_rows(ss, idx):
    big = 3.0e38
    vals = [[] for _ in ss]
    poss = [[] for _ in ss]
    for _ in range(PEER_TOPK):
        ms = [jnp.max(s, axis=0, keepdims=True) for s in ss]
        ps = [jnp.min(jnp.where(s == m, idx, big), axis=0, keepdims=True) for s, m in zip(ss, ms)]
        ss = [jnp.where(idx == p, -jnp.inf, s) for s, p in zip(ss, ps)]
        for i, (m, p) in enumerate(zip(ms, ps)):
            vals[i].append(m)
            poss[i].append(p)
    return [(jnp.concatenate(v, axis=0), jnp.concatenate(p, axis=0)) for v, p in zip(vals, poss)]


def _candidates(sv0, sv1):
    tok = sv0.shape[1]
    io = lambda rows: lax.broadcasted_iota(jnp.int32, (rows, tok), 0).astype(F32)
    half = PEER_TOPK // 2
    parts = [sv0[0:1, :] + sv1]
    idxs = [io(PEER_TOPK)]
    for a in range(1, half):
        parts.append(sv0[a:a + 1, :] + sv1[0:half, :])
        idxs.append(io(half) + float(a * PEER_TOPK))
    parts.append(sv0[half:, :] + sv1[0:1, :])
    idxs.append((io(half) + float(half)) * float(PEER_TOPK))
    return jnp.concatenate(parts, axis=0), jnp.concatenate(idxs, axis=0)


def _pick_rows(table, sel):
    out = jnp.zeros(sel.shape, table.dtype)
    for a in range(PEER_TOPK):
        out = jnp.where(sel == a, table[a:a + 1, :], out)
    return out


def _peer_topk_body(q_ref, sk_ref, i1_ref, i2_ref, gt_ref):
    i1s, i2s, gts = [], [], []
    key_idx = lax.broadcasted_iota(jnp.int32, (N_KEYS, q_ref.shape[0]), 0).astype(F32)
    for h in range(PEER_HEADS):
        ss = []
        for p in range(2):
            c0 = (2 * h + p) * PEER_HALF
            qb = q_ref[:, c0:c0 + PEER_HALF].astype(BF16)
            ss.append(lax.dot_general(sk_ref[p], qb, _TB, preferred_element_type=F32))
        (sv0, si0), (sv1, si1) = _top16_rows(ss, key_idx)
        cand, cand_idx = _candidates(sv0, sv1)
        ((top_s, top_j),) = _top16_rows([cand], cand_idx)
        ja = jnp.floor(top_j * (1.0 / PEER_TOPK))
        i1s.append(_pick_rows(si0, ja))
        i2s.append(_pick_rows(si1, top_j - ja * PEER_TOPK))
        e = jnp.exp(top_s - top_s[0:1, :])
        gts.append(e / jnp.sum(e, axis=0, keepdims=True))
    i1_ref[...] = jnp.concatenate(i1s, axis=0).T
    i2_ref[...] = jnp.concatenate(i2s, axis=0).T
    gt_ref[...] = jnp.concatenate(gts, axis=0).T


def peer_topk(q, sub_keys):
    n = q.shape[0]
    tmk = LANES
    out = pl.BlockSpec((tmk, NSLOT), lambda i: (i, 0))
    return pl.pallas_call(
        _peer_topk_body,
        grid=(n // tmk,),
        in_specs=[pl.BlockSpec((tmk, 2 * PEER_HALF * PEER_HEADS), lambda i: (i, 0)),
                  pl.BlockSpec(sub_keys.shape, lambda i: (0, 0, 0))],
        out_specs=[out, out, out],
        out_shape=[jax.ShapeDtypeStruct((n, NSLOT), F32)] * 3,
        compiler_params=_cparams("parallel"),
        name="peer_topk",
    )(q, sub_keys)


def _gelu_exact(x):
    return 0.5 * x * (1.0 + lax.erf(x * (1.0 / math.sqrt(2.0))))


_HI16 = 0xFFFF0000
G_UNROLL = SUBLANES
E_SUB = 2 * N_KEYS
PEER_TE = 2 * E_SUB


def _bf16_bits(x):
    return pltpu.bitcast(x, jnp.uint32) + jnp.uint32(0x8000)


def _peer_dense_body(xn_ref, i1_ref, i2_ref, gt_ref, ut0_ref, utb_ref, utn_ref, v_ref, x_ref, lnf_ref, scal_ref,
                     o_ref, g_ref, h_ref, *, tm, te, final_norm):
    e = pl.program_id(1)
    half = tm // 2
    mm_tb = lambda a, b: lax.dot_general(a, b, _TB, preferred_element_type=F32)
    hsc, csc, osc = (scal_ref[r:r + 1, 0:1] for r in range(3))
    nsub = te // E_SUB

    def activations(ut_ref):
        xn = xn_ref[...]
        return [jnp.dot(xn, ut_ref[:, j * E_SUB:(j + 1) * E_SUB], preferred_element_type=F32) * hsc
                for j in range(nsub)]

    def accumulate(block, hs):
        coefs = []
        for j in range(nsub):
            row0 = block * (te // N_KEYS) + j * (E_SUB // N_KEYS)
            words = [g_ref[pl.ds(row0 + r, half, stride=G_PITCH), :] for r in range(E_SUB // N_KEYS)]
            top = jnp.concatenate([pltpu.bitcast(w & jnp.uint32(_HI16), F32) for w in words], axis=1)
            bot = jnp.concatenate([pltpu.bitcast(w << 16, F32) for w in words], axis=1)
            gate = jnp.concatenate([top, bot], axis=0)
            coef = gate * _gelu_exact(hs[j]) * csc
            coefs.append(jnp.clip(coef, -FP8_MAX, FP8_MAX).astype(FP8))
        return coefs

    @pl.when(e == 0)
    def _():
        for j, h in enumerate(activations(ut0_ref)):
            h_ref[:, j * E_SUB:(j + 1) * E_SUB] = h
        o_ref[...] = x_ref[...] * osc
        sub = lax.broadcasted_iota(jnp.int32, (N_KEYS, NSLOT), 0).astype(F32).astype(BF16)
        one = jnp.ones((), BF16)
        zero = jnp.zeros((), BF16)

        def build(it, carry):
            base = pl.multiple_of(it * G_UNROLL, SUBLANES)
            tiles = [(ref[pl.ds(base, G_UNROLL), :], ref[pl.ds(half + base, G_UNROLL), :])
                     for ref in (i1_ref, i2_ref, gt_ref)]
            a_ts, b_ts = [], []
            for j in range(G_UNROLL):
                for t in range(2):
                    i1, i2, gt = (tiles[q][t][j:j + 1, :].astype(BF16) for q in range(3))
                    a_ts.append(jnp.where(sub == i1, one, zero))
                    b_ts.append(jnp.where(sub == i2, gt, zero))
            gs = [mm_tb(a, b) for a, b in zip(a_ts, b_ts)]
            for j in range(G_UNROLL):
                word = (_bf16_bits(gs[2 * j]) & jnp.uint32(_HI16)) | (_bf16_bits(gs[2 * j + 1]) >> 16)
                g_ref[pl.ds(pl.multiple_of((base + j) * G_PITCH, SUBLANES), N_KEYS), :] = word
            return carry

        lax.fori_loop(0, half // G_UNROLL, build, 0)

    hs_b = activations(utb_ref)
    coef_a = accumulate(2 * e, [h_ref[:, j * E_SUB:(j + 1) * E_SUB] for j in range(nsub)])
    for j, h in enumerate(activations(utn_ref)):
        h_ref[:, j * E_SUB:(j + 1) * E_SUB] = h
    coef_b = accumulate(2 * e + 1, hs_b)
    o_ref[...] += jnp.dot(jnp.concatenate(coef_a + coef_b, axis=1), v_ref[...], preferred_element_type=F32)

    @pl.when(e == pl.num_programs(1) - 1)
    def _():
        out = o_ref[...] * scal_ref[3:4, 0:1]
        if final_norm:
            out = _rms_rows(out, lnf_ref[...], EPS)
        o_ref[...] = out


def peer_dense(xn, i1, i2, gt, ut, v, x, ln_f, scal, *, tm, te, final_norm):
    n = xn.shape[0]
    ne = v.shape[0] // te
    assert tm % (2 * G_UNROLL) == 0 and te % E_SUB == 0 and ne % 2 == 0 and ut.shape == (ne, D_MODEL, te)
    row = lambda w, **kw: pl.BlockSpec((tm, w), lambda i, e: (i, 0), **kw)
    once = dict(pipeline_mode=pl.Buffered(1))
    return pl.pallas_call(
        functools.partial(_peer_dense_body, tm=tm, te=te, final_norm=final_norm),
        grid=(n // tm, ne // 2),
        in_specs=[row(D_MODEL, **once), row(NSLOT), row(NSLOT), row(NSLOT),
                  pl.BlockSpec((None, D_MODEL, te), lambda i, e: (0, 0, 0)),
                  pl.BlockSpec((None, D_MODEL, te), lambda i, e: (2 * e + 1, 0, 0)),
                  pl.BlockSpec((None, D_MODEL, te), lambda i, e: (jnp.minimum(2 * e + 2, ne - 1), 0, 0)),
                  pl.BlockSpec((2 * te, D_MODEL), lambda i, e: (e, 0)),
                  row(D_MODEL, **once),
                  pl.BlockSpec((1, D_MODEL), lambda i, e: (0, 0)),
                  pl.BlockSpec(scal.shape, lambda i, e: (0, 0))],
        out_specs=row(D_MODEL),
        out_shape=jax.ShapeDtypeStruct((n, D_MODEL), F32),
        scratch_shapes=[pltpu.VMEM((tm // 2 * G_PITCH, N_KEYS), jnp.uint32), pltpu.VMEM((tm, te), F32)],
        compiler_params=_cparams("parallel", "arbitrary"),
        name="peer_dense",
    )(xn, i1, i2, gt, ut, ut, ut, v, x, ln_f.reshape(1, D_MODEL), scal)


def _pad_rows(a, h):
    return jnp.pad(a, ((0, h - a.shape[0]),) + ((0, 0),) * (a.ndim - 1))


def _swap_halves(a):
    half = a.shape[-1] // 2
    return jnp.concatenate([a[..., half:], a[..., :half]], axis=-1)


def _layer_weights(i, p):
    first = i == 0
    w_in = p["w_in_first"] if first else p["w_in_rest"][i - 1]
    mu = p["mu_first"] if first else p["mu_rest"][i - 1]
    mla_cols = Q_LORA + KV_LORA + ROPE_DIM
    gate0 = mla_cols
    rw0 = mla_cols + 2 * D_MODEL
    k_rope_w = w_in[:, Q_LORA + KV_LORA:mla_cols]
    w_mla = jnp.concatenate([w_in[:, :mla_cols], _swap_halves(k_rope_w)], axis=1)
    w_gate = w_in[:, gate0:rw0]

    def rw_layout(a):
        c = 3 * RW_DIM
        dwf, dwb = a[..., c:c + DECAY_LORA], a[..., c + DECAY_LORA:c + 2 * DECAY_LORA]
        c += 2 * DECAY_LORA
        daf, dab = a[..., c:c + AAA_LORA], a[..., c + AAA_LORA:c + 2 * AAA_LORA]
        c += 2 * AAA_LORA
        dg = a[..., c:c + GATE_LORA]
        c += GATE_LORA
        dv = a[..., c:]
        z = lambda w: jnp.zeros(a.shape[:-1] + (w,), a.dtype)
        return jnp.concatenate([a[..., :3 * RW_DIM], dwf, dwb, daf, dab, dg, z(2 * LANES - GATE_LORA),
                                dv, z(2 * LANES - dv.shape[-1])], axis=-1)

    w_rw = rw_layout(w_in[:, rw0:])
    mu_l = rw_layout(mu[None, :])
    mu_l = jnp.concatenate([1.0 - mu_l, 0.5 * mu_l], axis=0)
    zeros64 = jnp.zeros((DECAY_LORA, RW_DIM), F32)
    wq = (p["w_uq"][i] * (NOPE_DIM + ROPE_DIM) ** -0.5).reshape(Q_LORA, MLA_HEADS, NOPE_DIM + ROPE_DIM)
    wq = jnp.concatenate([wq, _swap_halves(wq[..., NOPE_DIM:])], axis=-1).reshape(Q_LORA, MLA_HEADS * QK_PAD)
    vec = jnp.stack([p["w0_f"][i], p["w0_b"][i], p["a0_f"][i], p["a0_b"][i], p["k_k"][i], p["k_a"][i],
                     p["v0_rest"][i - 1] if not first else jnp.zeros((RW_DIM,), F32),
                     jnp.zeros((RW_DIM,), F32)])
    lw = {
        "ln1": p["ln1"][i], "w_mla": w_mla.astype(BF16), "w_gate": w_gate.astype(BF16),
        "w_rw": w_rw.astype(BF16),
        "mu": mu_l, "q_norm": p["q_norm"][i], "wq": wq.astype(BF16), "kv_norm": p["kv_norm"][i],
        "wkv": p["w_ukv"][i].astype(BF16), "w_o_attn": p["w_o_attn"][i].astype(BF16),
        "vec": vec,
        "w2f": jnp.concatenate([p["w2_f"][i], zeros64]).astype(BF16),
        "w2b": jnp.concatenate([zeros64, p["w2_b"][i]]).astype(BF16),
        "a2f": jnp.concatenate([p["a2_f"][i], zeros64]).astype(BF16),
        "a2b": jnp.concatenate([zeros64, p["a2_b"][i]]).astype(BF16),
        "g2": _pad_rows(p["g2"][i], 2 * LANES).astype(BF16),
        "v2": None if first else _pad_rows(p["v2_rest"][i - 1], LANES).astype(BF16),
        "post_vec": jnp.stack([p["lnx_w"][i], p["lnx_b"][i], p["r_k"][i].reshape(RW_DIM)]
                              + [jnp.zeros((RW_DIM,), F32)] * 5),
        "w_o_rwkv": p["w_o_rwkv"][i].astype(BF16), "w_out": p["w_out"][i].astype(BF16),
        "ln2": p["ln2"][i], "w_pq": p["w_pq"][i].astype(BF16), "sub_keys": p["sub_keys"][i].astype(BF16),
    }
    pow2_below = lambda v: jnp.exp2(jnp.floor(jnp.log2(v)))
    tiny = 1e-30
    u, v, g2n = p["peer_u"][i], p["peer_v"][i], p["ln2"][i]
    xmax = math.sqrt(D_MODEL) * jnp.maximum(jnp.max(jnp.abs(g2n)), tiny)
    sx = pow2_below(FP8_MAX / xmax)
    su = pow2_below(FP8_MAX / jnp.maximum(jnp.max(jnp.abs(u)), tiny))
    sv = pow2_below(FP8_MAX / jnp.maximum(jnp.max(jnp.abs(v)), tiny))
    hmax = xmax * jnp.sqrt(jnp.max(jnp.sum(u * u, axis=1))) * (1.0 + 2.0 ** -4) ** 2
    sc = pow2_below(FP8_MAX / (PEER_HEADS * jnp.maximum(hmax, tiny)))
    lw["ln2_rows"] = jnp.stack([g2n, g2n * sx])
    lw["peer_ut"] = (u * su).astype(FP8).reshape(-1, PEER_TE, D_MODEL).transpose(0, 2, 1)
    lw["peer_v"] = (v * sv).astype(FP8)
    scal = jnp.stack([1.0 / (sx * su), sc, sc * sv, 1.0 / (sc * sv)] + [jnp.zeros((), F32)] * 4)
    lw["peer_scal"] = jnp.broadcast_to(scal[:, None], (SUBLANES, LANES)).astype(F32)
    return lw


def _rope_table(seq):
    inv = 1.0 / (ROPE_THETA ** (jnp.arange(0, ROPE_DIM, 2, dtype=F32) / ROPE_DIM))
    ang = jnp.arange(seq, dtype=F32)[:, None] * inv[None, :]
    c, s = jnp.cos(ang), jnp.sin(ang)
    return jnp.concatenate([c, c, -s, s], axis=1)


def _tiles(seq):
    return dict(tm=min(512, seq), tmm=min(1024, seq), tq=min(16 * ATTN_SUB, seq), tprep=min(256, seq),
                tpeer=min(512, seq))


def _trunk(x, layers, ln_f, bd):
    batch, seq, _ = x.shape
    n = batch * seq
    t = _tiles(seq)
    tm, tmm = t["tm"], t["tmm"]
    ccss = _rope_table(seq)
    x = x.reshape(n, D_MODEL)
    v_first = None
    for li, lw in enumerate(layers):
        (xn1,) = rms_norm(x, lw["ln1"].reshape(1, D_MODEL), (BF16,), tm=tm)
        h_mla = matmul(xn1, lw["w_mla"], tm=tmm, tn=MLA_W, out_dtype=BF16)
        gates = matmul(xn1, lw["w_gate"], tm=tmm, tn=1024, out_dtype=BF16)
        h_rw = matmul(xn1, lw["w_rw"], tm=tmm, tn=1280, out_dtype=F32)
        q = mla_q_proj(h_mla, lw["q_norm"], lw["wq"], ccss, seq=seq, tm=tm)
        k, v = mla_kv_proj(h_mla, lw["kv_norm"], lw["wkv"], ccss, seq=seq, tm=tm)
        o = attention(q, k, v, batch=batch, seq=seq, tq=t["tq"])
        r, vv, kk, ag2, lw2, k2, g = rwkv_prep(h_rw, lw["mu"], lw["vec"], lw["w2f"], lw["w2b"], lw["a2f"],
                                              lw["a2b"], lw["g2"], bd, lw["v2"], v_first,
                                              seq=seq, tm=t["tprep"])
        if v_first is None:
            v_first = vv
        yf, yb = wkv(r, vv, kk, lw2, k2, ag2, batch=batch, seq=seq)
        yg = rwkv_post(yf, yb, r, k2, vv, g, lw["post_vec"], bd, tm=t["tprep"])
        m = merge_branches(o, yg, gates, lw["w_o_attn"], lw["w_o_rwkv"], tm=tmm, tn=512)
        x = matmul_residual(m, lw["w_out"], x, tm=tmm, tn=1024)
        xn, xn8 = rms_norm(x, lw["ln2_rows"], (BF16, FP8), tm=tm)
        qp = matmul(xn, lw["w_pq"], tm=tmm, tn=1024, out_dtype=F32)
        i1, i2, gt = peer_topk(qp, lw["sub_keys"])
        x = peer_dense(xn8, i1, i2, gt, lw["peer_ut"], lw["peer_v"], x, ln_f, lw["peer_scal"],
                       tm=t["tpeer"], te=PEER_TE,
                       final_norm=li == len(layers) - 1)
    return x.reshape(batch, seq, D_MODEL)


def kernel(x_prompt, x_sample, ln1, w_in_first, mu_first, w_in_rest, mu_rest, q_norm, w_uq, kv_norm, w_ukv, w_o_attn, w0_f, w2_f, w0_b, w2_b, a0_f, a2_f, a0_b, a2_b, g2, k_k, k_a, r_k, lnx_w, lnx_b, v0_rest, v2_rest, w_o_rwkv, w_out, ln2, w_pq, sub_keys, peer_u, peer_v, ln_f):
    p = dict(ln1=ln1, w_in_first=w_in_first, mu_first=mu_first, w_in_rest=w_in_rest, mu_rest=mu_rest,
             q_norm=q_norm, w_uq=w_uq, kv_norm=kv_norm, w_ukv=w_ukv, w_o_attn=w_o_attn, w0_f=w0_f, w2_f=w2_f,
             w0_b=w0_b, w2_b=w2_b, a0_f=a0_f, a2_f=a2_f, a0_b=a0_b, a2_b=a2_b, g2=g2, k_k=k_k, k_a=k_a, r_k=r_k,
             lnx_w=lnx_w, lnx_b=lnx_b, v0_rest=v0_rest, v2_rest=v2_rest, w_o_rwkv=w_o_rwkv, w_out=w_out,
             ln2=ln2, w_pq=w_pq, sub_keys=sub_keys, peer_u=peer_u, peer_v=peer_v)
    depth = ln1.shape[0]
    layers = [_layer_weights(i, p) for i in range(depth)]
    lane = jnp.arange(LANES)
    bd = (lane[:, None] // RW_HEAD == lane[None, :] // RW_HEAD).astype(F32)
    return (_trunk(x_prompt, layers, ln_f, bd), _trunk(x_sample, layers, ln_f, bd))
```

```python
import functools
import math

import jax
import jax.numpy as jnp
from jax import lax
from jax.experimental import pallas as pl
from jax.experimental.pallas import tpu as pltpu

F32 = jnp.float32
BF16 = jnp.bfloat16
FP8 = jnp.float8_e4m3fn
FP8_MAX = 448.0
HIGHEST = lax.Precision.HIGHEST

LANES = 128
SUBLANES = 8
VMEM_LIMIT_BYTES = 56 * 1024 * 1024

D_MODEL = 2048
MLA_HEADS = 16
Q_LORA = 512
KV_LORA = 512
NOPE_DIM = 128
ROPE_DIM = 64
V_DIM = 128
ROPE_THETA = 10000.0
RW_HEAD = 64
RW_HEADS = 16
RW_DIM = RW_HEADS * RW_HEAD
DECAY_LORA = 64
AAA_LORA = 64
MV_LORA = 32
GATE_LORA = 160
LNX_EPS = 64e-5
PEER_HEADS = 8
N_KEYS = 128
PEER_HALF = 128
PEER_TOPK = 16
EPS = 1e-6

QK_PAD = 2 * LANES
MLA_W = Q_LORA + KV_LORA + LANES
LORA_W = 2 * DECAY_LORA + 2 * AAA_LORA + 2 * LANES + 2 * LANES
RW_W = 3 * RW_DIM + LORA_W
PROJ_HEADS = 8
ATTN_SUB = 256
WKV_CHUNK = 64
G_PITCH = N_KEYS + SUBLANES
NSLOT = PEER_HEADS * PEER_TOPK

_TB = (((1,), (1,)), ((), ()))
_TA = (((0,), (0,)), ((), ()))


def _cparams(*sem):
    return pltpu.CompilerParams(dimension_semantics=sem, vmem_limit_bytes=VMEM_LIMIT_BYTES)


def _rms_rows(x, g, eps):
    ms = jnp.mean(x * x, axis=-1, keepdims=True)
    return x * lax.rsqrt(ms + eps) * g


def _rms_norm_body(x_ref, g_ref, *o_refs):
    x = x_ref[...].astype(F32)
    xhat = x * lax.rsqrt(jnp.mean(x * x, axis=-1, keepdims=True) + EPS)
    for row, o_ref in enumerate(o_refs):
        o_ref[...] = (xhat * g_ref[row:row + 1, :]).astype(o_ref.dtype)


def rms_norm(x, gains, out_dtypes, *, tm):
    n, k = x.shape
    spec = pl.BlockSpec((tm, k), lambda i: (i, 0))
    return pl.pallas_call(
        _rms_norm_body,
        grid=(n // tm,),
        in_specs=[spec, pl.BlockSpec(gains.shape, lambda i: (0, 0))],
        out_specs=[spec] * len(out_dtypes),
        out_shape=[jax.ShapeDtypeStruct((n, k), dt) for dt in out_dtypes],
        compiler_params=_cparams("parallel"),
        name="rms_norm",
    )(x, gains)


def _mm_body(a_ref, w_ref, o_ref):
    o_ref[...] = jnp.dot(a_ref[...], w_ref[...], preferred_element_type=F32).astype(o_ref.dtype)


def matmul(a, w, *, tm, tn, out_dtype):
    n, k = a.shape
    nc = w.shape[1]
    tn = min(tn, nc)
    assert n % tm == 0 and nc % tn == 0 and w.shape[0] == k
    return pl.pallas_call(
        _mm_body,
        grid=(n // tm, nc // tn),
        in_specs=[pl.BlockSpec((tm, k), lambda i, j: (i, 0)),
                  pl.BlockSpec((k, tn), lambda i, j: (0, j))],
        out_specs=pl.BlockSpec((tm, tn), lambda i, j: (i, j)),
        out_shape=jax.ShapeDtypeStruct((n, nc), out_dtype),
        compiler_params=_cparams("parallel", "parallel"),
        name="matmul",
    )(a, w)


def _rope_lanes(y2, ccss):
    w = y2 * ccss
    return w + pltpu.roll(w, ROPE_DIM, axis=1)


def _qproj_body(c_ref, g_ref, w_ref, t_ref, q_ref, cn_ref):
    @pl.when(pl.program_id(1) == 0)
    def _():
        cn_ref[...] = _rms_rows(c_ref[...].astype(F32), g_ref[...], EPS).astype(BF16)

    y = jnp.dot(cn_ref[...], w_ref[...], preferred_element_type=F32)
    tab = t_ref[...]
    lane = lax.broadcasted_iota(jnp.int32, tab.shape, 1)
    for hh in range(PROJ_HEADS):
        c0 = hh * QK_PAD
        q_ref[:, c0:c0 + LANES] = y[:, c0:c0 + LANES].astype(q_ref.dtype)
        rot = _rope_lanes(y[:, c0 + LANES:c0 + QK_PAD], tab)
        q_ref[:, c0 + LANES:c0 + QK_PAD] = jnp.where(lane < ROPE_DIM, rot, 0.0).astype(q_ref.dtype)


def mla_q_proj(h_mla, q_norm, wq, ccss, *, seq, tm):
    n = h_mla.shape[0]
    nt = seq // tm
    return pl.pallas_call(
        _qproj_body,
        grid=(n // tm, MLA_HEADS // PROJ_HEADS),
        in_specs=[pl.BlockSpec((tm, Q_LORA), lambda i, h: (i, 0)),
                  pl.BlockSpec((1, Q_LORA), lambda i, h: (0, 0)),
                  pl.BlockSpec((Q_LORA, PROJ_HEADS * QK_PAD), lambda i, h: (0, h)),
                  pl.BlockSpec((tm, LANES), lambda i, h: (i % nt, 0))],
        out_specs=pl.BlockSpec((tm, PROJ_HEADS * QK_PAD), lambda i, h: (i, h)),
        out_shape=jax.ShapeDtypeStruct((n, MLA_HEADS * QK_PAD), BF16),
        scratch_shapes=[pltpu.VMEM((tm, Q_LORA), BF16)],
        compiler_params=_cparams("parallel", "arbitrary"),
        name="mla_q_proj",
    )(h_mla, q_norm.reshape(1, Q_LORA), wq, ccss)


def _kvproj_body(c_ref, kr_ref, g_ref, w_ref, t_ref, k_ref, v_ref, cn_ref, krr_ref):
    @pl.when(pl.program_id(1) == 0)
    def _():
        cn_ref[...] = _rms_rows(c_ref[...].astype(F32), g_ref[...], EPS).astype(BF16)
        krr_ref[...] = _rope_lanes(kr_ref[...].astype(F32), t_ref[...]).astype(BF16)

    y = jnp.dot(cn_ref[...], w_ref[...], preferred_element_type=F32)
    for hh in range(PROJ_HEADS):
        c0 = hh * (NOPE_DIM + V_DIM)
        k_ref[:, hh * QK_PAD:hh * QK_PAD + LANES] = y[:, c0:c0 + NOPE_DIM].astype(k_ref.dtype)
        k_ref[:, hh * QK_PAD + LANES:(hh + 1) * QK_PAD] = krr_ref[...]
        v_ref[:, hh * V_DIM:(hh + 1) * V_DIM] = y[:, c0 + NOPE_DIM:c0 + NOPE_DIM + V_DIM].astype(v_ref.dtype)


def mla_kv_proj(h_mla, kv_norm, wkv, ccss, *, seq, tm):
    n = h_mla.shape[0]
    nt = seq // tm
    return pl.pallas_call(
        _kvproj_body,
        grid=(n // tm, MLA_HEADS // PROJ_HEADS),
        in_specs=[pl.BlockSpec((tm, KV_LORA), lambda i, h: (i, 1)),
                  pl.BlockSpec((tm, LANES), lambda i, h: (i, (Q_LORA + KV_LORA) // LANES)),
                  pl.BlockSpec((1, KV_LORA), lambda i, h: (0, 0)),
                  pl.BlockSpec((KV_LORA, PROJ_HEADS * (NOPE_DIM + V_DIM)), lambda i, h: (0, h)),
                  pl.BlockSpec((tm, LANES), lambda i, h: (i % nt, 0))],
        out_specs=[pl.BlockSpec((tm, PROJ_HEADS * QK_PAD), lambda i, h: (i, h)),
                   pl.BlockSpec((tm, PROJ_HEADS * V_DIM), lambda i, h: (i, h))],
        out_shape=[jax.ShapeDtypeStruct((n, MLA_HEADS * QK_PAD), BF16),
                   jax.ShapeDtypeStruct((n, MLA_HEADS * V_DIM), BF16)],
        scratch_shapes=[pltpu.VMEM((tm, KV_LORA), BF16), pltpu.VMEM((tm, LANES), BF16)],
        compiler_params=_cparams("parallel", "arbitrary"),
        name="mla_kv_proj",
    )(h_mla, h_mla, kv_norm.reshape(1, KV_LORA), wkv, ccss)


def _attn_body(q_ref, k_ref, v_ref, o_ref, *, nsub):
    ts = q_ref.shape[0] // nsub
    k = k_ref[...]
    v = v_ref[...]

    def scores(j):
        return lax.dot_general(q_ref[j * ts:(j + 1) * ts, :], k, _TB, preferred_element_type=F32)

    def finish(s, j):
        m = jnp.max(s, axis=-1, keepdims=True)
        p = jnp.exp(s - m)
        l = jnp.sum(p, axis=-1, keepdims=True)
        o = jnp.dot(p.astype(BF16), v, preferred_element_type=F32)
        o_ref[j * ts:(j + 1) * ts, :] = (o / l).astype(o_ref.dtype)

    s_prev = scores(0)
    for j in range(1, nsub):
        s_next = scores(j)
        finish(s_prev, j - 1)
        s_prev = s_next
    finish(s_prev, nsub - 1)


def attention(q, k, v, *, batch, seq, tq):
    n = q.shape[0]
    nq = seq // tq
    return pl.pallas_call(
        functools.partial(_attn_body, nsub=tq // ATTN_SUB),
        grid=(batch, MLA_HEADS, nq),
        in_specs=[pl.BlockSpec((tq, QK_PAD), lambda b, h, i: (b * nq + i, h)),
                  pl.BlockSpec((seq, QK_PAD), lambda b, h, i: (b, h)),
                  pl.BlockSpec((seq, V_DIM), lambda b, h, i: (b, h))],
        out_specs=pl.BlockSpec((tq, V_DIM), lambda b, h, i: (b * nq + i, h)),
        out_shape=jax.ShapeDtypeStruct((n, MLA_HEADS * V_DIM), BF16),
        compiler_params=_cparams("parallel", "parallel", "arbitrary"),
        name="mla_attention",
    )(q, k, v)


def _head_sums(x, bd):
    hi = x.astype(BF16)
    lo = (x - hi.astype(F32)).astype(BF16)
    bd2 = jnp.concatenate([bd, bd], axis=0).astype(BF16)
    parts = []
    for gi in range(RW_DIM // LANES):
        ls = slice(gi * LANES, (gi + 1) * LANES)
        parts.append(jnp.dot(jnp.concatenate([hi[:, ls], lo[:, ls]], axis=1), bd2, preferred_element_type=F32))
    return jnp.concatenate(parts, axis=1)


def _sigmoid(x):
    return 1.0 / (1.0 + jnp.exp(-x))


def _rw_prep_body(*refs, tiles_per_seq, tm, has_vfirst):
    (h_ref, hp_ref, hn_ref, mu_ref, vec_ref, w2f_ref, w2b_ref, a2f_ref, a2b_ref, g2_ref, bd_ref) = refs[:11]
    pos = 11
    if has_vfirst:
        v2_ref, vf_ref = refs[pos:pos + 2]
        pos += 2
    r_ref, v_ref, kk_ref, ag_ref, lw_ref, k2_ref, g_ref = refs[pos:pos + 7]

    ti = pl.program_id(0) % tiles_per_seq
    keep_prev = jnp.where(ti == 0, 0.0, 1.0)
    keep_next = jnp.where(ti == tiles_per_seq - 1, 0.0, 1.0)
    row = lax.broadcasted_iota(jnp.int32, (tm, 1), 0)

    def shifted(c0, c1):
        x = h_ref[:, c0:c1]
        prev = jnp.where(row == 0, hp_ref[SUBLANES - 1:SUBLANES, c0:c1] * keep_prev, pltpu.roll(x, 1, axis=0))
        nxt = jnp.where(row == tm - 1, hn_ref[0:1, c0:c1] * keep_next, pltpu.roll(x, tm - 1, axis=0))
        return x * mu_ref[0:1, c0:c1] + (prev + nxt) * mu_ref[1:2, c0:c1]

    vec = vec_ref[...]
    w0f, w0b, a0f, a0b, k_k, k_a, v0 = (vec[i:i + 1, :] for i in range(7))

    r = shifted(0, RW_DIM)
    k = shifted(RW_DIM, 2 * RW_DIM)
    v = shifted(2 * RW_DIM, 3 * RW_DIM)
    lo = shifted(3 * RW_DIM, RW_W)
    dw = jnp.tanh(lo[:, 0:LANES]).astype(BF16)
    da = lo[:, LANES:2 * LANES].astype(BF16)
    dg = _sigmoid(lo[:, 2 * LANES:4 * LANES]).astype(BF16)

    def decay(dw_half, w0, w2_ref):
        z = w0 + jnp.dot(dw_half, w2_ref[...], preferred_element_type=F32)
        return _sigmoid(z) * (-math.exp(-0.5))

    lw_ref[:, 0:RW_DIM] = decay(dw, w0f, w2f_ref)
    lw_ref[:, RW_DIM:] = decay(dw, w0b, w2b_ref)
    af = _sigmoid(a0f + jnp.dot(da, a2f_ref[...], preferred_element_type=F32))
    ab = _sigmoid(a0b + jnp.dot(da, a2b_ref[...], preferred_element_type=F32))
    ag_ref[:, 0:RW_DIM] = af
    ag_ref[:, RW_DIM:] = ab
    g_ref[...] = jnp.dot(dg, g2_ref[...], preferred_element_type=F32)

    if has_vfirst:
        dv = lo[:, 4 * LANES:5 * LANES].astype(BF16)
        mix = _sigmoid(v0 + jnp.dot(dv, v2_ref[...], preferred_element_type=F32))
        v = v + (vf_ref[...] - v) * mix
    v_ref[...] = v
    r_ref[...] = r

    kk = k * k_k
    ss = _head_sums(kk * kk, bd_ref[...])
    kk_ref[...] = kk * lax.rsqrt(jnp.maximum(ss, 1e-24))
    k2_ref[:, 0:RW_DIM] = k * (1.0 + (af - 1.0) * k_a)
    k2_ref[:, RW_DIM:] = k * (1.0 + (ab - 1.0) * k_a)


def rwkv_prep(h_rw, mu, vec, w2f, w2b, a2f, a2b, g2, bd, v2, v_first, *, seq, tm):
    n = h_rw.shape[0]
    tps = seq // tm
    nb8 = n // SUBLANES
    has_vfirst = v_first is not None
    full = lambda a: pl.BlockSpec(a.shape, lambda i: (0,) * a.ndim)
    in_specs = [pl.BlockSpec((tm, RW_W), lambda i: (i, 0)),
                pl.BlockSpec((SUBLANES, RW_W), lambda i: (jnp.maximum(i * (tm // SUBLANES) - 1, 0), 0)),
                pl.BlockSpec((SUBLANES, RW_W), lambda i: (jnp.minimum((i + 1) * (tm // SUBLANES), nb8 - 1), 0)),
                full(mu), full(vec), full(w2f), full(w2b), full(a2f), full(a2b), full(g2), full(bd)]
    args = [h_rw, h_rw, h_rw, mu, vec, w2f, w2b, a2f, a2b, g2, bd]
    if has_vfirst:
        in_specs += [full(v2), pl.BlockSpec((tm, RW_DIM), lambda i: (i, 0))]
        args += [v2, v_first]
    row = lambda w: pl.BlockSpec((tm, w), lambda i: (i, 0))
    widths = [RW_DIM, RW_DIM, RW_DIM, 2 * RW_DIM, 2 * RW_DIM, 2 * RW_DIM, RW_DIM]
    return pl.pallas_call(
        functools.partial(_rw_prep_body, tiles_per_seq=tps, tm=tm, has_vfirst=has_vfirst),
        grid=(n // tm,),
        in_specs=in_specs,
        out_specs=[row(w) for w in widths],
        out_shape=[jax.ShapeDtypeStruct((n, w), F32) for w in widths],
        compiler_params=_cparams("parallel"),
        name="rwkv_prep",
    )(*args)


def _wkv_body(*refs):
    C = WKV_CHUNK
    in_refs, (yf_ref, yb_ref, st_ref) = refs[:12], refs[12:]
    npair = RW_DIM // LANES

    @pl.when(pl.program_id(1) == 0)
    def _():
        st_ref[...] = jnp.zeros_like(st_ref)

    ti = lax.broadcasted_iota(jnp.int32, (C, C), 0)
    si = lax.broadcasted_iota(jnp.int32, (C, C), 1)
    lane = lax.broadcasted_iota(jnp.int32, (1, LANES), 1)
    hm0 = jnp.where(lane < RW_HEAD, 1.0, 0.0)
    hm1 = 1.0 - hm0
    stack = lambda x: jnp.concatenate([x * hm0, x * hm1], axis=0).astype(BF16)
    dup = lambda x: jnp.concatenate([x, x], axis=0).astype(BF16)
    ri = lax.broadcasted_iota(jnp.int32, (2 * C, 2 * C), 0)
    ci = lax.broadcasted_iota(jnp.int32, (2 * C, 2 * C), 1)
    same = (ri & C) == (ci & C)
    dt = (ri & (C - 1)) - (ci & (C - 1))
    eye = jnp.where(ri == ci, 1.0, 0.0)
    valid = (lax.broadcasted_iota(jnp.int32, (2 * C, LANES), 0) & C) == (
        lax.broadcasted_iota(jnp.int32, (2 * C, LANES), 1) & RW_HEAD)
    mm = functools.partial(jnp.dot, preferred_element_type=F32)
    mm_tb = lambda a, b: lax.dot_general(a, b, _TB, preferred_element_type=F32)
    mm_ta = lambda a, b: lax.dot_general(a, b, _TA, preferred_element_type=F32)

    chains = [(d, p) for d in range(2) for p in range(npair)]
    xa, xr, vst, bws, kws, wtots, gms, strict, incl = [], [], [], [], [], [], [], [], []
    for d in range(2):
        r_ref, v_ref, kk_ref, lw_ref, k_ref, ag_ref = in_refs[6 * d:6 * d + 6]
        sgn = 1 - 2 * d
        tri = jnp.where((ti - si) * sgn >= 0, 1.0, 0.0)
        lw_all = lw_ref[...]
        cum_all = jnp.dot(tri, lw_all, precision=HIGHEST, preferred_element_type=F32)
        tot_all = jnp.sum(lw_all, axis=0, keepdims=True)
        strict_d = same & (dt * sgn > 0)
        incl_d = same & (dt * sgn >= 0)
        for p in range(npair):
            ls = slice(p * LANES, (p + 1) * LANES)
            r, v, kk, k, ag = (x[:, ls] for x in (r_ref, v_ref, kk_ref, k_ref, ag_ref))
            lw, cum, tot = lw_all[:, ls], cum_all[:, ls], tot_all[:, ls]
            ei = jnp.exp(-cum)
            ew = jnp.exp(tot - cum)
            b = kk * ag
            xa.append(stack(-kk * jnp.exp(cum - lw)))
            xr.append(stack(r * jnp.exp(cum)))
            vst.append(stack(v))
            bws.append(stack(b * ew))
            kws.append(stack(k * ew))
            wtots.append(jnp.exp(tot))
            strict.append(strict_d)
            incl.append(incl_d)
            x2 = jnp.concatenate([xa[-1], xr[-1]], axis=0)
            y2 = jnp.concatenate([dup(b * ei), dup(k * ei)], axis=0)
            gms.append(mm_tb(x2, y2))
    nch = range(len(chains))
    l_ab = [jnp.where(strict[i], gms[i][0:2 * C, 0:2 * C], 0.0) for i in nch]
    l_ak = [jnp.where(strict[i], gms[i][0:2 * C, 2 * C:], 0.0).astype(BF16) for i in nch]
    m_rb = [jnp.where(incl[i], gms[i][2 * C:, 0:2 * C], 0.0).astype(BF16) for i in nch]
    m_rk = [jnp.where(incl[i], gms[i][2 * C:, 2 * C:], 0.0).astype(BF16) for i in nch]

    pinv = [eye + l for l in l_ab]
    lp = l_ab
    for _ in range(int(math.log2(C)) - 1):
        lpb = [l.astype(BF16) for l in lp]
        lp = [mm(l, l) for l in lpb]
        pinv = [pi + mm(pi.astype(BF16), l.astype(BF16)) for pi, l in zip(pinv, lp)]

    st = [st_ref[d, p] for d, p in chains]
    s_kv = [s.T.astype(BF16) for s in st]
    pre = [mm(jnp.concatenate([xa[i], l_ak[i]], axis=1), jnp.concatenate([s_kv[i], vst[i]], axis=0))
           for i in nch]
    ustb = [jnp.where(valid, mm(pinv[i].astype(BF16), pre[i].astype(BF16)), 0.0).astype(BF16) for i in nch]
    for i, (d, p) in enumerate(chains):
        yst = mm(jnp.concatenate([xr[i], m_rb[i], m_rk[i]], axis=1),
                 jnp.concatenate([s_kv[i], ustb[i], vst[i]], axis=0))
        yst = jnp.where(valid, yst, 0.0)
        (yf_ref, yb_ref)[d][:, p * LANES:(p + 1) * LANES] = yst[0:C] + yst[C:]
    for i, (d, p) in enumerate(chains):
        st_ref[d, p] = st[i] * wtots[i] + mm_ta(jnp.concatenate([ustb[i], vst[i]], axis=0),
                                                jnp.concatenate([bws[i], kws[i]], axis=0))


def wkv(r, v, kk, lw2, k2, ag2, *, batch, seq):
    n = r.shape[0]
    C = WKV_CHUNK
    nc = seq // C
    fwd = lambda col: pl.BlockSpec((C, RW_DIM), lambda b, c: (b * nc + c, col))
    bwd = lambda col: pl.BlockSpec((C, RW_DIM), lambda b, c: (b * nc + nc - 1 - c, col))
    return pl.pallas_call(
        _wkv_body,
        grid=(batch, nc),
        in_specs=[fwd(0), fwd(0), fwd(0), fwd(0), fwd(0), fwd(0),
                  bwd(0), bwd(0), bwd(0), bwd(1), bwd(1), bwd(1)],
        out_specs=[fwd(0), bwd(0)],
        out_shape=[jax.ShapeDtypeStruct((n, RW_DIM), F32)] * 2,
        scratch_shapes=[pltpu.VMEM((2, RW_DIM // LANES, LANES, LANES), F32)],
        compiler_params=_cparams("parallel", "arbitrary"),
        name="wkv_scan",
    )(r, v, kk, lw2, k2, ag2, r, v, kk, lw2, k2, ag2)


def _rw_post_body(yf_ref, yb_ref, r_ref, k2_ref, v_ref, g_ref, vec_ref, bd_ref, o_ref):
    vec = vec_ref[...]
    lnx_w, lnx_b, r_k = (vec[i:i + 1, :] for i in range(3))
    bd = bd_ref[...]
    y = yf_ref[...] + yb_ref[...]
    inv = 1.0 / RW_HEAD
    mean = _head_sums(y, bd) * inv
    d = y - mean
    var = _head_sums(d * d, bd) * inv
    yn = d * lax.rsqrt(var + LNX_EPS) * lnx_w + lnx_b
    ksum = k2_ref[:, 0:RW_DIM] + k2_ref[:, RW_DIM:]
    bonus = _head_sums(r_ref[...] * ksum * r_k, bd) * v_ref[...]
    o_ref[...] = ((yn + bonus) * g_ref[...]).astype(o_ref.dtype)


def rwkv_post(yf, yb, r, k2, v, g, vec, bd, *, tm):
    n = r.shape[0]
    row = lambda w: pl.BlockSpec((tm, w), lambda i: (i, 0))
    full = lambda a: pl.BlockSpec(a.shape, lambda i: (0,) * a.ndim)
    return pl.pallas_call(
        _rw_post_body,
        grid=(n // tm,),
        in_specs=[row(RW_DIM), row(RW_DIM), row(RW_DIM), row(2 * RW_DIM), row(RW_DIM), row(RW_DIM),
                  full(vec), full(bd)],
        out_specs=row(RW_DIM),
        out_shape=jax.ShapeDtypeStruct((n, RW_DIM), BF16),
        compiler_params=_cparams("parallel"),
        name="rwkv_post",
    )(yf, yb, r, k2, v, g, vec, bd)


def _merge_body(o_ref, yg_ref, ga_ref, gr_ref, wa_ref, wr_ref, m_ref):
    attn = jnp.dot(o_ref[...], wa_ref[...], preferred_element_type=F32)
    rw = jnp.dot(yg_ref[...], wr_ref[...], preferred_element_type=F32)
    m_ref[...] = (_sigmoid(ga_ref[...].astype(F32)) * attn + _sigmoid(gr_ref[...].astype(F32)) * rw).astype(m_ref.dtype)


def merge_branches(o, yg, gates, wa, wr, *, tm, tn):
    n = o.shape[0]
    nj = D_MODEL // tn
    return pl.pallas_call(
        _merge_body,
        grid=(n // tm, nj),
        in_specs=[pl.BlockSpec((tm, MLA_HEADS * V_DIM), lambda i, j: (i, 0)),
                  pl.BlockSpec((tm, RW_DIM), lambda i, j: (i, 0)),
                  pl.BlockSpec((tm, tn), lambda i, j: (i, j)),
                  pl.BlockSpec((tm, tn), lambda i, j: (i, nj + j)),
                  pl.BlockSpec((MLA_HEADS * V_DIM, tn), lambda i, j: (0, j)),
                  pl.BlockSpec((RW_DIM, tn), lambda i, j: (0, j))],
        out_specs=pl.BlockSpec((tm, tn), lambda i, j: (i, j)),
        out_shape=jax.ShapeDtypeStruct((n, D_MODEL), BF16),
        compiler_params=_cparams("parallel", "parallel"),
        name="merge_branches",
    )(o, yg, gates, gates, wa, wr)


def _mm_res_norm_body(a_ref, w_ref, x_ref, g_ref, o_ref, *n_refs):
    out = x_ref[...] + jnp.dot(a_ref[...], w_ref[...], preferred_element_type=F32)
    o_ref[...] = out
    xhat = out * lax.rsqrt(jnp.mean(out * out, axis=-1, keepdims=True) + EPS)
    for row, n_ref in enumerate(n_refs):
        n_ref[...] = (xhat * g_ref[row:row + 1, :]).astype(n_ref.dtype)


def matmul_residual_norm(a, w, x, gains, out_dtypes, *, tm):
    n, k = a.shape
    nc = w.shape[1]
    row = lambda width: pl.BlockSpec((tm, width), lambda i: (i, 0))
    return pl.pallas_call(
        _mm_res_norm_body,
        grid=(n // tm,),
        in_specs=[row(k),
                  pl.BlockSpec((k, nc), lambda i: (0, 0), pipeline_mode=pl.Buffered(1)),
                  row(nc),
                  pl.BlockSpec(gains.shape, lambda i: (0, 0))],
        out_specs=[row(nc)] * (1 + len(out_dtypes)),
        out_shape=[jax.ShapeDtypeStruct((n, nc), F32)] + [jax.ShapeDtypeStruct((n, nc), dt) for dt in out_dtypes],
        compiler_params=_cparams("parallel"),
        name="matmul_residual_norm",
    )(a, w, x, gains)


def _top16_rows(ss, idx):
    big = 3.0e38
    vals = [[] for _ in ss]
    poss = [[] for _ in ss]
    for _ in range(PEER_TOPK):
        ms = [jnp.max(s, axis=0, keepdims=True) for s in ss]
        ps = [jnp.min(jnp.where(s == m, idx, big), axis=0, keepdims=True) for s, m in zip(ss, ms)]
        ss = [jnp.where(idx == p, -jnp.inf, s) for s, p in zip(ss, ps)]
        for i, (m, p) in enumerate(zip(ms, ps)):
            vals[i].append(m)
            poss[i].append(p)
    return [(jnp.concatenate(v, axis=0), jnp.concatenate(p, axis=0)) for v, p in zip(vals, poss)]


def _candidates(sv0, sv1):
    tok = sv0.shape[1]
    io = lambda rows: lax.broadcasted_iota(jnp.int32, (rows, tok), 0).astype(F32)
    half = PEER_TOPK // 2
    parts = [sv0[0:1, :] + sv1]
    idxs = [io(PEER_TOPK)]
    for a in range(1, half):
        parts.append(sv0[a:a + 1, :] + sv1[0:half, :])
        idxs.append(io(half) + float(a * PEER_TOPK))
    parts.append(sv0[half:, :] + sv1[0:1, :])
    idxs.append((io(half) + float(half)) * float(PEER_TOPK))
    return jnp.concatenate(parts, axis=0), jnp.concatenate(idxs, axis=0)


def _pick_rows(table, sel):
    out = jnp.zeros(sel.shape, table.dtype)
    for a in range(PEER_TOPK):
        out = jnp.where(sel == a, table[a:a + 1, :], out)
    return out


def _peer_topk_body(q_ref, sk_ref, i1_ref, i2_ref, gt_ref):
    i1s, i2s, gts = [], [], []
    key_idx = lax.broadcasted_iota(jnp.int32, (N_KEYS, q_ref.shape[0]), 0).astype(F32)
    for h in range(PEER_HEADS):
        ss = []
        for p in range(2):
            c0 = (2 * h + p) * PEER_HALF
            qb = q_ref[:, c0:c0 + PEER_HALF].astype(BF16)
            ss.append(lax.dot_general(sk_ref[p], qb, _TB, preferred_element_type=F32))
        (sv0, si0), (sv1, si1) = _top16_rows(ss, key_idx)
        cand, cand_idx = _candidates(sv0, sv1)
        ((top_s, top_j),) = _top16_rows([cand], cand_idx)
        ja = jnp.floor(top_j * (1.0 / PEER_TOPK))
        i1s.append(_pick_rows(si0, ja))
        i2s.append(_pick_rows(si1, top_j - ja * PEER_TOPK))
        e = jnp.exp(top_s - top_s[0:1, :])
        gts.append(e / jnp.sum(e, axis=0, keepdims=True))
    i1_ref[...] = jnp.concatenate(i1s, axis=0).T
    i2_ref[...] = jnp.concatenate(i2s, axis=0).T
    gt_ref[...] = jnp.concatenate(gts, axis=0).T


def peer_topk(q, sub_keys):
    n = q.shape[0]
    tmk = LANES
    out = pl.BlockSpec((tmk, NSLOT), lambda i: (i, 0))
    return pl.pallas_call(
        _peer_topk_body,
        grid=(n // tmk,),
        in_specs=[pl.BlockSpec((tmk, 2 * PEER_HALF * PEER_HEADS), lambda i: (i, 0)),
                  pl.BlockSpec(sub_keys.shape, lambda i: (0, 0, 0))],
        out_specs=[out, out, out],
        out_shape=[jax.ShapeDtypeStruct((n, NSLOT), F32)] * 3,
        compiler_params=_cparams("parallel"),
        name="peer_topk",
    )(q, sub_keys)


def _gelu_exact(x):
    return 0.5 * x * (1.0 + lax.erf(x * (1.0 / math.sqrt(2.0))))


_HI16 = 0xFFFF0000
G_UNROLL = SUBLANES
E_SUB = 2 * N_KEYS
PEER_TE = 2 * E_SUB


def _bf16_bits(x):
    return pltpu.bitcast(x, jnp.uint32) + jnp.uint32(0x8000)


def _peer_dense_body(xn_ref, i1_ref, i2_ref, gt_ref, ut0_ref, utb_ref, utn_ref, v_ref, x_ref, lnf_ref, scal_ref,
                     o_ref, g_ref, h_ref, *, tm, te, final_norm):
    e = pl.program_id(1)
    half = tm // 2
    mm_tb = lambda a, b: lax.dot_general(a, b, _TB, preferred_element_type=F32)
    hsc, csc, osc = (scal_ref[r:r + 1, 0:1] for r in range(3))
    nsub = te // E_SUB

    def activations(ut_ref):
        xn = xn_ref[...]
        return [jnp.dot(xn, ut_ref[:, j * E_SUB:(j + 1) * E_SUB], preferred_element_type=F32) * hsc
                for j in range(nsub)]

    def accumulate(block, hs):
        coefs = []
        for j in range(nsub):
            row0 = block * (te // N_KEYS) + j * (E_SUB // N_KEYS)
            words = [g_ref[pl.ds(row0 + r, half, stride=G_PITCH), :] for r in range(E_SUB // N_KEYS)]
            top = jnp.concatenate([pltpu.bitcast(w & jnp.uint32(_HI16), F32) for w in words], axis=1)
            bot = jnp.concatenate([pltpu.bitcast(w << 16, F32) for w in words], axis=1)
            gate = jnp.concatenate([top, bot], axis=0)
            coef = gate * _gelu_exact(hs[j]) * csc
            coefs.append(jnp.clip(coef, -FP8_MAX, FP8_MAX).astype(FP8))
        return coefs

    @pl.when(e == 0)
    def _():
        for j, h in enumerate(activations(ut0_ref)):
            h_ref[:, j * E_SUB:(j + 1) * E_SUB] = h
        o_ref[...] = x_ref[...] * osc
        sub = lax.broadcasted_iota(jnp.int32, (N_KEYS, NSLOT), 0).astype(F32).astype(BF16)
        one = jnp.ones((), BF16)
        zero = jnp.zeros((), BF16)

        def build(it, carry):
            base = pl.multiple_of(it * G_UNROLL, SUBLANES)
            tiles = [(ref[pl.ds(base, G_UNROLL), :], ref[pl.ds(half + base, G_UNROLL), :])
                     for ref in (i1_ref, i2_ref, gt_ref)]
            a_ts, b_ts = [], []
            for j in range(G_UNROLL):
                for t in range(2):
                    i1, i2, gt = (tiles[q][t][j:j + 1, :].astype(BF16) for q in range(3))
                    a_ts.append(jnp.where(sub == i1, one, zero))
                    b_ts.append(jnp.where(sub == i2, gt, zero))
            gs = [mm_tb(a, b) for a, b in zip(a_ts, b_ts)]
            for j in range(G_UNROLL):
                word = (_bf16_bits(gs[2 * j]) & jnp.uint32(_HI16)) | (_bf16_bits(gs[2 * j + 1]) >> 16)
                g_ref[pl.ds(pl.multiple_of((base + j) * G_PITCH, SUBLANES), N_KEYS), :] = word
            return carry

        lax.fori_loop(0, half // G_UNROLL, build, 0)

    hs_b = activations(utb_ref)
    coef_a = accumulate(2 * e, [h_ref[:, j * E_SUB:(j + 1) * E_SUB] for j in range(nsub)])
    for j, h in enumerate(activations(utn_ref)):
        h_ref[:, j * E_SUB:(j + 1) * E_SUB] = h
    coef_b = accumulate(2 * e + 1, hs_b)
    o_ref[...] += jnp.dot(jnp.concatenate(coef_a + coef_b, axis=1), v_ref[...], preferred_element_type=F32)

    @pl.when(e == pl.num_programs(1) - 1)
    def _():
        out = o_ref[...] * scal_ref[3:4, 0:1]
        if final_norm:
            out = _rms_rows(out, lnf_ref[...], EPS)
        o_ref[...] = out


def peer_dense(xn, i1, i2, gt, ut, v, x, ln_f, scal, *, tm, te, final_norm):
    n = xn.shape[0]
    ne = v.shape[0] // te
    assert tm % (2 * G_UNROLL) == 0 and te % E_SUB == 0 and ne % 2 == 0 and ut.shape == (ne, D_MODEL, te)
    row = lambda w: pl.BlockSpec((tm, w), lambda i, e: (i, 0))
    return pl.pallas_call(
        functools.partial(_peer_dense_body, tm=tm, te=te, final_norm=final_norm),
        grid=(n // tm, ne // 2),
        in_specs=[row(D_MODEL), row(NSLOT), row(NSLOT), row(NSLOT),
                  pl.BlockSpec((None, D_MODEL, te), lambda i, e: (0, 0, 0)),
                  pl.BlockSpec((None, D_MODEL, te), lambda i, e: (2 * e + 1, 0, 0)),
                  pl.BlockSpec((None, D_MODEL, te), lambda i, e: (jnp.minimum(2 * e + 2, ne - 1), 0, 0)),
                  pl.BlockSpec((2 * te, D_MODEL), lambda i, e: (e, 0)),
                  row(D_MODEL),
                  pl.BlockSpec((1, D_MODEL), lambda i, e: (0, 0)),
                  pl.BlockSpec(scal.shape, lambda i, e: (0, 0))],
        out_specs=row(D_MODEL),
        out_shape=jax.ShapeDtypeStruct((n, D_MODEL), F32),
        scratch_shapes=[pltpu.VMEM((tm // 2 * G_PITCH, N_KEYS), jnp.uint32), pltpu.VMEM((tm, te), F32)],
        compiler_params=_cparams("parallel", "arbitrary"),
        name="peer_dense",
    )(xn, i1, i2, gt, ut, ut, ut, v, x, ln_f.reshape(1, D_MODEL), scal)


def _pad_rows(a, h):
    return jnp.pad(a, ((0, h - a.shape[0]),) + ((0, 0),) * (a.ndim - 1))


def _swap_halves(a):
    half = a.shape[-1] // 2
    return jnp.concatenate([a[..., half:], a[..., :half]], axis=-1)


def _layer_weights(i, p):
    first = i == 0
    w_in = p["w_in_first"] if first else p["w_in_rest"][i - 1]
    mu = p["mu_first"] if first else p["mu_rest"][i - 1]
    mla_cols = Q_LORA + KV_LORA + ROPE_DIM
    gate0 = mla_cols
    rw0 = mla_cols + 2 * D_MODEL
    k_rope_w = w_in[:, Q_LORA + KV_LORA:mla_cols]
    w_mla = jnp.concatenate([w_in[:, :mla_cols], _swap_halves(k_rope_w)], axis=1)
    w_gate = w_in[:, gate0:rw0]

    def rw_layout(a):
        c = 3 * RW_DIM
        dwf, dwb = a[..., c:c + DECAY_LORA], a[..., c + DECAY_LORA:c + 2 * DECAY_LORA]
        c += 2 * DECAY_LORA
        daf, dab = a[..., c:c + AAA_LORA], a[..., c + AAA_LORA:c + 2 * AAA_LORA]
        c += 2 * AAA_LORA
        dg = a[..., c:c + GATE_LORA]
        c += GATE_LORA
        dv = a[..., c:]
        z = lambda w: jnp.zeros(a.shape[:-1] + (w,), a.dtype)
        return jnp.concatenate([a[..., :3 * RW_DIM], dwf, dwb, daf, dab, dg, z(2 * LANES - GATE_LORA),
                                dv, z(2 * LANES - dv.shape[-1])], axis=-1)

    w_rw = rw_layout(w_in[:, rw0:])
    mu_l = rw_layout(mu[None, :])
    mu_l = jnp.concatenate([1.0 - mu_l, 0.5 * mu_l], axis=0)
    zeros64 = jnp.zeros((DECAY_LORA, RW_DIM), F32)
    wq = (p["w_uq"][i] * (NOPE_DIM + ROPE_DIM) ** -0.5).reshape(Q_LORA, MLA_HEADS, NOPE_DIM + ROPE_DIM)
    wq = jnp.concatenate([wq, _swap_halves(wq[..., NOPE_DIM:])], axis=-1).reshape(Q_LORA, MLA_HEADS * QK_PAD)
    vec = jnp.stack([p["w0_f"][i], p["w0_b"][i], p["a0_f"][i], p["a0_b"][i], p["k_k"][i], p["k_a"][i],
                     p["v0_rest"][i - 1] if not first else jnp.zeros((RW_DIM,), F32),
                     jnp.zeros((RW_DIM,), F32)])
    lw = {
        "ln1": p["ln1"][i], "w_mla": w_mla.astype(BF16), "w_gate": w_gate.astype(BF16),
        "w_rw": w_rw.astype(BF16),
        "mu": mu_l, "q_norm": p["q_norm"][i], "wq": wq.astype(BF16), "kv_norm": p["kv_norm"][i],
        "wkv": p["w_ukv"][i].astype(BF16), "w_o_attn": p["w_o_attn"][i].astype(BF16),
        "vec": vec,
        "w2f": jnp.concatenate([p["w2_f"][i], zeros64]).astype(BF16),
        "w2b": jnp.concatenate([zeros64, p["w2_b"][i]]).astype(BF16),
        "a2f": jnp.concatenate([p["a2_f"][i], zeros64]).astype(BF16),
        "a2b": jnp.concatenate([zeros64, p["a2_b"][i]]).astype(BF16),
        "g2": _pad_rows(p["g2"][i], 2 * LANES).astype(BF16),
        "v2": None if first else _pad_rows(p["v2_rest"][i - 1], LANES).astype(BF16),
        "post_vec": jnp.stack([p["lnx_w"][i], p["lnx_b"][i], p["r_k"][i].reshape(RW_DIM)]
                              + [jnp.zeros((RW_DIM,), F32)] * 5),
        "w_o_rwkv": p["w_o_rwkv"][i].astype(BF16), "w_out": p["w_out"][i].astype(BF16),
        "ln2": p["ln2"][i], "w_pq": p["w_pq"][i].astype(BF16), "sub_keys": p["sub_keys"][i].astype(BF16),
    }
    pow2_below = lambda v: jnp.exp2(jnp.floor(jnp.log2(v)))
    tiny = 1e-30
    u, v, g2n = p["peer_u"][i], p["peer_v"][i], p["ln2"][i]
    xmax = math.sqrt(D_MODEL) * jnp.maximum(jnp.max(jnp.abs(g2n)), tiny)
    sx = pow2_below(FP8_MAX / xmax)
    su = pow2_below(FP8_MAX / jnp.maximum(jnp.max(jnp.abs(u)), tiny))
    sv = pow2_below(FP8_MAX / jnp.maximum(jnp.max(jnp.abs(v)), tiny))
    hmax = xmax * jnp.sqrt(jnp.max(jnp.sum(u * u, axis=1))) * (1.0 + 2.0 ** -4) ** 2
    sc = pow2_below(FP8_MAX / (PEER_HEADS * jnp.maximum(hmax, tiny)))
    lw["ln2_rows"] = jnp.stack([g2n, g2n * sx])
    lw["peer_ut"] = (u * su).astype(FP8).reshape(-1, PEER_TE, D_MODEL).transpose(0, 2, 1)
    lw["peer_v"] = (v * sv).astype(FP8)
    scal = jnp.stack([1.0 / (sx * su), sc, sc * sv, 1.0 / (sc * sv)] + [jnp.zeros((), F32)] * 4)
    lw["peer_scal"] = jnp.broadcast_to(scal[:, None], (SUBLANES, LANES)).astype(F32)
    return lw


def _rope_table(seq):
    inv = 1.0 / (ROPE_THETA ** (jnp.arange(0, ROPE_DIM, 2, dtype=F32) / ROPE_DIM))
    ang = jnp.arange(seq, dtype=F32)[:, None] * inv[None, :]
    c, s = jnp.cos(ang), jnp.sin(ang)
    return jnp.concatenate([c, c, -s, s], axis=1)


def _tiles(seq):
    return dict(tm=min(512, seq), tmm=min(1024, seq), tq=min(16 * ATTN_SUB, seq), tprep=min(256, seq),
                tpeer=min(512, seq))


def _trunk(x, layers, ln_f, bd):
    batch, seq, _ = x.shape
    n = batch * seq
    t = _tiles(seq)
    tm, tmm = t["tm"], t["tmm"]
    ccss = _rope_table(seq)
    x = x.reshape(n, D_MODEL)
    v_first = None
    for li, lw in enumerate(layers):
        (xn1,) = rms_norm(x, lw["ln1"].reshape(1, D_MODEL), (BF16,), tm=tm)
        h_mla = matmul(xn1, lw["w_mla"], tm=tmm, tn=MLA_W, out_dtype=BF16)
        gates = matmul(xn1, lw["w_gate"], tm=tmm, tn=1024, out_dtype=BF16)
        h_rw = matmul(xn1, lw["w_rw"], tm=tmm, tn=1280, out_dtype=F32)
        q = mla_q_proj(h_mla, lw["q_norm"], lw["wq"], ccss, seq=seq, tm=tm)
        k, v = mla_kv_proj(h_mla, lw["kv_norm"], lw["wkv"], ccss, seq=seq, tm=tm)
        o = attention(q, k, v, batch=batch, seq=seq, tq=t["tq"])
        r, vv, kk, ag2, lw2, k2, g = rwkv_prep(h_rw, lw["mu"], lw["vec"], lw["w2f"], lw["w2b"], lw["a2f"],
                                              lw["a2b"], lw["g2"], bd, lw["v2"], v_first,
                                              seq=seq, tm=t["tprep"])
        if v_first is None:
            v_first = vv
        yf, yb = wkv(r, vv, kk, lw2, k2, ag2, batch=batch, seq=seq)
        yg = rwkv_post(yf, yb, r, k2, vv, g, lw["post_vec"], bd, tm=t["tprep"])
        m = merge_branches(o, yg, gates, lw["w_o_attn"], lw["w_o_rwkv"], tm=tmm, tn=512)
        x, xn, xn8 = matmul_residual_norm(m, lw["w_out"], x, lw["ln2_rows"], (BF16, FP8), tm=tm)
        qp = matmul(xn, lw["w_pq"], tm=tmm, tn=1024, out_dtype=F32)
        i1, i2, gt = peer_topk(qp, lw["sub_keys"])
        x = peer_dense(xn8, i1, i2, gt, lw["peer_ut"], lw["peer_v"], x, ln_f, lw["peer_scal"],
                       tm=t["tpeer"], te=PEER_TE,
                       final_norm=li == len(layers) - 1)
    return x.reshape(batch, seq, D_MODEL)


def kernel(x_prompt, x_sample, ln1, w_in_first, mu_first, w_in_rest, mu_rest, q_norm, w_uq, kv_norm, w_ukv, w_o_attn, w0_f, w2_f, w0_b, w2_b, a0_f, a2_f, a0_b, a2_b, g2, k_k, k_a, r_k, lnx_w, lnx_b, v0_rest, v2_rest, w_o_rwkv, w_out, ln2, w_pq, sub_keys, peer_u, peer_v, ln_f):
    p = dict(ln1=ln1, w_in_first=w_in_first, mu_first=mu_first, w_in_rest=w_in_rest, mu_rest=mu_rest,
             q_norm=q_norm, w_uq=w_uq, kv_norm=kv_norm, w_ukv=w_ukv, w_o_attn=w_o_attn, w0_f=w0_f, w2_f=w2_f,
             w0_b=w0_b, w2_b=w2_b, a0_f=a0_f, a2_f=a2_f, a0_b=a0_b, a2_b=a2_b, g2=g2, k_k=k_k, k_a=k_a, r_k=r_k,
             lnx_w=lnx_w, lnx_b=lnx_b, v0_rest=v0_rest, v2_rest=v2_rest, w_o_rwkv=w_o_rwkv, w_out=w_out,
             ln2=ln2, w_pq=w_pq, sub_keys=sub_keys, peer_u=peer_u, peer_v=peer_v)
    depth = ln1.shape[0]
    layers = [_layer_weights(i, p) for i in range(depth)]
    lane = jnp.arange(LANES)
    bd = (lane[:, None] // RW_HEAD == lane[None, :] // RW_HEAD).astype(F32)
    return (_trunk(x_prompt, layers, ln_f, bd), _trunk(x_sample, layers, ln_f, bd))
```

```python
import functools
import math

import jax
import jax.numpy as jnp
from jax import lax
from jax.experimental import pallas as pl
from jax.experimental.pallas import tpu as pltpu

F32 = jnp.float32
BF16 = jnp.bfloat16
FP8 = jnp.float8_e4m3fn
FP8_MAX = 448.0
HIGHEST = lax.Precision.HIGHEST

LANES = 128
SUBLANES = 8
VMEM_LIMIT_BYTES = 56 * 1024 * 1024

D_MODEL = 2048
MLA_HEADS = 16
Q_LORA = 512
KV_LORA = 512
NOPE_DIM = 128
ROPE_DIM = 64
V_DIM = 128
ROPE_THETA = 10000.0
RW_HEAD = 64
RW_HEADS = 16
RW_DIM = RW_HEADS * RW_HEAD
DECAY_LORA = 64
AAA_LORA = 64
MV_LORA = 32
GATE_LORA = 160
LNX_EPS = 64e-5
PEER_HEADS = 8
N_KEYS = 128
PEER_HALF = 128
PEER_TOPK = 16
EPS = 1e-6

QK_PAD = 2 * LANES
MLA_W = Q_LORA + KV_LORA + LANES
LORA_W = 2 * DECAY_LORA + 2 * AAA_LORA + 2 * LANES + 2 * LANES
RW_W = 3 * RW_DIM + LORA_W
PROJ_HEADS = 8
ATTN_SUB = 512
WKV_CHUNK = 64
G_PITCH = N_KEYS + SUBLANES
NSLOT = PEER_HEADS * PEER_TOPK

_TB = (((1,), (1,)), ((), ()))
_TA = (((0,), (0,)), ((), ()))


def _cparams(*sem):
    return pltpu.CompilerParams(dimension_semantics=sem, vmem_limit_bytes=VMEM_LIMIT_BYTES)


def _rms_rows(x, g, eps):
    ms = jnp.mean(x * x, axis=-1, keepdims=True)
    return x * lax.rsqrt(ms + eps) * g


def _rms_norm_body(x_ref, g_ref, *o_refs):
    x = x_ref[...].astype(F32)
    xhat = x * lax.rsqrt(jnp.mean(x * x, axis=-1, keepdims=True) + EPS)
    for row, o_ref in enumerate(o_refs):
        o_ref[...] = (xhat * g_ref[row:row + 1, :]).astype(o_ref.dtype)


def rms_norm(x, gains, out_dtypes, *, tm):
    n, k = x.shape
    spec = pl.BlockSpec((tm, k), lambda i: (i, 0))
    return pl.pallas_call(
        _rms_norm_body,
        grid=(n // tm,),
        in_specs=[spec, pl.BlockSpec(gains.shape, lambda i: (0, 0))],
        out_specs=[spec] * len(out_dtypes),
        out_shape=[jax.ShapeDtypeStruct((n, k), dt) for dt in out_dtypes],
        compiler_params=_cparams("parallel"),
        name="rms_norm",
    )(x, gains)


def _mm_body(a_ref, w_ref, o_ref):
    o_ref[...] = jnp.dot(a_ref[...], w_ref[...], preferred_element_type=F32).astype(o_ref.dtype)


def matmul(a, w, *, tm, tn, out_dtype):
    n, k = a.shape
    nc = w.shape[1]
    tn = min(tn, nc)
    assert n % tm == 0 and nc % tn == 0 and w.shape[0] == k
    return pl.pallas_call(
        _mm_body,
        grid=(n // tm, nc // tn),
        in_specs=[pl.BlockSpec((tm, k), lambda i, j: (i, 0)),
                  pl.BlockSpec((k, tn), lambda i, j: (0, j))],
        out_specs=pl.BlockSpec((tm, tn), lambda i, j: (i, j)),
        out_shape=jax.ShapeDtypeStruct((n, nc), out_dtype),
        compiler_params=_cparams("parallel", "parallel"),
        name="matmul",
    )(a, w)


def _rope_lanes(y2, ccss):
    w = y2 * ccss
    return w + pltpu.roll(w, ROPE_DIM, axis=1)


def _qproj_body(c_ref, g_ref, w_ref, t_ref, q_ref, cn_ref):
    @pl.when(pl.program_id(1) == 0)
    def _():
        cn_ref[...] = _rms_rows(c_ref[...].astype(F32), g_ref[...], EPS).astype(BF16)

    y = jnp.dot(cn_ref[...], w_ref[...], preferred_element_type=F32)
    tab = t_ref[...]
    lane = lax.broadcasted_iota(jnp.int32, tab.shape, 1)
    for hh in range(PROJ_HEADS):
        c0 = hh * QK_PAD
        q_ref[:, c0:c0 + LANES] = y[:, c0:c0 + LANES].astype(q_ref.dtype)
        rot = _rope_lanes(y[:, c0 + LANES:c0 + QK_PAD], tab)
        q_ref[:, c0 + LANES:c0 + QK_PAD] = jnp.where(lane < ROPE_DIM, rot, 0.0).astype(q_ref.dtype)


def mla_q_proj(h_mla, q_norm, wq, ccss, *, seq, tm):
    n = h_mla.shape[0]
    nt = seq // tm
    return pl.pallas_call(
        _qproj_body,
        grid=(n // tm, MLA_HEADS // PROJ_HEADS),
        in_specs=[pl.BlockSpec((tm, Q_LORA), lambda i, h: (i, 0)),
                  pl.BlockSpec((1, Q_LORA), lambda i, h: (0, 0)),
                  pl.BlockSpec((Q_LORA, PROJ_HEADS * QK_PAD), lambda i, h: (0, h)),
                  pl.BlockSpec((tm, LANES), lambda i, h: (i % nt, 0))],
        out_specs=pl.BlockSpec((tm, PROJ_HEADS * QK_PAD), lambda i, h: (i, h)),
        out_shape=jax.ShapeDtypeStruct((n, MLA_HEADS * QK_PAD), BF16),
        scratch_shapes=[pltpu.VMEM((tm, Q_LORA), BF16)],
        compiler_params=_cparams("parallel", "arbitrary"),
        name="mla_q_proj",
    )(h_mla, q_norm.reshape(1, Q_LORA), wq, ccss)


def _kvproj_body(c_ref, kr_ref, g_ref, w_ref, t_ref, k_ref, v_ref, cn_ref, krr_ref):
    @pl.when(pl.program_id(1) == 0)
    def _():
        cn_ref[...] = _rms_rows(c_ref[...].astype(F32), g_ref[...], EPS).astype(BF16)
        krr_ref[...] = _rope_lanes(kr_ref[...].astype(F32), t_ref[...]).astype(BF16)

    y = jnp.dot(cn_ref[...], w_ref[...], preferred_element_type=F32)
    for hh in range(PROJ_HEADS):
        c0 = hh * (NOPE_DIM + V_DIM)
        k_ref[:, hh * QK_PAD:hh * QK_PAD + LANES] = y[:, c0:c0 + NOPE_DIM].astype(k_ref.dtype)
        k_ref[:, hh * QK_PAD + LANES:(hh + 1) * QK_PAD] = krr_ref[...]
        v_ref[:, hh * V_DIM:(hh + 1) * V_DIM] = y[:, c0 + NOPE_DIM:c0 + NOPE_DIM + V_DIM].astype(v_ref.dtype)


def mla_kv_proj(h_mla, kv_norm, wkv, ccss, *, seq, tm):
    n = h_mla.shape[0]
    nt = seq // tm
    return pl.pallas_call(
        _kvproj_body,
        grid=(n // tm, MLA_HEADS // PROJ_HEADS),
        in_specs=[pl.BlockSpec((tm, KV_LORA), lambda i, h: (i, 1)),
                  pl.BlockSpec((tm, LANES), lambda i, h: (i, (Q_LORA + KV_LORA) // LANES)),
                  pl.BlockSpec((1, KV_LORA), lambda i, h: (0, 0)),
                  pl.BlockSpec((KV_LORA, PROJ_HEADS * (NOPE_DIM + V_DIM)), lambda i, h: (0, h)),
                  pl.BlockSpec((tm, LANES), lambda i, h: (i % nt, 0))],
        out_specs=[pl.BlockSpec((tm, PROJ_HEADS * QK_PAD), lambda i, h: (i, h)),
                   pl.BlockSpec((tm, PROJ_HEADS * V_DIM), lambda i, h: (i, h))],
        out_shape=[jax.ShapeDtypeStruct((n, MLA_HEADS * QK_PAD), BF16),
                   jax.ShapeDtypeStruct((n, MLA_HEADS * V_DIM), BF16)],
        scratch_shapes=[pltpu.VMEM((tm, KV_LORA), BF16), pltpu.VMEM((tm, LANES), BF16)],
        compiler_params=_cparams("parallel", "arbitrary"),
        name="mla_kv_proj",
    )(h_mla, h_mla, kv_norm.reshape(1, KV_LORA), wkv, ccss)


def _attn_body(q_ref, k_ref, v_ref, o_ref, *, nsub):
    ts = q_ref.shape[0] // nsub
    k = k_ref[...]
    v = v_ref[...]

    def scores(j):
        return lax.dot_general(q_ref[j * ts:(j + 1) * ts, :], k, _TB, preferred_element_type=F32)

    def finish(s, j):
        m = jnp.max(s, axis=-1, keepdims=True)
        p = jnp.exp(s - m)
        l = jnp.sum(p, axis=-1, keepdims=True)
        o = jnp.dot(p.astype(BF16), v, preferred_element_type=F32)
        o_ref[j * ts:(j + 1) * ts, :] = (o / l).astype(o_ref.dtype)

    s_prev = scores(0)
    for j in range(1, nsub):
        s_next = scores(j)
        finish(s_prev, j - 1)
        s_prev = s_next
    finish(s_prev, nsub - 1)


def attention(q, k, v, *, batch, seq, tq):
    n = q.shape[0]
    nq = seq // tq
    return pl.pallas_call(
        functools.partial(_attn_body, nsub=tq // ATTN_SUB),
        grid=(batch, MLA_HEADS, nq),
        in_specs=[pl.BlockSpec((tq, QK_PAD), lambda b, h, i: (b * nq + i, h)),
                  pl.BlockSpec((seq, QK_PAD), lambda b, h, i: (b, h)),
                  pl.BlockSpec((seq, V_DIM), lambda b, h, i: (b, h))],
        out_specs=pl.BlockSpec((tq, V_DIM), lambda b, h, i: (b * nq + i, h)),
        out_shape=jax.ShapeDtypeStruct((n, MLA_HEADS * V_DIM), BF16),
        compiler_params=_cparams("parallel", "parallel", "arbitrary"),
        name="mla_attention",
    )(q, k, v)


def _head_sums(x, bd):
    hi = x.astype(BF16)
    lo = (x - hi.astype(F32)).astype(BF16)
    bd2 = jnp.concatenate([bd, bd], axis=0).astype(BF16)
    parts = []
    for gi in range(RW_DIM // LANES):
        ls = slice(gi * LANES, (gi + 1) * LANES)
        parts.append(jnp.dot(jnp.concatenate([hi[:, ls], lo[:, ls]], axis=1), bd2, preferred_element_type=F32))
    return jnp.concatenate(parts, axis=1)


def _sigmoid(x):
    return 1.0 / (1.0 + jnp.exp(-x))


def _rw_prep_body(*refs, tiles_per_seq, tm, has_vfirst):
    (h_ref, hp_ref, hn_ref, mu_ref, vec_ref, w2f_ref, w2b_ref, a2f_ref, a2b_ref, g2_ref, bd_ref) = refs[:11]
    pos = 11
    if has_vfirst:
        v2_ref, vf_ref = refs[pos:pos + 2]
        pos += 2
    r_ref, v_ref, kk_ref, ag_ref, lw_ref, k2_ref, g_ref = refs[pos:pos + 7]

    ti = pl.program_id(0) % tiles_per_seq
    keep_prev = jnp.where(ti == 0, 0.0, 1.0)
    keep_next = jnp.where(ti == tiles_per_seq - 1, 0.0, 1.0)
    row = lax.broadcasted_iota(jnp.int32, (tm, 1), 0)

    def shifted(c0, c1):
        x = h_ref[:, c0:c1]
        prev = jnp.where(row == 0, hp_ref[SUBLANES - 1:SUBLANES, c0:c1] * keep_prev, pltpu.roll(x, 1, axis=0))
        nxt = jnp.where(row == tm - 1, hn_ref[0:1, c0:c1] * keep_next, pltpu.roll(x, tm - 1, axis=0))
        return x * mu_ref[0:1, c0:c1] + (prev + nxt) * mu_ref[1:2, c0:c1]

    vec = vec_ref[...]
    w0f, w0b, a0f, a0b, k_k, k_a, v0 = (vec[i:i + 1, :] for i in range(7))

    r = shifted(0, RW_DIM)
    k = shifted(RW_DIM, 2 * RW_DIM)
    v = shifted(2 * RW_DIM, 3 * RW_DIM)
    lo = shifted(3 * RW_DIM, RW_W)
    dw = jnp.tanh(lo[:, 0:LANES]).astype(BF16)
    da = lo[:, LANES:2 * LANES].astype(BF16)
    dg = _sigmoid(lo[:, 2 * LANES:4 * LANES]).astype(BF16)

    def decay(dw_half, w0, w2_ref):
        z = w0 + jnp.dot(dw_half, w2_ref[...], preferred_element_type=F32)
        return _sigmoid(z) * (-math.exp(-0.5))

    lw_ref[:, 0:RW_DIM] = decay(dw, w0f, w2f_ref)
    lw_ref[:, RW_DIM:] = decay(dw, w0b, w2b_ref)
    af = _sigmoid(a0f + jnp.dot(da, a2f_ref[...], preferred_element_type=F32))
    ab = _sigmoid(a0b + jnp.dot(da, a2b_ref[...], preferred_element_type=F32))
    ag_ref[:, 0:RW_DIM] = af
    ag_ref[:, RW_DIM:] = ab
    g_ref[...] = jnp.dot(dg, g2_ref[...], preferred_element_type=F32)

    if has_vfirst:
        dv = lo[:, 4 * LANES:5 * LANES].astype(BF16)
        mix = _sigmoid(v0 + jnp.dot(dv, v2_ref[...], preferred_element_type=F32))
        v = v + (vf_ref[...] - v) * mix
    v_ref[...] = v
    r_ref[...] = r

    kk = k * k_k
    ss = _head_sums(kk * kk, bd_ref[...])
    kk_ref[...] = kk * lax.rsqrt(jnp.maximum(ss, 1e-24))
    k2_ref[:, 0:RW_DIM] = k * (1.0 + (af - 1.0) * k_a)
    k2_ref[:, RW_DIM:] = k * (1.0 + (ab - 1.0) * k_a)


def rwkv_prep(h_rw, mu, vec, w2f, w2b, a2f, a2b, g2, bd, v2, v_first, *, seq, tm):
    n = h_rw.shape[0]
    tps = seq // tm
    nb8 = n // SUBLANES
    has_vfirst = v_first is not None
    full = lambda a: pl.BlockSpec(a.shape, lambda i: (0,) * a.ndim)
    in_specs = [pl.BlockSpec((tm, RW_W), lambda i: (i, 0)),
                pl.BlockSpec((SUBLANES, RW_W), lambda i: (jnp.maximum(i * (tm // SUBLANES) - 1, 0), 0)),
                pl.BlockSpec((SUBLANES, RW_W), lambda i: (jnp.minimum((i + 1) * (tm // SUBLANES), nb8 - 1), 0)),
                full(mu), full(vec), full(w2f), full(w2b), full(a2f), full(a2b), full(g2), full(bd)]
    args = [h_rw, h_rw, h_rw, mu, vec, w2f, w2b, a2f, a2b, g2, bd]
    if has_vfirst:
        in_specs += [full(v2), pl.BlockSpec((tm, RW_DIM), lambda i: (i, 0))]
        args += [v2, v_first]
    row = lambda w: pl.BlockSpec((tm, w), lambda i: (i, 0))
    widths = [RW_DIM, RW_DIM, RW_DIM, 2 * RW_DIM, 2 * RW_DIM, 2 * RW_DIM, RW_DIM]
    return pl.pallas_call(
        functools.partial(_rw_prep_body, tiles_per_seq=tps, tm=tm, has_vfirst=has_vfirst),
        grid=(n // tm,),
        in_specs=in_specs,
        out_specs=[row(w) for w in widths],
        out_shape=[jax.ShapeDtypeStruct((n, w), F32) for w in widths],
        compiler_params=_cparams("parallel"),
        name="rwkv_prep",
    )(*args)


def _wkv_body(*refs):
    C = WKV_CHUNK
    in_refs, (yf_ref, yb_ref, st_ref) = refs[:12], refs[12:]
    npair = RW_DIM // LANES

    @pl.when(pl.program_id(1) == 0)
    def _():
        st_ref[...] = jnp.zeros_like(st_ref)

    ti = lax.broadcasted_iota(jnp.int32, (C, C), 0)
    si = lax.broadcasted_iota(jnp.int32, (C, C), 1)
    lane = lax.broadcasted_iota(jnp.int32, (1, LANES), 1)
    hm0 = jnp.where(lane < RW_HEAD, 1.0, 0.0)
    hm1 = 1.0 - hm0
    stack = lambda x: jnp.concatenate([x * hm0, x * hm1], axis=0).astype(BF16)
    dup = lambda x: jnp.concatenate([x, x], axis=0).astype(BF16)
    ri = lax.broadcasted_iota(jnp.int32, (2 * C, 2 * C), 0)
    ci = lax.broadcasted_iota(jnp.int32, (2 * C, 2 * C), 1)
    same = (ri & C) == (ci & C)
    dt = (ri & (C - 1)) - (ci & (C - 1))
    eye = jnp.where(ri == ci, 1.0, 0.0)
    valid = (lax.broadcasted_iota(jnp.int32, (2 * C, LANES), 0) & C) == (
        lax.broadcasted_iota(jnp.int32, (2 * C, LANES), 1) & RW_HEAD)
    mm = functools.partial(jnp.dot, preferred_element_type=F32)
    mm_tb = lambda a, b: lax.dot_general(a, b, _TB, preferred_element_type=F32)
    mm_ta = lambda a, b: lax.dot_general(a, b, _TA, preferred_element_type=F32)

    chains = [(d, p) for d in range(2) for p in range(npair)]
    xa, xr, vst, bws, kws, wtots, gms, strict, incl = [], [], [], [], [], [], [], [], []
    for d in range(2):
        r_ref, v_ref, kk_ref, lw_ref, k_ref, ag_ref = in_refs[6 * d:6 * d + 6]
        sgn = 1 - 2 * d
        tri = jnp.where((ti - si) * sgn >= 0, 1.0, 0.0)
        lw_all = lw_ref[...]
        cum_all = jnp.dot(tri, lw_all, precision=HIGHEST, preferred_element_type=F32)
        tot_all = jnp.sum(lw_all, axis=0, keepdims=True)
        strict_d = same & (dt * sgn > 0)
        incl_d = same & (dt * sgn >= 0)
        for p in range(npair):
            ls = slice(p * LANES, (p + 1) * LANES)
            r, v, kk, k, ag = (x[:, ls] for x in (r_ref, v_ref, kk_ref, k_ref, ag_ref))
            lw, cum, tot = lw_all[:, ls], cum_all[:, ls], tot_all[:, ls]
            ei = jnp.exp(-cum)
            ew = jnp.exp(tot - cum)
            b = kk * ag
            xa.append(stack(-kk * jnp.exp(cum - lw)))
            xr.append(stack(r * jnp.exp(cum)))
            vst.append(stack(v))
            bws.append(stack(b * ew))
            kws.append(stack(k * ew))
            wtots.append(jnp.exp(tot))
            strict.append(strict_d)
            incl.append(incl_d)
            x2 = jnp.concatenate([xa[-1], xr[-1]], axis=0)
            y2 = jnp.concatenate([dup(b * ei), dup(k * ei)], axis=0)
            gms.append(mm_tb(x2, y2))
    nch = range(len(chains))
    l_ab = [jnp.where(strict[i], gms[i][0:2 * C, 0:2 * C], 0.0) for i in nch]
    l_ak = [jnp.where(strict[i], gms[i][0:2 * C, 2 * C:], 0.0).astype(BF16) for i in nch]
    m_rb = [jnp.where(incl[i], gms[i][2 * C:, 0:2 * C], 0.0).astype(BF16) for i in nch]
    m_rk = [jnp.where(incl[i], gms[i][2 * C:, 2 * C:], 0.0).astype(BF16) for i in nch]

    pinv = [eye + l for l in l_ab]
    lp = l_ab
    for _ in range(int(math.log2(C)) - 1):
        lpb = [l.astype(BF16) for l in lp]
        lp = [mm(l, l) for l in lpb]
        pinv = [pi + mm(pi.astype(BF16), l.astype(BF16)) for pi, l in zip(pinv, lp)]

    st = [st_ref[d, p] for d, p in chains]
    s_kv = [s.T.astype(BF16) for s in st]
    pre = [mm(jnp.concatenate([xa[i], l_ak[i]], axis=1), jnp.concatenate([s_kv[i], vst[i]], axis=0))
           for i in nch]
    ustb = [jnp.where(valid, mm(pinv[i].astype(BF16), pre[i].astype(BF16)), 0.0).astype(BF16) for i in nch]
    for i, (d, p) in enumerate(chains):
        yst = mm(jnp.concatenate([xr[i], m_rb[i], m_rk[i]], axis=1),
                 jnp.concatenate([s_kv[i], ustb[i], vst[i]], axis=0))
        yst = jnp.where(valid, yst, 0.0)
        (yf_ref, yb_ref)[d][:, p * LANES:(p + 1) * LANES] = yst[0:C] + yst[C:]
    for i, (d, p) in enumerate(chains):
        st_ref[d, p] = st[i] * wtots[i] + mm_ta(jnp.concatenate([ustb[i], vst[i]], axis=0),
                                                jnp.concatenate([bws[i], kws[i]], axis=0))


def wkv(r, v, kk, lw2, k2, ag2, *, batch, seq):
    n = r.shape[0]
    C = WKV_CHUNK
    nc = seq // C
    fwd = lambda col: pl.BlockSpec((C, RW_DIM), lambda b, c: (b * nc + c, col))
    bwd = lambda col: pl.BlockSpec((C, RW_DIM), lambda b, c: (b * nc + nc - 1 - c, col))
    return pl.pallas_call(
        _wkv_body,
        grid=(batch, nc),
        in_specs=[fwd(0), fwd(0), fwd(0), fwd(0), fwd(0), fwd(0),
                  bwd(0), bwd(0), bwd(0), bwd(1), bwd(1), bwd(1)],
        out_specs=[fwd(0), bwd(0)],
        out_shape=[jax.ShapeDtypeStruct((n, RW_DIM), F32)] * 2,
        scratch_shapes=[pltpu.VMEM((2, RW_DIM // LANES, LANES, LANES), F32)],
        compiler_params=_cparams("parallel", "arbitrary"),
        name="wkv_scan",
    )(r, v, kk, lw2, k2, ag2, r, v, kk, lw2, k2, ag2)


def _rw_post_body(yf_ref, yb_ref, r_ref, k2_ref, v_ref, g_ref, vec_ref, bd_ref, o_ref):
    vec = vec_ref[...]
    lnx_w, lnx_b, r_k = (vec[i:i + 1, :] for i in range(3))
    bd = bd_ref[...]
    y = yf_ref[...] + yb_ref[...]
    inv = 1.0 / RW_HEAD
    mean = _head_sums(y, bd) * inv
    d = y - mean
    var = _head_sums(d * d, bd) * inv
    yn = d * lax.rsqrt(var + LNX_EPS) * lnx_w + lnx_b
    ksum = k2_ref[:, 0:RW_DIM] + k2_ref[:, RW_DIM:]
    bonus = _head_sums(r_ref[...] * ksum * r_k, bd) * v_ref[...]
    o_ref[...] = ((yn + bonus) * g_ref[...]).astype(o_ref.dtype)


def rwkv_post(yf, yb, r, k2, v, g, vec, bd, *, tm):
    n = r.shape[0]
    row = lambda w: pl.BlockSpec((tm, w), lambda i: (i, 0))
    full = lambda a: pl.BlockSpec(a.shape, lambda i: (0,) * a.ndim)
    return pl.pallas_call(
        _rw_post_body,
        grid=(n // tm,),
        in_specs=[row(RW_DIM), row(RW_DIM), row(RW_DIM), row(2 * RW_DIM), row(RW_DIM), row(RW_DIM),
                  full(vec), full(bd)],
        out_specs=row(RW_DIM),
        out_shape=jax.ShapeDtypeStruct((n, RW_DIM), BF16),
        compiler_params=_cparams("parallel"),
        name="rwkv_post",
    )(yf, yb, r, k2, v, g, vec, bd)


def _merge_body(o_ref, yg_ref, ga_ref, gr_ref, wa_ref, wr_ref, m_ref):
    attn = jnp.dot(o_ref[...], wa_ref[...], preferred_element_type=F32)
    rw = jnp.dot(yg_ref[...], wr_ref[...], preferred_element_type=F32)
    m_ref[...] = (_sigmoid(ga_ref[...].astype(F32)) * attn + _sigmoid(gr_ref[...].astype(F32)) * rw).astype(m_ref.dtype)


def merge_branches(o, yg, gates, wa, wr, *, tm, tn):
    n = o.shape[0]
    nj = D_MODEL // tn
    return pl.pallas_call(
        _merge_body,
        grid=(n // tm, nj),
        in_specs=[pl.BlockSpec((tm, MLA_HEADS * V_DIM), lambda i, j: (i, 0)),
                  pl.BlockSpec((tm, RW_DIM), lambda i, j: (i, 0)),
                  pl.BlockSpec((tm, tn), lambda i, j: (i, j)),
                  pl.BlockSpec((tm, tn), lambda i, j: (i, nj + j)),
                  pl.BlockSpec((MLA_HEADS * V_DIM, tn), lambda i, j: (0, j)),
                  pl.BlockSpec((RW_DIM, tn), lambda i, j: (0, j))],
        out_specs=pl.BlockSpec((tm, tn), lambda i, j: (i, j)),
        out_shape=jax.ShapeDtypeStruct((n, D_MODEL), BF16),
        compiler_params=_cparams("parallel", "parallel"),
        name="merge_branches",
    )(o, yg, gates, gates, wa, wr)


def _mm_res_norm_body(a_ref, w_ref, x_ref, g_ref, o_ref, *n_refs):
    out = x_ref[...] + jnp.dot(a_ref[...], w_ref[...], preferred_element_type=F32)
    o_ref[...] = out
    xhat = out * lax.rsqrt(jnp.mean(out * out, axis=-1, keepdims=True) + EPS)
    for row, n_ref in enumerate(n_refs):
        n_ref[...] = (xhat * g_ref[row:row + 1, :]).astype(n_ref.dtype)


def matmul_residual_norm(a, w, x, gains, out_dtypes, *, tm):
    n, k = a.shape
    nc = w.shape[1]
    row = lambda width: pl.BlockSpec((tm, width), lambda i: (i, 0))
    return pl.pallas_call(
        _mm_res_norm_body,
        grid=(n // tm,),
        in_specs=[row(k),
                  pl.BlockSpec((k, nc), lambda i: (0, 0), pipeline_mode=pl.Buffered(1)),
                  row(nc),
                  pl.BlockSpec(gains.shape, lambda i: (0, 0))],
        out_specs=[row(nc)] * (1 + len(out_dtypes)),
        out_shape=[jax.ShapeDtypeStruct((n, nc), F32)] + [jax.ShapeDtypeStruct((n, nc), dt) for dt in out_dtypes],
        compiler_params=_cparams("parallel"),
        name="matmul_residual_norm",
    )(a, w, x, gains)


def _top16_rows(ss, idx):
    big = 3.0e38
    vals = [[] for _ in ss]
    poss = [[] for _ in ss]
    for _ in range(PEER_TOPK):
        ms = [jnp.max(s, axis=0, keepdims=True) for s in ss]
        ps = [jnp.min(jnp.where(s == m, idx, big), axis=0, keepdims=True) for s, m in zip(ss, ms)]
        ss = [jnp.where(idx == p, -jnp.inf, s) for s, p in zip(ss, ps)]
        for i, (m, p) in enumerate(zip(ms, ps)):
            vals[i].append(m)
            poss[i].append(p)
    return [(jnp.concatenate(v, axis=0), jnp.concatenate(p, axis=0)) for v, p in zip(vals, poss)]


def _candidates(sv0, sv1):
    tok = sv0.shape[1]
    io = lambda rows: lax.broadcasted_iota(jnp.int32, (rows, tok), 0).astype(F32)
    half = PEER_TOPK // 2
    parts = [sv0[0:1, :] + sv1]
    idxs = [io(PEER_TOPK)]
    for a in range(1, half):
        parts.append(sv0[a:a + 1, :] + sv1[0:half, :])
        idxs.append(io(half) + float(a * PEER_TOPK))
    parts.append(sv0[half:, :] + sv1[0:1, :])
    idxs.append((io(half) + float(half)) * float(PEER_TOPK))
    return jnp.concatenate(parts, axis=0), jnp.concatenate(idxs, axis=0)


def _pick_rows(table, sel):
    out = jnp.zeros(sel.shape, table.dtype)
    for a in range(PEER_TOPK):
        out = jnp.where(sel == a, table[a:a + 1, :], out)
    return out


def _peer_topk_body(q_ref, sk_ref, i1_ref, i2_ref, gt_ref):
    i1s, i2s, gts = [], [], []
    key_idx = lax.broadcasted_iota(jnp.int32, (N_KEYS, q_ref.shape[0]), 0).astype(F32)
    for h in range(PEER_HEADS):
        ss = []
        for p in range(2):
            c0 = (2 * h + p) * PEER_HALF
            qb = q_ref[:, c0:c0 + PEER_HALF].astype(BF16)
            ss.append(lax.dot_general(sk_ref[p], qb, _TB, preferred_element_type=F32))
        (sv0, si0), (sv1, si1) = _top16_rows(ss, key_idx)
        cand, cand_idx = _candidates(sv0, sv1)
        ((top_s, top_j),) = _top16_rows([cand], cand_idx)
        ja = jnp.floor(top_j * (1.0 / PEER_TOPK))
        i1s.append(_pick_rows(si0, ja))
        i2s.append(_pick_rows(si1, top_j - ja * PEER_TOPK))
        e = jnp.exp(top_s - top_s[0:1, :])
        gts.append(e / jnp.sum(e, axis=0, keepdims=True))
    i1_ref[...] = jnp.concatenate(i1s, axis=0).T
    i2_ref[...] = jnp.concatenate(i2s, axis=0).T
    gt_ref[...] = jnp.concatenate(gts, axis=0).T


def peer_topk(q, sub_keys):
    n = q.shape[0]
    tmk = LANES
    out = pl.BlockSpec((tmk, NSLOT), lambda i: (i, 0))
    return pl.pallas_call(
        _peer_topk_body,
        grid=(n // tmk,),
        in_specs=[pl.BlockSpec((tmk, 2 * PEER_HALF * PEER_HEADS), lambda i: (i, 0)),
                  pl.BlockSpec(sub_keys.shape, lambda i: (0, 0, 0))],
        out_specs=[out, out, out],
        out_shape=[jax.ShapeDtypeStruct((n, NSLOT), F32)] * 3,
        compiler_params=_cparams("parallel"),
        name="peer_topk",
    )(q, sub_keys)


def _gelu_exact(x):
    return 0.5 * x * (1.0 + lax.erf(x * (1.0 / math.sqrt(2.0))))


_HI16 = 0xFFFF0000
G_UNROLL = SUBLANES
E_SUB = 2 * N_KEYS
PEER_TE = 2 * E_SUB


def _bf16_bits(x):
    return pltpu.bitcast(x, jnp.uint32) + jnp.uint32(0x8000)


def _peer_dense_body(xn_ref, i1_ref, i2_ref, gt_ref, ut0_ref, utb_ref, utn_ref, v_ref, x_ref, lnf_ref, scal_ref,
                     o_ref, g_ref, h_ref, *, tm, te, final_norm):
    e = pl.program_id(1)
    half = tm // 2
    mm_tb = lambda a, b: lax.dot_general(a, b, _TB, preferred_element_type=F32)
    hsc, csc, osc = (scal_ref[r:r + 1, 0:1] for r in range(3))
    nsub = te // E_SUB

    def activations(ut_ref):
        xn = xn_ref[...]
        return [jnp.dot(xn, ut_ref[:, j * E_SUB:(j + 1) * E_SUB], preferred_element_type=F32) * hsc
                for j in range(nsub)]

    def accumulate(block, hs):
        coefs = []
        for j in range(nsub):
            row0 = block * (te // N_KEYS) + j * (E_SUB // N_KEYS)
            words = [g_ref[pl.ds(row0 + r, half, stride=G_PITCH), :] for r in range(E_SUB // N_KEYS)]
            top = jnp.concatenate([pltpu.bitcast(w & jnp.uint32(_HI16), F32) for w in words], axis=1)
            bot = jnp.concatenate([pltpu.bitcast(w << 16, F32) for w in words], axis=1)
            gate = jnp.concatenate([top, bot], axis=0)
            coef = gate * _gelu_exact(hs[j]) * csc
            coefs.append(jnp.clip(coef, -FP8_MAX, FP8_MAX).astype(FP8))
        return coefs

    @pl.when(e == 0)
    def _():
        for j, h in enumerate(activations(ut0_ref)):
            h_ref[:, j * E_SUB:(j + 1) * E_SUB] = h
        o_ref[...] = x_ref[...] * osc
        sub = lax.broadcasted_iota(jnp.int32, (N_KEYS, NSLOT), 0).astype(F32).astype(BF16)
        one = jnp.ones((), BF16)
        zero = jnp.zeros((), BF16)

        def build(it, carry):
            base = pl.multiple_of(it * G_UNROLL, SUBLANES)
            tiles = [(ref[pl.ds(base, G_UNROLL), :], ref[pl.ds(half + base, G_UNROLL), :])
                     for ref in (i1_ref, i2_ref, gt_ref)]
            a_ts, b_ts = [], []
            for j in range(G_UNROLL):
                for t in range(2):
                    i1, i2, gt = (tiles[q][t][j:j + 1, :].astype(BF16) for q in range(3))
                    a_ts.append(jnp.where(sub == i1, one, zero))
                    b_ts.append(jnp.where(sub == i2, gt, zero))
            gs = [mm_tb(a, b) for a, b in zip(a_ts, b_ts)]
            for j in range(G_UNROLL):
                word = (_bf16_bits(gs[2 * j]) & jnp.uint32(_HI16)) | (_bf16_bits(gs[2 * j + 1]) >> 16)
                g_ref[pl.ds(pl.multiple_of((base + j) * G_PITCH, SUBLANES), N_KEYS), :] = word
            return carry

        lax.fori_loop(0, half // G_UNROLL, build, 0)

    hs_b = activations(utb_ref)
    coef_a = accumulate(2 * e, [h_ref[:, j * E_SUB:(j + 1) * E_SUB] for j in range(nsub)])
    for j, h in enumerate(activations(utn_ref)):
        h_ref[:, j * E_SUB:(j + 1) * E_SUB] = h
    coef_b = accumulate(2 * e + 1, hs_b)
    o_ref[...] += jnp.dot(jnp.concatenate(coef_a + coef_b, axis=1), v_ref[...], preferred_element_type=F32)

    @pl.when(e == pl.num_programs(1) - 1)
    def _():
        out = o_ref[...] * scal_ref[3:4, 0:1]
        if final_norm:
            out = _rms_rows(out, lnf_ref[...], EPS)
        o_ref[...] = out


def peer_dense(xn, i1, i2, gt, ut, v, x, ln_f, scal, *, tm, te, final_norm):
    n = xn.shape[0]
    ne = v.shape[0] // te
    assert tm % (2 * G_UNROLL) == 0 and te % E_SUB == 0 and ne % 2 == 0 and ut.shape == (ne, D_MODEL, te)
    row = lambda w: pl.BlockSpec((tm, w), lambda i, e: (i, 0))
    return pl.pallas_call(
        functools.partial(_peer_dense_body, tm=tm, te=te, final_norm=final_norm),
        grid=(n // tm, ne // 2),
        in_specs=[row(D_MODEL), row(NSLOT), row(NSLOT), row(NSLOT),
                  pl.BlockSpec((None, D_MODEL, te), lambda i, e: (0, 0, 0)),
                  pl.BlockSpec((None, D_MODEL, te), lambda i, e: (2 * e + 1, 0, 0)),
                  pl.BlockSpec((None, D_MODEL, te), lambda i, e: (jnp.minimum(2 * e + 2, ne - 1), 0, 0)),
                  pl.BlockSpec((2 * te, D_MODEL), lambda i, e: (e, 0)),
                  row(D_MODEL),
                  pl.BlockSpec((1, D_MODEL), lambda i, e: (0, 0)),
                  pl.BlockSpec(scal.shape, lambda i, e: (0, 0))],
        out_specs=row(D_MODEL),
        out_shape=jax.ShapeDtypeStruct((n, D_MODEL), F32),
        scratch_shapes=[pltpu.VMEM((tm // 2 * G_PITCH, N_KEYS), jnp.uint32), pltpu.VMEM((tm, te), F32)],
        compiler_params=_cparams("parallel", "arbitrary"),
        name="peer_dense",
    )(xn, i1, i2, gt, ut, ut, ut, v, x, ln_f.reshape(1, D_MODEL), scal)


def _pad_rows(a, h):
    return jnp.pad(a, ((0, h - a.shape[0]),) + ((0, 0),) * (a.ndim - 1))


def _swap_halves(a):
    half = a.shape[-1] // 2
    return jnp.concatenate([a[..., half:], a[..., :half]], axis=-1)


def _layer_weights(i, p):
    first = i == 0
    w_in = p["w_in_first"] if first else p["w_in_rest"][i - 1]
    mu = p["mu_first"] if first else p["mu_rest"][i - 1]
    mla_cols = Q_LORA + KV_LORA + ROPE_DIM
    gate0 = mla_cols
    rw0 = mla_cols + 2 * D_MODEL
    k_rope_w = w_in[:, Q_LORA + KV_LORA:mla_cols]
    w_mla = jnp.concatenate([w_in[:, :mla_cols], _swap_halves(k_rope_w)], axis=1)
    w_gate = w_in[:, gate0:rw0]

    def rw_layout(a):
        c = 3 * RW_DIM
        dwf, dwb = a[..., c:c + DECAY_LORA], a[..., c + DECAY_LORA:c + 2 * DECAY_LORA]
        c += 2 * DECAY_LORA
        daf, dab = a[..., c:c + AAA_LORA], a[..., c + AAA_LORA:c + 2 * AAA_LORA]
        c += 2 * AAA_LORA
        dg = a[..., c:c + GATE_LORA]
        c += GATE_LORA
        dv = a[..., c:]
        z = lambda w: jnp.zeros(a.shape[:-1] + (w,), a.dtype)
        return jnp.concatenate([a[..., :3 * RW_DIM], dwf, dwb, daf, dab, dg, z(2 * LANES - GATE_LORA),
                                dv, z(2 * LANES - dv.shape[-1])], axis=-1)

    w_rw = rw_layout(w_in[:, rw0:])
    mu_l = rw_layout(mu[None, :])
    mu_l = jnp.concatenate([1.0 - mu_l, 0.5 * mu_l], axis=0)
    zeros64 = jnp.zeros((DECAY_LORA, RW_DIM), F32)
    wq = (p["w_uq"][i] * (NOPE_DIM + ROPE_DIM) ** -0.5).reshape(Q_LORA, MLA_HEADS, NOPE_DIM + ROPE_DIM)
    wq = jnp.concatenate([wq, _swap_halves(wq[..., NOPE_DIM:])], axis=-1).reshape(Q_LORA, MLA_HEADS * QK_PAD)
    vec = jnp.stack([p["w0_f"][i], p["w0_b"][i], p["a0_f"][i], p["a0_b"][i], p["k_k"][i], p["k_a"][i],
                     p["v0_rest"][i - 1] if not first else jnp.zeros((RW_DIM,), F32),
                     jnp.zeros((RW_DIM,), F32)])
    lw = {
        "ln1": p["ln1"][i], "w_mla": w_mla.astype(BF16), "w_gate": w_gate.astype(BF16),
        "w_rw": w_rw.astype(BF16),
        "mu": mu_l, "q_norm": p["q_norm"][i], "wq": wq.astype(BF16), "kv_norm": p["kv_norm"][i],
        "wkv": p["w_ukv"][i].astype(BF16), "w_o_attn": p["w_o_attn"][i].astype(BF16),
        "vec": vec,
        "w2f": jnp.concatenate([p["w2_f"][i], zeros64]).astype(BF16),
        "w2b": jnp.concatenate([zeros64, p["w2_b"][i]]).astype(BF16),
        "a2f": jnp.concatenate([p["a2_f"][i], zeros64]).astype(BF16),
        "a2b": jnp.concatenate([zeros64, p["a2_b"][i]]).astype(BF16),
        "g2": _pad_rows(p["g2"][i], 2 * LANES).astype(BF16),
        "v2": None if first else _pad_rows(p["v2_rest"][i - 1], LANES).astype(BF16),
        "post_vec": jnp.stack([p["lnx_w"][i], p["lnx_b"][i], p["r_k"][i].reshape(RW_DIM)]
                              + [jnp.zeros((RW_DIM,), F32)] * 5),
        "w_o_rwkv": p["w_o_rwkv"][i].astype(BF16), "w_out": p["w_out"][i].astype(BF16),
        "ln2": p["ln2"][i], "w_pq": p["w_pq"][i].astype(BF16), "sub_keys": p["sub_keys"][i].astype(BF16),
    }
    pow2_below = lambda v: jnp.exp2(jnp.floor(jnp.log2(v)))
    tiny = 1e-30
    u, v, g2n = p["peer_u"][i], p["peer_v"][i], p["ln2"][i]
    xmax = math.sqrt(D_MODEL) * jnp.maximum(jnp.max(jnp.abs(g2n)), tiny)
    sx = pow2_below(FP8_MAX / xmax)
    su = pow2_below(FP8_MAX / jnp.maximum(jnp.max(jnp.abs(u)), tiny))
    sv = pow2_below(FP8_MAX / jnp.maximum(jnp.max(jnp.abs(v)), tiny))
    hmax = xmax * jnp.sqrt(jnp.max(jnp.sum(u * u, axis=1))) * (1.0 + 2.0 ** -4) ** 2
    sc = pow2_below(FP8_MAX / (PEER_HEADS * jnp.maximum(hmax, tiny)))
    lw["ln2_rows"] = jnp.stack([g2n, g2n * sx])
    lw["peer_ut"] = (u * su).astype(FP8).reshape(-1, PEER_TE, D_MODEL).transpose(0, 2, 1)
    lw["peer_v"] = (v * sv).astype(FP8)
    scal = jnp.stack([1.0 / (sx * su), sc, sc * sv, 1.0 / (sc * sv)] + [jnp.zeros((), F32)] * 4)
    lw["peer_scal"] = jnp.broadcast_to(scal[:, None], (SUBLANES, LANES)).astype(F32)
    return lw


def _rope_table(seq):
    inv = 1.0 / (ROPE_THETA ** (jnp.arange(0, ROPE_DIM, 2, dtype=F32) / ROPE_DIM))
    ang = jnp.arange(seq, dtype=F32)[:, None] * inv[None, :]
    c, s = jnp.cos(ang), jnp.sin(ang)
    return jnp.concatenate([c, c, -s, s], axis=1)


def _tiles(seq):
    return dict(tm=min(512, seq), tmm=min(1024, seq), tq=min(8 * ATTN_SUB, seq), tprep=min(256, seq),
                tpeer=min(512, seq))


def _trunk(x, layers, ln_f, bd):
    batch, seq, _ = x.shape
    n = batch * seq
    t = _tiles(seq)
    tm, tmm = t["tm"], t["tmm"]
    ccss = _rope_table(seq)
    x = x.reshape(n, D_MODEL)
    v_first = None
    for li, lw in enumerate(layers):
        (xn1,) = rms_norm(x, lw["ln1"].reshape(1, D_MODEL), (BF16,), tm=tm)
        h_mla = matmul(xn1, lw["w_mla"], tm=tmm, tn=MLA_W, out_dtype=BF16)
        gates = matmul(xn1, lw["w_gate"], tm=tmm, tn=1024, out_dtype=BF16)
        h_rw = matmul(xn1, lw["w_rw"], tm=tmm, tn=1280, out_dtype=F32)
        q = mla_q_proj(h_mla, lw["q_norm"], lw["wq"], ccss, seq=seq, tm=tm)
        k, v = mla_kv_proj(h_mla, lw["kv_norm"], lw["wkv"], ccss, seq=seq, tm=tm)
        o = attention(q, k, v, batch=batch, seq=seq, tq=t["tq"])
        r, vv, kk, ag2, lw2, k2, g = rwkv_prep(h_rw, lw["mu"], lw["vec"], lw["w2f"], lw["w2b"], lw["a2f"],
                                              lw["a2b"], lw["g2"], bd, lw["v2"], v_first,
                                              seq=seq, tm=t["tprep"])
        if v_first is None:
            v_first = vv
        yf, yb = wkv(r, vv, kk, lw2, k2, ag2, batch=batch, seq=seq)
        yg = rwkv_post(yf, yb, r, k2, vv, g, lw["post_vec"], bd, tm=t["tprep"])
        m = merge_branches(o, yg, gates, lw["w_o_attn"], lw["w_o_rwkv"], tm=tmm, tn=512)
        x, xn, xn8 = matmul_residual_norm(m, lw["w_out"], x, lw["ln2_rows"], (BF16, FP8), tm=tm)
        qp = matmul(xn, lw["w_pq"], tm=tmm, tn=1024, out_dtype=F32)
        i1, i2, gt = peer_topk(qp, lw["sub_keys"])
        x = peer_dense(xn8, i1, i2, gt, lw["peer_ut"], lw["peer_v"], x, ln_f, lw["peer_scal"],
                       tm=t["tpeer"], te=PEER_TE,
                       final_norm=li == len(layers) - 1)
    return x.reshape(batch, seq, D_MODEL)


def kernel(x_prompt, x_sample, ln1, w_in_first, mu_first, w_in_rest, mu_rest, q_norm, w_uq, kv_norm, w_ukv, w_o_attn, w0_f, w2_f, w0_b, w2_b, a0_f, a2_f, a0_b, a2_b, g2, k_k, k_a, r_k, lnx_w, lnx_b, v0_rest, v2_rest, w_o_rwkv, w_out, ln2, w_pq, sub_keys, peer_u, peer_v, ln_f):
    p = dict(ln1=ln1, w_in_first=w_in_first, mu_first=mu_first, w_in_rest=w_in_rest, mu_rest=mu_rest,
             q_norm=q_norm, w_uq=w_uq, kv_norm=kv_norm, w_ukv=w_ukv, w_o_attn=w_o_attn, w0_f=w0_f, w2_f=w2_f,
             w0_b=w0_b, w2_b=w2_b, a0_f=a0_f, a2_f=a2_f, a0_b=a0_b, a2_b=a2_b, g2=g2, k_k=k_k, k_a=k_a, r_k=r_k,
             lnx_w=lnx_w, lnx_b=lnx_b, v0_rest=v0_rest, v2_rest=v2_rest, w_o_rwkv=w_o_rwkv, w_out=w_out,
             ln2=ln2, w_pq=w_pq, sub_keys=sub_keys, peer_u=peer_u, peer_v=peer_v)
    depth = ln1.shape[0]
    layers = [_layer_weights(i, p) for i in range(depth)]
    lane = jnp.arange(LANES)
    bd = (lane[:, None] // RW_HEAD == lane[None, :] // RW_HEAD).astype(F32)
    return (_trunk(x_prompt, layers, ln_f, bd), _trunk(x_sample, layers, ln_f, bd))
```

```python
import functools
import math

import jax
import jax.numpy as jnp
from jax import lax
from jax.experimental import pallas as pl
from jax.experimental.pallas import tpu as pltpu

F32 = jnp.float32
BF16 = jnp.bfloat16
FP8 = jnp.float8_e4m3fn
FP8_MAX = 448.0

LANES = 128
SUBLANES = 8
VMEM_LIMIT_BYTES = 56 * 1024 * 1024

D_MODEL = 2048
MLA_HEADS = 16
Q_LORA = 512
KV_LORA = 512
NOPE_DIM = 128
ROPE_DIM = 64
V_DIM = 128
ROPE_THETA = 10000.0
RW_HEAD = 64
RW_HEADS = 16
RW_DIM = RW_HEADS * RW_HEAD
DECAY_LORA = 64
AAA_LORA = 64
GATE_LORA = 160
LNX_EPS = 64e-5
PEER_HEADS = 8
N_KEYS = 128
PEER_HALF = 128
PEER_TOPK = 16
EPS = 1e-6

QK_PAD = 2 * LANES
MLA_W = Q_LORA + KV_LORA + LANES
LORA_W = 2 * DECAY_LORA + 2 * AAA_LORA + 2 * LANES + 2 * LANES
RW_W = 3 * RW_DIM + LORA_W
PROJ_HEADS = 8
ATTN_SUB = 256
WKV_CHUNK = 64
G_PITCH = N_KEYS + SUBLANES
NSLOT = PEER_HEADS * PEER_TOPK

_TB = (((1,), (1,)), ((), ()))
_TA = (((0,), (0,)), ((), ()))


def _cparams(*sem):
    return pltpu.CompilerParams(dimension_semantics=sem, vmem_limit_bytes=VMEM_LIMIT_BYTES)


def _rms_rows(x, g, eps):
    ms = jnp.mean(x * x, axis=-1, keepdims=True)
    return x * lax.rsqrt(ms + eps) * g


def _rms_norm_body(x_ref, g_ref, *o_refs):
    x = x_ref[...].astype(F32)
    xhat = x * lax.rsqrt(jnp.mean(x * x, axis=-1, keepdims=True) + EPS)
    for row, o_ref in enumerate(o_refs):
        o_ref[...] = (xhat * g_ref[row:row + 1, :]).astype(o_ref.dtype)


def rms_norm(x, gains, out_dtypes, *, tm):
    n, k = x.shape
    spec = pl.BlockSpec((tm, k), lambda i: (i, 0))
    return pl.pallas_call(
        _rms_norm_body,
        grid=(n // tm,),
        in_specs=[spec, pl.BlockSpec(gains.shape, lambda i: (0, 0))],
        out_specs=[spec] * len(out_dtypes),
        out_shape=[jax.ShapeDtypeStruct((n, k), dt) for dt in out_dtypes],
        compiler_params=_cparams("parallel"),
        name="rms_norm",
    )(x, gains)


def _mm_body(a_ref, w_ref, o_ref):
    o_ref[...] = jnp.dot(a_ref[...], w_ref[...], preferred_element_type=F32).astype(o_ref.dtype)


def matmul(a, w, *, tm, tn, out_dtype):
    n, k = a.shape
    nc = w.shape[1]
    tn = min(tn, nc)
    assert n % tm == 0 and nc % tn == 0 and w.shape[0] == k
    return pl.pallas_call(
        _mm_body,
        grid=(n // tm, nc // tn),
        in_specs=[pl.BlockSpec((tm, k), lambda i, j: (i, 0)),
                  pl.BlockSpec((k, tn), lambda i, j: (0, j))],
        out_specs=pl.BlockSpec((tm, tn), lambda i, j: (i, j)),
        out_shape=jax.ShapeDtypeStruct((n, nc), out_dtype),
        compiler_params=_cparams("parallel", "parallel"),
        name="matmul",
    )(a, w)


def _rope_lanes(y2, ccss):
    w = y2 * ccss
    return w + pltpu.roll(w, ROPE_DIM, axis=1)


def _qproj_body(c_ref, g_ref, w_ref, t_ref, q_ref, cn_ref):
    @pl.when(pl.program_id(1) == 0)
    def _():
        cn_ref[...] = _rms_rows(c_ref[...].astype(F32), g_ref[...], EPS).astype(BF16)

    y = jnp.dot(cn_ref[...], w_ref[...], preferred_element_type=F32)
    tab = t_ref[...]
    lane = lax.broadcasted_iota(jnp.int32, tab.shape, 1)
    for hh in range(PROJ_HEADS):
        c0 = hh * QK_PAD
        q_ref[:, c0:c0 + LANES] = y[:, c0:c0 + LANES].astype(q_ref.dtype)
        rot = _rope_lanes(y[:, c0 + LANES:c0 + QK_PAD], tab)
        q_ref[:, c0 + LANES:c0 + QK_PAD] = jnp.where(lane < ROPE_DIM, rot, 0.0).astype(q_ref.dtype)


def mla_q_proj(h_mla, q_norm, wq, ccss, *, seq, tm):
    n = h_mla.shape[0]
    nt = seq // tm
    return pl.pallas_call(
        _qproj_body,
        grid=(n // tm, MLA_HEADS // PROJ_HEADS),
        in_specs=[pl.BlockSpec((tm, Q_LORA), lambda i, h: (i, 0)),
                  pl.BlockSpec((1, Q_LORA), lambda i, h: (0, 0)),
                  pl.BlockSpec((Q_LORA, PROJ_HEADS * QK_PAD), lambda i, h: (0, h)),
                  pl.BlockSpec((tm, LANES), lambda i, h: (i % nt, 0))],
        out_specs=pl.BlockSpec((tm, PROJ_HEADS * QK_PAD), lambda i, h: (i, h)),
        out_shape=jax.ShapeDtypeStruct((n, MLA_HEADS * QK_PAD), BF16),
        scratch_shapes=[pltpu.VMEM((tm, Q_LORA), BF16)],
        compiler_params=_cparams("parallel", "arbitrary"),
        name="mla_q_proj",
    )(h_mla, q_norm.reshape(1, Q_LORA), wq, ccss)


def _kvproj_body(c_ref, kr_ref, g_ref, w_ref, t_ref, k_ref, v_ref, cn_ref, krr_ref):
    @pl.when(pl.program_id(1) == 0)
    def _():
        cn_ref[...] = _rms_rows(c_ref[...].astype(F32), g_ref[...], EPS).astype(BF16)
        krr_ref[...] = _rope_lanes(kr_ref[...].astype(F32), t_ref[...]).astype(BF16)

    y = jnp.dot(cn_ref[...], w_ref[...], preferred_element_type=F32)
    for hh in range(PROJ_HEADS):
        c0 = hh * (NOPE_DIM + V_DIM)
        k_ref[:, hh * QK_PAD:hh * QK_PAD + LANES] = y[:, c0:c0 + NOPE_DIM].astype(k_ref.dtype)
        k_ref[:, hh * QK_PAD + LANES:(hh + 1) * QK_PAD] = krr_ref[...]
        v_ref[:, hh * V_DIM:(hh + 1) * V_DIM] = y[:, c0 + NOPE_DIM:c0 + NOPE_DIM + V_DIM].astype(v_ref.dtype)


def mla_kv_proj(h_mla, kv_norm, wkv, ccss, *, seq, tm):
    n = h_mla.shape[0]
    nt = seq // tm
    return pl.pallas_call(
        _kvproj_body,
        grid=(n // tm, MLA_HEADS // PROJ_HEADS),
        in_specs=[pl.BlockSpec((tm, KV_LORA), lambda i, h: (i, 1)),
                  pl.BlockSpec((tm, LANES), lambda i, h: (i, (Q_LORA + KV_LORA) // LANES)),
                  pl.BlockSpec((1, KV_LORA), lambda i, h: (0, 0)),
                  pl.BlockSpec((KV_LORA, PROJ_HEADS * (NOPE_DIM + V_DIM)), lambda i, h: (0, h)),
                  pl.BlockSpec((tm, LANES), lambda i, h: (i % nt, 0))],
        out_specs=[pl.BlockSpec((tm, PROJ_HEADS * QK_PAD), lambda i, h: (i, h)),
                   pl.BlockSpec((tm, PROJ_HEADS * V_DIM), lambda i, h: (i, h))],
        out_shape=[jax.ShapeDtypeStruct((n, MLA_HEADS * QK_PAD), BF16),
                   jax.ShapeDtypeStruct((n, MLA_HEADS * V_DIM), BF16)],
        scratch_shapes=[pltpu.VMEM((tm, KV_LORA), BF16), pltpu.VMEM((tm, LANES), BF16)],
        compiler_params=_cparams("parallel", "arbitrary"),
        name="mla_kv_proj",
    )(h_mla, h_mla, kv_norm.reshape(1, KV_LORA), wkv, ccss)


def _attn_body(q_ref, k_ref, v_ref, o_ref, *, nsub):
    ts = q_ref.shape[0] // nsub
    k = k_ref[...]
    v = v_ref[...]

    def scores(j):
        return lax.dot_general(q_ref[j * ts:(j + 1) * ts, :], k, _TB, preferred_element_type=F32)

    def finish(s, j):
        m = jnp.max(s, axis=-1, keepdims=True)
        p = jnp.exp(s - m)
        l = jnp.sum(p, axis=-1, keepdims=True)
        o = jnp.dot(p.astype(BF16), v, preferred_element_type=F32)
        o_ref[j * ts:(j + 1) * ts, :] = (o / l).astype(o_ref.dtype)

    s_prev = scores(0)
    for j in range(1, nsub):
        s_next = scores(j)
        finish(s_prev, j - 1)
        s_prev = s_next
    finish(s_prev, nsub - 1)


def attention(q, k, v, *, batch, seq, tq):
    n = q.shape[0]
    nq = seq // tq
    return pl.pallas_call(
        functools.partial(_attn_body, nsub=tq // ATTN_SUB),
        grid=(batch, MLA_HEADS, nq),
        in_specs=[pl.BlockSpec((tq, QK_PAD), lambda b, h, i: (b * nq + i, h)),
                  pl.BlockSpec((seq, QK_PAD), lambda b, h, i: (b, h)),
                  pl.BlockSpec((seq, V_DIM), lambda b, h, i: (b, h))],
        out_specs=pl.BlockSpec((tq, V_DIM), lambda b, h, i: (b * nq + i, h)),
        out_shape=jax.ShapeDtypeStruct((n, MLA_HEADS * V_DIM), BF16),
        compiler_params=_cparams("parallel", "parallel", "arbitrary"),
        name="mla_attention",
    )(q, k, v)


def _head_sums(x, bd):
    hi = x.astype(BF16)
    lo = (x - hi.astype(F32)).astype(BF16)
    bd2 = jnp.concatenate([bd, bd], axis=0).astype(BF16)
    parts = []
    for gi in range(RW_DIM // LANES):
        ls = slice(gi * LANES, (gi + 1) * LANES)
        parts.append(jnp.dot(jnp.concatenate([hi[:, ls], lo[:, ls]], axis=1), bd2, preferred_element_type=F32))
    return jnp.concatenate(parts, axis=1)


def _sigmoid(x):
    return 1.0 / (1.0 + jnp.exp(-x))


def _rw_prep_body(*refs, tiles_per_seq, tm, has_vfirst):
    (h_ref, hp_ref, hn_ref, mu_ref, vec_ref, w2f_ref, w2b_ref, a2f_ref, a2b_ref, g2_ref, bd_ref) = refs[:11]
    pos = 11
    if has_vfirst:
        v2_ref, vf_ref = refs[pos:pos + 2]
        pos += 2
    r_ref, v_ref, kk_ref, ag_ref, lw_ref, k2_ref, g_ref = refs[pos:pos + 7]

    ti = pl.program_id(0) % tiles_per_seq
    keep_prev = jnp.where(ti == 0, 0.0, 1.0)
    keep_next = jnp.where(ti == tiles_per_seq - 1, 0.0, 1.0)
    row = lax.broadcasted_iota(jnp.int32, (tm, 1), 0)

    def shifted(c0, c1):
        x = h_ref[:, c0:c1]
        prev = jnp.where(row == 0, hp_ref[SUBLANES - 1:SUBLANES, c0:c1] * keep_prev, pltpu.roll(x, 1, axis=0))
        nxt = jnp.where(row == tm - 1, hn_ref[0:1, c0:c1] * keep_next, pltpu.roll(x, tm - 1, axis=0))
        return x * mu_ref[0:1, c0:c1] + (prev + nxt) * mu_ref[1:2, c0:c1]

    vec = vec_ref[...]
    w0f, w0b, a0f, a0b, k_k, k_a, v0 = (vec[i:i + 1, :] for i in range(7))

    r = shifted(0, RW_DIM)
    k = shifted(RW_DIM, 2 * RW_DIM)
    v = shifted(2 * RW_DIM, 3 * RW_DIM)
    lo = shifted(3 * RW_DIM, RW_W)
    dw = jnp.tanh(lo[:, 0:LANES]).astype(BF16)
    da = lo[:, LANES:2 * LANES].astype(BF16)
    dg = _sigmoid(lo[:, 2 * LANES:4 * LANES]).astype(BF16)

    def decay(dw_half, w0, w2_ref):
        z = w0 + jnp.dot(dw_half, w2_ref[...], preferred_element_type=F32)
        return _sigmoid(z) * (-math.exp(-0.5))

    lw_ref[:, 0:RW_DIM] = decay(dw, w0f, w2f_ref)
    lw_ref[:, RW_DIM:] = decay(dw, w0b, w2b_ref)
    af = _sigmoid(a0f + jnp.dot(da, a2f_ref[...], preferred_element_type=F32))
    ab = _sigmoid(a0b + jnp.dot(da, a2b_ref[...], preferred_element_type=F32))
    ag_ref[:, 0:RW_DIM] = af
    ag_ref[:, RW_DIM:] = ab
    g_ref[...] = jnp.dot(dg, g2_ref[...], preferred_element_type=F32)

    if has_vfirst:
        dv = lo[:, 4 * LANES:5 * LANES].astype(BF16)
        mix = _sigmoid(v0 + jnp.dot(dv, v2_ref[...], preferred_element_type=F32))
        v = v + (vf_ref[...] - v) * mix
    v_ref[...] = v
    r_ref[...] = r

    kk = k * k_k
    ss = _head_sums(kk * kk, bd_ref[...])
    kk_ref[...] = kk * lax.rsqrt(jnp.maximum(ss, 1e-24))
    k2_ref[:, 0:RW_DIM] = k * (1.0 + (af - 1.0) * k_a)
    k2_ref[:, RW_DIM:] = k * (1.0 + (ab - 1.0) * k_a)


def rwkv_prep(h_rw, mu, vec, w2f, w2b, a2f, a2b, g2, bd, v2, v_first, *, seq, tm):
    n = h_rw.shape[0]
    tps = seq // tm
    nb8 = n // SUBLANES
    has_vfirst = v_first is not None
    full = lambda a: pl.BlockSpec(a.shape, lambda i: (0,) * a.ndim)
    in_specs = [pl.BlockSpec((tm, RW_W), lambda i: (i, 0)),
                pl.BlockSpec((SUBLANES, RW_W), lambda i: (jnp.maximum(i * (tm // SUBLANES) - 1, 0), 0)),
                pl.BlockSpec((SUBLANES, RW_W), lambda i: (jnp.minimum((i + 1) * (tm // SUBLANES), nb8 - 1), 0)),
                full(mu), full(vec), full(w2f), full(w2b), full(a2f), full(a2b), full(g2), full(bd)]
    args = [h_rw, h_rw, h_rw, mu, vec, w2f, w2b, a2f, a2b, g2, bd]
    if has_vfirst:
        in_specs += [full(v2), pl.BlockSpec((tm, RW_DIM), lambda i: (i, 0))]
        args += [v2, v_first]
    row = lambda w: pl.BlockSpec((tm, w), lambda i: (i, 0))
    widths = [RW_DIM, RW_DIM, RW_DIM, 2 * RW_DIM, 2 * RW_DIM, 2 * RW_DIM, RW_DIM]
    return pl.pallas_call(
        functools.partial(_rw_prep_body, tiles_per_seq=tps, tm=tm, has_vfirst=has_vfirst),
        grid=(n // tm,),
        in_specs=in_specs,
        out_specs=[row(w) for w in widths],
        out_shape=[jax.ShapeDtypeStruct((n, w), F32) for w in widths],
        compiler_params=_cparams("parallel"),
        name="rwkv_prep",
    )(*args)


def _wkv_body(*refs):
    C = WKV_CHUNK
    in_refs, (yf_ref, yb_ref, st_ref) = refs[:12], refs[12:]
    npair = RW_DIM // LANES

    @pl.when(pl.program_id(1) == 0)
    def _():
        st_ref[...] = jnp.zeros_like(st_ref)

    ti = lax.broadcasted_iota(jnp.int32, (C, 3 * C), 0)
    si = lax.broadcasted_iota(jnp.int32, (C, 3 * C), 1) & (C - 1)
    lane = lax.broadcasted_iota(jnp.int32, (1, LANES), 1)
    hm0 = jnp.where(lane < RW_HEAD, 1.0, 0.0)
    hm1 = 1.0 - hm0
    stack = lambda x: jnp.concatenate([x * hm0, x * hm1], axis=0).astype(BF16)
    dup = lambda x: jnp.concatenate([x, x], axis=0).astype(BF16)
    ri = lax.broadcasted_iota(jnp.int32, (2 * C, 2 * C), 0)
    ci = lax.broadcasted_iota(jnp.int32, (2 * C, 2 * C), 1)
    same = (ri & C) == (ci & C)
    dt = (ri & (C - 1)) - (ci & (C - 1))
    eye = jnp.where(ri == ci, 1.0, 0.0)
    valid = (lax.broadcasted_iota(jnp.int32, (2 * C, LANES), 0) & C) == (
        lax.broadcasted_iota(jnp.int32, (2 * C, LANES), 1) & RW_HEAD)
    mm = functools.partial(jnp.dot, preferred_element_type=F32)
    mm_tb = lambda a, b: lax.dot_general(a, b, _TB, preferred_element_type=F32)
    mm_ta = lambda a, b: lax.dot_general(a, b, _TA, preferred_element_type=F32)

    chains = [(d, p) for d in range(2) for p in range(npair)]
    xa, xr, vst, bws, kws, wtots, gms, strict, incl = [], [], [], [], [], [], [], [], []
    for d in range(2):
        r_ref, v_ref, kk_ref, lw_ref, k_ref, ag_ref = in_refs[6 * d:6 * d + 6]
        sgn = 1 - 2 * d
        tri3 = jnp.where((ti - si) * sgn >= 0, 1.0, 0.0).astype(BF16)
        lw_all = lw_ref[...]
        lw_hi = lw_all.astype(BF16)
        rem = lw_all - lw_hi.astype(F32)
        lw_mid = rem.astype(BF16)
        lw_lo = (rem - lw_mid.astype(F32)).astype(BF16)
        cum_all = jnp.dot(tri3, jnp.concatenate([lw_hi, lw_mid, lw_lo], axis=0), preferred_element_type=F32)
        tot_all = jnp.sum(lw_all, axis=0, keepdims=True)
        strict_d = same & (dt * sgn > 0)
        incl_d = same & (dt * sgn >= 0)
        for p in range(npair):
            ls = slice(p * LANES, (p + 1) * LANES)
            r, v, kk, k, ag = (x[:, ls] for x in (r_ref, v_ref, kk_ref, k_ref, ag_ref))
            lw, cum, tot = lw_all[:, ls], cum_all[:, ls], tot_all[:, ls]
            ei = jnp.exp(-cum)
            ew = jnp.exp(tot - cum)
            b = kk * ag
            xa.append(stack(-kk * jnp.exp(cum - lw)))
            xr.append(stack(r * jnp.exp(cum)))
            vst.append(stack(v))
            bws.append(stack(b * ew))
            kws.append(stack(k * ew))
            wtots.append(jnp.exp(tot))
            strict.append(strict_d)
            incl.append(incl_d)
            x2 = jnp.concatenate([xa[-1], xr[-1]], axis=0)
            y2 = jnp.concatenate([dup(b * ei), dup(k * ei)], axis=0)
            gms.append(mm_tb(x2, y2))
    nch = range(len(chains))
    l_ab = [jnp.where(strict[i], gms[i][0:2 * C, 0:2 * C], 0.0) for i in nch]
    l_ak = [jnp.where(strict[i], gms[i][0:2 * C, 2 * C:], 0.0).astype(BF16) for i in nch]
    m_rb = [jnp.where(incl[i], gms[i][2 * C:, 0:2 * C], 0.0).astype(BF16) for i in nch]
    m_rk = [jnp.where(incl[i], gms[i][2 * C:, 2 * C:], 0.0).astype(BF16) for i in nch]

    pinv = [eye + l for l in l_ab]
    lp = l_ab
    for _ in range(int(math.log2(C)) - 1):
        lpb = [l.astype(BF16) for l in lp]
        lp = [mm(l, l) for l in lpb]
        pinv = [pi + mm(pi.astype(BF16), l.astype(BF16)) for pi, l in zip(pinv, lp)]

    st = [st_ref[d, p] for d, p in chains]
    s_kv = [s.T.astype(BF16) for s in st]
    pre = [mm(jnp.concatenate([xa[i], l_ak[i]], axis=1), jnp.concatenate([s_kv[i], vst[i]], axis=0))
           for i in nch]
    ustb = [jnp.where(valid, mm(pinv[i].astype(BF16), pre[i].astype(BF16)), 0.0).astype(BF16) for i in nch]
    for i, (d, p) in enumerate(chains):
        yst = mm(jnp.concatenate([xr[i], m_rb[i], m_rk[i]], axis=1),
                 jnp.concatenate([s_kv[i], ustb[i], vst[i]], axis=0))
        yst = jnp.where(valid, yst, 0.0)
        (yf_ref, yb_ref)[d][:, p * LANES:(p + 1) * LANES] = yst[0:C] + yst[C:]
    for i, (d, p) in enumerate(chains):
        st_ref[d, p] = st[i] * wtots[i] + mm_ta(jnp.concatenate([ustb[i], vst[i]], axis=0),
                                                jnp.concatenate([bws[i], kws[i]], axis=0))


def wkv(r, v, kk, lw2, k2, ag2, *, batch, seq):
    n = r.shape[0]
    C = WKV_CHUNK
    nc = seq // C
    fwd = lambda col: pl.BlockSpec((C, RW_DIM), lambda b, c: (b * nc + c, col))
    bwd = lambda col: pl.BlockSpec((C, RW_DIM), lambda b, c: (b * nc + nc - 1 - c, col))
    return pl.pallas_call(
        _wkv_body,
        grid=(batch, nc),
        in_specs=[fwd(0), fwd(0), fwd(0), fwd(0), fwd(0), fwd(0),
                  bwd(0), bwd(0), bwd(0), bwd(1), bwd(1), bwd(1)],
        out_specs=[fwd(0), bwd(0)],
        out_shape=[jax.ShapeDtypeStruct((n, RW_DIM), F32)] * 2,
        scratch_shapes=[pltpu.VMEM((2, RW_DIM // LANES, LANES, LANES), F32)],
        compiler_params=_cparams("parallel", "arbitrary"),
        name="wkv_scan",
    )(r, v, kk, lw2, k2, ag2, r, v, kk, lw2, k2, ag2)


def _rw_post_body(yf_ref, yb_ref, r_ref, k2_ref, v_ref, g_ref, vec_ref, bd_ref, o_ref):
    vec = vec_ref[...]
    lnx_w, lnx_b, r_k = (vec[i:i + 1, :] for i in range(3))
    bd = bd_ref[...]
    y = yf_ref[...] + yb_ref[...]
    inv = 1.0 / RW_HEAD
    mean = _head_sums(y, bd) * inv
    d = y - mean
    var = _head_sums(d * d, bd) * inv
    yn = d * lax.rsqrt(var + LNX_EPS) * lnx_w + lnx_b
    ksum = k2_ref[:, 0:RW_DIM] + k2_ref[:, RW_DIM:]
    bonus = _head_sums(r_ref[...] * ksum * r_k, bd) * v_ref[...]
    o_ref[...] = ((yn + bonus) * g_ref[...]).astype(o_ref.dtype)


def rwkv_post(yf, yb, r, k2, v, g, vec, bd, *, tm):
    n = r.shape[0]
    row = lambda w: pl.BlockSpec((tm, w), lambda i: (i, 0))
    full = lambda a: pl.BlockSpec(a.shape, lambda i: (0,) * a.ndim)
    return pl.pallas_call(
        _rw_post_body,
        grid=(n // tm,),
        in_specs=[row(RW_DIM), row(RW_DIM), row(RW_DIM), row(2 * RW_DIM), row(RW_DIM), row(RW_DIM),
                  full(vec), full(bd)],
        out_specs=row(RW_DIM),
        out_shape=jax.ShapeDtypeStruct((n, RW_DIM), BF16),
        compiler_params=_cparams("parallel"),
        name="rwkv_post",
    )(yf, yb, r, k2, v, g, vec, bd)


def _merge_body(o_ref, yg_ref, ga_ref, gr_ref, wa_ref, wr_ref, m_ref):
    attn = jnp.dot(o_ref[...], wa_ref[...], preferred_element_type=F32)
    rw = jnp.dot(yg_ref[...], wr_ref[...], preferred_element_type=F32)
    m_ref[...] = (_sigmoid(ga_ref[...].astype(F32)) * attn + _sigmoid(gr_ref[...].astype(F32)) * rw).astype(m_ref.dtype)


def merge_branches(o, yg, gates, wa, wr, *, tm, tn):
    n = o.shape[0]
    nj = D_MODEL // tn
    return pl.pallas_call(
        _merge_body,
        grid=(n // tm, nj),
        in_specs=[pl.BlockSpec((tm, MLA_HEADS * V_DIM), lambda i, j: (i, 0)),
                  pl.BlockSpec((tm, RW_DIM), lambda i, j: (i, 0)),
                  pl.BlockSpec((tm, tn), lambda i, j: (i, j)),
                  pl.BlockSpec((tm, tn), lambda i, j: (i, nj + j)),
                  pl.BlockSpec((MLA_HEADS * V_DIM, tn), lambda i, j: (0, j)),
                  pl.BlockSpec((RW_DIM, tn), lambda i, j: (0, j))],
        out_specs=pl.BlockSpec((tm, tn), lambda i, j: (i, j)),
        out_shape=jax.ShapeDtypeStruct((n, D_MODEL), BF16),
        compiler_params=_cparams("parallel", "parallel"),
        name="merge_branches",
    )(o, yg, gates, gates, wa, wr)


def _mm_res_norm_body(a_ref, w_ref, x_ref, g_ref, o_ref, *n_refs):
    out = x_ref[...] + jnp.dot(a_ref[...], w_ref[...], preferred_element_type=F32)
    o_ref[...] = out
    xhat = out * lax.rsqrt(jnp.mean(out * out, axis=-1, keepdims=True) + EPS)
    for row, n_ref in enumerate(n_refs):
        n_ref[...] = (xhat * g_ref[row:row + 1, :]).astype(n_ref.dtype)


def matmul_residual_norm(a, w, x, gains, out_dtypes, *, tm):
    n, k = a.shape
    nc = w.shape[1]
    row = lambda width: pl.BlockSpec((tm, width), lambda i: (i, 0))
    return pl.pallas_call(
        _mm_res_norm_body,
        grid=(n // tm,),
        in_specs=[row(k),
                  pl.BlockSpec((k, nc), lambda i: (0, 0), pipeline_mode=pl.Buffered(1)),
                  row(nc),
                  pl.BlockSpec(gains.shape, lambda i: (0, 0))],
        out_specs=[row(nc)] * (1 + len(out_dtypes)),
        out_shape=[jax.ShapeDtypeStruct((n, nc), F32)] + [jax.ShapeDtypeStruct((n, nc), dt) for dt in out_dtypes],
        compiler_params=_cparams("parallel"),
        name="matmul_residual_norm",
    )(a, w, x, gains)


def _top16_rows(ss, idx):
    big = 3.0e38
    vals = [[] for _ in ss]
    poss = [[] for _ in ss]
    for _ in range(PEER_TOPK):
        ms = [jnp.max(s, axis=0, keepdims=True) for s in ss]
        ps = [jnp.min(jnp.where(s == m, idx, big), axis=0, keepdims=True) for s, m in zip(ss, ms)]
        ss = [jnp.where(idx == p, -jnp.inf, s) for s, p in zip(ss, ps)]
        for i, (m, p) in enumerate(zip(ms, ps)):
            vals[i].append(m)
            poss[i].append(p)
    return [(jnp.concatenate(v, axis=0), jnp.concatenate(p, axis=0)) for v, p in zip(vals, poss)]


def _candidates(sv0, sv1):
    tok = sv0.shape[1]
    io = lambda rows: lax.broadcasted_iota(jnp.int32, (rows, tok), 0).astype(F32)
    half = PEER_TOPK // 2
    parts = [sv0[0:1, :] + sv1]
    idxs = [io(PEER_TOPK)]
    for a in range(1, half):
        parts.append(sv0[a:a + 1, :] + sv1[0:half, :])
        idxs.append(io(half) + float(a * PEER_TOPK))
    parts.append(sv0[half:, :] + sv1[0:1, :])
    idxs.append((io(half) + float(half)) * float(PEER_TOPK))
    return jnp.concatenate(parts, axis=0), jnp.concatenate(idxs, axis=0)


def _pick_rows(table, sel):
    out = jnp.zeros(sel.shape, table.dtype)
    for a in range(PEER_TOPK):
        out = jnp.where(sel == a, table[a:a + 1, :], out)
    return out


def _peer_topk_body(q_ref, sk_ref, i1_ref, i2_ref, gt_ref):
    i1s, i2s, gts = [], [], []
    key_idx = lax.broadcasted_iota(jnp.int32, (N_KEYS, q_ref.shape[0]), 0).astype(F32)
    for h in range(PEER_HEADS):
        ss = []
        for p in range(2):
            c0 = (2 * h + p) * PEER_HALF
            qb = q_ref[:, c0:c0 + PEER_HALF].astype(BF16)
            ss.append(lax.dot_general(sk_ref[p], qb, _TB, preferred_element_type=F32))
        (sv0, si0), (sv1, si1) = _top16_rows(ss, key_idx)
        cand, cand_idx = _candidates(sv0, sv1)
        ((top_s, top_j),) = _top16_rows([cand], cand_idx)
        ja = jnp.floor(top_j * (1.0 / PEER_TOPK))
        i1s.append(_pick_rows(si0, ja))
        i2s.append(_pick_rows(si1, top_j - ja * PEER_TOPK))
        e = jnp.exp(top_s - top_s[0:1, :])
        gts.append(e / jnp.sum(e, axis=0, keepdims=True))
    i1_ref[...] = jnp.concatenate(i1s, axis=0).T
    i2_ref[...] = jnp.concatenate(i2s, axis=0).T
    gt_ref[...] = jnp.concatenate(gts, axis=0).T


def peer_topk(q, sub_keys):
    n = q.shape[0]
    tmk = LANES
    out = pl.BlockSpec((tmk, NSLOT), lambda i: (i, 0))
    return pl.pallas_call(
        _peer_topk_body,
        grid=(n // tmk,),
        in_specs=[pl.BlockSpec((tmk, 2 * PEER_HALF * PEER_HEADS), lambda i: (i, 0)),
                  pl.BlockSpec(sub_keys.shape, lambda i: (0, 0, 0))],
        out_specs=[out, out, out],
        out_shape=[jax.ShapeDtypeStruct((n, NSLOT), F32)] * 3,
        compiler_params=_cparams("parallel"),
        name="peer_topk",
    )(q, sub_keys)


def _gelu_exact(x):
    return 0.5 * x * (1.0 + lax.erf(x * (1.0 / math.sqrt(2.0))))


_HI16 = 0xFFFF0000
G_UNROLL = SUBLANES
E_SUB = 2 * N_KEYS
PEER_TE = 2 * E_SUB


def _bf16_bits(x):
    return pltpu.bitcast(x, jnp.uint32) + jnp.uint32(0x8000)


def _peer_dense_body(xn_ref, i1_ref, i2_ref, gt_ref, ut0_ref, utb_ref, utn_ref, v_ref, x_ref, lnf_ref, scal_ref,
                     o_ref, g_ref, h_ref, *, tm, te, final_norm):
    e = pl.program_id(1)
    half = tm // 2
    mm_tb = lambda a, b: lax.dot_general(a, b, _TB, preferred_element_type=F32)
    hsc, csc, osc = (scal_ref[r:r + 1, 0:1] for r in range(3))
    nsub = te // E_SUB

    def activations(ut_ref):
        xn = xn_ref[...]
        return [jnp.dot(xn, ut_ref[:, j * E_SUB:(j + 1) * E_SUB], preferred_element_type=F32) * hsc
                for j in range(nsub)]

    def accumulate(block, hs):
        coefs = []
        for j in range(nsub):
            row0 = block * (te // N_KEYS) + j * (E_SUB // N_KEYS)
            words = [g_ref[pl.ds(row0 + r, half, stride=G_PITCH), :] for r in range(E_SUB // N_KEYS)]
            top = jnp.concatenate([pltpu.bitcast(w & jnp.uint32(_HI16), F32) for w in words], axis=1)
            bot = jnp.concatenate([pltpu.bitcast(w << 16, F32) for w in words], axis=1)
            gate = jnp.concatenate([top, bot], axis=0)
            coef = gate * _gelu_exact(hs[j]) * csc
            coefs.append(jnp.clip(coef, -FP8_MAX, FP8_MAX).astype(FP8))
        return coefs

    @pl.when(e == 0)
    def _():
        for j, h in enumerate(activations(ut0_ref)):
            h_ref[:, j * E_SUB:(j + 1) * E_SUB] = h
        o_ref[...] = x_ref[...] * osc
        sub = lax.broadcasted_iota(jnp.int32, (N_KEYS, NSLOT), 0).astype(F32).astype(BF16)
        one = jnp.ones((), BF16)
        zero = jnp.zeros((), BF16)

        def build(it, carry):
            base = pl.multiple_of(it * G_UNROLL, SUBLANES)
            tiles = [(ref[pl.ds(base, G_UNROLL), :], ref[pl.ds(half + base, G_UNROLL), :])
                     for ref in (i1_ref, i2_ref, gt_ref)]
            a_ts, b_ts = [], []
            for j in range(G_UNROLL):
                for t in range(2):
                    i1, i2, gt = (tiles[q][t][j:j + 1, :].astype(BF16) for q in range(3))
                    a_ts.append(jnp.where(sub == i1, one, zero))
                    b_ts.append(jnp.where(sub == i2, gt, zero))
            gs = [mm_tb(a, b) for a, b in zip(a_ts, b_ts)]
            for j in range(G_UNROLL):
                word = (_bf16_bits(gs[2 * j]) & jnp.uint32(_HI16)) | (_bf16_bits(gs[2 * j + 1]) >> 16)
                g_ref[pl.ds(pl.multiple_of((base + j) * G_PITCH, SUBLANES), N_KEYS), :] = word
            return carry

        lax.fori_loop(0, half // G_UNROLL, build, 0)

    hs_b = activations(utb_ref)
    coef_a = accumulate(2 * e, [h_ref[:, j * E_SUB:(j + 1) * E_SUB] for j in range(nsub)])
    for j, h in enumerate(activations(utn_ref)):
        h_ref[:, j * E_SUB:(j + 1) * E_SUB] = h
    coef_b = accumulate(2 * e + 1, hs_b)
    o_ref[...] += jnp.dot(jnp.concatenate(coef_a + coef_b, axis=1), v_ref[...], preferred_element_type=F32)

    @pl.when(e == pl.num_programs(1) - 1)
    def _():
        out = o_ref[...] * scal_ref[3:4, 0:1]
        if final_norm:
            out = _rms_rows(out, lnf_ref[...], EPS)
        o_ref[...] = out


def peer_dense(xn, i1, i2, gt, ut, v, x, ln_f, scal, *, tm, te, final_norm):
    n = xn.shape[0]
    ne = v.shape[0] // te
    assert tm % (2 * G_UNROLL) == 0 and te % E_SUB == 0 and ne % 2 == 0 and ut.shape == (ne, D_MODEL, te)
    row = lambda w: pl.BlockSpec((tm, w), lambda i, e: (i, 0))
    return pl.pallas_call(
        functools.partial(_peer_dense_body, tm=tm, te=te, final_norm=final_norm),
        grid=(n // tm, ne // 2),
        in_specs=[row(D_MODEL), row(NSLOT), row(NSLOT), row(NSLOT),
                  pl.BlockSpec((None, D_MODEL, te), lambda i, e: (0, 0, 0)),
                  pl.BlockSpec((None, D_MODEL, te), lambda i, e: (2 * e + 1, 0, 0)),
                  pl.BlockSpec((None, D_MODEL, te), lambda i, e: (jnp.minimum(2 * e + 2, ne - 1), 0, 0)),
                  pl.BlockSpec((2 * te, D_MODEL), lambda i, e: (e, 0)),
                  row(D_MODEL),
                  pl.BlockSpec((1, D_MODEL), lambda i, e: (0, 0)),
                  pl.BlockSpec(scal.shape, lambda i, e: (0, 0))],
        out_specs=row(D_MODEL),
        out_shape=jax.ShapeDtypeStruct((n, D_MODEL), F32),
        scratch_shapes=[pltpu.VMEM((tm // 2 * G_PITCH, N_KEYS), jnp.uint32), pltpu.VMEM((tm, te), F32)],
        compiler_params=_cparams("parallel", "arbitrary"),
        name="peer_dense",
    )(xn, i1, i2, gt, ut, ut, ut, v, x, ln_f.reshape(1, D_MODEL), scal)


def _pad_rows(a, h):
    return jnp.pad(a, ((0, h - a.shape[0]),) + ((0, 0),) * (a.ndim - 1))


def _swap_halves(a):
    half = a.shape[-1] // 2
    return jnp.concatenate([a[..., half:], a[..., :half]], axis=-1)


def _layer_weights(i, p):
    first = i == 0
    w_in = p["w_in_first"] if first else p["w_in_rest"][i - 1]
    mu = p["mu_first"] if first else p["mu_rest"][i - 1]
    mla_cols = Q_LORA + KV_LORA + ROPE_DIM
    gate0 = mla_cols
    rw0 = mla_cols + 2 * D_MODEL
    k_rope_w = w_in[:, Q_LORA + KV_LORA:mla_cols]
    w_mla = jnp.concatenate([w_in[:, :mla_cols], _swap_halves(k_rope_w)], axis=1)
    w_gate = w_in[:, gate0:rw0]

    def rw_layout(a):
        c = 3 * RW_DIM
        dwf, dwb = a[..., c:c + DECAY_LORA], a[..., c + DECAY_LORA:c + 2 * DECAY_LORA]
        c += 2 * DECAY_LORA
        daf, dab = a[..., c:c + AAA_LORA], a[..., c + AAA_LORA:c + 2 * AAA_LORA]
        c += 2 * AAA_LORA
        dg = a[..., c:c + GATE_LORA]
        c += GATE_LORA
        dv = a[..., c:]
        z = lambda w: jnp.zeros(a.shape[:-1] + (w,), a.dtype)
        return jnp.concatenate([a[..., :3 * RW_DIM], dwf, dwb, daf, dab, dg, z(2 * LANES - GATE_LORA),
                                dv, z(2 * LANES - dv.shape[-1])], axis=-1)

    w_rw = rw_layout(w_in[:, rw0:])
    mu_l = rw_layout(mu[None, :])
    mu_l = jnp.concatenate([1.0 - mu_l, 0.5 * mu_l], axis=0)
    zeros64 = jnp.zeros((DECAY_LORA, RW_DIM), F32)
    wq = (p["w_uq"][i] * (NOPE_DIM + ROPE_DIM) ** -0.5).reshape(Q_LORA, MLA_HEADS, NOPE_DIM + ROPE_DIM)
    wq = jnp.concatenate([wq, _swap_halves(wq[..., NOPE_DIM:])], axis=-1).reshape(Q_LORA, MLA_HEADS * QK_PAD)
    vec = jnp.stack([p["w0_f"][i], p["w0_b"][i], p["a0_f"][i], p["a0_b"][i], p["k_k"][i], p["k_a"][i],
                     p["v0_rest"][i - 1] if not first else jnp.zeros((RW_DIM,), F32),
                     jnp.zeros((RW_DIM,), F32)])
    lw = {
        "ln1": p["ln1"][i], "w_mla": w_mla.astype(BF16), "w_gate": w_gate.astype(BF16),
        "w_rw": w_rw.astype(BF16),
        "mu": mu_l, "q_norm": p["q_norm"][i], "wq": wq.astype(BF16), "kv_norm": p["kv_norm"][i],
        "wkv": p["w_ukv"][i].astype(BF16), "w_o_attn": p["w_o_attn"][i].astype(BF16),
        "vec": vec,
        "w2f": jnp.concatenate([p["w2_f"][i], zeros64]).astype(BF16),
        "w2b": jnp.concatenate([zeros64, p["w2_b"][i]]).astype(BF16),
        "a2f": jnp.concatenate([p["a2_f"][i], zeros64]).astype(BF16),
        "a2b": jnp.concatenate([zeros64, p["a2_b"][i]]).astype(BF16),
        "g2": _pad_rows(p["g2"][i], 2 * LANES).astype(BF16),
        "v2": None if first else _pad_rows(p["v2_rest"][i - 1], LANES).astype(BF16),
        "post_vec": jnp.stack([p["lnx_w"][i], p["lnx_b"][i], p["r_k"][i].reshape(RW_DIM)]
                              + [jnp.zeros((RW_DIM,), F32)] * 5),
        "w_o_rwkv": p["w_o_rwkv"][i].astype(BF16), "w_out": p["w_out"][i].astype(BF16),
        "ln2": p["ln2"][i], "w_pq": p["w_pq"][i].astype(BF16), "sub_keys": p["sub_keys"][i].astype(BF16),
    }
    pow2_below = lambda v: jnp.exp2(jnp.floor(jnp.log2(v)))
    tiny = 1e-30
    u, v, g2n = p["peer_u"][i], p["peer_v"][i], p["ln2"][i]
    xmax = math.sqrt(D_MODEL) * jnp.maximum(jnp.max(jnp.abs(g2n)), tiny)
    sx = pow2_below(FP8_MAX / xmax)
    su = pow2_below(FP8_MAX / jnp.maximum(jnp.max(jnp.abs(u)), tiny))
    sv = pow2_below(FP8_MAX / jnp.maximum(jnp.max(jnp.abs(v)), tiny))
    hmax = xmax * jnp.sqrt(jnp.max(jnp.sum(u * u, axis=1))) * (1.0 + 2.0 ** -4) ** 2
    sc = pow2_below(FP8_MAX / (PEER_HEADS * jnp.maximum(hmax, tiny)))
    lw["ln2_rows"] = jnp.stack([g2n, g2n * sx])
    lw["peer_ut"] = (u * su).astype(FP8).reshape(-1, PEER_TE, D_MODEL).transpose(0, 2, 1)
    lw["peer_v"] = (v * sv).astype(FP8)
    scal = jnp.stack([1.0 / (sx * su), sc, sc * sv, 1.0 / (sc * sv)] + [jnp.zeros((), F32)] * 4)
    lw["peer_scal"] = jnp.broadcast_to(scal[:, None], (SUBLANES, LANES)).astype(F32)
    return lw


def _rope_table(seq):
    inv = 1.0 / (ROPE_THETA ** (jnp.arange(0, ROPE_DIM, 2, dtype=F32) / ROPE_DIM))
    ang = jnp.arange(seq, dtype=F32)[:, None] * inv[None, :]
    c, s = jnp.cos(ang), jnp.sin(ang)
    return jnp.concatenate([c, c, -s, s], axis=1)


def _tiles(seq):
    return dict(tm=min(512, seq), tmm=min(1024, seq), tq=min(16 * ATTN_SUB, seq), tprep=min(256, seq),
                tpeer=min(512, seq))


def _trunk(x, layers, ln_f, bd):
    batch, seq, _ = x.shape
    n = batch * seq
    t = _tiles(seq)
    tm, tmm = t["tm"], t["tmm"]
    ccss = _rope_table(seq)
    x = x.reshape(n, D_MODEL)
    v_first = None
    for li, lw in enumerate(layers):
        (xn1,) = rms_norm(x, lw["ln1"].reshape(1, D_MODEL), (BF16,), tm=tm)
        h_mla = matmul(xn1, lw["w_mla"], tm=tmm, tn=MLA_W, out_dtype=BF16)
        gates = matmul(xn1, lw["w_gate"], tm=tmm, tn=1024, out_dtype=BF16)
        h_rw = matmul(xn1, lw["w_rw"], tm=tmm, tn=1280, out_dtype=F32)
        q = mla_q_proj(h_mla, lw["q_norm"], lw["wq"], ccss, seq=seq, tm=tm)
        k, v = mla_kv_proj(h_mla, lw["kv_norm"], lw["wkv"], ccss, seq=seq, tm=tm)
        o = attention(q, k, v, batch=batch, seq=seq, tq=t["tq"])
        r, vv, kk, ag2, lw2, k2, g = rwkv_prep(h_rw, lw["mu"], lw["vec"], lw["w2f"], lw["w2b"], lw["a2f"],
                                              lw["a2b"], lw["g2"], bd, lw["v2"], v_first,
                                              seq=seq, tm=t["tprep"])
        if v_first is None:
            v_first = vv
        yf, yb = wkv(r, vv, kk, lw2, k2, ag2, batch=batch, seq=seq)
        yg = rwkv_post(yf, yb, r, k2, vv, g, lw["post_vec"], bd, tm=t["tprep"])
        m = merge_branches(o, yg, gates, lw["w_o_attn"], lw["w_o_rwkv"], tm=tmm, tn=512)
        x, xn, xn8 = matmul_residual_norm(m, lw["w_out"], x, lw["ln2_rows"], (BF16, FP8), tm=tm)
        qp = matmul(xn, lw["w_pq"], tm=tmm, tn=1024, out_dtype=F32)
        i1, i2, gt = peer_topk(qp, lw["sub_keys"])
        x = peer_dense(xn8, i1, i2, gt, lw["peer_ut"], lw["peer_v"], x, ln_f, lw["peer_scal"],
                       tm=t["tpeer"], te=PEER_TE,
                       final_norm=li == len(layers) - 1)
    return x.reshape(batch, seq, D_MODEL)


def kernel(x_prompt, x_sample, ln1, w_in_first, mu_first, w_in_rest, mu_rest, q_norm, w_uq, kv_norm, w_ukv, w_o_attn, w0_f, w2_f, w0_b, w2_b, a0_f, a2_f, a0_b, a2_b, g2, k_k, k_a, r_k, lnx_w, lnx_b, v0_rest, v2_rest, w_o_rwkv, w_out, ln2, w_pq, sub_keys, peer_u, peer_v, ln_f):
    p = dict(ln1=ln1, w_in_first=w_in_first, mu_first=mu_first, w_in_rest=w_in_rest, mu_rest=mu_rest,
             q_norm=q_norm, w_uq=w_uq, kv_norm=kv_norm, w_ukv=w_ukv, w_o_attn=w_o_attn, w0_f=w0_f, w2_f=w2_f,
             w0_b=w0_b, w2_b=w2_b, a0_f=a0_f, a2_f=a2_f, a0_b=a0_b, a2_b=a2_b, g2=g2, k_k=k_k, k_a=k_a, r_k=r_k,
             lnx_w=lnx_w, lnx_b=lnx_b, v0_rest=v0_rest, v2_rest=v2_rest, w_o_rwkv=w_o_rwkv, w_out=w_out,
             ln2=ln2, w_pq=w_pq, sub_keys=sub_keys, peer_u=peer_u, peer_v=peer_v)
    depth = ln1.shape[0]
    layers = [_layer_weights(i, p) for i in range(depth)]
    lane = jnp.arange(LANES)
    bd = (lane[:, None] // RW_HEAD == lane[None, :] // RW_HEAD).astype(F32)
    return (_trunk(x_prompt, layers, ln_f, bd), _trunk(x_sample, layers, ln_f, bd))
```

```python
import functools
import math

import jax
import jax.numpy as jnp
from jax import lax
from jax.experimental import pallas as pl
from jax.experimental.pallas import tpu as pltpu

F32 = jnp.float32
BF16 = jnp.bfloat16
FP8 = jnp.float8_e4m3fn
FP8_MAX = 448.0

LANES = 128
SUBLANES = 8
VMEM_LIMIT_BYTES = 56 * 1024 * 1024

D_MODEL = 2048
MLA_HEADS = 16
Q_LORA = 512
KV_LORA = 512
NOPE_DIM = 128
ROPE_DIM = 64
V_DIM = 128
ROPE_THETA = 10000.0
RW_HEAD = 64
RW_HEADS = 16
RW_DIM = RW_HEADS * RW_HEAD
DECAY_LORA = 64
AAA_LORA = 64
GATE_LORA = 160
LNX_EPS = 64e-5
PEER_HEADS = 8
N_KEYS = 128
PEER_HALF = 128
PEER_TOPK = 16
EPS = 1e-6

QK_PAD = 2 * LANES
MLA_W = Q_LORA + KV_LORA + LANES
LORA_W = 2 * DECAY_LORA + 2 * AAA_LORA + 2 * LANES + 2 * LANES
RW_W = 3 * RW_DIM + LORA_W
PROJ_HEADS = 8
ATTN_SUB = 256
WKV_CHUNK = 64
G_PITCH = N_KEYS + SUBLANES
NSLOT = PEER_HEADS * PEER_TOPK

_TB = (((1,), (1,)), ((), ()))
_TA = (((0,), (0,)), ((), ()))


def _cparams(*sem):
    return pltpu.CompilerParams(dimension_semantics=sem, vmem_limit_bytes=VMEM_LIMIT_BYTES)


def _rms_rows(x, g, eps):
    ms = jnp.mean(x * x, axis=-1, keepdims=True)
    return x * lax.rsqrt(ms + eps) * g


def _rms_norm_body(x_ref, g_ref, *o_refs):
    x = x_ref[...].astype(F32)
    xhat = x * lax.rsqrt(jnp.mean(x * x, axis=-1, keepdims=True) + EPS)
    for row, o_ref in enumerate(o_refs):
        o_ref[...] = (xhat * g_ref[row:row + 1, :]).astype(o_ref.dtype)


def rms_norm(x, gains, out_dtypes, *, tm):
    n, k = x.shape
    spec = pl.BlockSpec((tm, k), lambda i: (i, 0))
    return pl.pallas_call(
        _rms_norm_body,
        grid=(n // tm,),
        in_specs=[spec, pl.BlockSpec(gains.shape, lambda i: (0, 0))],
        out_specs=[spec] * len(out_dtypes),
        out_shape=[jax.ShapeDtypeStruct((n, k), dt) for dt in out_dtypes],
        compiler_params=_cparams("parallel"),
        name="rms_norm",
    )(x, gains)


def _mm_body(a_ref, w_ref, o_ref):
    o_ref[...] = jnp.dot(a_ref[...], w_ref[...], preferred_element_type=F32).astype(o_ref.dtype)


def matmul(a, w, *, tm, tn, out_dtype):
    n, k = a.shape
    nc = w.shape[1]
    tn = min(tn, nc)
    assert n % tm == 0 and nc % tn == 0 and w.shape[0] == k
    return pl.pallas_call(
        _mm_body,
        grid=(n // tm, nc // tn),
        in_specs=[pl.BlockSpec((tm, k), lambda i, j: (i, 0)),
                  pl.BlockSpec((k, tn), lambda i, j: (0, j))],
        out_specs=pl.BlockSpec((tm, tn), lambda i, j: (i, j)),
        out_shape=jax.ShapeDtypeStruct((n, nc), out_dtype),
        compiler_params=_cparams("parallel", "parallel"),
        name="matmul",
    )(a, w)


def _rope_lanes(y2, ccss):
    w = y2 * ccss
    return w + pltpu.roll(w, ROPE_DIM, axis=1)


def _qproj_body(c_ref, g_ref, w_ref, t_ref, q_ref, cn_ref):
    @pl.when(pl.program_id(1) == 0)
    def _():
        cn_ref[...] = _rms_rows(c_ref[...].astype(F32), g_ref[...], EPS).astype(BF16)

    y = jnp.dot(cn_ref[...], w_ref[...], preferred_element_type=F32)
    tab = t_ref[...]
    lane = lax.broadcasted_iota(jnp.int32, tab.shape, 1)
    for hh in range(PROJ_HEADS):
        c0 = hh * QK_PAD
        q_ref[:, c0:c0 + LANES] = y[:, c0:c0 + LANES].astype(q_ref.dtype)
        rot = _rope_lanes(y[:, c0 + LANES:c0 + QK_PAD], tab)
        q_ref[:, c0 + LANES:c0 + QK_PAD] = jnp.where(lane < ROPE_DIM, rot, 0.0).astype(q_ref.dtype)


def mla_q_proj(h_mla, q_norm, wq, ccss, *, seq, tm):
    n = h_mla.shape[0]
    nt = seq // tm
    return pl.pallas_call(
        _qproj_body,
        grid=(n // tm, MLA_HEADS // PROJ_HEADS),
        in_specs=[pl.BlockSpec((tm, Q_LORA), lambda i, h: (i, 0)),
                  pl.BlockSpec((1, Q_LORA), lambda i, h: (0, 0)),
                  pl.BlockSpec((Q_LORA, PROJ_HEADS * QK_PAD), lambda i, h: (0, h)),
                  pl.BlockSpec((tm, LANES), lambda i, h: (i % nt, 0))],
        out_specs=pl.BlockSpec((tm, PROJ_HEADS * QK_PAD), lambda i, h: (i, h)),
        out_shape=jax.ShapeDtypeStruct((n, MLA_HEADS * QK_PAD), BF16),
        scratch_shapes=[pltpu.VMEM((tm, Q_LORA), BF16)],
        compiler_params=_cparams("parallel", "arbitrary"),
        name="mla_q_proj",
    )(h_mla, q_norm.reshape(1, Q_LORA), wq, ccss)


def _kvproj_body(c_ref, kr_ref, g_ref, w_ref, t_ref, k_ref, v_ref, cn_ref, krr_ref):
    @pl.when(pl.program_id(1) == 0)
    def _():
        cn_ref[...] = _rms_rows(c_ref[...].astype(F32), g_ref[...], EPS).astype(BF16)
        krr_ref[...] = _rope_lanes(kr_ref[...].astype(F32), t_ref[...]).astype(BF16)

    y = jnp.dot(cn_ref[...], w_ref[...], preferred_element_type=F32)
    for hh in range(PROJ_HEADS):
        c0 = hh * (NOPE_DIM + V_DIM)
        k_ref[:, hh * QK_PAD:hh * QK_PAD + LANES] = y[:, c0:c0 + NOPE_DIM].astype(k_ref.dtype)
        k_ref[:, hh * QK_PAD + LANES:(hh + 1) * QK_PAD] = krr_ref[...]
        v_ref[:, hh * V_DIM:(hh + 1) * V_DIM] = y[:, c0 + NOPE_DIM:c0 + NOPE_DIM + V_DIM].astype(v_ref.dtype)


def mla_kv_proj(h_mla, kv_norm, wkv, ccss, *, seq, tm):
    n = h_mla.shape[0]
    nt = seq // tm
    return pl.pallas_call(
        _kvproj_body,
        grid=(n // tm, MLA_HEADS // PROJ_HEADS),
        in_specs=[pl.BlockSpec((tm, KV_LORA), lambda i, h: (i, 1)),
                  pl.BlockSpec((tm, LANES), lambda i, h: (i, (Q_LORA + KV_LORA) // LANES)),
                  pl.BlockSpec((1, KV_LORA), lambda i, h: (0, 0)),
                  pl.BlockSpec((KV_LORA, PROJ_HEADS * (NOPE_DIM + V_DIM)), lambda i, h: (0, h)),
                  pl.BlockSpec((tm, LANES), lambda i, h: (i % nt, 0))],
        out_specs=[pl.BlockSpec((tm, PROJ_HEADS * QK_PAD), lambda i, h: (i, h)),
                   pl.BlockSpec((tm, PROJ_HEADS * V_DIM), lambda i, h: (i, h))],
        out_shape=[jax.ShapeDtypeStruct((n, MLA_HEADS * QK_PAD), BF16),
                   jax.ShapeDtypeStruct((n, MLA_HEADS * V_DIM), BF16)],
        scratch_shapes=[pltpu.VMEM((tm, KV_LORA), BF16), pltpu.VMEM((tm, LANES), BF16)],
        compiler_params=_cparams("parallel", "arbitrary"),
        name="mla_kv_proj",
    )(h_mla, h_mla, kv_norm.reshape(1, KV_LORA), wkv, ccss)


def _attn_body(q_ref, k_ref, v_ref, o_ref, *, nsub):
    ts = q_ref.shape[0] // nsub
    k = k_ref[...]
    v = v_ref[...]

    def scores(j):
        return lax.dot_general(q_ref[j * ts:(j + 1) * ts, :], k, _TB, preferred_element_type=F32)

    def finish(s, j):
        m = jnp.max(s, axis=-1, keepdims=True)
        p = jnp.exp(s - m)
        l = jnp.sum(p, axis=-1, keepdims=True)
        o = jnp.dot(p.astype(BF16), v, preferred_element_type=F32)
        o_ref[j * ts:(j + 1) * ts, :] = (o / l).astype(o_ref.dtype)

    s_prev = scores(0)
    for j in range(1, nsub):
        s_next = scores(j)
        finish(s_prev, j - 1)
        s_prev = s_next
    finish(s_prev, nsub - 1)


def attention(q, k, v, *, batch, seq, tq):
    n = q.shape[0]
    nq = seq // tq
    return pl.pallas_call(
        functools.partial(_attn_body, nsub=tq // ATTN_SUB),
        grid=(batch, MLA_HEADS, nq),
        in_specs=[pl.BlockSpec((tq, QK_PAD), lambda b, h, i: (b * nq + i, h)),
                  pl.BlockSpec((seq, QK_PAD), lambda b, h, i: (b, h)),
                  pl.BlockSpec((seq, V_DIM), lambda b, h, i: (b, h))],
        out_specs=pl.BlockSpec((tq, V_DIM), lambda b, h, i: (b * nq + i, h)),
        out_shape=jax.ShapeDtypeStruct((n, MLA_HEADS * V_DIM), BF16),
        compiler_params=_cparams("parallel", "parallel", "arbitrary"),
        name="mla_attention",
    )(q, k, v)


def _head_sums(x, bd):
    hi = x.astype(BF16)
    lo = (x - hi.astype(F32)).astype(BF16)
    bd2 = jnp.concatenate([bd, bd], axis=0).astype(BF16)
    parts = []
    for gi in range(RW_DIM // LANES):
        ls = slice(gi * LANES, (gi + 1) * LANES)
        parts.append(jnp.dot(jnp.concatenate([hi[:, ls], lo[:, ls]], axis=1), bd2, preferred_element_type=F32))
    return jnp.concatenate(parts, axis=1)


def _sigmoid(x):
    return 1.0 / (1.0 + jnp.exp(-x))


def _rw_prep_body(*refs, tiles_per_seq, tm, has_vfirst):
    (h_ref, hp_ref, hn_ref, mu_ref, vec_ref, w2f_ref, w2b_ref, a2f_ref, a2b_ref, g2_ref, bd_ref) = refs[:11]
    pos = 11
    if has_vfirst:
        v2_ref, vf_ref = refs[pos:pos + 2]
        pos += 2
    r_ref, v_ref, kk_ref, ag_ref, lw_ref, k2_ref, g_ref = refs[pos:pos + 7]

    ti = pl.program_id(0) % tiles_per_seq
    keep_prev = jnp.where(ti == 0, 0.0, 1.0)
    keep_next = jnp.where(ti == tiles_per_seq - 1, 0.0, 1.0)
    row = lax.broadcasted_iota(jnp.int32, (tm, 1), 0)

    def shifted(c0, c1):
        x = h_ref[:, c0:c1]
        prev = jnp.where(row == 0, hp_ref[SUBLANES - 1:SUBLANES, c0:c1] * keep_prev, pltpu.roll(x, 1, axis=0))
        nxt = jnp.where(row == tm - 1, hn_ref[0:1, c0:c1] * keep_next, pltpu.roll(x, tm - 1, axis=0))
        return x * mu_ref[0:1, c0:c1] + (prev + nxt) * mu_ref[1:2, c0:c1]

    vec = vec_ref[...]
    w0f, w0b, a0f, a0b, k_k, k_a, v0 = (vec[i:i + 1, :] for i in range(7))

    r = shifted(0, RW_DIM)
    k = shifted(RW_DIM, 2 * RW_DIM)
    v = shifted(2 * RW_DIM, 3 * RW_DIM)
    lo = shifted(3 * RW_DIM, RW_W)
    dw = jnp.tanh(lo[:, 0:LANES]).astype(BF16)
    da = lo[:, LANES:2 * LANES].astype(BF16)
    dg = _sigmoid(lo[:, 2 * LANES:4 * LANES]).astype(BF16)

    def decay(dw_half, w0, w2_ref):
        z = w0 + jnp.dot(dw_half, w2_ref[...], preferred_element_type=F32)
        return _sigmoid(z) * (-math.exp(-0.5))

    lw_ref[:, 0:RW_DIM] = decay(dw, w0f, w2f_ref)
    lw_ref[:, RW_DIM:] = decay(dw, w0b, w2b_ref)
    af = _sigmoid(a0f + jnp.dot(da, a2f_ref[...], preferred_element_type=F32))
    ab = _sigmoid(a0b + jnp.dot(da, a2b_ref[...], preferred_element_type=F32))
    ag_ref[:, 0:RW_DIM] = af
    ag_ref[:, RW_DIM:] = ab
    g_ref[...] = jnp.dot(dg, g2_ref[...], preferred_element_type=F32)

    if has_vfirst:
        dv = lo[:, 4 * LANES:5 * LANES].astype(BF16)
        mix = _sigmoid(v0 + jnp.dot(dv, v2_ref[...], preferred_element_type=F32))
        v = v + (vf_ref[...] - v) * mix
    v_ref[...] = v
    r_ref[...] = r

    kk = k * k_k
    ss = _head_sums(kk * kk, bd_ref[...])
    kk_ref[...] = kk * lax.rsqrt(jnp.maximum(ss, 1e-24))
    k2_ref[:, 0:RW_DIM] = k * (1.0 + (af - 1.0) * k_a)
    k2_ref[:, RW_DIM:] = k * (1.0 + (ab - 1.0) * k_a)


def rwkv_prep(h_rw, mu, vec, w2f, w2b, a2f, a2b, g2, bd, v2, v_first, *, seq, tm):
    n = h_rw.shape[0]
    tps = seq // tm
    nb8 = n // SUBLANES
    has_vfirst = v_first is not None
    full = lambda a: pl.BlockSpec(a.shape, lambda i: (0,) * a.ndim)
    in_specs = [pl.BlockSpec((tm, RW_W), lambda i: (i, 0)),
                pl.BlockSpec((SUBLANES, RW_W), lambda i: (jnp.maximum(i * (tm // SUBLANES) - 1, 0), 0)),
                pl.BlockSpec((SUBLANES, RW_W), lambda i: (jnp.minimum((i + 1) * (tm // SUBLANES), nb8 - 1), 0)),
                full(mu), full(vec), full(w2f), full(w2b), full(a2f), full(a2b), full(g2), full(bd)]
    args = [h_rw, h_rw, h_rw, mu, vec, w2f, w2b, a2f, a2b, g2, bd]
    if has_vfirst:
        in_specs += [full(v2), pl.BlockSpec((tm, RW_DIM), lambda i: (i, 0))]
        args += [v2, v_first]
    row = lambda w: pl.BlockSpec((tm, w), lambda i: (i, 0))
    widths = [RW_DIM, RW_DIM, RW_DIM, 2 * RW_DIM, 2 * RW_DIM, 2 * RW_DIM, RW_DIM]
    return pl.pallas_call(
        functools.partial(_rw_prep_body, tiles_per_seq=tps, tm=tm, has_vfirst=has_vfirst),
        grid=(n // tm,),
        in_specs=in_specs,
        out_specs=[row(w) for w in widths],
        out_shape=[jax.ShapeDtypeStruct((n, w), F32) for w in widths],
        compiler_params=_cparams("parallel"),
        name="rwkv_prep",
    )(*args)


def _wkv_body(*refs):
    C = WKV_CHUNK
    in_refs, (yf_ref, yb_ref, st_ref) = refs[:12], refs[12:]
    npair = RW_DIM // LANES

    @pl.when(pl.program_id(1) == 0)
    def _():
        st_ref[...] = jnp.zeros_like(st_ref)

    ti = lax.broadcasted_iota(jnp.int32, (C, 3 * C), 0)
    si = lax.broadcasted_iota(jnp.int32, (C, 3 * C), 1) & (C - 1)
    lane = lax.broadcasted_iota(jnp.int32, (1, LANES), 1)
    hm0 = jnp.where(lane < RW_HEAD, 1.0, 0.0)
    hm1 = 1.0 - hm0
    stack = lambda x: jnp.concatenate([x * hm0, x * hm1], axis=0).astype(BF16)
    dup = lambda x: jnp.concatenate([x, x], axis=0).astype(BF16)
    ri = lax.broadcasted_iota(jnp.int32, (2 * C, 2 * C), 0)
    ci = lax.broadcasted_iota(jnp.int32, (2 * C, 2 * C), 1)
    same = (ri & C) == (ci & C)
    dt = (ri & (C - 1)) - (ci & (C - 1))
    eye = jnp.where(ri == ci, 1.0, 0.0)
    valid = (lax.broadcasted_iota(jnp.int32, (2 * C, LANES), 0) & C) == (
        lax.broadcasted_iota(jnp.int32, (2 * C, LANES), 1) & RW_HEAD)
    mm = functools.partial(jnp.dot, preferred_element_type=F32)
    mm_tb = lambda a, b: lax.dot_general(a, b, _TB, preferred_element_type=F32)
    mm_ta = lambda a, b: lax.dot_general(a, b, _TA, preferred_element_type=F32)

    chains = [(d, p) for d in range(2) for p in range(npair)]
    xa, xr, vst, bws, kws, wtots, gms, strict, incl = [], [], [], [], [], [], [], [], []
    for d in range(2):
        r_ref, v_ref, kk_ref, lw_ref, k_ref, ag_ref = in_refs[6 * d:6 * d + 6]
        sgn = 1 - 2 * d
        tri3 = jnp.where((ti - si) * sgn >= 0, 1.0, 0.0).astype(BF16)
        lw_all = lw_ref[...]
        lw_hi = lw_all.astype(BF16)
        rem = lw_all - lw_hi.astype(F32)
        lw_mid = rem.astype(BF16)
        lw_lo = (rem - lw_mid.astype(F32)).astype(BF16)
        cum_all = jnp.dot(tri3, jnp.concatenate([lw_hi, lw_mid, lw_lo], axis=0), preferred_element_type=F32)
        tot_all = jnp.sum(lw_all, axis=0, keepdims=True)
        strict_d = same & (dt * sgn > 0)
        incl_d = same & (dt * sgn >= 0)
        for p in range(npair):
            ls = slice(p * LANES, (p + 1) * LANES)
            r, v, kk, k, ag = (x[:, ls] for x in (r_ref, v_ref, kk_ref, k_ref, ag_ref))
            lw, cum, tot = lw_all[:, ls], cum_all[:, ls], tot_all[:, ls]
            ei = jnp.exp(-cum)
            ew = jnp.exp(tot - cum)
            b = kk * ag
            xa.append(stack(-kk * jnp.exp(cum - lw)))
            xr.append(stack(r * jnp.exp(cum)))
            vst.append(stack(v))
            bws.append(stack(b * ew))
            kws.append(stack(k * ew))
            wtots.append(jnp.exp(tot))
            strict.append(strict_d)
            incl.append(incl_d)
            x2 = jnp.concatenate([xa[-1], xr[-1]], axis=0)
            y2 = jnp.concatenate([dup(b * ei), dup(k * ei)], axis=0)
            gms.append(mm_tb(x2, y2))
    nch = range(len(chains))
    l_ab = [jnp.where(strict[i], gms[i][0:2 * C, 0:2 * C], 0.0) for i in nch]
    l_ak = [jnp.where(strict[i], gms[i][0:2 * C, 2 * C:], 0.0).astype(BF16) for i in nch]
    m_rb = [jnp.where(incl[i], gms[i][2 * C:, 0:2 * C], 0.0).astype(BF16) for i in nch]
    m_rk = [jnp.where(incl[i], gms[i][2 * C:, 2 * C:], 0.0).astype(BF16) for i in nch]

    pinv = [eye + l for l in l_ab]
    lp = l_ab
    for _ in range(int(math.log2(C)) - 1):
        lpb = [l.astype(BF16) for l in lp]
        lp = [mm(l, l) for l in lpb]
        pinv = [pi + mm(pi.astype(BF16), l.astype(BF16)) for pi, l in zip(pinv, lp)]

    st = [st_ref[d, p] for d, p in chains]
    s_kv = [s.T.astype(BF16) for s in st]
    pre = [mm(jnp.concatenate([xa[i], l_ak[i]], axis=1), jnp.concatenate([s_kv[i], vst[i]], axis=0))
           for i in nch]
    ustb = [jnp.where(valid, mm(pinv[i].astype(BF16), pre[i].astype(BF16)), 0.0).astype(BF16) for i in nch]
    for i, (d, p) in enumerate(chains):
        yst = mm(jnp.concatenate([xr[i], m_rb[i], m_rk[i]], axis=1),
                 jnp.concatenate([s_kv[i], ustb[i], vst[i]], axis=0))
        yst = jnp.where(valid, yst, 0.0)
        (yf_ref, yb_ref)[d][:, p * LANES:(p + 1) * LANES] = yst[0:C] + yst[C:]
    for i, (d, p) in enumerate(chains):
        st_ref[d, p] = st[i] * wtots[i] + mm_ta(jnp.concatenate([ustb[i], vst[i]], axis=0),
                                                jnp.concatenate([bws[i], kws[i]], axis=0))


def wkv(r, v, kk, lw2, k2, ag2, *, batch, seq):
    n = r.shape[0]
    C = WKV_CHUNK
    nc = seq // C
    fwd = lambda col: pl.BlockSpec((C, RW_DIM), lambda b, c: (b * nc + c, col))
    bwd = lambda col: pl.BlockSpec((C, RW_DIM), lambda b, c: (b * nc + nc - 1 - c, col))
    return pl.pallas_call(
        _wkv_body,
        grid=(batch, nc),
        in_specs=[fwd(0), fwd(0), fwd(0), fwd(0), fwd(0), fwd(0),
                  bwd(0), bwd(0), bwd(0), bwd(1), bwd(1), bwd(1)],
        out_specs=[fwd(0), bwd(0)],
        out_shape=[jax.ShapeDtypeStruct((n, RW_DIM), F32)] * 2,
        scratch_shapes=[pltpu.VMEM((2, RW_DIM // LANES, LANES, LANES), F32)],
        compiler_params=_cparams("parallel", "arbitrary"),
        name="wkv_scan",
    )(r, v, kk, lw2, k2, ag2, r, v, kk, lw2, k2, ag2)


def _rw_post_body(yf_ref, yb_ref, r_ref, k2_ref, v_ref, g_ref, vec_ref, bd_ref, o_ref):
    vec = vec_ref[...]
    lnx_w, lnx_b, r_k = (vec[i:i + 1, :] for i in range(3))
    bd = bd_ref[...]
    y = yf_ref[...] + yb_ref[...]
    inv = 1.0 / RW_HEAD
    mean = _head_sums(y, bd) * inv
    d = y - mean
    var = _head_sums(d * d, bd) * inv
    yn = d * lax.rsqrt(var + LNX_EPS) * lnx_w + lnx_b
    ksum = k2_ref[:, 0:RW_DIM] + k2_ref[:, RW_DIM:]
    bonus = _head_sums(r_ref[...] * ksum * r_k, bd) * v_ref[...]
    o_ref[...] = ((yn + bonus) * g_ref[...]).astype(o_ref.dtype)


def rwkv_post(yf, yb, r, k2, v, g, vec, bd, *, tm):
    n = r.shape[0]
    row = lambda w: pl.BlockSpec((tm, w), lambda i: (i, 0))
    full = lambda a: pl.BlockSpec(a.shape, lambda i: (0,) * a.ndim)
    return pl.pallas_call(
        _rw_post_body,
        grid=(n // tm,),
        in_specs=[row(RW_DIM), row(RW_DIM), row(RW_DIM), row(2 * RW_DIM), row(RW_DIM), row(RW_DIM),
                  full(vec), full(bd)],
        out_specs=row(RW_DIM),
        out_shape=jax.ShapeDtypeStruct((n, RW_DIM), BF16),
        compiler_params=_cparams("parallel"),
        name="rwkv_post",
    )(yf, yb, r, k2, v, g, vec, bd)


def _merge_body(o_ref, yg_ref, ga_ref, gr_ref, wa_ref, wr_ref, m_ref):
    attn = jnp.dot(o_ref[...], wa_ref[...], preferred_element_type=F32)
    rw = jnp.dot(yg_ref[...], wr_ref[...], preferred_element_type=F32)
    m_ref[...] = (_sigmoid(ga_ref[...].astype(F32)) * attn + _sigmoid(gr_ref[...].astype(F32)) * rw).astype(m_ref.dtype)


def merge_branches(o, yg, gates, wa, wr, *, tm, tn):
    n = o.shape[0]
    nj = D_MODEL // tn
    return pl.pallas_call(
        _merge_body,
        grid=(n // tm, nj),
        in_specs=[pl.BlockSpec((tm, MLA_HEADS * V_DIM), lambda i, j: (i, 0)),
                  pl.BlockSpec((tm, RW_DIM), lambda i, j: (i, 0)),
                  pl.BlockSpec((tm, tn), lambda i, j: (i, j)),
                  pl.BlockSpec((tm, tn), lambda i, j: (i, nj + j)),
                  pl.BlockSpec((MLA_HEADS * V_DIM, tn), lambda i, j: (0, j)),
                  pl.BlockSpec((RW_DIM, tn), lambda i, j: (0, j))],
        out_specs=pl.BlockSpec((tm, tn), lambda i, j: (i, j)),
        out_shape=jax.ShapeDtypeStruct((n, D_MODEL), BF16),
        compiler_params=_cparams("parallel", "parallel"),
        name="merge_branches",
    )(o, yg, gates, gates, wa, wr)


def _mm_res_norm_body(a_ref, w_ref, x_ref, g_ref, o_ref, *n_refs):
    out = x_ref[...] + jnp.dot(a_ref[...], w_ref[...], preferred_element_type=F32)
    o_ref[...] = out
    xhat = out * lax.rsqrt(jnp.mean(out * out, axis=-1, keepdims=True) + EPS)
    for row, n_ref in enumerate(n_refs):
        n_ref[...] = (xhat * g_ref[row:row + 1, :]).astype(n_ref.dtype)


def matmul_residual_norm(a, w, x, gains, out_dtypes, *, tm):
    n, k = a.shape
    nc = w.shape[1]
    row = lambda width: pl.BlockSpec((tm, width), lambda i: (i, 0))
    return pl.pallas_call(
        _mm_res_norm_body,
        grid=(n // tm,),
        in_specs=[row(k),
                  pl.BlockSpec((k, nc), lambda i: (0, 0), pipeline_mode=pl.Buffered(1)),
                  row(nc),
                  pl.BlockSpec(gains.shape, lambda i: (0, 0))],
        out_specs=[row(nc)] * (1 + len(out_dtypes)),
        out_shape=[jax.ShapeDtypeStruct((n, nc), F32)] + [jax.ShapeDtypeStruct((n, nc), dt) for dt in out_dtypes],
        compiler_params=_cparams("parallel"),
        name="matmul_residual_norm",
    )(a, w, x, gains)


def _top16_rows(ss, idx):
    big = 3.0e38
    vals = [[] for _ in ss]
    poss = [[] for _ in ss]
    for _ in range(PEER_TOPK):
        ms = [jnp.max(s, axis=0, keepdims=True) for s in ss]
        ps = [jnp.min(jnp.where(s == m, idx, big), axis=0, keepdims=True) for s, m in zip(ss, ms)]
        ss = [jnp.where(idx == p, -jnp.inf, s) for s, p in zip(ss, ps)]
        for i, (m, p) in enumerate(zip(ms, ps)):
            vals[i].append(m)
            poss[i].append(p)
    return [(jnp.concatenate(v, axis=0), jnp.concatenate(p, axis=0)) for v, p in zip(vals, poss)]


def _candidates(sv0, sv1):
    tok = sv0.shape[1]
    io = lambda rows: lax.broadcasted_iota(jnp.int32, (rows, tok), 0).astype(F32)
    half = PEER_TOPK // 2
    parts = [sv0[0:1, :] + sv1]
    idxs = [io(PEER_TOPK)]
    for a in range(1, half):
        parts.append(sv0[a:a + 1, :] + sv1[0:half, :])
        idxs.append(io(half) + float(a * PEER_TOPK))
    parts.append(sv0[half:, :] + sv1[0:1, :])
    idxs.append((io(half) + float(half)) * float(PEER_TOPK))
    return jnp.concatenate(parts, axis=0), jnp.concatenate(idxs, axis=0)


def _pick_rows(table, sel):
    out = jnp.zeros(sel.shape, table.dtype)
    for a in range(PEER_TOPK):
        out = jnp.where(sel == a, table[a:a + 1, :], out)
    return out


def _peer_topk_body(q_ref, sk_ref, i1_ref, i2_ref, gt_ref):
    i1s, i2s, gts = [], [], []
    key_idx = lax.broadcasted_iota(jnp.int32, (N_KEYS, q_ref.shape[0]), 0).astype(F32)
    for h in range(PEER_HEADS):
        ss = []
        for p in range(2):
            c0 = (2 * h + p) * PEER_HALF
            qb = q_ref[:, c0:c0 + PEER_HALF].astype(BF16)
            ss.append(lax.dot_general(sk_ref[p], qb, _TB, preferred_element_type=F32))
        (sv0, si0), (sv1, si1) = _top16_rows(ss, key_idx)
        cand, cand_idx = _candidates(sv0, sv1)
        ((top_s, top_j),) = _top16_rows([cand], cand_idx)
        ja = jnp.floor(top_j * (1.0 / PEER_TOPK))
        i1s.append(_pick_rows(si0, ja))
        i2s.append(_pick_rows(si1, top_j - ja * PEER_TOPK))
        e = jnp.exp(top_s - top_s[0:1, :])
        gts.append(e / jnp.sum(e, axis=0, keepdims=True))
    i1_ref[...] = jnp.concatenate(i1s, axis=0).T
    i2_ref[...] = jnp.concatenate(i2s, axis=0).T
    gt_ref[...] = jnp.concatenate(gts, axis=0).T


def peer_topk(q, sub_keys):
    n = q.shape[0]
    tmk = LANES
    out = pl.BlockSpec((tmk, NSLOT), lambda i: (i, 0))
    return pl.pallas_call(
        _peer_topk_body,
        grid=(n // tmk,),
        in_specs=[pl.BlockSpec((tmk, 2 * PEER_HALF * PEER_HEADS), lambda i: (i, 0)),
                  pl.BlockSpec(sub_keys.shape, lambda i: (0, 0, 0))],
        out_specs=[out, out, out],
        out_shape=[jax.ShapeDtypeStruct((n, NSLOT), F32)] * 3,
        compiler_params=_cparams("parallel"),
        name="peer_topk",
    )(q, sub_keys)


def _gelu_exact(x):
    return 0.5 * x * (1.0 + lax.erf(x * (1.0 / math.sqrt(2.0))))


_HI16 = 0xFFFF0000
G_UNROLL = SUBLANES
E_SUB = 2 * N_KEYS
PEER_TE = 2 * E_SUB


def _bf16_bits(x):
    return pltpu.bitcast(x, jnp.uint32) + jnp.uint32(0x8000)


def _peer_dense_body(xn_ref, i1_ref, i2_ref, gt_ref, ut0_ref, utb_ref, utn_ref, v_ref, x_ref, lnf_ref, scal_ref,
                     o_ref, *rest, tm, te, final_norm):
    xnext_ref = None if final_norm else rest[0]
    g_ref, h_ref = rest[-2:]
    e = pl.program_id(1)
    half = tm // 2
    mm_tb = lambda a, b: lax.dot_general(a, b, _TB, preferred_element_type=F32)
    hsc, csc, osc = (scal_ref[r:r + 1, 0:1] for r in range(3))
    nsub = te // E_SUB

    def activations(ut_ref):
        xn = xn_ref[...]
        return [jnp.dot(xn, ut_ref[:, j * E_SUB:(j + 1) * E_SUB], preferred_element_type=F32) * hsc
                for j in range(nsub)]

    def accumulate(block, hs):
        coefs = []
        for j in range(nsub):
            row0 = block * (te // N_KEYS) + j * (E_SUB // N_KEYS)
            words = [g_ref[pl.ds(row0 + r, half, stride=G_PITCH), :] for r in range(E_SUB // N_KEYS)]
            top = jnp.concatenate([pltpu.bitcast(w & jnp.uint32(_HI16), F32) for w in words], axis=1)
            bot = jnp.concatenate([pltpu.bitcast(w << 16, F32) for w in words], axis=1)
            gate = jnp.concatenate([top, bot], axis=0)
            coef = gate * _gelu_exact(hs[j]) * csc
            coefs.append(jnp.clip(coef, -FP8_MAX, FP8_MAX).astype(FP8))
        return coefs

    @pl.when(e == 0)
    def _():
        for j, h in enumerate(activations(ut0_ref)):
            h_ref[:, j * E_SUB:(j + 1) * E_SUB] = h
        o_ref[...] = x_ref[...] * osc
        sub = lax.broadcasted_iota(jnp.int32, (N_KEYS, NSLOT), 0).astype(F32).astype(BF16)
        one = jnp.ones((), BF16)
        zero = jnp.zeros((), BF16)

        def build(it, carry):
            base = pl.multiple_of(it * G_UNROLL, SUBLANES)
            tiles = [(ref[pl.ds(base, G_UNROLL), :], ref[pl.ds(half + base, G_UNROLL), :])
                     for ref in (i1_ref, i2_ref, gt_ref)]
            a_ts, b_ts = [], []
            for j in range(G_UNROLL):
                for t in range(2):
                    i1, i2, gt = (tiles[q][t][j:j + 1, :].astype(BF16) for q in range(3))
                    a_ts.append(jnp.where(sub == i1, one, zero))
                    b_ts.append(jnp.where(sub == i2, gt, zero))
            gs = [mm_tb(a, b) for a, b in zip(a_ts, b_ts)]
            for j in range(G_UNROLL):
                word = (_bf16_bits(gs[2 * j]) & jnp.uint32(_HI16)) | (_bf16_bits(gs[2 * j + 1]) >> 16)
                g_ref[pl.ds(pl.multiple_of((base + j) * G_PITCH, SUBLANES), N_KEYS), :] = word
            return carry

        lax.fori_loop(0, half // G_UNROLL, build, 0)

    hs_b = activations(utb_ref)
    coef_a = accumulate(2 * e, [h_ref[:, j * E_SUB:(j + 1) * E_SUB] for j in range(nsub)])
    for j, h in enumerate(activations(utn_ref)):
        h_ref[:, j * E_SUB:(j + 1) * E_SUB] = h
    coef_b = accumulate(2 * e + 1, hs_b)
    o_ref[...] += jnp.dot(jnp.concatenate(coef_a + coef_b, axis=1), v_ref[...], preferred_element_type=F32)

    @pl.when(e == pl.num_programs(1) - 1)
    def _():
        out = o_ref[...] * scal_ref[3:4, 0:1]
        normed = _rms_rows(out, lnf_ref[...], EPS)
        if final_norm:
            o_ref[...] = normed
        else:
            o_ref[...] = out
            xnext_ref[...] = normed.astype(xnext_ref.dtype)


def peer_dense(xn, i1, i2, gt, ut, v, x, ln_f, scal, *, tm, te, final_norm):
    n = xn.shape[0]
    ne = v.shape[0] // te
    assert tm % (2 * G_UNROLL) == 0 and te % E_SUB == 0 and ne % 2 == 0 and ut.shape == (ne, D_MODEL, te)
    row = lambda w: pl.BlockSpec((tm, w), lambda i, e: (i, 0))
    return pl.pallas_call(
        functools.partial(_peer_dense_body, tm=tm, te=te, final_norm=final_norm),
        grid=(n // tm, ne // 2),
        in_specs=[row(D_MODEL), row(NSLOT), row(NSLOT), row(NSLOT),
                  pl.BlockSpec((None, D_MODEL, te), lambda i, e: (0, 0, 0)),
                  pl.BlockSpec((None, D_MODEL, te), lambda i, e: (2 * e + 1, 0, 0)),
                  pl.BlockSpec((None, D_MODEL, te), lambda i, e: (jnp.minimum(2 * e + 2, ne - 1), 0, 0)),
                  pl.BlockSpec((2 * te, D_MODEL), lambda i, e: (e, 0)),
                  row(D_MODEL),
                  pl.BlockSpec((1, D_MODEL), lambda i, e: (0, 0)),
                  pl.BlockSpec(scal.shape, lambda i, e: (0, 0))],
        out_specs=[row(D_MODEL)] * (1 if final_norm else 2),
        out_shape=[jax.ShapeDtypeStruct((n, D_MODEL), F32)] + ([] if final_norm else
                                                                [jax.ShapeDtypeStruct((n, D_MODEL), BF16)]),
        scratch_shapes=[pltpu.VMEM((tm // 2 * G_PITCH, N_KEYS), jnp.uint32), pltpu.VMEM((tm, te), F32)],
        compiler_params=_cparams("parallel", "arbitrary"),
        name="peer_dense",
    )(xn, i1, i2, gt, ut, ut, ut, v, x, ln_f.reshape(1, D_MODEL), scal)


def _pad_rows(a, h):
    return jnp.pad(a, ((0, h - a.shape[0]),) + ((0, 0),) * (a.ndim - 1))


def _swap_halves(a):
    half = a.shape[-1] // 2
    return jnp.concatenate([a[..., half:], a[..., :half]], axis=-1)


def _layer_weights(i, p):
    first = i == 0
    w_in = p["w_in_first"] if first else p["w_in_rest"][i - 1]
    mu = p["mu_first"] if first else p["mu_rest"][i - 1]
    mla_cols = Q_LORA + KV_LORA + ROPE_DIM
    gate0 = mla_cols
    rw0 = mla_cols + 2 * D_MODEL
    k_rope_w = w_in[:, Q_LORA + KV_LORA:mla_cols]
    w_mla = jnp.concatenate([w_in[:, :mla_cols], _swap_halves(k_rope_w)], axis=1)
    w_gate = w_in[:, gate0:rw0]

    def rw_layout(a):
        c = 3 * RW_DIM
        dwf, dwb = a[..., c:c + DECAY_LORA], a[..., c + DECAY_LORA:c + 2 * DECAY_LORA]
        c += 2 * DECAY_LORA
        daf, dab = a[..., c:c + AAA_LORA], a[..., c + AAA_LORA:c + 2 * AAA_LORA]
        c += 2 * AAA_LORA
        dg = a[..., c:c + GATE_LORA]
        c += GATE_LORA
        dv = a[..., c:]
        z = lambda w: jnp.zeros(a.shape[:-1] + (w,), a.dtype)
        return jnp.concatenate([a[..., :3 * RW_DIM], dwf, dwb, daf, dab, dg, z(2 * LANES - GATE_LORA),
                                dv, z(2 * LANES - dv.shape[-1])], axis=-1)

    w_rw = rw_layout(w_in[:, rw0:])
    mu_l = rw_layout(mu[None, :])
    mu_l = jnp.concatenate([1.0 - mu_l, 0.5 * mu_l], axis=0)
    zeros64 = jnp.zeros((DECAY_LORA, RW_DIM), F32)
    wq = (p["w_uq"][i] * (NOPE_DIM + ROPE_DIM) ** -0.5).reshape(Q_LORA, MLA_HEADS, NOPE_DIM + ROPE_DIM)
    wq = jnp.concatenate([wq, _swap_halves(wq[..., NOPE_DIM:])], axis=-1).reshape(Q_LORA, MLA_HEADS * QK_PAD)
    vec = jnp.stack([p["w0_f"][i], p["w0_b"][i], p["a0_f"][i], p["a0_b"][i], p["k_k"][i], p["k_a"][i],
                     p["v0_rest"][i - 1] if not first else jnp.zeros((RW_DIM,), F32),
                     jnp.zeros((RW_DIM,), F32)])
    lw = {
        "ln1": p["ln1"][i], "w_mla": w_mla.astype(BF16), "w_gate": w_gate.astype(BF16),
        "w_rw": w_rw.astype(BF16),
        "mu": mu_l, "q_norm": p["q_norm"][i], "wq": wq.astype(BF16), "kv_norm": p["kv_norm"][i],
        "wkv": p["w_ukv"][i].astype(BF16), "w_o_attn": p["w_o_attn"][i].astype(BF16),
        "vec": vec,
        "w2f": jnp.concatenate([p["w2_f"][i], zeros64]).astype(BF16),
        "w2b": jnp.concatenate([zeros64, p["w2_b"][i]]).astype(BF16),
        "a2f": jnp.concatenate([p["a2_f"][i], zeros64]).astype(BF16),
        "a2b": jnp.concatenate([zeros64, p["a2_b"][i]]).astype(BF16),
        "g2": _pad_rows(p["g2"][i], 2 * LANES).astype(BF16),
        "v2": None if first else _pad_rows(p["v2_rest"][i - 1], LANES).astype(BF16),
        "post_vec": jnp.stack([p["lnx_w"][i], p["lnx_b"][i], p["r_k"][i].reshape(RW_DIM)]
                              + [jnp.zeros((RW_DIM,), F32)] * 5),
        "w_o_rwkv": p["w_o_rwkv"][i].astype(BF16), "w_out": p["w_out"][i].astype(BF16),
        "ln2": p["ln2"][i], "w_pq": p["w_pq"][i].astype(BF16), "sub_keys": p["sub_keys"][i].astype(BF16),
    }
    pow2_below = lambda v: jnp.exp2(jnp.floor(jnp.log2(v)))
    tiny = 1e-30
    u, v, g2n = p["peer_u"][i], p["peer_v"][i], p["ln2"][i]
    xmax = math.sqrt(D_MODEL) * jnp.maximum(jnp.max(jnp.abs(g2n)), tiny)
    sx = pow2_below(FP8_MAX / xmax)
    su = pow2_below(FP8_MAX / jnp.maximum(jnp.max(jnp.abs(u)), tiny))
    sv = pow2_below(FP8_MAX / jnp.maximum(jnp.max(jnp.abs(v)), tiny))
    hmax = xmax * jnp.sqrt(jnp.max(jnp.sum(u * u, axis=1))) * (1.0 + 2.0 ** -4) ** 2
    sc = pow2_below(FP8_MAX / (PEER_HEADS * jnp.maximum(hmax, tiny)))
    lw["ln2_rows"] = jnp.stack([g2n, g2n * sx])
    lw["peer_ut"] = (u * su).astype(FP8).reshape(-1, PEER_TE, D_MODEL).transpose(0, 2, 1)
    lw["peer_v"] = (v * sv).astype(FP8)
    scal = jnp.stack([1.0 / (sx * su), sc, sc * sv, 1.0 / (sc * sv)] + [jnp.zeros((), F32)] * 4)
    lw["peer_scal"] = jnp.broadcast_to(scal[:, None], (SUBLANES, LANES)).astype(F32)
    return lw


def _rope_table(seq):
    inv = 1.0 / (ROPE_THETA ** (jnp.arange(0, ROPE_DIM, 2, dtype=F32) / ROPE_DIM))
    ang = jnp.arange(seq, dtype=F32)[:, None] * inv[None, :]
    c, s = jnp.cos(ang), jnp.sin(ang)
    return jnp.concatenate([c, c, -s, s], axis=1)


def _tiles(seq):
    return dict(tm=min(512, seq), tmm=min(1024, seq), tq=min(16 * ATTN_SUB, seq), tprep=min(256, seq),
                tpeer=min(512, seq))


def _trunk(x, layers, ln_f, bd):
    batch, seq, _ = x.shape
    n = batch * seq
    t = _tiles(seq)
    tm, tmm = t["tm"], t["tmm"]
    ccss = _rope_table(seq)
    x = x.reshape(n, D_MODEL)
    v_first = None
    xn1 = None
    for li, lw in enumerate(layers):
        if xn1 is None:
            (xn1,) = rms_norm(x, lw["ln1"].reshape(1, D_MODEL), (BF16,), tm=tm)
        h_mla = matmul(xn1, lw["w_mla"], tm=tmm, tn=MLA_W, out_dtype=BF16)
        gates = matmul(xn1, lw["w_gate"], tm=tmm, tn=1024, out_dtype=BF16)
        h_rw = matmul(xn1, lw["w_rw"], tm=tmm, tn=1280, out_dtype=F32)
        q = mla_q_proj(h_mla, lw["q_norm"], lw["wq"], ccss, seq=seq, tm=tm)
        k, v = mla_kv_proj(h_mla, lw["kv_norm"], lw["wkv"], ccss, seq=seq, tm=tm)
        o = attention(q, k, v, batch=batch, seq=seq, tq=t["tq"])
        r, vv, kk, ag2, lw2, k2, g = rwkv_prep(h_rw, lw["mu"], lw["vec"], lw["w2f"], lw["w2b"], lw["a2f"],
                                              lw["a2b"], lw["g2"], bd, lw["v2"], v_first,
                                              seq=seq, tm=t["tprep"])
        if v_first is None:
            v_first = vv
        yf, yb = wkv(r, vv, kk, lw2, k2, ag2, batch=batch, seq=seq)
        yg = rwkv_post(yf, yb, r, k2, vv, g, lw["post_vec"], bd, tm=t["tprep"])
        m = merge_branches(o, yg, gates, lw["w_o_attn"], lw["w_o_rwkv"], tm=tmm, tn=512)
        x, xn, xn8 = matmul_residual_norm(m, lw["w_out"], x, lw["ln2_rows"], (BF16, FP8), tm=tm)
        qp = matmul(xn, lw["w_pq"], tm=tmm, tn=1024, out_dtype=F32)
        i1, i2, gt = peer_topk(qp, lw["sub_keys"])
        last = li == len(layers) - 1
        res = peer_dense(xn8, i1, i2, gt, lw["peer_ut"], lw["peer_v"], x, ln_f if last else layers[li + 1]["ln1"],
                         lw["peer_scal"], tm=t["tpeer"], te=PEER_TE, final_norm=last)
        x, xn1 = (res[0], None) if last else res
    return x.reshape(batch, seq, D_MODEL)


def kernel(x_prompt, x_sample, ln1, w_in_first, mu_first, w_in_rest, mu_rest, q_norm, w_uq, kv_norm, w_ukv, w_o_attn, w0_f, w2_f, w0_b, w2_b, a0_f, a2_f, a0_b, a2_b, g2, k_k, k_a, r_k, lnx_w, lnx_b, v0_rest, v2_rest, w_o_rwkv, w_out, ln2, w_pq, sub_keys, peer_u, peer_v, ln_f):
    p = dict(ln1=ln1, w_in_first=w_in_first, mu_first=mu_first, w_in_rest=w_in_rest, mu_rest=mu_rest,
             q_norm=q_norm, w_uq=w_uq, kv_norm=kv_norm, w_ukv=w_ukv, w_o_attn=w_o_attn, w0_f=w0_f, w2_f=w2_f,
             w0_b=w0_b, w2_b=w2_b, a0_f=a0_f, a2_f=a2_f, a0_b=a0_b, a2_b=a2_b, g2=g2, k_k=k_k, k_a=k_a, r_k=r_k,
             lnx_w=lnx_w, lnx_b=lnx_b, v0_rest=v0_rest, v2_rest=v2_rest, w_o_rwkv=w_o_rwkv, w_out=w_out,
             ln2=ln2, w_pq=w_pq, sub_keys=sub_keys, peer_u=peer_u, peer_v=peer_v)
    depth = ln1.shape[0]
    layers = [_layer_weights(i, p) for i in range(depth)]
    lane = jnp.arange(LANES)
    bd = (lane[:, None] // RW_HEAD == lane[None, :] // RW_HEAD).astype(F32)
    return (_trunk(x_prompt, layers, ln_f, bd), _trunk(x_sample, layers, ln_f, bd))
```
